```python
import math
import jax, jax.numpy as jnp
from jax import lax
import numpy as np

D_MODEL = 1024
BATCH = 16
SEQ = 2048
DEPTH = 2

A_HEADS = 8
A_HEAD_DIM = 64
A_WIDTH = A_HEADS * A_HEAD_DIM
A_DECAY_LORA = 64
A_ICLR_LORA = 64
A_GATE_LORA = 128
A_PROJ = 3 * A_WIDTH + A_DECAY_LORA + A_ICLR_LORA + A_GATE_LORA
A_GN_EPS = 64e-5

B_HEADS = 4
B_QK_DIM = 128
B_V_DIM = 256
B_QK_WIDTH = B_HEADS * B_QK_DIM
B_V_WIDTH = B_HEADS * B_V_DIM
B_CHUNK = 128
B_ROPE_BASE = 10000.0
B_GN_EPS = 1e-5

C_WIDTH = 512
C_GROUP = 16
C_GROUPS = C_WIDTH // C_GROUP
C_STATE = 64
C_DT_MIN = 1e-3
C_DT_MAX = 1e-1

PROJ_TOTAL = A_PROJ + 2 * B_QK_WIDTH + 2 * B_V_WIDTH + C_WIDTH + 3 * D_MODEL
PROJ_SPLITS = (A_PROJ,
               A_PROJ + B_QK_WIDTH,
               A_PROJ + 2 * B_QK_WIDTH,
               A_PROJ + 2 * B_QK_WIDTH + B_V_WIDTH,
               A_PROJ + 2 * B_QK_WIDTH + 2 * B_V_WIDTH,
               A_PROJ + 2 * B_QK_WIDTH + 2 * B_V_WIDTH + C_WIDTH)
A_SPLITS = (A_WIDTH, 2 * A_WIDTH, 3 * A_WIDTH,
            3 * A_WIDTH + A_DECAY_LORA, 3 * A_WIDTH + A_DECAY_LORA + A_ICLR_LORA)
BRANCH_ROWS = A_WIDTH + B_V_WIDTH + C_WIDTH

D_FF = 4 * D_MODEL
DN_ALPHA = (2.0 * DEPTH) ** 0.25
DN_BETA = (8.0 * DEPTH) ** -0.25
LN_EPS = 1e-5

kernel_name = 'hybrid_rwkv7_retnet_s5_deepnorm'


def _layer_norm(x, g, b):
    xf = x.astype(jnp.float32)
    mu = xf.mean(-1, keepdims=True)
    var = jnp.square(xf - mu).mean(-1, keepdims=True)
    y = (xf - mu) * lax.rsqrt(var + LN_EPS)
    return (y * g.astype(jnp.float32) + b.astype(jnp.float32)).astype(x.dtype)


def _group_norm(x, eps):
    xf = x.astype(jnp.float32)
    mu = xf.mean(-1, keepdims=True)
    var = jnp.square(xf - mu).mean(-1, keepdims=True)
    return (xf - mu) * lax.rsqrt(var + eps)


def _token_shift(z, mu):
    prev = jnp.pad(z, ((0, 0), (1, 0), (0, 0)))[:, :-1]
    return z + mu * (prev - z)


def _rwkv7_mixer(z, w0, w2, a0, a2, g2, k_k, k_a, r_k, lnx_g, lnx_b):
    bsz, seq, _ = z.shape
    r, k, v, zw, za, zg = jnp.split(z, A_SPLITS, axis=-1)
    w_ll = -jax.nn.softplus(-(w0 + jnp.tanh(zw) @ w2)) - 0.5
    a = jax.nn.sigmoid(a0 + za @ a2)
    g = jax.nn.sigmoid(zg) @ g2
    heads = lambda t: t.reshape(bsz, seq, A_HEADS, A_HEAD_DIM).astype(jnp.float32)
    r, k, v, a, w_ll = heads(r), heads(k), heads(v), heads(a), heads(w_ll)
    kk = k * k_k.astype(jnp.float32)
    kk = kk / jnp.maximum(jnp.sqrt(jnp.sum(kk * kk, -1, keepdims=True)), 1e-12)
    k = k * (1.0 + (a - 1.0) * k_a.astype(jnp.float32))
    decay = jnp.exp(-jnp.exp(w_ll))

    def step(state, inp):
        r_t, w_t, k_t, v_t, na_t, nb_t = inp
        sa = jnp.einsum('bhvk,bhk->bhv', state, na_t)
        state = (state * w_t[:, :, None, :] + sa[..., None] * nb_t[:, :, None, :]
                 + v_t[..., None] * k_t[:, :, None, :])
        return state, jnp.einsum('bhvk,bhk->bhv', state, r_t)

    xs = tuple(jnp.moveaxis(t, 1, 0) for t in (r, decay, k, v, -kk, kk * a))
    s0 = jnp.zeros((bsz, A_HEADS, A_HEAD_DIM, A_HEAD_DIM), jnp.float32)
    _, o = lax.scan(step, s0, xs)
    o = jnp.moveaxis(o, 0, 1)
    o = _group_norm(o, A_GN_EPS) * lnx_g.astype(jnp.float32) + lnx_b.astype(jnp.float32)
    o = o + jnp.sum(r * k * r_k.astype(jnp.float32), -1, keepdims=True) * v
    return (o.reshape(bsz, seq, A_WIDTH) * g.astype(jnp.float32)).astype(z.dtype)


def _rope(t, cos, sin):
    t1, t2 = jnp.split(t, 2, axis=-1)
    return jnp.concatenate([t1 * cos - t2 * sin, t2 * cos + t1 * sin], axis=-1)


def _retention_mixer(q, k, v, g):
    bsz, seq, _ = q.shape
    n_chunks = seq // B_CHUNK
    f32 = jnp.float32
    pos = jnp.arange(seq, dtype=f32)
    half = B_QK_DIM // 2
    inv_freq = B_ROPE_BASE ** (-jnp.arange(half, dtype=f32) / half)
    ang = pos[:, None] * inv_freq[None, :]
    cos, sin = jnp.cos(ang)[:, None, :], jnp.sin(ang)[:, None, :]
    q = _rope(q.reshape(bsz, seq, B_HEADS, B_QK_DIM).astype(f32), cos, sin)
    k = _rope(k.reshape(bsz, seq, B_HEADS, B_QK_DIM).astype(f32), cos, sin) * (B_QK_DIM ** -0.5)
    v = v.reshape(bsz, seq, B_HEADS, B_V_DIM).astype(f32)
    log_gamma = jnp.log(1.0 - 2.0 ** (-5.0 - jnp.arange(B_HEADS, dtype=f32)))
    idx = jnp.arange(B_CHUNK, dtype=f32)
    rel = idx[:, None] - idx[None, :]
    inner_decay = jnp.where(rel >= 0, jnp.exp(log_gamma[:, None, None] * jnp.maximum(rel, 0.0)), 0.0)
    q_decay = jnp.exp(log_gamma[:, None] * (idx + 1.0))
    k_decay = jnp.exp(log_gamma[:, None] * (B_CHUNK - 1.0 - idx))
    chunk_decay = jnp.exp(log_gamma * B_CHUNK)
    to_chunks = lambda t: t.reshape(bsz, n_chunks, B_CHUNK, B_HEADS, t.shape[-1])
    qc, kc, vc = to_chunks(q), to_chunks(k), to_chunks(v)
    scores = jnp.einsum('bnihd,bnjhd->bnhij', qc, kc) * inner_decay
    inner = jnp.einsum('bnhij,bnjhe->bnihe', scores, vc)
    upd = jnp.einsum('bnjhd,hj,bnjhe->nbhde', kc, k_decay, vc)

    def step(state, u_n):
        return chunk_decay[None, :, None, None] * state + u_n, state

    r0 = jnp.zeros((bsz, B_HEADS, B_QK_DIM, B_V_DIM), f32)
    _, r_prev = lax.scan(step, r0, upd)
    cross = jnp.einsum('bnihd,hi,nbhde->bnihe', qc, q_decay, r_prev)
    o = (inner + cross).reshape(bsz, seq, B_HEADS, B_V_DIM)
    o = _group_norm(o, B_GN_EPS).reshape(bsz, seq, B_V_WIDTH)
    return (jax.nn.silu(g.astype(f32)) * o).astype(g.dtype)


def _s5_mixer(u, lam_re, lam_im, log_dt, b_re, b_im, c_re, c_im, d_skip, w_glu, b_glu):
    bsz, seq, _ = u.shape
    f32 = jnp.float32
    uf = u.reshape(bsz, seq, C_GROUPS, C_GROUP).astype(f32)
    dt = jnp.exp(log_dt.astype(f32))[:, None]
    lr, li = lam_re.astype(f32), lam_im.astype(f32)
    mag = jnp.exp(lr * dt)
    ab_re, ab_im = mag * jnp.cos(li * dt), mag * jnp.sin(li * dt)
    den = lr * lr + li * li
    f_re = ((ab_re - 1.0) * lr + ab_im * li) / den
    f_im = (ab_im * lr - (ab_re - 1.0) * li) / den
    bre, bim = b_re.astype(f32), b_im.astype(f32)
    bb_re = f_re[..., None] * bre - f_im[..., None] * bim
    bb_im = f_re[..., None] * bim + f_im[..., None] * bre
    bu_re = jnp.einsum('gpc,bsgc->bsgp', bb_re, uf)
    bu_im = jnp.einsum('gpc,bsgc->bsgp', bb_im, uf)
    a_re = jnp.broadcast_to(ab_re, (1, seq, C_GROUPS, C_STATE))
    a_im = jnp.broadcast_to(ab_im, (1, seq, C_GROUPS, C_STATE))

    def combine(e1, e2):
        a1r, a1i, b1r, b1i = e1
        a2r, a2i, b2r, b2i = e2
        return (a2r * a1r - a2i * a1i, a2r * a1i + a2i * a1r,
                a2r * b1r - a2i * b1i + b2r, a2r * b1i + a2i * b1r + b2i)

    _, _, x_re, x_im = lax.associative_scan(combine, (a_re, a_im, bu_re, bu_im), axis=1)
    y = (jnp.einsum('gcp,bsgp->bsgc', c_re.astype(f32), x_re)
         - jnp.einsum('gcp,bsgp->bsgc', c_im.astype(f32), x_im)
         + d_skip.astype(f32) * uf)
    y = jax.nn.gelu(y.reshape(bsz, seq, C_WIDTH))
    y = y * jax.nn.sigmoid(y @ w_glu.astype(f32) + b_glu.astype(f32))
    return y.astype(u.dtype)


def setup_inputs(seed: int = 0) -> dict:
    key = jax.random.key(seed)
    ks = iter(jax.random.split(key, 40))
    f32 = jnp.float32
    nrm = lambda shape, scale: scale * jax.random.normal(next(ks), shape, f32)
    L, D = DEPTH, D_MODEL
    ratio = jnp.linspace(0.0, 1.0, A_WIDTH, dtype=f32)
    branch_scale = jnp.concatenate([
        jnp.full((A_WIDTH,), A_WIDTH ** -0.5, f32),
        jnp.full((B_V_WIDTH,), B_V_WIDTH ** -0.5, f32),
        jnp.full((C_WIDTH,), C_WIDTH ** -0.5, f32)])[:, None]
    lam_im0 = math.pi * jnp.arange(C_STATE, dtype=f32)
    return {
        'x': nrm((BATCH, SEQ, D), 1.0),
        'w_in': nrm((L, D, PROJ_TOTAL), D ** -0.5),
        'b_gate': nrm((L, 3 * D), 0.02),
        'a_shift': jax.random.uniform(next(ks), (L, A_PROJ), f32),
        'a_w0': -5.5 + 5.0 * ratio ** 0.85 + nrm((L, A_WIDTH), 0.1),
        'a_w2': nrm((L, A_DECAY_LORA, A_WIDTH), 0.1 * A_DECAY_LORA ** -0.5),
        'a_a0': nrm((L, A_WIDTH), 0.1),
        'a_a2': nrm((L, A_ICLR_LORA, A_WIDTH), 0.1 * A_ICLR_LORA ** -0.5),
        'a_g2': nrm((L, A_GATE_LORA, A_WIDTH), A_GATE_LORA ** -0.5),
        'a_kk': 0.85 + nrm((L, A_HEADS, A_HEAD_DIM), 0.02),
        'a_ka': 1.0 + nrm((L, A_HEADS, A_HEAD_DIM), 0.02),
        'a_rk': nrm((L, A_HEADS, A_HEAD_DIM), 0.1),
        'a_lnx_g': 1.0 + nrm((L, A_HEADS, A_HEAD_DIM), 0.02),
        'a_lnx_b': nrm((L, A_HEADS, A_HEAD_DIM), 0.02),
        'c_lam_re': -0.5 + nrm((L, C_GROUPS, C_STATE), 0.01),
        'c_lam_im': lam_im0 + nrm((L, C_GROUPS, C_STATE), 0.01),
        'c_log_dt': jax.random.uniform(next(ks), (L, C_GROUPS), f32,
                                       math.log(C_DT_MIN), math.log(C_DT_MAX)),
        'c_b_re': nrm((L, C_GROUPS, C_STATE, C_GROUP), (2.0 * C_GROUP) ** -0.5),
        'c_b_im': nrm((L, C_GROUPS, C_STATE, C_GROUP), (2.0 * C_GROUP) ** -0.5),
        'c_c_re': nrm((L, C_GROUPS, C_GROUP, C_STATE), C_STATE ** -0.5),
        'c_c_im': nrm((L, C_GROUPS, C_GROUP, C_STATE), C_STATE ** -0.5),
        'c_d': nrm((L, C_GROUPS, C_GROUP), 1.0),
        'c_w_glu': nrm((L, C_WIDTH, C_WIDTH), C_WIDTH ** -0.5),
        'c_b_glu': nrm((L, C_WIDTH), 0.02),
        'w_branch': nrm((L, BRANCH_ROWS, D), 1.0) * branch_scale,
        'w_out': nrm((L, D, D), DN_BETA * D ** -0.5),
        'ln1_g': 1.0 + nrm((L, D), 0.02),
        'ln1_b': nrm((L, D), 0.02),
        'w_ff1': nrm((L, D, D_FF), D ** -0.5),
        'w_ff2': nrm((L, D_FF, D), DN_BETA * D_FF ** -0.5),
        'ln2_g': 1.0 + nrm((L, D), 0.02),
        'ln2_b': nrm((L, D), 0.02),
    }


def reference(x, w_in, b_gate, a_shift, a_w0, a_w2, a_a0, a_a2, a_g2, a_kk, a_ka, a_rk,
              a_lnx_g, a_lnx_b, c_lam_re, c_lam_im, c_log_dt, c_b_re, c_b_im, c_c_re,
              c_c_im, c_d, c_w_glu, c_b_glu, w_branch, w_out, ln1_g, ln1_b, w_ff1, w_ff2,
              ln2_g, ln2_b):
    for l in range(DEPTH):
        proj = x @ w_in[l]
        z_a, q_b, k_b, v_b, g_b, u_c, gate_logits = jnp.split(proj, PROJ_SPLITS, axis=-1)
        z_a = _token_shift(z_a, a_shift[l])
        y_a = _rwkv7_mixer(z_a, a_w0[l], a_w2[l], a_a0[l], a_a2[l], a_g2[l],
                           a_kk[l], a_ka[l], a_rk[l], a_lnx_g[l], a_lnx_b[l])
        y_b = _retention_mixer(q_b, k_b, v_b, g_b)
        y_c = _s5_mixer(u_c, c_lam_re[l], c_lam_im[l], c_log_dt[l], c_b_re[l], c_b_im[l],
                        c_c_re[l], c_c_im[l], c_d[l], c_w_glu[l], c_b_glu[l])
        wb_a, wb_b, wb_c = jnp.split(w_branch[l], (A_WIDTH, A_WIDTH + B_V_WIDTH), axis=0)
        gate_a, gate_b, gate_c = jnp.split(jax.nn.sigmoid(gate_logits + b_gate[l]), 3, axis=-1)
        merged = gate_a * (y_a @ wb_a) + gate_b * (y_b @ wb_b) + gate_c * (y_c @ wb_c)
        x = _layer_norm(DN_ALPHA * x + merged @ w_out[l], ln1_g[l], ln1_b[l])
        ff = jnp.square(jax.nn.relu(x @ w_ff1[l])) @ w_ff2[l]
        x = _layer_norm(DN_ALPHA * x + ff, ln2_g[l], ln2_b[l])
    return x
```

```python
import functools
import math

import jax
import jax.numpy as jnp
from jax import lax
from jax.experimental import pallas as pl
from jax.experimental.pallas import tpu as pltpu

F32 = jnp.float32
BF16 = jnp.bfloat16

A_HEADS = 8
A_HEAD_DIM = 64
A_WIDTH = A_HEADS * A_HEAD_DIM
A_DECAY_LORA = 64
A_ICLR_LORA = 64
A_GATE_LORA = 128
A_PROJ = 3 * A_WIDTH + A_DECAY_LORA + A_ICLR_LORA + A_GATE_LORA
A_GN_EPS = 64e-5
A_CHUNK = 64

B_HEADS = 4
B_QK_DIM = 128
B_V_DIM = 256
B_QK_WIDTH = B_HEADS * B_QK_DIM
B_V_WIDTH = B_HEADS * B_V_DIM
B_CHUNK = 128
B_ROPE_BASE = 10000.0
B_GN_EPS = 1e-5

C_WIDTH = 512
C_GROUP = 16
C_GROUPS = C_WIDTH // C_GROUP
C_STATE = 64
C_LANES = C_GROUPS * C_STATE
C_BLOCKS = 4
C_ULANES = C_WIDTH // C_BLOCKS
C_SLANES = C_LANES // C_BLOCKS

LN_EPS = 1e-5

MIX_COLS = 2 * B_QK_WIDTH + 2 * B_V_WIDTH + C_WIDTH + A_PROJ

V7X_VMEM_LIMIT_BYTES = 56 * 1024 * 1024


def _cparams(*sem):
    return pltpu.CompilerParams(dimension_semantics=sem, vmem_limit_bytes=V7X_VMEM_LIMIT_BYTES)


def _full(shape):
    n = len(shape)
    return pl.BlockSpec(shape, lambda *_: (0,) * n)


def _dot(a, b):
    return jnp.dot(a.astype(BF16), b.astype(BF16), preferred_element_type=F32)


def _dot_nt(a, b):
    return lax.dot_general(a.astype(BF16), b.astype(BF16), (((1,), (1,)), ((), ())),
                           preferred_element_type=F32)


def _dot_tn(a, b):
    return lax.dot_general(a.astype(BF16), b.astype(BF16), (((0,), (0,)), ((), ())),
                           preferred_element_type=F32)


def _split2(x):
    hi = x.astype(BF16)
    lo = (x - hi.astype(F32)).astype(BF16)
    return hi, lo


def _split3(x):
    hi = x.astype(BF16)
    r1 = x - hi.astype(F32)
    mid = r1.astype(BF16)
    lo = (r1 - mid.astype(F32)).astype(BF16)
    return hi, mid, lo


def _sigmoid(x):
    return 1.0 / (1.0 + jnp.exp(-x))


def _layer_norm(y, g, b):
    mu = jnp.mean(y, axis=-1, keepdims=True)
    d = y - mu
    var = jnp.mean(d * d, axis=-1, keepdims=True)
    return d * lax.rsqrt(var + LN_EPS) * g + b


def _proj_kernel(x_ref, w_ref, o_ref):
    o_ref[...] = jnp.dot(x_ref[...].astype(BF16), w_ref[...], preferred_element_type=F32)


def _proj(x2d, w_bf16, tm, tn):
    m, k = x2d.shape
    n = w_bf16.shape[1]
    return pl.pallas_call(
        _proj_kernel,
        grid=(m // tm, n // tn),
        in_specs=[pl.BlockSpec((tm, k), lambda i, j: (i, 0)),
                  pl.BlockSpec((k, tn), lambda i, j: (0, j))],
        out_specs=pl.BlockSpec((tm, tn), lambda i, j: (i, j)),
        out_shape=jax.ShapeDtypeStruct((m, n), F32),
        compiler_params=_cparams("parallel", "arbitrary"),
        name="proj",
    )(x2d, w_bf16)


def _rwkv_kernel(z_ref, mu_ref, w0_ref, a0_ref, lora_ref, g2_ref, kkp_ref, kap_ref, rkp_ref,
                 lng_ref, lnb_ref, ones_ref, tri_ref, y_ref,
                 carry_s, state_s, r_s, k_s, v_s, kn_s, ia_s, lw_s, o_s, rk_s, gate_s):
    tc = z_ref.shape[0]
    n_chunks = tc // A_CHUNK
    w = A_WIDTH

    @pl.when(pl.program_id(1) == 0)
    def _():
        carry_s[...] = jnp.zeros_like(carry_s)
        state_s[...] = jnp.zeros_like(state_s)

    ones = ones_ref[...]

    def seg_sum(x):
        hi, lo = _split2(x)
        return (jnp.dot(hi, ones, preferred_element_type=F32)
                + jnp.dot(lo, ones, preferred_element_type=F32))

    z = z_ref[...]
    rolled = pltpu.roll(z, 1, 0)
    rowid = lax.broadcasted_iota(jnp.int32, z.shape, 0)
    prev = jnp.where(rowid == 0, jnp.broadcast_to(carry_s[0:1, :], z.shape), rolled)
    zs = z + mu_ref[...] * (prev - z)
    carry_s[0:1, :] = z[tc - 1:tc, :]

    lz = zs[:, 3 * w:3 * w + 128]
    lane = lax.broadcasted_iota(jnp.int32, lz.shape, 1)
    lin = jnp.where(lane < A_DECAY_LORA, jnp.tanh(lz), lz)
    wa = _dot(lin, lora_ref[...])
    lw_s[...] = (-math.exp(-0.5)) * _sigmoid(w0_ref[...] + wa[:, :w])
    ia = _sigmoid(a0_ref[...] + wa[:, w:])
    ia_s[...] = ia
    gate_s[...] = _dot(_sigmoid(zs[:, 3 * w + 128:3 * w + 256]), g2_ref[...])

    r = zs[:, :w]
    k = zs[:, w:2 * w]
    v = zs[:, 2 * w:3 * w]
    kk = k * kkp_ref[...]
    kn_s[...] = kk / jnp.maximum(jnp.sqrt(seg_sum(kk * kk)), 1e-12)
    kmod = k * (1.0 + (ia - 1.0) * kap_ref[...])
    r_s[...] = r
    k_s[...] = kmod
    v_s[...] = v
    rk_s[...] = seg_sum(r * kmod * rkp_ref[...])

    tri = tri_ref[...]
    rid = lax.broadcasted_iota(jnp.int32, (A_CHUNK, A_CHUNK), 0)
    cid = lax.broadcasted_iota(jnp.int32, (A_CHUNK, A_CHUNK), 1)
    strict = rid > cid
    incl = rid >= cid
    eye = (rid == cid).astype(F32)
    n = A_HEAD_DIM

    def chunk_body(c, carry):
        rows = pl.ds(pl.multiple_of(c * A_CHUNK, A_CHUNK), A_CHUNK)
        lw = lw_s[rows, :]
        h3 = _split3(lw)
        cum = (jnp.dot(tri, h3[0], preferred_element_type=F32)
               + jnp.dot(tri, h3[1], preferred_element_type=F32)
               + jnp.dot(tri, h3[2], preferred_element_type=F32))
        e_in = jnp.exp(cum)
        e_ex = jnp.exp(cum - lw)
        e_ng = jnp.exp(-cum)
        kn = kn_s[rows, :]
        rt = r_s[rows, :] * e_in
        at = -kn * e_ex
        bt = kn * ia_s[rows, :] * e_ng
        kt = k_s[rows, :] * e_ng
        vv = v_s[rows, :]
        gam = e_in[A_CHUNK - 1:A_CHUNK, :]
        outs = []
        for h in range(A_HEADS):
            hs = slice(h * n, (h + 1) * n)
            ah, rh, bh, kh, vh = at[:, hs], rt[:, hs], bt[:, hs], kt[:, hs], vv[:, hs]
            ar = jnp.concatenate([ah, rh], axis=0)
            gb = _dot_nt(ar, bh)
            gk = _dot_nt(ar, kh)
            l_ab = jnp.where(strict, gb[:A_CHUNK], 0.0)
            a_ak = jnp.where(strict, gk[:A_CHUNK], 0.0)
            a_qb = jnp.where(incl, gb[A_CHUNK:], 0.0)
            a_qk = jnp.where(incl, gk[A_CHUNK:], 0.0)
            tinv = eye + l_ab
            p = l_ab
            for _ in range(5):
                p = _dot(p, p)
                tinv = tinv + _dot(tinv, p)
            wmat = _dot(tinv, ah)
            uv = _dot(tinv, _dot(a_ak, vh))
            s0 = state_s[h]
            u = _dot_nt(wmat, s0) + uv
            o = _dot_nt(rh, s0) + _dot(a_qb, u) + _dot(a_qk, vh)
            state_s[h] = (s0 + _dot_tn(u, bh) + _dot_tn(vh, kh)) * gam[:, hs]
            outs.append(o)
        o_s[rows, :] = jnp.concatenate(outs, axis=1)
        return carry

    lax.fori_loop(0, n_chunks, chunk_body, 0)

    o = o_s[...]
    inv_n = 1.0 / n
    mean = seg_sum(o) * inv_n
    d = o - mean
    var = seg_sum(d * d) * inv_n
    on = d * lax.rsqrt(var + A_GN_EPS) * lng_ref[...] + lnb_ref[...]
    on = on + rk_s[...] * v_s[...]
    y_ref[...] = on * gate_s[...]


def _rwkv(zmix, bsz, seq, p, tc):
    zblk = MIX_COLS // A_PROJ - 1
    row = lambda a: a.reshape(1, -1).astype(F32)
    w = A_WIDTH
    lora = jnp.zeros((128, 2 * w), F32)
    lora = lora.at[:A_DECAY_LORA, :w].set(p["a_w2"]).at[A_DECAY_LORA:, w:].set(p["a_a2"]).astype(BF16)
    hid = jnp.arange(w) // A_HEAD_DIM
    ones = (hid[:, None] == hid[None, :]).astype(BF16)
    ti = jnp.arange(A_CHUNK)
    tri = (ti[:, None] >= ti[None, :]).astype(BF16)
    small = [row(p["a_shift"]), row(p["a_w0"]), row(p["a_a0"]), lora, p["a_g2"].astype(BF16),
             row(p["a_kk"]), row(p["a_ka"]), row(p["a_rk"]), row(p["a_lnx_g"]), row(p["a_lnx_b"]),
             ones, tri]
    stage = pltpu.VMEM((tc, w), F32)
    return pl.pallas_call(
        _rwkv_kernel,
        grid=(bsz, seq // tc),
        in_specs=[pl.BlockSpec((None, tc, A_PROJ), lambda b, i: (b, i, zblk))]
                 + [_full(a.shape) for a in small],
        out_specs=pl.BlockSpec((None, tc, w), lambda b, i: (b, i, 0)),
        out_shape=jax.ShapeDtypeStruct((bsz, seq, w), F32),
        scratch_shapes=[pltpu.VMEM((8, A_PROJ), F32),
                        pltpu.VMEM((A_HEADS, A_HEAD_DIM, A_HEAD_DIM), F32),
                        stage, stage, stage, stage, stage, stage, stage, stage, stage],
        compiler_params=_cparams("parallel", "arbitrary"),
        name="rwkv7",
    )(zmix, *small)


def _ret_kernel(q_ref, k_ref, v_ref, g_ref, cos_ref, sin_ref, dmask_ref, qd_ref, kd_ref, cd_ref,
                y_ref, state_s):
    tc = q_ref.shape[0]
    n_chunks = tc // B_CHUNK

    @pl.when(pl.program_id(2) == 0)
    def _():
        state_s[...] = jnp.zeros_like(state_s)

    cos2 = cos_ref[...]
    sin2 = sin_ref[...]
    half = B_QK_DIM // 2

    def rope(t):
        return t * cos2 + pltpu.roll(t, half, 1) * sin2

    q = rope(q_ref[...])
    k = rope(k_ref[...]) * (B_QK_DIM ** -0.5)
    dmask = dmask_ref[...]
    qd = qd_ref[...]
    kd = kd_ref[...]
    cd = cd_ref[...]
    for c in range(n_chunks):
        rows = slice(c * B_CHUNK, (c + 1) * B_CHUNK)
        qc, kc, vc = q[rows], k[rows], v_ref[rows, :]
        scores = _dot_nt(qc, kc) * dmask
        st = state_s[...]
        o = _dot(scores, vc) + _dot(qc * qd, st)
        state_s[...] = cd * st + _dot_tn(kc * kd, vc)
        mu = jnp.mean(o, axis=-1, keepdims=True)
        d = o - mu
        var = jnp.mean(d * d, axis=-1, keepdims=True)
        on = d * lax.rsqrt(var + B_GN_EPS)
        g = g_ref[rows, :]
        y_ref[rows, :] = g * _sigmoid(g) * on


def _retention(zmix, bsz, seq, tc):
    f32 = F32
    pos = jnp.arange(seq, dtype=f32)
    half = B_QK_DIM // 2
    inv_freq = B_ROPE_BASE ** (-jnp.arange(half, dtype=f32) / half)
    ang = pos[:, None] * inv_freq[None, :]
    cos, sin = jnp.cos(ang), jnp.sin(ang)
    cos2 = jnp.concatenate([cos, cos], axis=1)
    sin2 = jnp.concatenate([-sin, sin], axis=1)
    log_gamma = jnp.log(1.0 - 2.0 ** (-5.0 - jnp.arange(B_HEADS, dtype=f32)))
    idx = jnp.arange(B_CHUNK, dtype=f32)
    rel = idx[:, None] - idx[None, :]
    dmask = jnp.where(rel >= 0, jnp.exp(log_gamma[:, None, None] * jnp.maximum(rel, 0.0)), 0.0)
    qd = jnp.broadcast_to(jnp.exp(log_gamma[:, None] * (idx + 1.0))[:, :, None],
                          (B_HEADS, B_CHUNK, B_QK_DIM))
    kd = jnp.broadcast_to(jnp.exp(log_gamma[:, None] * (B_CHUNK - 1.0 - idx))[:, :, None],
                          (B_HEADS, B_CHUNK, B_QK_DIM))
    cd = jnp.broadcast_to(jnp.exp(log_gamma * B_CHUNK)[:, None, None], (B_HEADS, B_QK_DIM, B_V_DIM))
    nq = B_QK_WIDTH // B_QK_DIM
    nv = (2 * B_QK_WIDTH) // B_V_DIM
    ng = nv + B_V_WIDTH // B_V_DIM
    head_tab = lambda a, b: pl.BlockSpec((None, a, b), lambda bb, h, i: (h, 0, 0))
    return pl.pallas_call(
        _ret_kernel,
        grid=(bsz, B_HEADS, seq // tc),
        in_specs=[pl.BlockSpec((None, tc, B_QK_DIM), lambda b, h, i: (b, i, h)),
                  pl.BlockSpec((None, tc, B_QK_DIM), lambda b, h, i: (b, i, nq + h)),
                  pl.BlockSpec((None, tc, B_V_DIM), lambda b, h, i: (b, i, nv + h)),
                  pl.BlockSpec((None, tc, B_V_DIM), lambda b, h, i: (b, i, ng + h)),
                  pl.BlockSpec((tc, B_QK_DIM), lambda b, h, i: (i, 0)),
                  pl.BlockSpec((tc, B_QK_DIM), lambda b, h, i: (i, 0)),
                  head_tab(B_CHUNK, B_CHUNK), head_tab(B_CHUNK, B_QK_DIM),
                  head_tab(B_CHUNK, B_QK_DIM), head_tab(B_QK_DIM, B_V_DIM)],
        out_specs=pl.BlockSpec((None, tc, B_V_DIM), lambda b, h, i: (b, i, h)),
        out_shape=jax.ShapeDtypeStruct((bsz, seq, B_V_WIDTH), F32),
        scratch_shapes=[pltpu.VMEM((B_QK_DIM, B_V_DIM), F32)],
        compiler_params=_cparams("parallel", "parallel", "arbitrary"),
        name="retention",
    )(zmix, zmix, zmix, zmix, cos2, sin2, dmask, qd, kd, cd)


def _s5_kernel(u_ref, bbr_ref, bbi_ref, ar_ref, ai_ref, cr_ref, ci_ref, d_ref, wg_ref, bg_ref,
               y_ref, xr_s, xi_s, sr_s, si_s):
    bsz, tt, _ = u_ref.shape

    @pl.when(pl.program_id(0) == 0)
    def _():
        sr_s[...] = jnp.zeros_like(sr_s)
        si_s[...] = jnp.zeros_like(si_s)

    lb = C_SLANES // 128

    for b in range(bsz):
        ub = u_ref[b].astype(BF16)
        rows = slice(b * tt, (b + 1) * tt)
        for m in range(C_BLOCKS):
            um = ub[:, m * C_ULANES:(m + 1) * C_ULANES]
            br = jnp.dot(um, bbr_ref[m], preferred_element_type=F32)
            bi = jnp.dot(um, bbi_ref[m], preferred_element_type=F32)
            for j in range(lb):
                xr_s[m * lb + j, rows, :] = br[:, j * 128:(j + 1) * 128]
                xi_s[m * lb + j, rows, :] = bi[:, j * 128:(j + 1) * 128]

    for m in range(C_BLOCKS):
        slabs = [m * lb + j for j in range(lb)]
        ar = [jnp.broadcast_to(ar_ref[:, s * 128:(s + 1) * 128], (bsz, 128)) for s in slabs]
        ai = [jnp.broadcast_to(ai_ref[:, s * 128:(s + 1) * 128], (bsz, 128)) for s in slabs]

        def step(t, carry):
            rows = pl.ds(t, bsz, stride=tt)
            new = []
            for j, s in enumerate(slabs):
                xr, xi = carry[2 * j], carry[2 * j + 1]
                nr = ar[j] * xr - ai[j] * xi + xr_s[s, rows, :]
                ni = ar[j] * xi + ai[j] * xr + xi_s[s, rows, :]
                xr_s[s, rows, :] = nr
                xi_s[s, rows, :] = ni
                new += [nr, ni]
            return tuple(new)

        init = []
        for s in slabs:
            init += [sr_s[:, s * 128:(s + 1) * 128], si_s[:, s * 128:(s + 1) * 128]]
        fin = lax.fori_loop(0, tt, step, tuple(init))
        for j, s in enumerate(slabs):
            sr_s[:, s * 128:(s + 1) * 128] = fin[2 * j]
            si_s[:, s * 128:(s + 1) * 128] = fin[2 * j + 1]

    for b in range(bsz):
        rows = slice(b * tt, (b + 1) * tt)
        parts = []
        for m in range(C_BLOCKS):
            xr = jnp.concatenate([xr_s[m * lb + j, rows, :] for j in range(lb)], axis=1)
            xi = jnp.concatenate([xi_s[m * lb + j, rows, :] for j in range(lb)], axis=1)
            parts.append(_dot(xr, cr_ref[m]) - _dot(xi, ci_ref[m]))
        y = jnp.concatenate(parts, axis=1) + d_ref[...] * u_ref[b]
        y = jax.nn.gelu(y)
        y_ref[b] = y * _sigmoid(_dot(y, wg_ref[...]) + bg_ref[...])


def _s5(zmix, bsz, seq, p, tt):
    f32 = F32
    dt = jnp.exp(p["c_log_dt"].astype(f32))[:, None]
    lr, li = p["c_lam_re"].astype(f32), p["c_lam_im"].astype(f32)
    mag = jnp.exp(lr * dt)
    ab_re, ab_im = mag * jnp.cos(li * dt), mag * jnp.sin(li * dt)
    den = lr * lr + li * li
    f_re = ((ab_re - 1.0) * lr + ab_im * li) / den
    f_im = (ab_im * lr - (ab_re - 1.0) * li) / den
    bre, bim = p["c_b_re"].astype(f32), p["c_b_im"].astype(f32)
    bb_re = f_re[..., None] * bre - f_im[..., None] * bim
    bb_im = f_re[..., None] * bim + f_im[..., None] * bre
    eye = jnp.eye(C_GROUPS, dtype=f32)

    def in_blocks(bb):
        full = jnp.einsum("gpc,gh->gchp", bb, eye).reshape(C_WIDTH, C_LANES)
        return jnp.stack([full[m * C_ULANES:(m + 1) * C_ULANES, m * C_SLANES:(m + 1) * C_SLANES]
                          for m in range(C_BLOCKS)]).astype(BF16)

    def out_blocks(cc):
        full = jnp.einsum("gcp,gh->gphc", cc.astype(f32), eye).reshape(C_LANES, C_WIDTH)
        return jnp.stack([full[m * C_SLANES:(m + 1) * C_SLANES, m * C_ULANES:(m + 1) * C_ULANES]
                          for m in range(C_BLOCKS)]).astype(BF16)

    consts = [in_blocks(bb_re), in_blocks(bb_im), ab_re.reshape(1, C_LANES), ab_im.reshape(1, C_LANES),
              out_blocks(p["c_c_re"]), out_blocks(p["c_c_im"]), p["c_d"].reshape(1, C_WIDTH).astype(f32),
              p["c_w_glu"].astype(BF16), p["c_b_glu"].reshape(1, C_WIDTH).astype(f32)]
    ublk = (2 * B_QK_WIDTH + 2 * B_V_WIDTH) // C_WIDTH
    return pl.pallas_call(
        _s5_kernel,
        grid=(seq // tt,),
        in_specs=[pl.BlockSpec((bsz, tt, C_WIDTH), lambda i: (0, i, ublk))]
                 + [_full(a.shape) for a in consts],
        out_specs=pl.BlockSpec((bsz, tt, C_WIDTH), lambda i: (0, i, 0)),
        out_shape=jax.ShapeDtypeStruct((bsz, seq, C_WIDTH), F32),
        scratch_shapes=[pltpu.VMEM((C_LANES // 128, bsz * tt, 128), F32),
                        pltpu.VMEM((C_LANES // 128, bsz * tt, 128), F32),
                        pltpu.VMEM((bsz, C_LANES), F32), pltpu.VMEM((bsz, C_LANES), F32)],
        compiler_params=_cparams("arbitrary"),
        name="s5",
    )(zmix, *consts)


def _merge_kernel(alpha, x_ref, ya_ref, yb_ref, yc_ref, wgate_ref, bgate_ref, wba_ref, wbb_ref, wbc_ref,
                  wout_ref, g_ref, b_ref, o_ref):
    d = x_ref.shape[1]
    x = x_ref[...]
    gates = _sigmoid(_dot(x, wgate_ref[...]) + bgate_ref[...])
    merged = (gates[:, :d] * _dot(ya_ref[...], wba_ref[...])
              + gates[:, d:2 * d] * _dot(yb_ref[...], wbb_ref[...])
              + gates[:, 2 * d:] * _dot(yc_ref[...], wbc_ref[...]))
    o_ref[...] = _layer_norm(alpha * x + _dot(merged, wout_ref[...]), g_ref[...], b_ref[...])


def _merge(x2d, ya, yb, yc, wgate, bgate, wb, wout, g, b, alpha, tm):
    m, d = x2d.shape
    wba, wbb, wbc = wb[:A_WIDTH], wb[A_WIDTH:A_WIDTH + B_V_WIDTH], wb[A_WIDTH + B_V_WIDTH:]
    consts = [wgate, bgate, wba, wbb, wbc, wout, g, b]
    tile = lambda n: pl.BlockSpec((tm, n), lambda i: (i, 0))
    return pl.pallas_call(
        functools.partial(_merge_kernel, alpha),
        grid=(m // tm,),
        in_specs=[tile(d), tile(A_WIDTH), tile(B_V_WIDTH), tile(C_WIDTH)] + [_full(a.shape) for a in consts],
        out_specs=tile(d),
        out_shape=jax.ShapeDtypeStruct((m, d), F32),
        compiler_params=_cparams("parallel"),
        name="merge",
    )(x2d, ya, yb, yc, *consts)


def _ffn_kernel(alpha, x_ref, w1_ref, w2_ref, g_ref, b_ref, o_ref, acc_s):
    j = pl.program_id(1)
    x = x_ref[...]
    h = jnp.maximum(_dot(x, w1_ref[...]), 0.0)
    part = _dot(h * h, w2_ref[...])

    @pl.when(j == 0)
    def _():
        acc_s[...] = part

    @pl.when(j > 0)
    def _():
        acc_s[...] += part

    @pl.when(j == pl.num_programs(1) - 1)
    def _():
        o_ref[...] = _layer_norm(alpha * x + acc_s[...], g_ref[...], b_ref[...])


def _ffn(x2d, w1, w2, g, b, alpha, tm, tf):
    m, d = x2d.shape
    dff = w1.shape[1]
    return pl.pallas_call(
        functools.partial(_ffn_kernel, alpha),
        grid=(m // tm, dff // tf),
        in_specs=[pl.BlockSpec((tm, d), lambda i, j: (i, 0)),
                  pl.BlockSpec((d, tf), lambda i, j: (0, j)),
                  pl.BlockSpec((tf, d), lambda i, j: (j, 0)),
                  _full(g.shape), _full(b.shape)],
        out_specs=pl.BlockSpec((tm, d), lambda i, j: (i, 0)),
        out_shape=jax.ShapeDtypeStruct((m, d), F32),
        scratch_shapes=[pltpu.VMEM((tm, d), F32)],
        compiler_params=_cparams("parallel", "arbitrary"),
        name="ffn",
    )(x2d, w1, w2, g, b)


def _tile(n, want):
    t = min(n, want)
    assert n % t == 0, (n, want)
    return t


def kernel(x, w_in, b_gate, a_shift, a_w0, a_w2, a_a0, a_a2, a_g2, a_kk, a_ka, a_rk, a_lnx_g, a_lnx_b, c_lam_re, c_lam_im, c_log_dt, c_b_re, c_b_im, c_c_re, c_c_im, c_d, c_w_glu, c_b_glu, w_branch, w_out, ln1_g, ln1_b, w_ff1, w_ff2, ln2_g, ln2_b):
    bsz, seq, d = x.shape
    depth = w_in.shape[0]
    alpha = (2.0 * depth) ** 0.25
    tokens = bsz * seq
    per_layer = dict(a_shift=a_shift, a_w0=a_w0, a_w2=a_w2, a_a0=a_a0, a_a2=a_a2, a_g2=a_g2, a_kk=a_kk,
                     a_ka=a_ka, a_rk=a_rk, a_lnx_g=a_lnx_g, a_lnx_b=a_lnx_b, c_lam_re=c_lam_re,
                     c_lam_im=c_lam_im, c_log_dt=c_log_dt, c_b_re=c_b_re, c_b_im=c_b_im, c_c_re=c_c_re,
                     c_c_im=c_c_im, c_d=c_d, c_w_glu=c_w_glu, c_b_glu=c_b_glu)
    mix_lo, mix_hi = A_PROJ, A_PROJ + MIX_COLS - A_PROJ
    xt = x.reshape(tokens, d)
    for l in range(depth):
        p = {k: v[l] for k, v in per_layer.items()}
        wl = w_in[l]
        w_mix = jnp.concatenate([wl[:, mix_lo:mix_hi], wl[:, :A_PROJ]], axis=1).astype(BF16)
        w_gate = wl[:, mix_hi:].astype(BF16)
        zmix = _proj(xt, w_mix, _tile(tokens, 512), A_PROJ).reshape(bsz, seq, MIX_COLS)
        ya = _rwkv(zmix, bsz, seq, p, _tile(seq, 256))
        yb = _retention(zmix, bsz, seq, _tile(seq, 512))
        yc = _s5(zmix, bsz, seq, p, _tile(seq, 64))
        x1 = _merge(xt, ya.reshape(tokens, -1), yb.reshape(tokens, -1), yc.reshape(tokens, -1),
                    w_gate, b_gate[l].reshape(1, -1), w_branch[l].astype(BF16), w_out[l].astype(BF16),
                    ln1_g[l].reshape(1, -1), ln1_b[l].reshape(1, -1), alpha, _tile(tokens, 256))
        xt = _ffn(x1, w_ff1[l].astype(BF16), w_ff2[l].astype(BF16), ln2_g[l].reshape(1, -1),
                  ln2_b[l].reshape(1, -1), alpha, _tile(tokens, 1024), 1024)
    return xt.reshape(bsz, seq, d)
```

```python
import functools
import math

import jax
import jax.numpy as jnp
from jax import lax
from jax.experimental import pallas as pl
from jax.experimental.pallas import tpu as pltpu

F32 = jnp.float32
BF16 = jnp.bfloat16

A_HEADS = 8
A_HEAD_DIM = 64
A_WIDTH = A_HEADS * A_HEAD_DIM
A_DECAY_LORA = 64
A_ICLR_LORA = 64
A_GATE_LORA = 128
A_PROJ = 3 * A_WIDTH + A_DECAY_LORA + A_ICLR_LORA + A_GATE_LORA
A_GN_EPS = 64e-5
A_CHUNK = 64

B_HEADS = 4
B_QK_DIM = 128
B_V_DIM = 256
B_QK_WIDTH = B_HEADS * B_QK_DIM
B_V_WIDTH = B_HEADS * B_V_DIM
B_CHUNK = 128
B_ROPE_BASE = 10000.0
B_GN_EPS = 1e-5

C_WIDTH = 512
C_GROUP = 16
C_GROUPS = C_WIDTH // C_GROUP
C_STATE = 64
C_LANES = C_GROUPS * C_STATE
C_BLOCKS = 4
C_ULANES = C_WIDTH // C_BLOCKS
C_SLANES = C_LANES // C_BLOCKS

LN_EPS = 1e-5

MIX_COLS = A_PROJ + 2 * B_QK_WIDTH + 2 * B_V_WIDTH
Q_BLK = A_PROJ // B_QK_DIM
K_BLK = Q_BLK + B_HEADS
V_BLK = (A_PROJ + 2 * B_QK_WIDTH) // B_V_DIM
G_BLK = V_BLK + B_HEADS
assert A_PROJ % B_QK_DIM == 0 and (A_PROJ + 2 * B_QK_WIDTH) % B_V_DIM == 0

V7X_VMEM_LIMIT_BYTES = 56 * 1024 * 1024


def _cparams(*sem):
    return pltpu.CompilerParams(dimension_semantics=sem, vmem_limit_bytes=V7X_VMEM_LIMIT_BYTES)


def _full(shape):
    n = len(shape)
    return pl.BlockSpec(shape, lambda *_: (0,) * n)


def _dot(a, b):
    return jnp.dot(a.astype(BF16), b.astype(BF16), preferred_element_type=F32)


def _dot_nt(a, b):
    return lax.dot_general(a.astype(BF16), b.astype(BF16), (((1,), (1,)), ((), ())),
                           preferred_element_type=F32)


def _dot_tn(a, b):
    return lax.dot_general(a.astype(BF16), b.astype(BF16), (((0,), (0,)), ((), ())),
                           preferred_element_type=F32)


def _split2(x):
    hi = x.astype(BF16)
    lo = (x - hi.astype(F32)).astype(BF16)
    return hi, lo


def _split3(x):
    hi = x.astype(BF16)
    r1 = x - hi.astype(F32)
    mid = r1.astype(BF16)
    lo = (r1 - mid.astype(F32)).astype(BF16)
    return hi, mid, lo


def _sigmoid(x):
    return 1.0 / (1.0 + jnp.exp(-x))


def _layer_norm(y, g, b):
    mu = jnp.mean(y, axis=-1, keepdims=True)
    d = y - mu
    var = jnp.mean(d * d, axis=-1, keepdims=True)
    return d * lax.rsqrt(var + LN_EPS) * g + b


def _proj_kernel(x_ref, w_ref, o_ref):
    o_ref[...] = jnp.dot(x_ref[...].astype(BF16), w_ref[...], preferred_element_type=F32)


def _proj(x2d, w_bf16, tm, tn):
    m, k = x2d.shape
    n = w_bf16.shape[1]
    return pl.pallas_call(
        _proj_kernel,
        grid=(m // tm, n // tn),
        in_specs=[pl.BlockSpec((tm, k), lambda i, j: (i, 0)),
                  pl.BlockSpec((k, tn), lambda i, j: (0, j))],
        out_specs=pl.BlockSpec((tm, tn), lambda i, j: (i, j)),
        out_shape=jax.ShapeDtypeStruct((m, n), F32),
        compiler_params=_cparams("parallel", "arbitrary"),
        name="proj",
    )(x2d, w_bf16)


def _proj_time_major(x2d, w_bf16, bsz, seq, tm):
    k = x2d.shape[1]
    n = w_bf16.shape[1]
    per_seq = seq // tm
    return pl.pallas_call(
        _proj_kernel,
        grid=(bsz, per_seq),
        in_specs=[pl.BlockSpec((tm, k), lambda b, i: (b * per_seq + i, 0)),
                  pl.BlockSpec((k, n), lambda b, i: (0, 0))],
        out_specs=pl.BlockSpec((tm, n), lambda b, i: (i, b)),
        out_shape=jax.ShapeDtypeStruct((seq, bsz * n), F32),
        compiler_params=_cparams("parallel", "parallel"),
        name="proj_u",
    )(x2d, w_bf16)


def _rwkv_kernel(z_ref, mu_ref, w0_ref, a0_ref, lora_ref, g2_ref, kkp_ref, kap_ref, rkp_ref,
                 lng_ref, lnb_ref, ones_ref, tri_ref, y_ref,
                 carry_s, state_s, r_s, k_s, v_s, kn_s, ia_s, lw_s, o_s, rk_s, gate_s,
                 gam_s, w_s, uv_s, aqb_s, ov_s, kv_s):
    tc = z_ref.shape[0]
    n_chunks = tc // A_CHUNK
    w = A_WIDTH

    @pl.when(pl.program_id(1) == 0)
    def _():
        carry_s[...] = jnp.zeros_like(carry_s)
        state_s[...] = jnp.zeros_like(state_s)

    ones = ones_ref[...]

    def seg_sum(x):
        hi, lo = _split2(x)
        return (jnp.dot(hi, ones, preferred_element_type=F32)
                + jnp.dot(lo, ones, preferred_element_type=F32))

    z = z_ref[...]
    rolled = pltpu.roll(z, 1, 0)
    rowid = lax.broadcasted_iota(jnp.int32, z.shape, 0)
    prev = jnp.where(rowid == 0, jnp.broadcast_to(carry_s[0:1, :], z.shape), rolled)
    zs = z + mu_ref[...] * (prev - z)
    carry_s[0:1, :] = z[tc - 1:tc, :]

    lz = zs[:, 3 * w:3 * w + 128]
    lane = lax.broadcasted_iota(jnp.int32, lz.shape, 1)
    lin = jnp.where(lane < A_DECAY_LORA, jnp.tanh(lz), lz)
    wa = _dot(lin, lora_ref[...])
    lw_s[...] = (-math.exp(-0.5)) * _sigmoid(w0_ref[...] + wa[:, :w])
    ia = _sigmoid(a0_ref[...] + wa[:, w:])
    ia_s[...] = ia
    gate_s[...] = _dot(_sigmoid(zs[:, 3 * w + 128:3 * w + 256]), g2_ref[...])

    r = zs[:, :w]
    k = zs[:, w:2 * w]
    v = zs[:, 2 * w:3 * w]
    kk = k * kkp_ref[...]
    kn_s[...] = kk / jnp.maximum(jnp.sqrt(seg_sum(kk * kk)), 1e-12)
    kmod = k * (1.0 + (ia - 1.0) * kap_ref[...])
    r_s[...] = r
    k_s[...] = kmod
    v_s[...] = v
    rk_s[...] = seg_sum(r * kmod * rkp_ref[...])

    tri = tri_ref[...]
    rid = lax.broadcasted_iota(jnp.int32, (A_CHUNK, A_CHUNK), 0)
    cid = lax.broadcasted_iota(jnp.int32, (A_CHUNK, A_CHUNK), 1)
    strict = rid > cid
    incl = rid >= cid
    eye = (rid == cid).astype(F32)
    n = A_HEAD_DIM

    heads = range(A_HEADS)
    hsl = [slice(h * n, (h + 1) * n) for h in heads]


    def state_free_part(c, carry):
        rows = pl.ds(pl.multiple_of(c * A_CHUNK, A_CHUNK), A_CHUNK)
        lw = lw_s[rows, :]
        h3 = _split3(lw)
        cum = (jnp.dot(tri, h3[0], preferred_element_type=F32)
               + jnp.dot(tri, h3[1], preferred_element_type=F32)
               + jnp.dot(tri, h3[2], preferred_element_type=F32))
        e_in = jnp.exp(cum)
        e_ex = jnp.exp(cum - lw)
        e_ng = jnp.exp(-cum)
        kn = kn_s[rows, :]
        rt = r_s[rows, :] * e_in
        at = -kn * e_ex
        bt = kn * ia_s[rows, :] * e_ng
        kt = k_s[rows, :] * e_ng
        vv = v_s[rows, :]
        r_s[rows, :] = rt
        kn_s[rows, :] = bt
        gam_s[pl.ds(c, 1), :] = e_in[A_CHUNK - 1:A_CHUNK, :]
        ah = [at[:, s] for s in hsl]
        bh = [bt[:, s] for s in hsl]
        kh = [kt[:, s] for s in hsl]
        vh = [vv[:, s] for s in hsl]
        ar = [jnp.concatenate([ah[h], rt[:, hsl[h]]], axis=0) for h in heads]
        gb = [_dot_nt(ar[h], bh[h]) for h in heads]
        gk = [_dot_nt(ar[h], kh[h]) for h in heads]
        l_ab = [jnp.where(strict, gb[h][:A_CHUNK], 0.0) for h in heads]
        a_ak = [jnp.where(strict, gk[h][:A_CHUNK], 0.0) for h in heads]
        for h in heads:
            aqb_s[c * A_HEADS + h] = jnp.where(incl, gb[h][A_CHUNK:], 0.0)
        akv = [_dot(a_ak[h], vh[h]) for h in heads]
        for h in heads:
            ov_s[c * A_HEADS + h] = _dot(jnp.where(incl, gk[h][A_CHUNK:], 0.0), vh[h])
        for h in heads:
            kv_s[c * A_HEADS + h] = _dot_tn(vh[h], kh[h])
        tinv = [eye + l_ab[h] for h in heads]
        p = l_ab
        for _ in range(5):
            p = [_dot(p[h], p[h]) for h in heads]
            tinv = [tinv[h] + _dot(tinv[h], p[h]) for h in heads]
        for h in heads:
            w_s[c * A_HEADS + h] = _dot(tinv[h], ah[h])
        for h in heads:
            uv_s[c * A_HEADS + h] = _dot(tinv[h], akv[h])
        return carry

    def state_part(c, carry):
        rows = pl.ds(pl.multiple_of(c * A_CHUNK, A_CHUNK), A_CHUNK)
        rt = r_s[rows, :]
        bt = kn_s[rows, :]
        gam = gam_s[pl.ds(c, 1), :]
        s0 = [state_s[h] for h in heads]
        u = [_dot_nt(w_s[c * A_HEADS + h], s0[h]) + uv_s[c * A_HEADS + h] for h in heads]
        su = [_dot_tn(u[h], bt[:, hsl[h]]) for h in heads]
        for h in heads:
            state_s[h] = (s0[h] + su[h] + kv_s[c * A_HEADS + h]) * gam[:, hsl[h]]
        o = [_dot_nt(rt[:, hsl[h]], s0[h]) for h in heads]
        o = [o[h] + _dot(aqb_s[c * A_HEADS + h], u[h]) + ov_s[c * A_HEADS + h] for h in heads]
        o_s[rows, :] = jnp.concatenate(o, axis=1)
        return carry

    lax.fori_loop(0, n_chunks, state_free_part, 0)
    lax.fori_loop(0, n_chunks, state_part, 0)

    o = o_s[...]
    inv_n = 1.0 / n
    mean = seg_sum(o) * inv_n
    d = o - mean
    var = seg_sum(d * d) * inv_n
    on = d * lax.rsqrt(var + A_GN_EPS) * lng_ref[...] + lnb_ref[...]
    on = on + rk_s[...] * v_s[...]
    y_ref[...] = on * gate_s[...]


def _rwkv(zmix, bsz, seq, p, tc):
    zblk = 0
    row = lambda a: a.reshape(1, -1).astype(F32)
    w = A_WIDTH
    lora = jnp.zeros((128, 2 * w), F32)
    lora = lora.at[:A_DECAY_LORA, :w].set(p["a_w2"]).at[A_DECAY_LORA:, w:].set(p["a_a2"]).astype(BF16)
    hid = jnp.arange(w) // A_HEAD_DIM
    ones = (hid[:, None] == hid[None, :]).astype(BF16)
    ti = jnp.arange(A_CHUNK)
    tri = (ti[:, None] >= ti[None, :]).astype(BF16)
    small = [row(p["a_shift"]), row(p["a_w0"]), row(p["a_a0"]), lora, p["a_g2"].astype(BF16),
             row(p["a_kk"]), row(p["a_ka"]), row(p["a_rk"]), row(p["a_lnx_g"]), row(p["a_lnx_b"]),
             ones, tri]
    stage = pltpu.VMEM((tc, w), F32)
    n_chunks = tc // A_CHUNK
    per_head = pltpu.VMEM((n_chunks * A_HEADS, A_CHUNK, A_HEAD_DIM), F32)
    return pl.pallas_call(
        _rwkv_kernel,
        grid=(bsz, seq // tc),
        in_specs=[pl.BlockSpec((None, tc, A_PROJ), lambda b, i: (b, i, zblk))]
                 + [_full(a.shape) for a in small],
        out_specs=pl.BlockSpec((None, tc, w), lambda b, i: (b, i, 0)),
        out_shape=jax.ShapeDtypeStruct((bsz, seq, w), F32),
        scratch_shapes=[pltpu.VMEM((8, A_PROJ), F32),
                        pltpu.VMEM((A_HEADS, A_HEAD_DIM, A_HEAD_DIM), F32),
                        stage, stage, stage, stage, stage, stage, stage, stage, stage,
                        pltpu.VMEM((max(8, n_chunks), w), F32),
                        per_head, per_head, per_head, per_head, per_head],
        compiler_params=_cparams("parallel", "arbitrary"),
        name="rwkv7",
    )(zmix, *small)


def _ret_kernel(q_ref, k_ref, v_ref, g_ref, cos_ref, sin_ref, dmask_ref, qd_ref, kd_ref, cd_ref,
                y_ref, state_s):
    tc = q_ref.shape[0]
    n_chunks = tc // B_CHUNK

    @pl.when(pl.program_id(2) == 0)
    def _():
        state_s[...] = jnp.zeros_like(state_s)

    cos2 = cos_ref[...]
    sin2 = sin_ref[...]
    half = B_QK_DIM // 2

    def rope(t):
        return t * cos2 + pltpu.roll(t, half, 1) * sin2

    q = rope(q_ref[...])
    k = rope(k_ref[...]) * (B_QK_DIM ** -0.5)
    dmask = dmask_ref[...]
    qd = qd_ref[...]
    kd = kd_ref[...]
    cd = cd_ref[...]
    for c in range(n_chunks):
        rows = slice(c * B_CHUNK, (c + 1) * B_CHUNK)
        qc, kc, vc = q[rows], k[rows], v_ref[rows, :]
        scores = _dot_nt(qc, kc) * dmask
        st = state_s[...]
        o = _dot(scores, vc) + _dot(qc * qd, st)
        state_s[...] = cd * st + _dot_tn(kc * kd, vc)
        mu = jnp.mean(o, axis=-1, keepdims=True)
        d = o - mu
        var = jnp.mean(d * d, axis=-1, keepdims=True)
        on = d * lax.rsqrt(var + B_GN_EPS)
        g = g_ref[rows, :]
        y_ref[rows, :] = g * _sigmoid(g) * on


def _retention(zmix, bsz, seq, tc):
    f32 = F32
    pos = jnp.arange(seq, dtype=f32)
    half = B_QK_DIM // 2
    inv_freq = B_ROPE_BASE ** (-jnp.arange(half, dtype=f32) / half)
    ang = pos[:, None] * inv_freq[None, :]
    cos, sin = jnp.cos(ang), jnp.sin(ang)
    cos2 = jnp.concatenate([cos, cos], axis=1)
    sin2 = jnp.concatenate([-sin, sin], axis=1)
    log_gamma = jnp.log(1.0 - 2.0 ** (-5.0 - jnp.arange(B_HEADS, dtype=f32)))
    idx = jnp.arange(B_CHUNK, dtype=f32)
    rel = idx[:, None] - idx[None, :]
    dmask = jnp.where(rel >= 0, jnp.exp(log_gamma[:, None, None] * jnp.maximum(rel, 0.0)), 0.0)
    qd = jnp.broadcast_to(jnp.exp(log_gamma[:, None] * (idx + 1.0))[:, :, None],
                          (B_HEADS, B_CHUNK, B_QK_DIM))
    kd = jnp.broadcast_to(jnp.exp(log_gamma[:, None] * (B_CHUNK - 1.0 - idx))[:, :, None],
                          (B_HEADS, B_CHUNK, B_QK_DIM))
    cd = jnp.broadcast_to(jnp.exp(log_gamma * B_CHUNK)[:, None, None], (B_HEADS, B_QK_DIM, B_V_DIM))
    head_tab = lambda a, b: pl.BlockSpec((None, a, b), lambda bb, h, i: (h, 0, 0))
    return pl.pallas_call(
        _ret_kernel,
        grid=(bsz, B_HEADS, seq // tc),
        in_specs=[pl.BlockSpec((None, tc, B_QK_DIM), lambda b, h, i: (b, i, Q_BLK + h)),
                  pl.BlockSpec((None, tc, B_QK_DIM), lambda b, h, i: (b, i, K_BLK + h)),
                  pl.BlockSpec((None, tc, B_V_DIM), lambda b, h, i: (b, i, V_BLK + h)),
                  pl.BlockSpec((None, tc, B_V_DIM), lambda b, h, i: (b, i, G_BLK + h)),
                  pl.BlockSpec((tc, B_QK_DIM), lambda b, h, i: (i, 0)),
                  pl.BlockSpec((tc, B_QK_DIM), lambda b, h, i: (i, 0)),
                  head_tab(B_CHUNK, B_CHUNK), head_tab(B_CHUNK, B_QK_DIM),
                  head_tab(B_CHUNK, B_QK_DIM), head_tab(B_QK_DIM, B_V_DIM)],
        out_specs=pl.BlockSpec((None, tc, B_V_DIM), lambda b, h, i: (b, i, h)),
        out_shape=jax.ShapeDtypeStruct((bsz, seq, B_V_WIDTH), F32),
        scratch_shapes=[pltpu.VMEM((B_QK_DIM, B_V_DIM), F32)],
        compiler_params=_cparams("parallel", "parallel", "arbitrary"),
        name="retention",
    )(zmix, zmix, zmix, zmix, cos2, sin2, dmask, qd, kd, cd)


def _s5_kernel(bsz, u_ref, bbr_ref, bbi_ref, ar_ref, ai_ref, cr_ref, ci_ref, d_ref, wg_ref, bg_ref,
               y_ref, xr_s, xi_s, sr_s, si_s):
    tt = u_ref.shape[0] // bsz

    @pl.when(pl.program_id(0) == 0)
    def _():
        sr_s[...] = jnp.zeros_like(sr_s)
        si_s[...] = jnp.zeros_like(si_s)

    u = u_ref[...]
    ub = u.astype(BF16)
    blk = [slice(m * C_SLANES, (m + 1) * C_SLANES) for m in range(C_BLOCKS)]

    for m in range(C_BLOCKS):
        um = ub[:, m * C_ULANES:(m + 1) * C_ULANES]
        xr_s[:, blk[m]] = jnp.dot(um, bbr_ref[m], preferred_element_type=F32)
        xi_s[:, blk[m]] = jnp.dot(um, bbi_ref[m], preferred_element_type=F32)

    for m in range(C_BLOCKS):
        cols = blk[m]
        ar = jnp.broadcast_to(ar_ref[:, cols], (bsz, C_SLANES))
        ai = jnp.broadcast_to(ai_ref[:, cols], (bsz, C_SLANES))

        def step(t, carry):
            xr, xi = carry
            rows = pl.ds(pl.multiple_of(t * bsz, bsz), bsz)
            nr = ar * xr - ai * xi + xr_s[rows, cols]
            ni = ar * xi + ai * xr + xi_s[rows, cols]
            xr_s[rows, cols] = nr
            xi_s[rows, cols] = ni
            return nr, ni

        xr, xi = lax.fori_loop(0, tt, step, (sr_s[:, cols], si_s[:, cols]), unroll=4)
        sr_s[:, cols] = xr
        si_s[:, cols] = xi

    parts = [_dot(xr_s[:, blk[m]], cr_ref[m]) - _dot(xi_s[:, blk[m]], ci_ref[m]) for m in range(C_BLOCKS)]
    y = jnp.concatenate(parts, axis=1) + d_ref[...] * u
    y = jax.nn.gelu(y)
    y_ref[...] = y * _sigmoid(_dot(y, wg_ref[...]) + bg_ref[...])


def _s5(u_tb, bsz, seq, p, tt):
    f32 = F32
    dt = jnp.exp(p["c_log_dt"].astype(f32))[:, None]
    lr, li = p["c_lam_re"].astype(f32), p["c_lam_im"].astype(f32)
    mag = jnp.exp(lr * dt)
    ab_re, ab_im = mag * jnp.cos(li * dt), mag * jnp.sin(li * dt)
    den = lr * lr + li * li
    f_re = ((ab_re - 1.0) * lr + ab_im * li) / den
    f_im = (ab_im * lr - (ab_re - 1.0) * li) / den
    bre, bim = p["c_b_re"].astype(f32), p["c_b_im"].astype(f32)
    bb_re = f_re[..., None] * bre - f_im[..., None] * bim
    bb_im = f_re[..., None] * bim + f_im[..., None] * bre
    eye = jnp.eye(C_GROUPS, dtype=f32)

    def in_blocks(bb):
        full = jnp.einsum("gpc,gh->gchp", bb, eye).reshape(C_WIDTH, C_LANES)
        return jnp.stack([full[m * C_ULANES:(m + 1) * C_ULANES, m * C_SLANES:(m + 1) * C_SLANES]
                          for m in range(C_BLOCKS)]).astype(BF16)

    def out_blocks(cc):
        full = jnp.einsum("gcp,gh->gphc", cc.astype(f32), eye).reshape(C_LANES, C_WIDTH)
        return jnp.stack([full[m * C_SLANES:(m + 1) * C_SLANES, m * C_ULANES:(m + 1) * C_ULANES]
                          for m in range(C_BLOCKS)]).astype(BF16)

    consts = [in_blocks(bb_re), in_blocks(bb_im), ab_re.reshape(1, C_LANES), ab_im.reshape(1, C_LANES),
              out_blocks(p["c_c_re"]), out_blocks(p["c_c_im"]), p["c_d"].reshape(1, C_WIDTH).astype(f32),
              p["c_w_glu"].astype(BF16), p["c_b_glu"].reshape(1, C_WIDTH).astype(f32)]
    rows = tt * bsz
    return pl.pallas_call(
        functools.partial(_s5_kernel, bsz),
        grid=(seq // tt,),
        in_specs=[pl.BlockSpec((rows, C_WIDTH), lambda i: (i, 0))] + [_full(a.shape) for a in consts],
        out_specs=pl.BlockSpec((rows, C_WIDTH), lambda i: (i, 0)),
        out_shape=jax.ShapeDtypeStruct((seq * bsz, C_WIDTH), F32),
        scratch_shapes=[pltpu.VMEM((rows, C_LANES), F32), pltpu.VMEM((rows, C_LANES), F32),
                        pltpu.VMEM((bsz, C_LANES), F32), pltpu.VMEM((bsz, C_LANES), F32)],
        compiler_params=_cparams("arbitrary"),
        name="s5",
    )(u_tb, *consts)


def _merge_kernel(alpha, x_ref, ya_ref, yb_ref, yc_ref, wgate_ref, bgate_ref, wba_ref, wbb_ref, wbc_ref,
                  wout_ref, g_ref, b_ref, o_ref):
    d = x_ref.shape[1]
    x = x_ref[...]
    gates = _sigmoid(_dot(x, wgate_ref[...]) + bgate_ref[...])
    merged = (gates[:, :d] * _dot(ya_ref[...], wba_ref[...])
              + gates[:, d:2 * d] * _dot(yb_ref[...], wbb_ref[...])
              + gates[:, 2 * d:] * _dot(yc_ref[...], wbc_ref[...]))
    o_ref[...] = _layer_norm(alpha * x + _dot(merged, wout_ref[...]), g_ref[...], b_ref[...])


def _merge(x2d, ya, yb, yc_tm, wgate, bgate, wb, wout, g, b, alpha, tm, seq):
    m, d = x2d.shape
    wba, wbb, wbc = wb[:A_WIDTH], wb[A_WIDTH:A_WIDTH + B_V_WIDTH], wb[A_WIDTH + B_V_WIDTH:]
    consts = [wgate, bgate, wba, wbb, wbc, wout, g, b]
    tile = lambda n: pl.BlockSpec((tm, n), lambda i: (i, 0))
    per_seq = seq // tm
    yc_spec = pl.BlockSpec((tm, C_WIDTH), lambda i: (i % per_seq, i // per_seq))
    return pl.pallas_call(
        functools.partial(_merge_kernel, alpha),
        grid=(m // tm,),
        in_specs=[tile(d), tile(A_WIDTH), tile(B_V_WIDTH), yc_spec] + [_full(a.shape) for a in consts],
        out_specs=tile(d),
        out_shape=jax.ShapeDtypeStruct((m, d), F32),
        compiler_params=_cparams("parallel"),
        name="merge",
    )(x2d, ya, yb, yc_tm, *consts)


def _ffn_kernel(alpha, x_ref, w1_ref, w2_ref, g_ref, b_ref, o_ref, acc_s):
    j = pl.program_id(1)
    x = x_ref[...]
    h = jnp.maximum(_dot(x, w1_ref[...]), 0.0)
    part = _dot(h * h, w2_ref[...])

    @pl.when(j == 0)
    def _():
        acc_s[...] = part

    @pl.when(j > 0)
    def _():
        acc_s[...] += part

    @pl.when(j == pl.num_programs(1) - 1)
    def _():
        o_ref[...] = _layer_norm(alpha * x + acc_s[...], g_ref[...], b_ref[...])


def _ffn(x2d, w1, w2, g, b, alpha, tm, tf):
    m, d = x2d.shape
    dff = w1.shape[1]
    return pl.pallas_call(
        functools.partial(_ffn_kernel, alpha),
        grid=(m // tm, dff // tf),
        in_specs=[pl.BlockSpec((tm, d), lambda i, j: (i, 0)),
                  pl.BlockSpec((d, tf), lambda i, j: (0, j)),
                  pl.BlockSpec((tf, d), lambda i, j: (j, 0)),
                  _full(g.shape), _full(b.shape)],
        out_specs=pl.BlockSpec((tm, d), lambda i, j: (i, 0)),
        out_shape=jax.ShapeDtypeStruct((m, d), F32),
        scratch_shapes=[pltpu.VMEM((tm, d), F32)],
        compiler_params=_cparams("parallel", "arbitrary"),
        name="ffn",
    )(x2d, w1, w2, g, b)


def _tile(n, want):
    t = min(n, want)
    assert n % t == 0, (n, want)
    return t


def kernel(x, w_in, b_gate, a_shift, a_w0, a_w2, a_a0, a_a2, a_g2, a_kk, a_ka, a_rk, a_lnx_g, a_lnx_b, c_lam_re, c_lam_im, c_log_dt, c_b_re, c_b_im, c_c_re, c_c_im, c_d, c_w_glu, c_b_glu, w_branch, w_out, ln1_g, ln1_b, w_ff1, w_ff2, ln2_g, ln2_b):
    bsz, seq, d = x.shape
    depth = w_in.shape[0]
    alpha = (2.0 * depth) ** 0.25
    tokens = bsz * seq
    per_layer = dict(a_shift=a_shift, a_w0=a_w0, a_w2=a_w2, a_a0=a_a0, a_a2=a_a2, a_g2=a_g2, a_kk=a_kk,
                     a_ka=a_ka, a_rk=a_rk, a_lnx_g=a_lnx_g, a_lnx_b=a_lnx_b, c_lam_re=c_lam_re,
                     c_lam_im=c_lam_im, c_log_dt=c_log_dt, c_b_re=c_b_re, c_b_im=c_b_im, c_c_re=c_c_re,
                     c_c_im=c_c_im, c_d=c_d, c_w_glu=c_w_glu, c_b_glu=c_b_glu)
    u_lo, u_hi = MIX_COLS, MIX_COLS + C_WIDTH
    tm_merge = _tile(seq, 256)
    xt = x.reshape(tokens, d)
    for l in range(depth):
        p = {k: v[l] for k, v in per_layer.items()}
        wl = w_in[l]
        w_mix = wl[:, :MIX_COLS].astype(BF16)
        w_u = wl[:, u_lo:u_hi].astype(BF16)
        w_gate = wl[:, u_hi:].astype(BF16)
        zmix = _proj(xt, w_mix, _tile(tokens, 512), MIX_COLS // 2).reshape(bsz, seq, MIX_COLS)
        u_tb = _proj_time_major(xt, w_u, bsz, seq, _tile(seq, 512)).reshape(seq * bsz, C_WIDTH)
        ya = _rwkv(zmix, bsz, seq, p, _tile(seq, 256))
        yb = _retention(zmix, bsz, seq, _tile(seq, 512))
        yc = _s5(u_tb, bsz, seq, p, _tile(seq, 64)).reshape(seq, bsz * C_WIDTH)
        x1 = _merge(xt, ya.reshape(tokens, -1), yb.reshape(tokens, -1), yc,
                    w_gate, b_gate[l].reshape(1, -1), w_branch[l].astype(BF16), w_out[l].astype(BF16),
                    ln1_g[l].reshape(1, -1), ln1_b[l].reshape(1, -1), alpha, tm_merge, seq)
        xt = _ffn(x1, w_ff1[l].astype(BF16), w_ff2[l].astype(BF16), ln2_g[l].reshape(1, -1),
                  ln2_b[l].reshape(1, -1), alpha, _tile(tokens, 1024), 1024)
    return xt.reshape(bsz, seq, d)
```

```python
import functools
import math

import jax
import jax.numpy as jnp
from jax import lax
from jax.experimental import pallas as pl
from jax.experimental.pallas import tpu as pltpu

F32 = jnp.float32
BF16 = jnp.bfloat16

A_HEADS = 8
A_HEAD_DIM = 64
A_WIDTH = A_HEADS * A_HEAD_DIM
A_DECAY_LORA = 64
A_ICLR_LORA = 64
A_GATE_LORA = 128
A_PROJ = 3 * A_WIDTH + A_DECAY_LORA + A_ICLR_LORA + A_GATE_LORA
A_GN_EPS = 64e-5
A_CHUNK = 64

B_HEADS = 4
B_QK_DIM = 128
B_V_DIM = 256
B_QK_WIDTH = B_HEADS * B_QK_DIM
B_V_WIDTH = B_HEADS * B_V_DIM
B_CHUNK = 128
B_ROPE_BASE = 10000.0
B_GN_EPS = 1e-5

C_WIDTH = 512
C_GROUP = 16
C_GROUPS = C_WIDTH // C_GROUP
C_STATE = 64
C_LANES = C_GROUPS * C_STATE
C_BLOCKS = 4
C_ULANES = C_WIDTH // C_BLOCKS
C_SLANES = C_LANES // C_BLOCKS

LN_EPS = 1e-5

MIX_COLS = A_PROJ + 2 * B_QK_WIDTH + 2 * B_V_WIDTH + C_WIDTH
U_BLK = (MIX_COLS - C_WIDTH) // (C_WIDTH // 2)
Q_BLK = A_PROJ // B_QK_DIM
K_BLK = Q_BLK + B_HEADS
V_BLK = (A_PROJ + 2 * B_QK_WIDTH) // B_V_DIM
G_BLK = V_BLK + B_HEADS
assert A_PROJ % B_QK_DIM == 0 and (A_PROJ + 2 * B_QK_WIDTH) % B_V_DIM == 0

V7X_VMEM_LIMIT_BYTES = 56 * 1024 * 1024


def _cparams(*sem):
    return pltpu.CompilerParams(dimension_semantics=sem, vmem_limit_bytes=V7X_VMEM_LIMIT_BYTES)


def _full(shape):
    n = len(shape)
    return pl.BlockSpec(shape, lambda *_: (0,) * n)


def _dot(a, b):
    return jnp.dot(a.astype(BF16), b.astype(BF16), preferred_element_type=F32)


def _dot_nt(a, b):
    return lax.dot_general(a.astype(BF16), b.astype(BF16), (((1,), (1,)), ((), ())),
                           preferred_element_type=F32)


def _dot_tn(a, b):
    return lax.dot_general(a.astype(BF16), b.astype(BF16), (((0,), (0,)), ((), ())),
                           preferred_element_type=F32)


def _split2(x):
    hi = x.astype(BF16)
    lo = (x - hi.astype(F32)).astype(BF16)
    return hi, lo


def _split3(x):
    hi = x.astype(BF16)
    r1 = x - hi.astype(F32)
    mid = r1.astype(BF16)
    lo = (r1 - mid.astype(F32)).astype(BF16)
    return hi, mid, lo


def _sigmoid(x):
    return 1.0 / (1.0 + jnp.exp(-x))


def _layer_norm(y, g, b):
    mu = jnp.mean(y, axis=-1, keepdims=True)
    d = y - mu
    var = jnp.mean(d * d, axis=-1, keepdims=True)
    return d * lax.rsqrt(var + LN_EPS) * g + b


def _proj_kernel(x_ref, w_ref, o_ref):
    o_ref[...] = jnp.dot(x_ref[...].astype(BF16), w_ref[...], preferred_element_type=F32)


def _proj(x2d, w_bf16, tm, tn):
    m, k = x2d.shape
    n = w_bf16.shape[1]
    return pl.pallas_call(
        _proj_kernel,
        grid=(m // tm, n // tn),
        in_specs=[pl.BlockSpec((tm, k), lambda i, j: (i, 0)),
                  pl.BlockSpec((k, tn), lambda i, j: (0, j))],
        out_specs=pl.BlockSpec((tm, tn), lambda i, j: (i, j)),
        out_shape=jax.ShapeDtypeStruct((m, n), F32),
        compiler_params=_cparams("parallel", "arbitrary"),
        name="proj",
    )(x2d, w_bf16)


def _rwkv_kernel(z_ref, mu_ref, w0_ref, a0_ref, lora_ref, g2_ref, kkp_ref, kap_ref, rkp_ref,
                 lng_ref, lnb_ref, ones_ref, tri_ref, y_ref,
                 carry_s, state_s, r_s, k_s, v_s, kn_s, ia_s, lw_s, o_s, rk_s, gate_s,
                 gam_s, w_s, uv_s, aqb_s, ov_s, kv_s):
    tc = z_ref.shape[0]
    n_chunks = tc // A_CHUNK
    w = A_WIDTH

    @pl.when(pl.program_id(1) == 0)
    def _():
        carry_s[...] = jnp.zeros_like(carry_s)
        state_s[...] = jnp.zeros_like(state_s)

    ones = ones_ref[...]

    def seg_sum(x):
        hi, lo = _split2(x)
        return (jnp.dot(hi, ones, preferred_element_type=F32)
                + jnp.dot(lo, ones, preferred_element_type=F32))

    z = z_ref[...]
    rolled = pltpu.roll(z, 1, 0)
    rowid = lax.broadcasted_iota(jnp.int32, z.shape, 0)
    prev = jnp.where(rowid == 0, jnp.broadcast_to(carry_s[0:1, :], z.shape), rolled)
    zs = z + mu_ref[...] * (prev - z)
    carry_s[0:1, :] = z[tc - 1:tc, :]

    lz = zs[:, 3 * w:3 * w + 128]
    lane = lax.broadcasted_iota(jnp.int32, lz.shape, 1)
    lin = jnp.where(lane < A_DECAY_LORA, jnp.tanh(lz), lz)
    wa = _dot(lin, lora_ref[...])
    lw_s[...] = (-math.exp(-0.5)) * _sigmoid(w0_ref[...] + wa[:, :w])
    ia = _sigmoid(a0_ref[...] + wa[:, w:])
    ia_s[...] = ia
    gate_s[...] = _dot(_sigmoid(zs[:, 3 * w + 128:3 * w + 256]), g2_ref[...])

    r = zs[:, :w]
    k = zs[:, w:2 * w]
    v = zs[:, 2 * w:3 * w]
    kk = k * kkp_ref[...]
    kn_s[...] = kk / jnp.maximum(jnp.sqrt(seg_sum(kk * kk)), 1e-12)
    kmod = k * (1.0 + (ia - 1.0) * kap_ref[...])
    r_s[...] = r
    k_s[...] = kmod
    v_s[...] = v
    rk_s[...] = seg_sum(r * kmod * rkp_ref[...])

    tri = tri_ref[...]
    rid = lax.broadcasted_iota(jnp.int32, (A_CHUNK, A_CHUNK), 0)
    cid = lax.broadcasted_iota(jnp.int32, (A_CHUNK, A_CHUNK), 1)
    strict = rid > cid
    incl = rid >= cid
    eye = (rid == cid).astype(F32)
    n = A_HEAD_DIM

    heads = range(A_HEADS)
    hsl = [slice(h * n, (h + 1) * n) for h in heads]


    def state_free_part(c, carry):
        rows = pl.ds(pl.multiple_of(c * A_CHUNK, A_CHUNK), A_CHUNK)
        lw = lw_s[rows, :]
        h3 = _split3(lw)
        cum = (jnp.dot(tri, h3[0], preferred_element_type=F32)
               + jnp.dot(tri, h3[1], preferred_element_type=F32)
               + jnp.dot(tri, h3[2], preferred_element_type=F32))
        e_in = jnp.exp(cum)
        e_ex = jnp.exp(cum - lw)
        e_ng = jnp.exp(-cum)
        kn = kn_s[rows, :]
        rt = r_s[rows, :] * e_in
        at = -kn * e_ex
        bt = kn * ia_s[rows, :] * e_ng
        kt = k_s[rows, :] * e_ng
        vv = v_s[rows, :]
        r_s[rows, :] = rt
        kn_s[rows, :] = bt
        gam_s[pl.ds(c, 1), :] = e_in[A_CHUNK - 1:A_CHUNK, :]
        ah = [at[:, s] for s in hsl]
        bh = [bt[:, s] for s in hsl]
        kh = [kt[:, s] for s in hsl]
        vh = [vv[:, s] for s in hsl]
        ar = [jnp.concatenate([ah[h], rt[:, hsl[h]]], axis=0) for h in heads]
        gb = [_dot_nt(ar[h], bh[h]) for h in heads]
        gk = [_dot_nt(ar[h], kh[h]) for h in heads]
        l_ab = [jnp.where(strict, gb[h][:A_CHUNK], 0.0) for h in heads]
        a_ak = [jnp.where(strict, gk[h][:A_CHUNK], 0.0) for h in heads]
        for h in heads:
            aqb_s[c * A_HEADS + h] = jnp.where(incl, gb[h][A_CHUNK:], 0.0)
        akv = [_dot(a_ak[h], vh[h]) for h in heads]
        for h in heads:
            ov_s[c * A_HEADS + h] = _dot(jnp.where(incl, gk[h][A_CHUNK:], 0.0), vh[h])
        for h in heads:
            kv_s[c * A_HEADS + h] = _dot_tn(vh[h], kh[h])
        tinv = [eye + l_ab[h] for h in heads]
        p = l_ab
        for _ in range(5):
            p = [_dot(p[h], p[h]) for h in heads]
            tinv = [tinv[h] + _dot(tinv[h], p[h]) for h in heads]
        for h in heads:
            w_s[c * A_HEADS + h] = _dot(tinv[h], ah[h])
        for h in heads:
            uv_s[c * A_HEADS + h] = _dot(tinv[h], akv[h])
        return carry

    def state_part(c, carry):
        rows = pl.ds(pl.multiple_of(c * A_CHUNK, A_CHUNK), A_CHUNK)
        rt = r_s[rows, :]
        bt = kn_s[rows, :]
        gam = gam_s[pl.ds(c, 1), :]
        s0 = [state_s[h] for h in heads]
        u = [_dot_nt(w_s[c * A_HEADS + h], s0[h]) + uv_s[c * A_HEADS + h] for h in heads]
        su = [_dot_tn(u[h], bt[:, hsl[h]]) for h in heads]
        for h in heads:
            state_s[h] = (s0[h] + su[h] + kv_s[c * A_HEADS + h]) * gam[:, hsl[h]]
        o = [_dot_nt(rt[:, hsl[h]], s0[h]) for h in heads]
        o = [o[h] + _dot(aqb_s[c * A_HEADS + h], u[h]) + ov_s[c * A_HEADS + h] for h in heads]
        o_s[rows, :] = jnp.concatenate(o, axis=1)
        return carry

    lax.fori_loop(0, n_chunks, state_free_part, 0)
    lax.fori_loop(0, n_chunks, state_part, 0)

    o = o_s[...]
    inv_n = 1.0 / n
    mean = seg_sum(o) * inv_n
    d = o - mean
    var = seg_sum(d * d) * inv_n
    on = d * lax.rsqrt(var + A_GN_EPS) * lng_ref[...] + lnb_ref[...]
    on = on + rk_s[...] * v_s[...]
    y_ref[...] = on * gate_s[...]


def _rwkv(zmix, bsz, seq, p, tc):
    zblk = 0
    row = lambda a: a.reshape(1, -1).astype(F32)
    w = A_WIDTH
    lora = jnp.zeros((128, 2 * w), F32)
    lora = lora.at[:A_DECAY_LORA, :w].set(p["a_w2"]).at[A_DECAY_LORA:, w:].set(p["a_a2"]).astype(BF16)
    hid = jnp.arange(w) // A_HEAD_DIM
    ones = (hid[:, None] == hid[None, :]).astype(BF16)
    ti = jnp.arange(A_CHUNK)
    tri = (ti[:, None] >= ti[None, :]).astype(BF16)
    small = [row(p["a_shift"]), row(p["a_w0"]), row(p["a_a0"]), lora, p["a_g2"].astype(BF16),
             row(p["a_kk"]), row(p["a_ka"]), row(p["a_rk"]), row(p["a_lnx_g"]), row(p["a_lnx_b"]),
             ones, tri]
    stage = pltpu.VMEM((tc, w), F32)
    n_chunks = tc // A_CHUNK
    per_head = pltpu.VMEM((n_chunks * A_HEADS, A_CHUNK, A_HEAD_DIM), F32)
    return pl.pallas_call(
        _rwkv_kernel,
        grid=(bsz, seq // tc),
        in_specs=[pl.BlockSpec((None, tc, A_PROJ), lambda b, i: (b, i, zblk))]
                 + [_full(a.shape) for a in small],
        out_specs=pl.BlockSpec((None, tc, w), lambda b, i: (b, i, 0)),
        out_shape=jax.ShapeDtypeStruct((bsz, seq, w), F32),
        scratch_shapes=[pltpu.VMEM((8, A_PROJ), F32),
                        pltpu.VMEM((A_HEADS, A_HEAD_DIM, A_HEAD_DIM), F32),
                        stage, stage, stage, stage, stage, stage, stage, stage, stage,
                        pltpu.VMEM((max(8, n_chunks), w), F32),
                        per_head, per_head, per_head, per_head, per_head],
        compiler_params=_cparams("parallel", "arbitrary"),
        name="rwkv7",
    )(zmix, *small)


def _ret_kernel(q_ref, k_ref, v_ref, g_ref, cos_ref, sin_ref, dmask_ref, qd_ref, kd_ref, cd_ref,
                y_ref, state_s):
    tc = q_ref.shape[0]
    n_chunks = tc // B_CHUNK

    @pl.when(pl.program_id(2) == 0)
    def _():
        state_s[...] = jnp.zeros_like(state_s)

    cos2 = cos_ref[...]
    sin2 = sin_ref[...]
    half = B_QK_DIM // 2

    def rope(t):
        return t * cos2 + pltpu.roll(t, half, 1) * sin2

    q = rope(q_ref[...])
    k = rope(k_ref[...]) * (B_QK_DIM ** -0.5)
    dmask = dmask_ref[...]
    qd = qd_ref[...]
    kd = kd_ref[...]
    cd = cd_ref[...]
    for c in range(n_chunks):
        rows = slice(c * B_CHUNK, (c + 1) * B_CHUNK)
        qc, kc, vc = q[rows], k[rows], v_ref[rows, :]
        scores = _dot_nt(qc, kc) * dmask
        st = state_s[...]
        o = _dot(scores, vc) + _dot(qc * qd, st)
        state_s[...] = cd * st + _dot_tn(kc * kd, vc)
        mu = jnp.mean(o, axis=-1, keepdims=True)
        d = o - mu
        var = jnp.mean(d * d, axis=-1, keepdims=True)
        on = d * lax.rsqrt(var + B_GN_EPS)
        g = g_ref[rows, :]
        y_ref[rows, :] = g * _sigmoid(g) * on


def _retention(zmix, bsz, seq, tc):
    f32 = F32
    pos = jnp.arange(seq, dtype=f32)
    half = B_QK_DIM // 2
    inv_freq = B_ROPE_BASE ** (-jnp.arange(half, dtype=f32) / half)
    ang = pos[:, None] * inv_freq[None, :]
    cos, sin = jnp.cos(ang), jnp.sin(ang)
    cos2 = jnp.concatenate([cos, cos], axis=1)
    sin2 = jnp.concatenate([-sin, sin], axis=1)
    log_gamma = jnp.log(1.0 - 2.0 ** (-5.0 - jnp.arange(B_HEADS, dtype=f32)))
    idx = jnp.arange(B_CHUNK, dtype=f32)
    rel = idx[:, None] - idx[None, :]
    dmask = jnp.where(rel >= 0, jnp.exp(log_gamma[:, None, None] * jnp.maximum(rel, 0.0)), 0.0)
    qd = jnp.broadcast_to(jnp.exp(log_gamma[:, None] * (idx + 1.0))[:, :, None],
                          (B_HEADS, B_CHUNK, B_QK_DIM))
    kd = jnp.broadcast_to(jnp.exp(log_gamma[:, None] * (B_CHUNK - 1.0 - idx))[:, :, None],
                          (B_HEADS, B_CHUNK, B_QK_DIM))
    cd = jnp.broadcast_to(jnp.exp(log_gamma * B_CHUNK)[:, None, None], (B_HEADS, B_QK_DIM, B_V_DIM))
    head_tab = lambda a, b: pl.BlockSpec((None, a, b), lambda bb, h, i: (h, 0, 0))
    return pl.pallas_call(
        _ret_kernel,
        grid=(bsz, B_HEADS, seq // tc),
        in_specs=[pl.BlockSpec((None, tc, B_QK_DIM), lambda b, h, i: (b, i, Q_BLK + h)),
                  pl.BlockSpec((None, tc, B_QK_DIM), lambda b, h, i: (b, i, K_BLK + h)),
                  pl.BlockSpec((None, tc, B_V_DIM), lambda b, h, i: (b, i, V_BLK + h)),
                  pl.BlockSpec((None, tc, B_V_DIM), lambda b, h, i: (b, i, G_BLK + h)),
                  pl.BlockSpec((tc, B_QK_DIM), lambda b, h, i: (i, 0)),
                  pl.BlockSpec((tc, B_QK_DIM), lambda b, h, i: (i, 0)),
                  head_tab(B_CHUNK, B_CHUNK), head_tab(B_CHUNK, B_QK_DIM),
                  head_tab(B_CHUNK, B_QK_DIM), head_tab(B_QK_DIM, B_V_DIM)],
        out_specs=pl.BlockSpec((None, tc, B_V_DIM), lambda b, h, i: (b, i, h)),
        out_shape=jax.ShapeDtypeStruct((bsz, seq, B_V_WIDTH), F32),
        scratch_shapes=[pltpu.VMEM((B_QK_DIM, B_V_DIM), F32)],
        compiler_params=_cparams("parallel", "parallel", "arbitrary"),
        name="retention",
    )(zmix, zmix, zmix, zmix, cos2, sin2, dmask, qd, kd, cd)


def _s5_kernel(ua_ref, ub_ref, bbr_ref, bbi_ref, ar_ref, ai_ref, cr_ref, ci_ref, d_ref, wg_ref, bg_ref,
               y_ref, xr_s, xi_s, sr_s, si_s):
    bsz, tt, _ = ua_ref.shape

    @pl.when(pl.program_id(0) == 0)
    def _():
        sr_s[...] = jnp.zeros_like(sr_s)
        si_s[...] = jnp.zeros_like(si_s)

    u = jnp.concatenate([ua_ref[...], ub_ref[...]], axis=2)
    u = pltpu.einshape("btc->tbc", u).reshape(tt * bsz, C_WIDTH)
    ub = u.astype(BF16)
    blk = [slice(m * C_SLANES, (m + 1) * C_SLANES) for m in range(C_BLOCKS)]

    for m in range(C_BLOCKS):
        um = ub[:, m * C_ULANES:(m + 1) * C_ULANES]
        xr_s[:, blk[m]] = jnp.dot(um, bbr_ref[m], preferred_element_type=F32)
        xi_s[:, blk[m]] = jnp.dot(um, bbi_ref[m], preferred_element_type=F32)

    for m in range(C_BLOCKS):
        cols = blk[m]
        ar = jnp.broadcast_to(ar_ref[:, cols], (bsz, C_SLANES))
        ai = jnp.broadcast_to(ai_ref[:, cols], (bsz, C_SLANES))

        def step(t, carry):
            xr, xi = carry
            rows = pl.ds(pl.multiple_of(t * bsz, bsz), bsz)
            nr = ar * xr - ai * xi + xr_s[rows, cols]
            ni = ar * xi + ai * xr + xi_s[rows, cols]
            xr_s[rows, cols] = nr
            xi_s[rows, cols] = ni
            return nr, ni

        xr, xi = lax.fori_loop(0, tt, step, (sr_s[:, cols], si_s[:, cols]), unroll=4)
        sr_s[:, cols] = xr
        si_s[:, cols] = xi

    parts = [_dot(xr_s[:, blk[m]], cr_ref[m]) - _dot(xi_s[:, blk[m]], ci_ref[m]) for m in range(C_BLOCKS)]
    y = jnp.concatenate(parts, axis=1) + d_ref[...] * u
    y = jax.nn.gelu(y)
    y = y * _sigmoid(_dot(y, wg_ref[...]) + bg_ref[...])
    y_ref[...] = pltpu.einshape("tbc->btc", y.reshape(tt, bsz, C_WIDTH))


def _s5(zmix, bsz, seq, p, tt):
    f32 = F32
    dt = jnp.exp(p["c_log_dt"].astype(f32))[:, None]
    lr, li = p["c_lam_re"].astype(f32), p["c_lam_im"].astype(f32)
    mag = jnp.exp(lr * dt)
    ab_re, ab_im = mag * jnp.cos(li * dt), mag * jnp.sin(li * dt)
    den = lr * lr + li * li
    f_re = ((ab_re - 1.0) * lr + ab_im * li) / den
    f_im = (ab_im * lr - (ab_re - 1.0) * li) / den
    bre, bim = p["c_b_re"].astype(f32), p["c_b_im"].astype(f32)
    bb_re = f_re[..., None] * bre - f_im[..., None] * bim
    bb_im = f_re[..., None] * bim + f_im[..., None] * bre
    eye = jnp.eye(C_GROUPS, dtype=f32)

    def in_blocks(bb):
        full = jnp.einsum("gpc,gh->gchp", bb, eye).reshape(C_WIDTH, C_LANES)
        return jnp.stack([full[m * C_ULANES:(m + 1) * C_ULANES, m * C_SLANES:(m + 1) * C_SLANES]
                          for m in range(C_BLOCKS)]).astype(BF16)

    def out_blocks(cc):
        full = jnp.einsum("gcp,gh->gphc", cc.astype(f32), eye).reshape(C_LANES, C_WIDTH)
        return jnp.stack([full[m * C_SLANES:(m + 1) * C_SLANES, m * C_ULANES:(m + 1) * C_ULANES]
                          for m in range(C_BLOCKS)]).astype(BF16)

    consts = [in_blocks(bb_re), in_blocks(bb_im), ab_re.reshape(1, C_LANES), ab_im.reshape(1, C_LANES),
              out_blocks(p["c_c_re"]), out_blocks(p["c_c_im"]), p["c_d"].reshape(1, C_WIDTH).astype(f32),
              p["c_w_glu"].astype(BF16), p["c_b_glu"].reshape(1, C_WIDTH).astype(f32)]
    rows = tt * bsz
    half = C_WIDTH // 2
    return pl.pallas_call(
        _s5_kernel,
        grid=(seq // tt,),
        in_specs=[pl.BlockSpec((bsz, tt, half), lambda i: (0, i, U_BLK)),
                  pl.BlockSpec((bsz, tt, half), lambda i: (0, i, U_BLK + 1))]
                 + [_full(a.shape) for a in consts],
        out_specs=pl.BlockSpec((bsz, tt, C_WIDTH), lambda i: (0, i, 0)),
        out_shape=jax.ShapeDtypeStruct((bsz, seq, C_WIDTH), F32),
        scratch_shapes=[pltpu.VMEM((rows, C_LANES), F32), pltpu.VMEM((rows, C_LANES), F32),
                        pltpu.VMEM((bsz, C_LANES), F32), pltpu.VMEM((bsz, C_LANES), F32)],
        compiler_params=_cparams("arbitrary"),
        name="s5",
    )(zmix, zmix, *consts)


def _merge_kernel(alpha, x_ref, ya_ref, yb_ref, yc_ref, wgate_ref, bgate_ref, wba_ref, wbb_ref, wbc_ref,
                  wout_ref, g_ref, b_ref, o_ref):
    d = x_ref.shape[1]
    x = x_ref[...]
    gates = _sigmoid(_dot(x, wgate_ref[...]) + bgate_ref[...])
    merged = (gates[:, :d] * _dot(ya_ref[...], wba_ref[...])
              + gates[:, d:2 * d] * _dot(yb_ref[...], wbb_ref[...])
              + gates[:, 2 * d:] * _dot(yc_ref[...], wbc_ref[...]))
    o_ref[...] = _layer_norm(alpha * x + _dot(merged, wout_ref[...]), g_ref[...], b_ref[...])


def _merge(x2d, ya, yb, yc, wgate, bgate, wb, wout, g, b, alpha, tm):
    m, d = x2d.shape
    wba, wbb, wbc = wb[:A_WIDTH], wb[A_WIDTH:A_WIDTH + B_V_WIDTH], wb[A_WIDTH + B_V_WIDTH:]
    consts = [wgate, bgate, wba, wbb, wbc, wout, g, b]
    tile = lambda n: pl.BlockSpec((tm, n), lambda i: (i, 0))
    return pl.pallas_call(
        functools.partial(_merge_kernel, alpha),
        grid=(m // tm,),
        in_specs=[tile(d), tile(A_WIDTH), tile(B_V_WIDTH), tile(C_WIDTH)] + [_full(a.shape) for a in consts],
        out_specs=tile(d),
        out_shape=jax.ShapeDtypeStruct((m, d), F32),
        compiler_params=_cparams("parallel"),
        name="merge",
    )(x2d, ya, yb, yc, *consts)


def _ffn_kernel(alpha, x_ref, w1_ref, w2_ref, g_ref, b_ref, o_ref, acc_s):
    j = pl.program_id(1)
    x = x_ref[...]
    h = jnp.maximum(_dot(x, w1_ref[...]), 0.0)
    part = _dot(h * h, w2_ref[...])

    @pl.when(j == 0)
    def _():
        acc_s[...] = part

    @pl.when(j > 0)
    def _():
        acc_s[...] += part

    @pl.when(j == pl.num_programs(1) - 1)
    def _():
        o_ref[...] = _layer_norm(alpha * x + acc_s[...], g_ref[...], b_ref[...])


def _ffn(x2d, w1, w2, g, b, alpha, tm, tf):
    m, d = x2d.shape
    dff = w1.shape[1]
    return pl.pallas_call(
        functools.partial(_ffn_kernel, alpha),
        grid=(m // tm, dff // tf),
        in_specs=[pl.BlockSpec((tm, d), lambda i, j: (i, 0)),
                  pl.BlockSpec((d, tf), lambda i, j: (0, j)),
                  pl.BlockSpec((tf, d), lambda i, j: (j, 0)),
                  _full(g.shape), _full(b.shape)],
        out_specs=pl.BlockSpec((tm, d), lambda i, j: (i, 0)),
        out_shape=jax.ShapeDtypeStruct((m, d), F32),
        scratch_shapes=[pltpu.VMEM((tm, d), F32)],
        compiler_params=_cparams("parallel", "arbitrary"),
        name="ffn",
    )(x2d, w1, w2, g, b)


def _tile(n, want):
    t = min(n, want)
    assert n % t == 0, (n, want)
    return t


def kernel(x, w_in, b_gate, a_shift, a_w0, a_w2, a_a0, a_a2, a_g2, a_kk, a_ka, a_rk, a_lnx_g, a_lnx_b, c_lam_re, c_lam_im, c_log_dt, c_b_re, c_b_im, c_c_re, c_c_im, c_d, c_w_glu, c_b_glu, w_branch, w_out, ln1_g, ln1_b, w_ff1, w_ff2, ln2_g, ln2_b):
    bsz, seq, d = x.shape
    depth = w_in.shape[0]
    alpha = (2.0 * depth) ** 0.25
    tokens = bsz * seq
    per_layer = dict(a_shift=a_shift, a_w0=a_w0, a_w2=a_w2, a_a0=a_a0, a_a2=a_a2, a_g2=a_g2, a_kk=a_kk,
                     a_ka=a_ka, a_rk=a_rk, a_lnx_g=a_lnx_g, a_lnx_b=a_lnx_b, c_lam_re=c_lam_re,
                     c_lam_im=c_lam_im, c_log_dt=c_log_dt, c_b_re=c_b_re, c_b_im=c_b_im, c_c_re=c_c_re,
                     c_c_im=c_c_im, c_d=c_d, c_w_glu=c_w_glu, c_b_glu=c_b_glu)
    xt = x.reshape(tokens, d)
    for l in range(depth):
        p = {k: v[l] for k, v in per_layer.items()}
        wl = w_in[l]
        w_mix = wl[:, :MIX_COLS].astype(BF16)
        w_gate = wl[:, MIX_COLS:].astype(BF16)
        zmix = _proj(xt, w_mix, _tile(tokens, 512), A_PROJ).reshape(bsz, seq, MIX_COLS)
        ya = _rwkv(zmix, bsz, seq, p, _tile(seq, 256))
        yb = _retention(zmix, bsz, seq, _tile(seq, 512))
        yc = _s5(zmix, bsz, seq, p, _tile(seq, 64))
        x1 = _merge(xt, ya.reshape(tokens, -1), yb.reshape(tokens, -1), yc.reshape(tokens, -1),
                    w_gate, b_gate[l].reshape(1, -1), w_branch[l].astype(BF16), w_out[l].astype(BF16),
                    ln1_g[l].reshape(1, -1), ln1_b[l].reshape(1, -1), alpha, _tile(tokens, 256))
        xt = _ffn(x1, w_ff1[l].astype(BF16), w_ff2[l].astype(BF16), ln2_g[l].reshape(1, -1),
                  ln2_b[l].reshape(1, -1), alpha, _tile(tokens, 1024), 1024)
    return xt.reshape(bsz, seq, d)
```

```python
import functools
import math

import jax
import jax.numpy as jnp
from jax import lax
from jax.experimental import pallas as pl
from jax.experimental.pallas import tpu as pltpu

F32 = jnp.float32
BF16 = jnp.bfloat16

A_HEADS = 8
A_HEAD_DIM = 64
A_WIDTH = A_HEADS * A_HEAD_DIM
A_DECAY_LORA = 64
A_ICLR_LORA = 64
A_GATE_LORA = 128
A_PROJ = 3 * A_WIDTH + A_DECAY_LORA + A_ICLR_LORA + A_GATE_LORA
A_GN_EPS = 64e-5
A_CHUNK = 64
A_GROUP = 2

B_HEADS = 4
B_QK_DIM = 128
B_V_DIM = 256
B_QK_WIDTH = B_HEADS * B_QK_DIM
B_V_WIDTH = B_HEADS * B_V_DIM
B_CHUNK = 128
B_ROPE_BASE = 10000.0
B_GN_EPS = 1e-5

C_WIDTH = 512
C_GROUP = 16
C_GROUPS = C_WIDTH // C_GROUP
C_STATE = 64
C_LANES = C_GROUPS * C_STATE
C_BLOCKS = 4
C_ULANES = C_WIDTH // C_BLOCKS
C_SLANES = C_LANES // C_BLOCKS

LN_EPS = 1e-5

MIX_COLS = A_PROJ + 2 * B_QK_WIDTH + 2 * B_V_WIDTH + C_WIDTH
U_BLK = (MIX_COLS - C_WIDTH) // (C_WIDTH // 2)
Q_BLK = A_PROJ // B_QK_DIM
K_BLK = Q_BLK + B_HEADS
V_BLK = (A_PROJ + 2 * B_QK_WIDTH) // B_V_DIM
G_BLK = V_BLK + B_HEADS
assert A_PROJ % B_QK_DIM == 0 and (A_PROJ + 2 * B_QK_WIDTH) % B_V_DIM == 0

V7X_VMEM_LIMIT_BYTES = 56 * 1024 * 1024


def _cparams(*sem):
    return pltpu.CompilerParams(dimension_semantics=sem, vmem_limit_bytes=V7X_VMEM_LIMIT_BYTES)


def _full(shape):
    n = len(shape)
    return pl.BlockSpec(shape, lambda *_: (0,) * n)


def _dot(a, b):
    return jnp.dot(a.astype(BF16), b.astype(BF16), preferred_element_type=F32)


def _dot_nt(a, b):
    return lax.dot_general(a.astype(BF16), b.astype(BF16), (((1,), (1,)), ((), ())),
                           preferred_element_type=F32)


def _dot_tn(a, b):
    return lax.dot_general(a.astype(BF16), b.astype(BF16), (((0,), (0,)), ((), ())),
                           preferred_element_type=F32)


def _split2(x):
    hi = x.astype(BF16)
    lo = (x - hi.astype(F32)).astype(BF16)
    return hi, lo


def _split3(x):
    hi = x.astype(BF16)
    r1 = x - hi.astype(F32)
    mid = r1.astype(BF16)
    lo = (r1 - mid.astype(F32)).astype(BF16)
    return hi, mid, lo


def _sigmoid(x):
    return 1.0 / (1.0 + jnp.exp(-x))


def _layer_norm(y, g, b):
    mu = jnp.mean(y, axis=-1, keepdims=True)
    d = y - mu
    var = jnp.mean(d * d, axis=-1, keepdims=True)
    return d * lax.rsqrt(var + LN_EPS) * g + b


def _proj_kernel(x_ref, w_ref, o_ref):
    o_ref[...] = jnp.dot(x_ref[...].astype(BF16), w_ref[...], preferred_element_type=F32)


def _proj(x2d, w_bf16, tm, tn):
    m, k = x2d.shape
    n = w_bf16.shape[1]
    return pl.pallas_call(
        _proj_kernel,
        grid=(m // tm, n // tn),
        in_specs=[pl.BlockSpec((tm, k), lambda i, j: (i, 0)),
                  pl.BlockSpec((k, tn), lambda i, j: (0, j))],
        out_specs=pl.BlockSpec((tm, tn), lambda i, j: (i, j)),
        out_shape=jax.ShapeDtypeStruct((m, n), F32),
        compiler_params=_cparams("parallel", "arbitrary"),
        name="proj",
    )(x2d, w_bf16)


def _rwkv_kernel(z_ref, mu_ref, w0_ref, a0_ref, lora_ref, g2_ref, kkp_ref, kap_ref, rkp_ref,
                 lng_ref, lnb_ref, ones_ref, tri_ref, y_ref,
                 carry_s, state_s, r_s, k_s, v_s, kn_s, ia_s, lw_s, o_s, rk_s, gate_s,
                 gam_s, x_s, m2_s, q_s, op_s):
    tc = z_ref.shape[0]
    n_chunks = tc // A_CHUNK
    w = A_WIDTH

    @pl.when(pl.program_id(1) == 0)
    def _():
        carry_s[...] = jnp.zeros_like(carry_s)
        state_s[...] = jnp.zeros_like(state_s)

    ones = ones_ref[...]

    def seg_sum(x):
        hi, lo = _split2(x)
        parts = []
        for j in range(x.shape[1] // 128):
            sl = slice(j * 128, (j + 1) * 128)
            parts.append(jnp.dot(hi[:, sl], ones, preferred_element_type=F32)
                         + jnp.dot(lo[:, sl], ones, preferred_element_type=F32))
        return jnp.concatenate(parts, axis=1)

    z = z_ref[...]
    rolled = pltpu.roll(z, 1, 0)
    rowid = lax.broadcasted_iota(jnp.int32, z.shape, 0)
    prev = jnp.where(rowid == 0, jnp.broadcast_to(carry_s[0:1, :], z.shape), rolled)
    zs = z + mu_ref[...] * (prev - z)
    carry_s[0:1, :] = z[tc - 1:tc, :]

    lz = zs[:, 3 * w:3 * w + 128]
    lane = lax.broadcasted_iota(jnp.int32, lz.shape, 1)
    lin = jnp.where(lane < A_DECAY_LORA, jnp.tanh(lz), lz)
    wa = _dot(lin, lora_ref[...])
    lw_s[...] = (-math.exp(-0.5)) * _sigmoid(w0_ref[...] + wa[:, :w])
    ia = _sigmoid(a0_ref[...] + wa[:, w:])
    ia_s[...] = ia
    gate_s[...] = _dot(_sigmoid(zs[:, 3 * w + 128:3 * w + 256]), g2_ref[...])

    r = zs[:, :w]
    k = zs[:, w:2 * w]
    v = zs[:, 2 * w:3 * w]
    kk = k * kkp_ref[...]
    kn_s[...] = kk / jnp.maximum(jnp.sqrt(seg_sum(kk * kk)), 1e-12)
    kmod = k * (1.0 + (ia - 1.0) * kap_ref[...])
    r_s[...] = r
    k_s[...] = kmod
    v_s[...] = v
    rk_s[...] = seg_sum(r * kmod * rkp_ref[...])

    tri = tri_ref[...]
    rid = lax.broadcasted_iota(jnp.int32, (A_CHUNK, A_CHUNK), 0)
    cid = lax.broadcasted_iota(jnp.int32, (A_CHUNK, A_CHUNK), 1)
    strict = rid > cid
    incl = rid >= cid
    eye = (rid == cid).astype(F32)
    n = A_HEAD_DIM

    heads = range(A_HEADS)
    hsl = [slice(h * n, (h + 1) * n) for h in heads]


    def state_free_part(gi, carry):
        items = []
        for j in range(A_GROUP):
            c = gi * A_GROUP + j
            rows = pl.ds(pl.multiple_of(c * A_CHUNK, A_CHUNK), A_CHUNK)
            lw = lw_s[rows, :]
            h3 = _split3(lw)
            cum = (jnp.dot(tri, h3[0], preferred_element_type=F32)
                   + jnp.dot(tri, h3[1], preferred_element_type=F32)
                   + jnp.dot(tri, h3[2], preferred_element_type=F32))
            e_in = jnp.exp(cum)
            e_ex = jnp.exp(cum - lw)
            e_ng = jnp.exp(-cum)
            kn = kn_s[rows, :]
            rt = r_s[rows, :] * e_in
            at = -kn * e_ex
            bt = kn * ia_s[rows, :] * e_ng
            kt = k_s[rows, :] * e_ng
            vv = v_s[rows, :]
            gam_s[pl.ds(c, 1), :] = e_in[A_CHUNK - 1:A_CHUNK, :]
            for h in heads:
                s = hsl[h]
                items.append((c * A_HEADS + h, at[:, s], rt[:, s], bt[:, s], kt[:, s], vv[:, s]))
        ids = range(len(items))
        idx = [it[0] for it in items]
        ah = [it[1] for it in items]
        rh = [it[2] for it in items]
        bh = [it[3] for it in items]
        kh = [it[4] for it in items]
        vh = [it[5] for it in items]
        ar = [jnp.concatenate([ah[i], rh[i]], axis=0) for i in ids]
        gb = [_dot_nt(ar[i], bh[i]) for i in ids]
        gk = [_dot_nt(ar[i], kh[i]) for i in ids]
        l_ab = [jnp.where(strict, gb[i][:A_CHUNK], 0.0) for i in ids]
        a_ak = [jnp.where(strict, gk[i][:A_CHUNK], 0.0) for i in ids]
        a_qb = [jnp.where(incl, gb[i][A_CHUNK:], 0.0) for i in ids]
        a_qk = [jnp.where(incl, gk[i][A_CHUNK:], 0.0) for i in ids]
        akv = [_dot(a_ak[i], vh[i]) for i in ids]
        ov = [_dot(a_qk[i], vh[i]) for i in ids]
        kv = [_dot_tn(vh[i], kh[i]) for i in ids]
        tinv = [eye + l_ab[i] for i in ids]
        p = l_ab
        for _ in range(5):
            p = [_dot(p[i], p[i]) for i in ids]
            tinv = [tinv[i] + _dot(tinv[i], p[i]) for i in ids]
        wm = [_dot(tinv[i], ah[i]) for i in ids]
        uv = [_dot(tinv[i], akv[i]) for i in ids]
        for i in ids:
            x_s[idx[i]] = _dot_tn(wm[i], bh[i])
        for i in ids:
            m2_s[idx[i]] = _dot_tn(uv[i], bh[i]) + kv[i]
        for i in ids:
            q_s[idx[i]] = rh[i] + _dot(a_qb[i], wm[i])
        for i in ids:
            op_s[idx[i]] = _dot(a_qb[i], uv[i]) + ov[i]
        return carry

    def state_part(c, carry):
        rows = pl.ds(pl.multiple_of(c * A_CHUNK, A_CHUNK), A_CHUNK)
        gam = gam_s[pl.ds(c, 1), :]
        s0 = [state_s[h] for h in heads]
        sx = [_dot(s0[h], x_s[c * A_HEADS + h]) for h in heads]
        o = [_dot_nt(q_s[c * A_HEADS + h], s0[h]) + op_s[c * A_HEADS + h] for h in heads]
        for h in heads:
            state_s[h] = (s0[h] + sx[h] + m2_s[c * A_HEADS + h]) * gam[:, hsl[h]]
        o_s[rows, :] = jnp.concatenate(o, axis=1)
        return carry

    lax.fori_loop(0, n_chunks // A_GROUP, state_free_part, 0)
    lax.fori_loop(0, n_chunks, state_part, 0)

    o = o_s[...]
    inv_n = 1.0 / n
    mean = seg_sum(o) * inv_n
    d = o - mean
    var = seg_sum(d * d) * inv_n
    on = d * lax.rsqrt(var + A_GN_EPS) * lng_ref[...] + lnb_ref[...]
    on = on + rk_s[...] * v_s[...]
    y_ref[...] = on * gate_s[...]


def _rwkv(zmix, bsz, seq, p, tc):
    zblk = 0
    row = lambda a: a.reshape(1, -1).astype(F32)
    w = A_WIDTH
    lora = jnp.zeros((128, 2 * w), F32)
    lora = lora.at[:A_DECAY_LORA, :w].set(p["a_w2"]).at[A_DECAY_LORA:, w:].set(p["a_a2"]).astype(BF16)
    hid = jnp.arange(128) // A_HEAD_DIM
    ones = (hid[:, None] == hid[None, :]).astype(BF16)
    ti = jnp.arange(A_CHUNK)
    tri = (ti[:, None] >= ti[None, :]).astype(BF16)
    small = [row(p["a_shift"]), row(p["a_w0"]), row(p["a_a0"]), lora, p["a_g2"].astype(BF16),
             row(p["a_kk"]), row(p["a_ka"]), row(p["a_rk"]), row(p["a_lnx_g"]), row(p["a_lnx_b"]),
             ones, tri]
    stage = pltpu.VMEM((tc, w), F32)
    n_chunks = tc // A_CHUNK
    per_head = pltpu.VMEM((n_chunks * A_HEADS, A_CHUNK, A_HEAD_DIM), F32)
    return pl.pallas_call(
        _rwkv_kernel,
        grid=(bsz, seq // tc),
        in_specs=[pl.BlockSpec((None, tc, A_PROJ), lambda b, i: (b, i, zblk))]
                 + [_full(a.shape) for a in small],
        out_specs=pl.BlockSpec((None, tc, w), lambda b, i: (b, i, 0)),
        out_shape=jax.ShapeDtypeStruct((bsz, seq, w), F32),
        scratch_shapes=[pltpu.VMEM((8, A_PROJ), F32),
                        pltpu.VMEM((A_HEADS, A_HEAD_DIM, A_HEAD_DIM), F32),
                        stage, stage, stage, stage, stage, stage, stage, stage, stage,
                        pltpu.VMEM((max(8, n_chunks), w), F32),
                        per_head, per_head, per_head, per_head],
        compiler_params=_cparams("parallel", "arbitrary"),
        name="rwkv7",
    )(zmix, *small)


def _ret_kernel(q_ref, k_ref, v_ref, g_ref, cos_ref, sin_ref, dmask_ref, qd_ref, kd_ref, cd_ref,
                y_ref, state_s):
    tc = q_ref.shape[0]
    n_chunks = tc // B_CHUNK

    @pl.when(pl.program_id(2) == 0)
    def _():
        state_s[...] = jnp.zeros_like(state_s)

    cos2 = cos_ref[...]
    sin2 = sin_ref[...]
    half = B_QK_DIM // 2

    def rope(t):
        return t * cos2 + pltpu.roll(t, half, 1) * sin2

    q = rope(q_ref[...])
    k = rope(k_ref[...]) * (B_QK_DIM ** -0.5)
    dmask = dmask_ref[...]
    qd = qd_ref[...]
    kd = kd_ref[...]
    cd = cd_ref[...]
    for c in range(n_chunks):
        rows = slice(c * B_CHUNK, (c + 1) * B_CHUNK)
        qc, kc, vc = q[rows], k[rows], v_ref[rows, :]
        scores = _dot_nt(qc, kc) * dmask
        st = state_s[...]
        o = _dot(scores, vc) + _dot(qc * qd, st)
        state_s[...] = cd * st + _dot_tn(kc * kd, vc)
        mu = jnp.mean(o, axis=-1, keepdims=True)
        d = o - mu
        var = jnp.mean(d * d, axis=-1, keepdims=True)
        on = d * lax.rsqrt(var + B_GN_EPS)
        g = g_ref[rows, :]
        y_ref[rows, :] = g * _sigmoid(g) * on


def _retention(zmix, bsz, seq, tc):
    f32 = F32
    pos = jnp.arange(seq, dtype=f32)
    half = B_QK_DIM // 2
    inv_freq = B_ROPE_BASE ** (-jnp.arange(half, dtype=f32) / half)
    ang = pos[:, None] * inv_freq[None, :]
    cos, sin = jnp.cos(ang), jnp.sin(ang)
    cos2 = jnp.concatenate([cos, cos], axis=1)
    sin2 = jnp.concatenate([-sin, sin], axis=1)
    log_gamma = jnp.log(1.0 - 2.0 ** (-5.0 - jnp.arange(B_HEADS, dtype=f32)))
    idx = jnp.arange(B_CHUNK, dtype=f32)
    rel = idx[:, None] - idx[None, :]
    dmask = jnp.where(rel >= 0, jnp.exp(log_gamma[:, None, None] * jnp.maximum(rel, 0.0)), 0.0)
    qd = jnp.broadcast_to(jnp.exp(log_gamma[:, None] * (idx + 1.0))[:, :, None],
                          (B_HEADS, B_CHUNK, B_QK_DIM))
    kd = jnp.broadcast_to(jnp.exp(log_gamma[:, None] * (B_CHUNK - 1.0 - idx))[:, :, None],
                          (B_HEADS, B_CHUNK, B_QK_DIM))
    cd = jnp.broadcast_to(jnp.exp(log_gamma * B_CHUNK)[:, None, None], (B_HEADS, B_QK_DIM, B_V_DIM))
    head_tab = lambda a, b: pl.BlockSpec((None, a, b), lambda bb, h, i: (h, 0, 0))
    return pl.pallas_call(
        _ret_kernel,
        grid=(bsz, B_HEADS, seq // tc),
        in_specs=[pl.BlockSpec((None, tc, B_QK_DIM), lambda b, h, i: (b, i, Q_BLK + h)),
                  pl.BlockSpec((None, tc, B_QK_DIM), lambda b, h, i: (b, i, K_BLK + h)),
                  pl.BlockSpec((None, tc, B_V_DIM), lambda b, h, i: (b, i, V_BLK + h)),
                  pl.BlockSpec((None, tc, B_V_DIM), lambda b, h, i: (b, i, G_BLK + h)),
                  pl.BlockSpec((tc, B_QK_DIM), lambda b, h, i: (i, 0)),
                  pl.BlockSpec((tc, B_QK_DIM), lambda b, h, i: (i, 0)),
                  head_tab(B_CHUNK, B_CHUNK), head_tab(B_CHUNK, B_QK_DIM),
                  head_tab(B_CHUNK, B_QK_DIM), head_tab(B_QK_DIM, B_V_DIM)],
        out_specs=pl.BlockSpec((None, tc, B_V_DIM), lambda b, h, i: (b, i, h)),
        out_shape=jax.ShapeDtypeStruct((bsz, seq, B_V_WIDTH), F32),
        scratch_shapes=[pltpu.VMEM((B_QK_DIM, B_V_DIM), F32)],
        compiler_params=_cparams("parallel", "parallel", "arbitrary"),
        name="retention",
    )(zmix, zmix, zmix, zmix, cos2, sin2, dmask, qd, kd, cd)


def _s5_kernel(ua_ref, ub_ref, bbr_ref, bbi_ref, ar_ref, ai_ref, cr_ref, ci_ref, d_ref, wg_ref, bg_ref,
               y_ref, xr_s, xi_s, sr_s, si_s):
    bsz, tt, _ = ua_ref.shape

    @pl.when(pl.program_id(0) == 0)
    def _():
        sr_s[...] = jnp.zeros_like(sr_s)
        si_s[...] = jnp.zeros_like(si_s)

    u = jnp.concatenate([ua_ref[...], ub_ref[...]], axis=2)
    u = pltpu.einshape("btc->tbc", u).reshape(tt * bsz, C_WIDTH)
    ub = u.astype(BF16)
    blk = [slice(m * C_SLANES, (m + 1) * C_SLANES) for m in range(C_BLOCKS)]

    for m in range(C_BLOCKS):
        um = ub[:, m * C_ULANES:(m + 1) * C_ULANES]
        xr_s[:, blk[m]] = jnp.dot(um, bbr_ref[m], preferred_element_type=F32)
        xi_s[:, blk[m]] = jnp.dot(um, bbi_ref[m], preferred_element_type=F32)

    for m in range(C_BLOCKS):
        cols = blk[m]
        ar = jnp.broadcast_to(ar_ref[:, cols], (bsz, C_SLANES))
        ai = jnp.broadcast_to(ai_ref[:, cols], (bsz, C_SLANES))

        def step(t, carry):
            xr, xi = carry
            rows = pl.ds(pl.multiple_of(t * bsz, bsz), bsz)
            nr = ar * xr - ai * xi + xr_s[rows, cols]
            ni = ar * xi + ai * xr + xi_s[rows, cols]
            xr_s[rows, cols] = nr
            xi_s[rows, cols] = ni
            return nr, ni

        xr, xi = lax.fori_loop(0, tt, step, (sr_s[:, cols], si_s[:, cols]), unroll=4)
        sr_s[:, cols] = xr
        si_s[:, cols] = xi

    parts = [_dot(xr_s[:, blk[m]], cr_ref[m]) - _dot(xi_s[:, blk[m]], ci_ref[m]) for m in range(C_BLOCKS)]
    y = jnp.concatenate(parts, axis=1) + d_ref[...] * u
    y = jax.nn.gelu(y)
    y = y * _sigmoid(_dot(y, wg_ref[...]) + bg_ref[...])
    y_ref[...] = pltpu.einshape("tbc->btc", y.reshape(tt, bsz, C_WIDTH))


def _s5(zmix, bsz, seq, p, tt):
    f32 = F32
    dt = jnp.exp(p["c_log_dt"].astype(f32))[:, None]
    lr, li = p["c_lam_re"].astype(f32), p["c_lam_im"].astype(f32)
    mag = jnp.exp(lr * dt)
    ab_re, ab_im = mag * jnp.cos(li * dt), mag * jnp.sin(li * dt)
    den = lr * lr + li * li
    f_re = ((ab_re - 1.0) * lr + ab_im * li) / den
    f_im = (ab_im * lr - (ab_re - 1.0) * li) / den
    bre, bim = p["c_b_re"].astype(f32), p["c_b_im"].astype(f32)
    bb_re = f_re[..., None] * bre - f_im[..., None] * bim
    bb_im = f_re[..., None] * bim + f_im[..., None] * bre
    eye = jnp.eye(C_GROUPS, dtype=f32)

    def in_blocks(bb):
        full = jnp.einsum("gpc,gh->gchp", bb, eye).reshape(C_WIDTH, C_LANES)
        return jnp.stack([full[m * C_ULANES:(m + 1) * C_ULANES, m * C_SLANES:(m + 1) * C_SLANES]
                          for m in range(C_BLOCKS)]).astype(BF16)

    def out_blocks(cc):
        full = jnp.einsum("gcp,gh->gphc", cc.astype(f32), eye).reshape(C_LANES, C_WIDTH)
        return jnp.stack([full[m * C_SLANES:(m + 1) * C_SLANES, m * C_ULANES:(m + 1) * C_ULANES]
                          for m in range(C_BLOCKS)]).astype(BF16)

    consts = [in_blocks(bb_re), in_blocks(bb_im), ab_re.reshape(1, C_LANES), ab_im.reshape(1, C_LANES),
              out_blocks(p["c_c_re"]), out_blocks(p["c_c_im"]), p["c_d"].reshape(1, C_WIDTH).astype(f32),
              p["c_w_glu"].astype(BF16), p["c_b_glu"].reshape(1, C_WIDTH).astype(f32)]
    rows = tt * bsz
    half = C_WIDTH // 2
    return pl.pallas_call(
        _s5_kernel,
        grid=(seq // tt,),
        in_specs=[pl.BlockSpec((bsz, tt, half), lambda i: (0, i, U_BLK)),
                  pl.BlockSpec((bsz, tt, half), lambda i: (0, i, U_BLK + 1))]
                 + [_full(a.shape) for a in consts],
        out_specs=pl.BlockSpec((bsz, tt, C_WIDTH), lambda i: (0, i, 0)),
        out_shape=jax.ShapeDtypeStruct((bsz, seq, C_WIDTH), F32),
        scratch_shapes=[pltpu.VMEM((rows, C_LANES), F32), pltpu.VMEM((rows, C_LANES), F32),
                        pltpu.VMEM((bsz, C_LANES), F32), pltpu.VMEM((bsz, C_LANES), F32)],
        compiler_params=_cparams("arbitrary"),
        name="s5",
    )(zmix, zmix, *consts)


def _merge_kernel(alpha, x_ref, ya_ref, yb_ref, yc_ref, wgate_ref, bgate_ref, wba_ref, wbb_ref, wbc_ref,
                  wout_ref, g_ref, b_ref, o_ref):
    d = x_ref.shape[1]
    x = x_ref[...]
    gates = _sigmoid(_dot(x, wgate_ref[...]) + bgate_ref[...])
    merged = (gates[:, :d] * _dot(ya_ref[...], wba_ref[...])
              + gates[:, d:2 * d] * _dot(yb_ref[...], wbb_ref[...])
              + gates[:, 2 * d:] * _dot(yc_ref[...], wbc_ref[...]))
    o_ref[...] = _layer_norm(alpha * x + _dot(merged, wout_ref[...]), g_ref[...], b_ref[...])


def _merge(x2d, ya, yb, yc, wgate, bgate, wb, wout, g, b, alpha, tm):
    m, d = x2d.shape
    wba, wbb, wbc = wb[:A_WIDTH], wb[A_WIDTH:A_WIDTH + B_V_WIDTH], wb[A_WIDTH + B_V_WIDTH:]
    consts = [wgate, bgate, wba, wbb, wbc, wout, g, b]
    tile = lambda n: pl.BlockSpec((tm, n), lambda i: (i, 0))
    return pl.pallas_call(
        functools.partial(_merge_kernel, alpha),
        grid=(m // tm,),
        in_specs=[tile(d), tile(A_WIDTH), tile(B_V_WIDTH), tile(C_WIDTH)] + [_full(a.shape) for a in consts],
        out_specs=tile(d),
        out_shape=jax.ShapeDtypeStruct((m, d), F32),
        compiler_params=_cparams("parallel"),
        name="merge",
    )(x2d, ya, yb, yc, *consts)


def _ffn_kernel(alpha, x_ref, w1_ref, w2_ref, g_ref, b_ref, o_ref, acc_s):
    j = pl.program_id(1)
    x = x_ref[...]
    h = jnp.maximum(_dot(x, w1_ref[...]), 0.0)
    part = _dot(h * h, w2_ref[...])

    @pl.when(j == 0)
    def _():
        acc_s[...] = part

    @pl.when(j > 0)
    def _():
        acc_s[...] += part

    @pl.when(j == pl.num_programs(1) - 1)
    def _():
        o_ref[...] = _layer_norm(alpha * x + acc_s[...], g_ref[...], b_ref[...])


def _ffn(x2d, w1, w2, g, b, alpha, tm, tf):
    m, d = x2d.shape
    dff = w1.shape[1]
    return pl.pallas_call(
        functools.partial(_ffn_kernel, alpha),
        grid=(m // tm, dff // tf),
        in_specs=[pl.BlockSpec((tm, d), lambda i, j: (i, 0)),
                  pl.BlockSpec((d, tf), lambda i, j: (0, j)),
                  pl.BlockSpec((tf, d), lambda i, j: (j, 0)),
                  _full(g.shape), _full(b.shape)],
        out_specs=pl.BlockSpec((tm, d), lambda i, j: (i, 0)),
        out_shape=jax.ShapeDtypeStruct((m, d), F32),
        scratch_shapes=[pltpu.VMEM((tm, d), F32)],
        compiler_params=_cparams("parallel", "arbitrary"),
        name="ffn",
    )(x2d, w1, w2, g, b)


def _tile(n, want):
    t = min(n, want)
    assert n % t == 0, (n, want)
    return t


def kernel(x, w_in, b_gate, a_shift, a_w0, a_w2, a_a0, a_a2, a_g2, a_kk, a_ka, a_rk, a_lnx_g, a_lnx_b, c_lam_re, c_lam_im, c_log_dt, c_b_re, c_b_im, c_c_re, c_c_im, c_d, c_w_glu, c_b_glu, w_branch, w_out, ln1_g, ln1_b, w_ff1, w_ff2, ln2_g, ln2_b):
    bsz, seq, d = x.shape
    depth = w_in.shape[0]
    alpha = (2.0 * depth) ** 0.25
    tokens = bsz * seq
    per_layer = dict(a_shift=a_shift, a_w0=a_w0, a_w2=a_w2, a_a0=a_a0, a_a2=a_a2, a_g2=a_g2, a_kk=a_kk,
                     a_ka=a_ka, a_rk=a_rk, a_lnx_g=a_lnx_g, a_lnx_b=a_lnx_b, c_lam_re=c_lam_re,
                     c_lam_im=c_lam_im, c_log_dt=c_log_dt, c_b_re=c_b_re, c_b_im=c_b_im, c_c_re=c_c_re,
                     c_c_im=c_c_im, c_d=c_d, c_w_glu=c_w_glu, c_b_glu=c_b_glu)
    xt = x.reshape(tokens, d)
    for l in range(depth):
        p = {k: v[l] for k, v in per_layer.items()}
        wl = w_in[l]
        w_mix = wl[:, :MIX_COLS].astype(BF16)
        w_gate = wl[:, MIX_COLS:].astype(BF16)
        zmix = _proj(xt, w_mix, _tile(tokens, 512), A_PROJ).reshape(bsz, seq, MIX_COLS)
        ya = _rwkv(zmix, bsz, seq, p, _tile(seq, 256))
        yb = _retention(zmix, bsz, seq, _tile(seq, 512))
        yc = _s5(zmix, bsz, seq, p, _tile(seq, 64))
        x1 = _merge(xt, ya.reshape(tokens, -1), yb.reshape(tokens, -1), yc.reshape(tokens, -1),
                    w_gate, b_gate[l].reshape(1, -1), w_branch[l].astype(BF16), w_out[l].astype(BF16),
                    ln1_g[l].reshape(1, -1), ln1_b[l].reshape(1, -1), alpha, _tile(tokens, 256))
        xt = _ffn(x1, w_ff1[l].astype(BF16), w_ff2[l].astype(BF16), ln2_g[l].reshape(1, -1),
                  ln2_b[l].reshape(1, -1), alpha, _tile(tokens, 1024), 1024)
    return xt.reshape(bsz, seq, d)
```

```python
import functools
import math

import jax
import jax.numpy as jnp
from jax import lax
from jax.experimental import pallas as pl
from jax.experimental.pallas import tpu as pltpu

F32 = jnp.float32
BF16 = jnp.bfloat16

A_HEADS = 8
A_HEAD_DIM = 64
A_WIDTH = A_HEADS * A_HEAD_DIM
A_DECAY_LORA = 64
A_ICLR_LORA = 64
A_GATE_LORA = 128
A_PROJ = 3 * A_WIDTH + A_DECAY_LORA + A_ICLR_LORA + A_GATE_LORA
A_GN_EPS = 64e-5
A_CHUNK = 64
A_GROUP = 4
A_GLANES = 256

B_HEADS = 4
B_QK_DIM = 128
B_V_DIM = 256
B_QK_WIDTH = B_HEADS * B_QK_DIM
B_V_WIDTH = B_HEADS * B_V_DIM
B_CHUNK = 128
B_ROPE_BASE = 10000.0
B_GN_EPS = 1e-5

C_WIDTH = 512
C_GROUP = 16
C_GROUPS = C_WIDTH // C_GROUP
C_STATE = 64
C_LANES = C_GROUPS * C_STATE
C_BLOCKS = 4
C_ULANES = C_WIDTH // C_BLOCKS
C_SLANES = C_LANES // C_BLOCKS

LN_EPS = 1e-5

RET_COLS = 2 * B_QK_WIDTH + 2 * B_V_WIDTH
MIX_COLS = RET_COLS + C_WIDTH + A_PROJ
RET_BLK = 0
U_BLK = RET_COLS // C_WIDTH
Z_BLK = (RET_COLS + C_WIDTH) // A_PROJ
assert RET_COLS % C_WIDTH == 0 and (RET_COLS + C_WIDTH) % A_PROJ == 0

V7X_VMEM_LIMIT_BYTES = 56 * 1024 * 1024


def _cparams(*sem):
    return pltpu.CompilerParams(dimension_semantics=sem, vmem_limit_bytes=V7X_VMEM_LIMIT_BYTES)


def _full(shape):
    n = len(shape)
    return pl.BlockSpec(shape, lambda *_: (0,) * n)


def _dot(a, b):
    return jnp.dot(a.astype(BF16), b.astype(BF16), preferred_element_type=F32)


def _dot_nt(a, b):
    return lax.dot_general(a.astype(BF16), b.astype(BF16), (((1,), (1,)), ((), ())),
                           preferred_element_type=F32)


def _dot_tn(a, b):
    return lax.dot_general(a.astype(BF16), b.astype(BF16), (((0,), (0,)), ((), ())),
                           preferred_element_type=F32)


def _split2(x):
    hi = x.astype(BF16)
    lo = (x - hi.astype(F32)).astype(BF16)
    return hi, lo


def _split3(x):
    hi = x.astype(BF16)
    r1 = x - hi.astype(F32)
    mid = r1.astype(BF16)
    lo = (r1 - mid.astype(F32)).astype(BF16)
    return hi, mid, lo


def _sigmoid(x):
    return 1.0 / (1.0 + jnp.exp(-x))


def _layer_norm(y, g, b):
    mu = jnp.mean(y, axis=-1, keepdims=True)
    d = y - mu
    var = jnp.mean(d * d, axis=-1, keepdims=True)
    return d * lax.rsqrt(var + LN_EPS) * g + b


def _proj_kernel(x_ref, w_ref, o_ref):
    o_ref[...] = jnp.dot(x_ref[...].astype(BF16), w_ref[...], preferred_element_type=F32)


def _proj(x2d, w_bf16, tm, tn):
    m, k = x2d.shape
    n = w_bf16.shape[1]
    return pl.pallas_call(
        _proj_kernel,
        grid=(m // tm, n // tn),
        in_specs=[pl.BlockSpec((tm, k), lambda i, j: (i, 0)),
                  pl.BlockSpec((k, tn), lambda i, j: (0, j))],
        out_specs=pl.BlockSpec((tm, tn), lambda i, j: (i, j)),
        out_shape=jax.ShapeDtypeStruct((m, n), F32),
        compiler_params=_cparams("parallel", "arbitrary"),
        name="proj",
    )(x2d, w_bf16)


def _rwkv_kernel(z_ref, mu_ref, w0_ref, a0_ref, lora_ref, g2_ref, kkp_ref, kap_ref, rkp_ref,
                 lng_ref, lnb_ref, ones_ref, tri_ref, y_ref,
                 carry_s, state_s, r_s, k_s, v_s, kn_s, ia_s, lw_s, o_s, rk_s, gate_s,
                 gam_s, x_s, m2_s, q_s, op_s):
    tc = z_ref.shape[0]
    n_chunks = tc // A_CHUNK
    w = A_WIDTH

    @pl.when(pl.program_id(1) == 0)
    def _():
        carry_s[...] = jnp.zeros_like(carry_s)
        state_s[...] = jnp.zeros_like(state_s)

    ones = ones_ref[...]

    def seg_sum(x):
        hi, lo = _split2(x)
        parts = []
        for j in range(x.shape[1] // 128):
            sl = slice(j * 128, (j + 1) * 128)
            parts.append(jnp.dot(hi[:, sl], ones, preferred_element_type=F32)
                         + jnp.dot(lo[:, sl], ones, preferred_element_type=F32))
        return jnp.concatenate(parts, axis=1)

    z = z_ref[...]
    rolled = pltpu.roll(z, 1, 0)
    rowid = lax.broadcasted_iota(jnp.int32, z.shape, 0)
    prev = jnp.where(rowid == 0, jnp.broadcast_to(carry_s[0:1, :], z.shape), rolled)
    zs = z + mu_ref[...] * (prev - z)
    carry_s[0:1, :] = z[tc - 1:tc, :]

    lz = zs[:, 3 * w:3 * w + 128]
    lane = lax.broadcasted_iota(jnp.int32, lz.shape, 1)
    lin = jnp.where(lane < A_DECAY_LORA, jnp.tanh(lz), lz)
    wa = _dot(lin, lora_ref[...])
    lw_s[...] = (-math.exp(-0.5)) * _sigmoid(w0_ref[...] + wa[:, :w])
    ia = _sigmoid(a0_ref[...] + wa[:, w:])
    ia_s[...] = ia
    gate_s[...] = _dot(_sigmoid(zs[:, 3 * w + 128:3 * w + 256]), g2_ref[...])

    r = zs[:, :w]
    k = zs[:, w:2 * w]
    v = zs[:, 2 * w:3 * w]
    kk = k * kkp_ref[...]
    kn_s[...] = kk / jnp.maximum(jnp.sqrt(seg_sum(kk * kk)), 1e-12)
    kmod = k * (1.0 + (ia - 1.0) * kap_ref[...])
    r_s[...] = r
    k_s[...] = kmod
    v_s[...] = v
    rk_s[...] = seg_sum(r * kmod * rkp_ref[...])

    tri = tri_ref[...]
    gw = A_GLANES
    hpg = gw // A_HEAD_DIM
    rid = lax.broadcasted_iota(jnp.int32, (A_CHUNK, gw), 0)
    cid = lax.broadcasted_iota(jnp.int32, (A_CHUNK, gw), 1) % A_HEAD_DIM
    strict = rid > cid
    incl = rid >= cid
    eye = (rid == cid).astype(F32)
    brow = lax.broadcasted_iota(jnp.int32, (gw, gw), 0) // A_HEAD_DIM
    bcol = lax.broadcasted_iota(jnp.int32, (gw, gw), 1) // A_HEAD_DIM
    same_head = brow == bcol
    same_head_bf = same_head.astype(BF16)
    n = A_HEAD_DIM
    groups = [slice(g * gw, (g + 1) * gw) for g in range(w // gw)]

    def bd(x):
        xb = x.astype(BF16)
        return jnp.concatenate([xb] * hpg, axis=0) * same_head_bf


    group = min(A_GROUP, n_chunks)
    assert n_chunks % group == 0

    def state_free_part(gi, carry):
        items = []
        for j in range(group):
            c = gi * group + j
            rows = pl.ds(pl.multiple_of(c * A_CHUNK, A_CHUNK), A_CHUNK)
            lw = lw_s[rows, :]
            h3 = _split3(lw)
            cum = (jnp.dot(tri, h3[0], preferred_element_type=F32)
                   + jnp.dot(tri, h3[1], preferred_element_type=F32)
                   + jnp.dot(tri, h3[2], preferred_element_type=F32))
            e_in = jnp.exp(cum)
            e_ex = jnp.exp(cum - lw)
            e_ng = jnp.exp(-cum)
            kn = kn_s[rows, :]
            rt = r_s[rows, :] * e_in
            at = -kn * e_ex
            bt = kn * ia_s[rows, :] * e_ng
            kt = k_s[rows, :] * e_ng
            vv = v_s[rows, :]
            gam_s[pl.ds(c, 1), :] = e_in[A_CHUNK - 1:A_CHUNK, :]
            for g, s in enumerate(groups):
                items.append((c * len(groups) + g, at[:, s], rt[:, s], bt[:, s], kt[:, s], vv[:, s]))
        ids = range(len(items))
        idx = [it[0] for it in items]
        a_ = [it[1] for it in items]
        r_ = [it[2] for it in items]
        b_ = [it[3] for it in items]
        k_ = [it[4] for it in items]
        v_ = [it[5] for it in items]
        ar = [jnp.concatenate([a_[i], r_[i]], axis=0) for i in ids]
        gb = [_dot_nt(ar[i], bd(b_[i])) for i in ids]
        gk = [_dot_nt(ar[i], bd(k_[i])) for i in ids]
        l_ab = [jnp.where(strict, gb[i][:A_CHUNK], 0.0) for i in ids]
        a_qb = [jnp.where(incl, gb[i][A_CHUNK:], 0.0) for i in ids]
        akq = [jnp.concatenate([jnp.where(strict, gk[i][:A_CHUNK], 0.0),
                                jnp.where(incl, gk[i][A_CHUNK:], 0.0)], axis=0) for i in ids]
        akqv = [_dot(akq[i], bd(v_[i])) for i in ids]
        akv = [akqv[i][:A_CHUNK] for i in ids]
        ov = [akqv[i][A_CHUNK:] for i in ids]
        tinv = [eye + l_ab[i] for i in ids]
        p = [_dot(l_ab[i], bd(l_ab[i])) for i in ids]
        for _ in range(4):
            tp = [_dot(jnp.concatenate([tinv[i], p[i]], axis=0), bd(p[i])) for i in ids]
            tinv = [tinv[i] + tp[i][:A_CHUNK] for i in ids]
            p = [tp[i][A_CHUNK:] for i in ids]
        tinv = [tinv[i] + _dot(tinv[i], bd(p[i])) for i in ids]
        wu = [_dot(tinv[i], jnp.concatenate([bd(a_[i]), bd(akv[i])], axis=1)) for i in ids]
        wm = [wu[i][:, :gw] for i in ids]
        uv = [wu[i][:, gw:] for i in ids]
        for i in ids:
            x_s[idx[i]] = jnp.where(same_head, _dot_tn(wm[i], b_[i]), 0.0)
        for i in ids:
            m2_s[idx[i]] = jnp.where(same_head,
                                     _dot_tn(jnp.concatenate([uv[i], v_[i]], axis=0),
                                             jnp.concatenate([b_[i], k_[i]], axis=0)), 0.0)
        qo = [_dot(a_qb[i], jnp.concatenate([bd(wm[i]), bd(uv[i])], axis=1)) for i in ids]
        for i in ids:
            q_s[idx[i]] = r_[i] + qo[i][:, :gw]
        for i in ids:
            op_s[idx[i]] = qo[i][:, gw:] + ov[i]
        return carry

    def state_part(c, carry):
        rows = pl.ds(pl.multiple_of(c * A_CHUNK, A_CHUNK), A_CHUNK)
        gam = gam_s[pl.ds(c, 1), :]
        gids = range(len(groups))
        s0 = [state_s[g] for g in gids]
        sx = [_dot(s0[g], x_s[c * len(groups) + g]) for g in gids]
        o = [_dot_nt(q_s[c * len(groups) + g], s0[g]) + op_s[c * len(groups) + g] for g in gids]
        for g in gids:
            state_s[g] = (s0[g] + sx[g] + m2_s[c * len(groups) + g]) * gam[:, groups[g]]
        o_s[rows, :] = jnp.concatenate(o, axis=1)
        return carry

    lax.fori_loop(0, n_chunks // group, state_free_part, 0)
    lax.fori_loop(0, n_chunks, state_part, 0)

    o = o_s[...]
    inv_n = 1.0 / n
    mean = seg_sum(o) * inv_n
    d = o - mean
    var = seg_sum(d * d) * inv_n
    on = d * lax.rsqrt(var + A_GN_EPS) * lng_ref[...] + lnb_ref[...]
    on = on + rk_s[...] * v_s[...]
    y_ref[...] = on * gate_s[...]


def _rwkv(zmix, bsz, seq, p, tc):
    zblk = Z_BLK
    row = lambda a: a.reshape(1, -1).astype(F32)
    w = A_WIDTH
    lora = jnp.zeros((128, 2 * w), F32)
    lora = lora.at[:A_DECAY_LORA, :w].set(p["a_w2"]).at[A_DECAY_LORA:, w:].set(p["a_a2"]).astype(BF16)
    hid = jnp.arange(128) // A_HEAD_DIM
    ones = (hid[:, None] == hid[None, :]).astype(BF16)
    ti = jnp.arange(A_CHUNK)
    tri = (ti[:, None] >= ti[None, :]).astype(BF16)
    small = [row(p["a_shift"]), row(p["a_w0"]), row(p["a_a0"]), lora, p["a_g2"].astype(BF16),
             row(p["a_kk"]), row(p["a_ka"]), row(p["a_rk"]), row(p["a_lnx_g"]), row(p["a_lnx_b"]),
             ones, tri]
    stage = pltpu.VMEM((tc, w), F32)
    n_chunks = tc // A_CHUNK
    n_groups = A_WIDTH // A_GLANES
    per_sq = pltpu.VMEM((n_chunks * n_groups, A_GLANES, A_GLANES), F32)
    per_row = pltpu.VMEM((n_chunks * n_groups, A_CHUNK, A_GLANES), F32)
    return pl.pallas_call(
        _rwkv_kernel,
        grid=(bsz, seq // tc),
        in_specs=[pl.BlockSpec((None, tc, A_PROJ), lambda b, i: (b, i, zblk))]
                 + [_full(a.shape) for a in small],
        out_specs=pl.BlockSpec((None, tc, w), lambda b, i: (b, i, 0)),
        out_shape=jax.ShapeDtypeStruct((bsz, seq, w), F32),
        scratch_shapes=[pltpu.VMEM((8, A_PROJ), F32),
                        pltpu.VMEM((A_WIDTH // A_GLANES, A_GLANES, A_GLANES), F32),
                        stage, stage, stage, stage, stage, stage, stage, stage, stage,
                        pltpu.VMEM((max(8, n_chunks), w), F32),
                        per_sq, per_sq, per_row, per_row],
        compiler_params=_cparams("parallel", "arbitrary"),
        name="rwkv7",
    )(zmix, *small)


def _ret_kernel(z_ref, cos_ref, sin_ref, dmask_ref, qd_ref, kd_ref, cd_ref,
                y_ref, state_s, qr_s, kr_s, upd_s, st_s):
    tc = z_ref.shape[0]
    n_chunks = tc // B_CHUNK
    k_off, v_off, g_off = B_QK_WIDTH, 2 * B_QK_WIDTH, 2 * B_QK_WIDTH + B_V_WIDTH

    @pl.when(pl.program_id(1) == 0)
    def _():
        state_s[...] = jnp.zeros_like(state_s)

    cos2 = cos_ref[...]
    sin2 = sin_ref[...]
    half = B_QK_DIM // 2

    def rope(t):
        return t * cos2 + pltpu.roll(t, half, 1) * sin2

    for h in range(B_HEADS):
        qs = slice(h * B_QK_DIM, (h + 1) * B_QK_DIM)
        qr_s[:, qs] = rope(z_ref[:, qs])
        kr_s[:, qs] = rope(z_ref[:, k_off + h * B_QK_DIM:k_off + (h + 1) * B_QK_DIM]) * (B_QK_DIM ** -0.5)

    items = [(c, h) for c in range(n_chunks) for h in range(B_HEADS)]
    rows = lambda c: slice(c * B_CHUNK, (c + 1) * B_CHUNK)
    qcol = lambda h: slice(h * B_QK_DIM, (h + 1) * B_QK_DIM)
    vcol = lambda h: slice(v_off + h * B_V_DIM, v_off + (h + 1) * B_V_DIM)

    for c, h in items:
        upd_s[c * B_HEADS + h] = _dot_tn(kr_s[rows(c), qcol(h)] * kd_ref[h], z_ref[rows(c), vcol(h)])
    for h in range(B_HEADS):
        st = state_s[h]
        for c in range(n_chunks):
            st_s[c * B_HEADS + h] = st
            st = cd_ref[h] * st + upd_s[c * B_HEADS + h]
        state_s[h] = st

    for c, h in items:
        qc = qr_s[rows(c), qcol(h)]
        scores = _dot_nt(qc, kr_s[rows(c), qcol(h)]) * dmask_ref[h]
        o = _dot(scores, z_ref[rows(c), vcol(h)]) + _dot(qc * qd_ref[h], st_s[c * B_HEADS + h])
        mu = jnp.mean(o, axis=-1, keepdims=True)
        d = o - mu
        var = jnp.mean(d * d, axis=-1, keepdims=True)
        on = d * lax.rsqrt(var + B_GN_EPS)
        g = z_ref[rows(c), g_off + h * B_V_DIM:g_off + (h + 1) * B_V_DIM]
        y_ref[rows(c), h * B_V_DIM:(h + 1) * B_V_DIM] = g * _sigmoid(g) * on


def _retention(zmix, bsz, seq, tc):
    f32 = F32
    pos = jnp.arange(seq, dtype=f32)
    half = B_QK_DIM // 2
    inv_freq = B_ROPE_BASE ** (-jnp.arange(half, dtype=f32) / half)
    ang = pos[:, None] * inv_freq[None, :]
    cos, sin = jnp.cos(ang), jnp.sin(ang)
    cos2 = jnp.concatenate([cos, cos], axis=1)
    sin2 = jnp.concatenate([-sin, sin], axis=1)
    log_gamma = jnp.log(1.0 - 2.0 ** (-5.0 - jnp.arange(B_HEADS, dtype=f32)))
    idx = jnp.arange(B_CHUNK, dtype=f32)
    rel = idx[:, None] - idx[None, :]
    dmask = jnp.where(rel >= 0, jnp.exp(log_gamma[:, None, None] * jnp.maximum(rel, 0.0)), 0.0)
    qd = jnp.broadcast_to(jnp.exp(log_gamma[:, None] * (idx + 1.0))[:, :, None],
                          (B_HEADS, B_CHUNK, B_QK_DIM))
    kd = jnp.broadcast_to(jnp.exp(log_gamma[:, None] * (B_CHUNK - 1.0 - idx))[:, :, None],
                          (B_HEADS, B_CHUNK, B_QK_DIM))
    cd = jnp.broadcast_to(jnp.exp(log_gamma * B_CHUNK)[:, None, None], (B_HEADS, B_QK_DIM, B_V_DIM))
    tabs = [dmask, qd, kd, cd]
    n_items = (tc // B_CHUNK) * B_HEADS
    per_item = pltpu.VMEM((n_items, B_QK_DIM, B_V_DIM), F32)
    return pl.pallas_call(
        _ret_kernel,
        grid=(bsz, seq // tc),
        in_specs=[pl.BlockSpec((None, tc, RET_COLS), lambda b, i: (b, i, RET_BLK)),
                  pl.BlockSpec((tc, B_QK_DIM), lambda b, i: (i, 0)),
                  pl.BlockSpec((tc, B_QK_DIM), lambda b, i: (i, 0))] + [_full(a.shape) for a in tabs],
        out_specs=pl.BlockSpec((None, tc, B_V_WIDTH), lambda b, i: (b, i, 0)),
        out_shape=jax.ShapeDtypeStruct((bsz, seq, B_V_WIDTH), F32),
        scratch_shapes=[pltpu.VMEM((B_HEADS, B_QK_DIM, B_V_DIM), F32),
                        pltpu.VMEM((tc, B_QK_WIDTH), F32), pltpu.VMEM((tc, B_QK_WIDTH), F32),
                        per_item, per_item],
        compiler_params=_cparams("parallel", "arbitrary"),
        name="retention",
    )(zmix, cos2, sin2, *tabs)


def _s5_kernel(u_ref, bbr_ref, bbi_ref, ar_ref, ai_ref, cr_ref, ci_ref, d_ref, wg_ref, bg_ref,
               y_ref, xr_s, xi_s, sr_s, si_s):
    bsz, tt, _ = u_ref.shape

    @pl.when(pl.program_id(0) == 0)
    def _():
        sr_s[...] = jnp.zeros_like(sr_s)
        si_s[...] = jnp.zeros_like(si_s)

    u = pltpu.einshape("btc->tbc", u_ref[...]).reshape(tt * bsz, C_WIDTH)
    ub = u.astype(BF16)
    blk = [slice(m * C_SLANES, (m + 1) * C_SLANES) for m in range(C_BLOCKS)]

    for m in range(C_BLOCKS):
        um = ub[:, m * C_ULANES:(m + 1) * C_ULANES]
        xr_s[:, blk[m]] = jnp.dot(um, bbr_ref[m], preferred_element_type=F32)
        xi_s[:, blk[m]] = jnp.dot(um, bbi_ref[m], preferred_element_type=F32)

    for m in range(C_BLOCKS):
        cols = blk[m]
        ar = jnp.broadcast_to(ar_ref[:, cols], (bsz, C_SLANES))
        ai = jnp.broadcast_to(ai_ref[:, cols], (bsz, C_SLANES))

        def step(t, carry):
            xr, xi = carry
            rows = pl.ds(pl.multiple_of(t * bsz, bsz), bsz)
            nr = ar * xr - ai * xi + xr_s[rows, cols]
            ni = ar * xi + ai * xr + xi_s[rows, cols]
            xr_s[rows, cols] = nr
            xi_s[rows, cols] = ni
            return nr, ni

        xr, xi = lax.fori_loop(0, tt, step, (sr_s[:, cols], si_s[:, cols]), unroll=4)
        sr_s[:, cols] = xr
        si_s[:, cols] = xi

    parts = [_dot(xr_s[:, blk[m]], cr_ref[m]) - _dot(xi_s[:, blk[m]], ci_ref[m]) for m in range(C_BLOCKS)]
    y = jnp.concatenate(parts, axis=1) + d_ref[...] * u
    y = jax.nn.gelu(y)
    y = y * _sigmoid(_dot(y, wg_ref[...]) + bg_ref[...])
    y_ref[...] = pltpu.einshape("tbc->btc", y.reshape(tt, bsz, C_WIDTH))


def _s5(zmix, bsz, seq, p, tt):
    f32 = F32
    dt = jnp.exp(p["c_log_dt"].astype(f32))[:, None]
    lr, li = p["c_lam_re"].astype(f32), p["c_lam_im"].astype(f32)
    mag = jnp.exp(lr * dt)
    ab_re, ab_im = mag * jnp.cos(li * dt), mag * jnp.sin(li * dt)
    den = lr * lr + li * li
    f_re = ((ab_re - 1.0) * lr + ab_im * li) / den
    f_im = (ab_im * lr - (ab_re - 1.0) * li) / den
    bre, bim = p["c_b_re"].astype(f32), p["c_b_im"].astype(f32)
    bb_re = f_re[..., None] * bre - f_im[..., None] * bim
    bb_im = f_re[..., None] * bim + f_im[..., None] * bre
    eye = jnp.eye(C_GROUPS, dtype=f32)

    def in_blocks(bb):
        full = jnp.einsum("gpc,gh->gchp", bb, eye).reshape(C_WIDTH, C_LANES)
        return jnp.stack([full[m * C_ULANES:(m + 1) * C_ULANES, m * C_SLANES:(m + 1) * C_SLANES]
                          for m in range(C_BLOCKS)]).astype(BF16)

    def out_blocks(cc):
        full = jnp.einsum("gcp,gh->gphc", cc.astype(f32), eye).reshape(C_LANES, C_WIDTH)
        return jnp.stack([full[m * C_SLANES:(m + 1) * C_SLANES, m * C_ULANES:(m + 1) * C_ULANES]
                          for m in range(C_BLOCKS)]).astype(BF16)

    consts = [in_blocks(bb_re), in_blocks(bb_im), ab_re.reshape(1, C_LANES), ab_im.reshape(1, C_LANES),
              out_blocks(p["c_c_re"]), out_blocks(p["c_c_im"]), p["c_d"].reshape(1, C_WIDTH).astype(f32),
              p["c_w_glu"].astype(BF16), p["c_b_glu"].reshape(1, C_WIDTH).astype(f32)]
    rows = tt * bsz
    return pl.pallas_call(
        _s5_kernel,
        grid=(seq // tt,),
        in_specs=[pl.BlockSpec((bsz, tt, C_WIDTH), lambda i: (0, i, U_BLK))] + [_full(a.shape) for a in consts],
        out_specs=pl.BlockSpec((bsz, tt, C_WIDTH), lambda i: (0, i, 0)),
        out_shape=jax.ShapeDtypeStruct((bsz, seq, C_WIDTH), F32),
        scratch_shapes=[pltpu.VMEM((rows, C_LANES), F32), pltpu.VMEM((rows, C_LANES), F32),
                        pltpu.VMEM((bsz, C_LANES), F32), pltpu.VMEM((bsz, C_LANES), F32)],
        compiler_params=_cparams("arbitrary"),
        name="s5",
    )(zmix, *consts)


def _merge_kernel(alpha, x_ref, ya_ref, yb_ref, yc_ref, wgate_ref, bgate_ref, wba_ref, wbb_ref, wbc_ref,
                  wout_ref, g_ref, b_ref, o_ref):
    d = x_ref.shape[1]
    x = x_ref[...]
    gates = _sigmoid(_dot(x, wgate_ref[...]) + bgate_ref[...])
    merged = (gates[:, :d] * _dot(ya_ref[...], wba_ref[...])
              + gates[:, d:2 * d] * _dot(yb_ref[...], wbb_ref[...])
              + gates[:, 2 * d:] * _dot(yc_ref[...], wbc_ref[...]))
    o_ref[...] = _layer_norm(alpha * x + _dot(merged, wout_ref[...]), g_ref[...], b_ref[...])


def _merge(x2d, ya, yb, yc, wgate, bgate, wb, wout, g, b, alpha, tm):
    m, d = x2d.shape
    wba, wbb, wbc = wb[:A_WIDTH], wb[A_WIDTH:A_WIDTH + B_V_WIDTH], wb[A_WIDTH + B_V_WIDTH:]
    consts = [wgate, bgate, wba, wbb, wbc, wout, g, b]
    tile = lambda n: pl.BlockSpec((tm, n), lambda i: (i, 0))
    return pl.pallas_call(
        functools.partial(_merge_kernel, alpha),
        grid=(m // tm,),
        in_specs=[tile(d), tile(A_WIDTH), tile(B_V_WIDTH), tile(C_WIDTH)] + [_full(a.shape) for a in consts],
        out_specs=tile(d),
        out_shape=jax.ShapeDtypeStruct((m, d), F32),
        compiler_params=_cparams("parallel"),
        name="merge",
    )(x2d, ya, yb, yc, *consts)


def _ffn_kernel(alpha, x_ref, w1_ref, w2_ref, g_ref, b_ref, o_ref, acc_s):
    j = pl.program_id(1)
    x = x_ref[...]
    h = jnp.maximum(_dot(x, w1_ref[...]), 0.0)
    part = _dot(h * h, w2_ref[...])

    @pl.when(j == 0)
    def _():
        acc_s[...] = part

    @pl.when(j > 0)
    def _():
        acc_s[...] += part

    @pl.when(j == pl.num_programs(1) - 1)
    def _():
        o_ref[...] = _layer_norm(alpha * x + acc_s[...], g_ref[...], b_ref[...])


def _ffn(x2d, w1, w2, g, b, alpha, tm, tf):
    m, d = x2d.shape
    dff = w1.shape[1]
    return pl.pallas_call(
        functools.partial(_ffn_kernel, alpha),
        grid=(m // tm, dff // tf),
        in_specs=[pl.BlockSpec((tm, d), lambda i, j: (i, 0)),
                  pl.BlockSpec((d, tf), lambda i, j: (0, j)),
                  pl.BlockSpec((tf, d), lambda i, j: (j, 0)),
                  _full(g.shape), _full(b.shape)],
        out_specs=pl.BlockSpec((tm, d), lambda i, j: (i, 0)),
        out_shape=jax.ShapeDtypeStruct((m, d), F32),
        scratch_shapes=[pltpu.VMEM((tm, d), F32)],
        compiler_params=_cparams("parallel", "arbitrary"),
        name="ffn",
    )(x2d, w1, w2, g, b)


def _tile(n, want):
    t = min(n, want)
    assert n % t == 0, (n, want)
    return t


def kernel(x, w_in, b_gate, a_shift, a_w0, a_w2, a_a0, a_a2, a_g2, a_kk, a_ka, a_rk, a_lnx_g, a_lnx_b, c_lam_re, c_lam_im, c_log_dt, c_b_re, c_b_im, c_c_re, c_c_im, c_d, c_w_glu, c_b_glu, w_branch, w_out, ln1_g, ln1_b, w_ff1, w_ff2, ln2_g, ln2_b):
    bsz, seq, d = x.shape
    depth = w_in.shape[0]
    alpha = (2.0 * depth) ** 0.25
    tokens = bsz * seq
    per_layer = dict(a_shift=a_shift, a_w0=a_w0, a_w2=a_w2, a_a0=a_a0, a_a2=a_a2, a_g2=a_g2, a_kk=a_kk,
                     a_ka=a_ka, a_rk=a_rk, a_lnx_g=a_lnx_g, a_lnx_b=a_lnx_b, c_lam_re=c_lam_re,
                     c_lam_im=c_lam_im, c_log_dt=c_log_dt, c_b_re=c_b_re, c_b_im=c_b_im, c_c_re=c_c_re,
                     c_c_im=c_c_im, c_d=c_d, c_w_glu=c_w_glu, c_b_glu=c_b_glu)
    xt = x.reshape(tokens, d)
    for l in range(depth):
        p = {k: v[l] for k, v in per_layer.items()}
        wl = w_in[l]
        w_mix = jnp.concatenate([wl[:, A_PROJ:MIX_COLS], wl[:, :A_PROJ]], axis=1).astype(BF16)
        w_gate = wl[:, MIX_COLS:].astype(BF16)
        zmix = _proj(xt, w_mix, _tile(tokens, 512), A_PROJ).reshape(bsz, seq, MIX_COLS)
        ya = _rwkv(zmix, bsz, seq, p, _tile(seq, 256))
        yb = _retention(zmix, bsz, seq, _tile(seq, 512))
        yc = _s5(zmix, bsz, seq, p, _tile(seq, 64))
        x1 = _merge(xt, ya.reshape(tokens, -1), yb.reshape(tokens, -1), yc.reshape(tokens, -1),
                    w_gate, b_gate[l].reshape(1, -1), w_branch[l].astype(BF16), w_out[l].astype(BF16),
                    ln1_g[l].reshape(1, -1), ln1_b[l].reshape(1, -1), alpha, _tile(tokens, 256))
        xt = _ffn(x1, w_ff1[l].astype(BF16), w_ff2[l].astype(BF16), ln2_g[l].reshape(1, -1),
                  ln2_b[l].reshape(1, -1), alpha, _tile(tokens, 1024), 1024)
    return xt.reshape(bsz, seq, d)
```

```python
import functools
import math

import jax
import jax.numpy as jnp
from jax import lax
from jax.experimental import pallas as pl
from jax.experimental.pallas import tpu as pltpu

F32 = jnp.float32
BF16 = jnp.bfloat16

A_HEADS = 8
A_HEAD_DIM = 64
A_WIDTH = A_HEADS * A_HEAD_DIM
A_DECAY_LORA = 64
A_ICLR_LORA = 64
A_GATE_LORA = 128
A_PROJ = 3 * A_WIDTH + A_DECAY_LORA + A_ICLR_LORA + A_GATE_LORA
A_GN_EPS = 64e-5
A_CHUNK = 64
A_GROUP = 4
A_GLANES = 256

B_HEADS = 4
B_QK_DIM = 128
B_V_DIM = 256
B_QK_WIDTH = B_HEADS * B_QK_DIM
B_V_WIDTH = B_HEADS * B_V_DIM
B_CHUNK = 128
B_ROPE_BASE = 10000.0
B_GN_EPS = 1e-5

C_WIDTH = 512
C_GROUP = 16
C_GROUPS = C_WIDTH // C_GROUP
C_STATE = 64
C_LANES = C_GROUPS * C_STATE
C_BLOCKS = 4
C_ULANES = C_WIDTH // C_BLOCKS
C_SLANES = C_LANES // C_BLOCKS

LN_EPS = 1e-5

RET_COLS = 2 * B_QK_WIDTH + 2 * B_V_WIDTH
MIX_COLS = RET_COLS + C_WIDTH + A_PROJ
RET_BLK = 0
U_BLK = RET_COLS // C_WIDTH
Z_BLK = (RET_COLS + C_WIDTH) // A_PROJ
assert RET_COLS % C_WIDTH == 0 and (RET_COLS + C_WIDTH) % A_PROJ == 0

V7X_VMEM_LIMIT_BYTES = 56 * 1024 * 1024


def _cparams(*sem):
    return pltpu.CompilerParams(dimension_semantics=sem, vmem_limit_bytes=V7X_VMEM_LIMIT_BYTES)


def _full(shape):
    n = len(shape)
    return pl.BlockSpec(shape, lambda *_: (0,) * n)


def _resident(shape):
    n = len(shape)
    return pl.BlockSpec(shape, lambda *_: (0,) * n, pipeline_mode=pl.Buffered(1))


def _dot(a, b):
    return jnp.dot(a.astype(BF16), b.astype(BF16), preferred_element_type=F32)


def _dot_nt(a, b):
    return lax.dot_general(a.astype(BF16), b.astype(BF16), (((1,), (1,)), ((), ())),
                           preferred_element_type=F32)


def _dot_tn(a, b):
    return lax.dot_general(a.astype(BF16), b.astype(BF16), (((0,), (0,)), ((), ())),
                           preferred_element_type=F32)


def _split2(x):
    hi = x.astype(BF16)
    lo = (x - hi.astype(F32)).astype(BF16)
    return hi, lo


def _split3(x):
    hi = x.astype(BF16)
    r1 = x - hi.astype(F32)
    mid = r1.astype(BF16)
    lo = (r1 - mid.astype(F32)).astype(BF16)
    return hi, mid, lo


def _sigmoid(x):
    return 1.0 / (1.0 + jnp.exp(-x))


def _layer_norm(y, g, b):
    mu = jnp.mean(y, axis=-1, keepdims=True)
    d = y - mu
    var = jnp.mean(d * d, axis=-1, keepdims=True)
    return d * lax.rsqrt(var + LN_EPS) * g + b


def _proj_kernel(x_ref, w_ref, o_ref):
    o_ref[...] = jnp.dot(x_ref[...].astype(BF16), w_ref[...], preferred_element_type=F32)


def _proj(x2d, w_bf16, tm):
    m, k = x2d.shape
    n = w_bf16.shape[1]
    return pl.pallas_call(
        _proj_kernel,
        grid=(m // tm,),
        in_specs=[pl.BlockSpec((tm, k), lambda i: (i, 0)), _resident(w_bf16.shape)],
        out_specs=pl.BlockSpec((tm, n), lambda i: (i, 0)),
        out_shape=jax.ShapeDtypeStruct((m, n), F32),
        compiler_params=_cparams("parallel"),
        name="proj",
    )(x2d, w_bf16)


def _rwkv_kernel(z_ref, mu_ref, w0_ref, a0_ref, lora_ref, g2_ref, kkp_ref, kap_ref, rkp_ref,
                 lng_ref, lnb_ref, ones_ref, tri_ref, y_ref,
                 carry_s, state_s, r_s, k_s, v_s, kn_s, ia_s, lw_s, o_s, rk_s, gate_s,
                 gam_s, x_s, m2_s, q_s, op_s):
    tc = z_ref.shape[0]
    n_chunks = tc // A_CHUNK
    w = A_WIDTH

    @pl.when(pl.program_id(1) == 0)
    def _():
        carry_s[...] = jnp.zeros_like(carry_s)
        state_s[...] = jnp.zeros_like(state_s)

    ones = ones_ref[...]

    def seg_sum(x):
        hi, lo = _split2(x)
        parts = []
        for j in range(x.shape[1] // 128):
            sl = slice(j * 128, (j + 1) * 128)
            parts.append(jnp.dot(hi[:, sl], ones, preferred_element_type=F32)
                         + jnp.dot(lo[:, sl], ones, preferred_element_type=F32))
        return jnp.concatenate(parts, axis=1)

    z = z_ref[...]
    rolled = pltpu.roll(z, 1, 0)
    rowid = lax.broadcasted_iota(jnp.int32, z.shape, 0)
    prev = jnp.where(rowid == 0, jnp.broadcast_to(carry_s[0:1, :], z.shape), rolled)
    zs = z + mu_ref[...] * (prev - z)
    carry_s[0:1, :] = z[tc - 1:tc, :]

    lz = zs[:, 3 * w:3 * w + 128]
    lane = lax.broadcasted_iota(jnp.int32, lz.shape, 1)
    lin = jnp.where(lane < A_DECAY_LORA, jnp.tanh(lz), lz)
    wa = _dot(lin, lora_ref[...])
    lw_s[...] = (-math.exp(-0.5)) * _sigmoid(w0_ref[...] + wa[:, :w])
    ia = _sigmoid(a0_ref[...] + wa[:, w:])
    ia_s[...] = ia
    gate_s[...] = _dot(_sigmoid(zs[:, 3 * w + 128:3 * w + 256]), g2_ref[...])

    r = zs[:, :w]
    k = zs[:, w:2 * w]
    v = zs[:, 2 * w:3 * w]
    kk = k * kkp_ref[...]
    kn_s[...] = kk / jnp.maximum(jnp.sqrt(seg_sum(kk * kk)), 1e-12)
    kmod = k * (1.0 + (ia - 1.0) * kap_ref[...])
    r_s[...] = r
    k_s[...] = kmod
    v_s[...] = v
    rk_s[...] = seg_sum(r * kmod * rkp_ref[...])

    tri = tri_ref[...]
    gw = A_GLANES
    hpg = gw // A_HEAD_DIM
    rid = lax.broadcasted_iota(jnp.int32, (A_CHUNK, gw), 0)
    cid = lax.broadcasted_iota(jnp.int32, (A_CHUNK, gw), 1) % A_HEAD_DIM
    strict = rid > cid
    incl = rid >= cid
    eye = (rid == cid).astype(F32)
    brow = lax.broadcasted_iota(jnp.int32, (gw, gw), 0) // A_HEAD_DIM
    bcol = lax.broadcasted_iota(jnp.int32, (gw, gw), 1) // A_HEAD_DIM
    same_head = brow == bcol
    same_head_bf = same_head.astype(BF16)
    n = A_HEAD_DIM
    groups = [slice(g * gw, (g + 1) * gw) for g in range(w // gw)]

    def bd(x):
        xb = x.astype(BF16)
        return jnp.concatenate([xb] * hpg, axis=0) * same_head_bf


    group = min(A_GROUP, n_chunks)
    assert n_chunks % group == 0

    def state_free_part(gi, carry):
        items = []
        for j in range(group):
            c = gi * group + j
            rows = pl.ds(pl.multiple_of(c * A_CHUNK, A_CHUNK), A_CHUNK)
            lw = lw_s[rows, :]
            h3 = _split3(lw)
            cum = (jnp.dot(tri, h3[0], preferred_element_type=F32)
                   + jnp.dot(tri, h3[1], preferred_element_type=F32)
                   + jnp.dot(tri, h3[2], preferred_element_type=F32))
            e_in = jnp.exp(cum)
            e_ex = jnp.exp(cum - lw)
            e_ng = jnp.exp(-cum)
            kn = kn_s[rows, :]
            rt = r_s[rows, :] * e_in
            at = -kn * e_ex
            bt = kn * ia_s[rows, :] * e_ng
            kt = k_s[rows, :] * e_ng
            vv = v_s[rows, :]
            gam_s[pl.ds(c, 1), :] = e_in[A_CHUNK - 1:A_CHUNK, :]
            for g, s in enumerate(groups):
                items.append((c * len(groups) + g, at[:, s], rt[:, s], bt[:, s], kt[:, s], vv[:, s]))
        ids = range(len(items))
        idx = [it[0] for it in items]
        a_ = [it[1] for it in items]
        r_ = [it[2] for it in items]
        b_ = [it[3] for it in items]
        k_ = [it[4] for it in items]
        v_ = [it[5] for it in items]
        ar = [jnp.concatenate([a_[i], r_[i]], axis=0) for i in ids]
        gb = [_dot_nt(ar[i], bd(b_[i])) for i in ids]
        gk = [_dot_nt(ar[i], bd(k_[i])) for i in ids]
        l_ab = [jnp.where(strict, gb[i][:A_CHUNK], 0.0) for i in ids]
        a_qb = [jnp.where(incl, gb[i][A_CHUNK:], 0.0) for i in ids]
        akq = [jnp.concatenate([jnp.where(strict, gk[i][:A_CHUNK], 0.0),
                                jnp.where(incl, gk[i][A_CHUNK:], 0.0)], axis=0) for i in ids]
        akqv = [_dot(akq[i], bd(v_[i])) for i in ids]
        akv = [akqv[i][:A_CHUNK] for i in ids]
        ov = [akqv[i][A_CHUNK:] for i in ids]
        tinv = [eye + l_ab[i] for i in ids]
        p = [_dot(l_ab[i], bd(l_ab[i])) for i in ids]
        for _ in range(4):
            tp = [_dot(jnp.concatenate([tinv[i], p[i]], axis=0), bd(p[i])) for i in ids]
            tinv = [tinv[i] + tp[i][:A_CHUNK] for i in ids]
            p = [tp[i][A_CHUNK:] for i in ids]
        tinv = [tinv[i] + _dot(tinv[i], bd(p[i])) for i in ids]
        wu = [_dot(tinv[i], jnp.concatenate([bd(a_[i]), bd(akv[i])], axis=1)) for i in ids]
        wm = [wu[i][:, :gw] for i in ids]
        uv = [wu[i][:, gw:] for i in ids]
        for i in ids:
            x_s[idx[i]] = jnp.where(same_head, _dot_tn(wm[i], b_[i]), 0.0)
        for i in ids:
            m2_s[idx[i]] = jnp.where(same_head,
                                     _dot_tn(jnp.concatenate([uv[i], v_[i]], axis=0),
                                             jnp.concatenate([b_[i], k_[i]], axis=0)), 0.0)
        qo = [_dot(a_qb[i], jnp.concatenate([bd(wm[i]), bd(uv[i])], axis=1)) for i in ids]
        for i in ids:
            q_s[idx[i]] = r_[i] + qo[i][:, :gw]
        for i in ids:
            op_s[idx[i]] = qo[i][:, gw:] + ov[i]
        return carry

    def state_part(c, carry):
        rows = pl.ds(pl.multiple_of(c * A_CHUNK, A_CHUNK), A_CHUNK)
        gam = gam_s[pl.ds(c, 1), :]
        gids = range(len(groups))
        s0 = [state_s[g] for g in gids]
        sx = [_dot(s0[g], x_s[c * len(groups) + g]) for g in gids]
        o = [_dot_nt(q_s[c * len(groups) + g], s0[g]) + op_s[c * len(groups) + g] for g in gids]
        for g in gids:
            state_s[g] = (s0[g] + sx[g] + m2_s[c * len(groups) + g]) * gam[:, groups[g]]
        o_s[rows, :] = jnp.concatenate(o, axis=1)
        return carry

    lax.fori_loop(0, n_chunks // group, state_free_part, 0)
    lax.fori_loop(0, n_chunks, state_part, 0)

    o = o_s[...]
    inv_n = 1.0 / n
    mean = seg_sum(o) * inv_n
    d = o - mean
    var = seg_sum(d * d) * inv_n
    on = d * lax.rsqrt(var + A_GN_EPS) * lng_ref[...] + lnb_ref[...]
    on = on + rk_s[...] * v_s[...]
    y_ref[...] = on * gate_s[...]


def _rwkv(zmix, bsz, seq, p, tc):
    zblk = Z_BLK
    row = lambda a: a.reshape(1, -1).astype(F32)
    w = A_WIDTH
    lora = jnp.zeros((128, 2 * w), F32)
    lora = lora.at[:A_DECAY_LORA, :w].set(p["a_w2"]).at[A_DECAY_LORA:, w:].set(p["a_a2"]).astype(BF16)
    hid = jnp.arange(128) // A_HEAD_DIM
    ones = (hid[:, None] == hid[None, :]).astype(BF16)
    ti = jnp.arange(A_CHUNK)
    tri = (ti[:, None] >= ti[None, :]).astype(BF16)
    small = [row(p["a_shift"]), row(p["a_w0"]), row(p["a_a0"]), lora, p["a_g2"].astype(BF16),
             row(p["a_kk"]), row(p["a_ka"]), row(p["a_rk"]), row(p["a_lnx_g"]), row(p["a_lnx_b"]),
             ones, tri]
    stage = pltpu.VMEM((tc, w), F32)
    n_chunks = tc // A_CHUNK
    n_groups = A_WIDTH // A_GLANES
    per_sq = pltpu.VMEM((n_chunks * n_groups, A_GLANES, A_GLANES), F32)
    per_row = pltpu.VMEM((n_chunks * n_groups, A_CHUNK, A_GLANES), F32)
    return pl.pallas_call(
        _rwkv_kernel,
        grid=(bsz, seq // tc),
        in_specs=[pl.BlockSpec((None, tc, A_PROJ), lambda b, i: (b, i, zblk))]
                 + [_full(a.shape) for a in small],
        out_specs=pl.BlockSpec((None, tc, w), lambda b, i: (b, i, 0)),
        out_shape=jax.ShapeDtypeStruct((bsz, seq, w), F32),
        scratch_shapes=[pltpu.VMEM((8, A_PROJ), F32),
                        pltpu.VMEM((A_WIDTH // A_GLANES, A_GLANES, A_GLANES), F32),
                        stage, stage, stage, stage, stage, stage, stage, stage, stage,
                        pltpu.VMEM((max(8, n_chunks), w), F32),
                        per_sq, per_sq, per_row, per_row],
        compiler_params=_cparams("parallel", "arbitrary"),
        name="rwkv7",
    )(zmix, *small)


def _ret_kernel(z_ref, cos_ref, sin_ref, dmask_ref, qd_ref, kd_ref, cd_ref,
                y_ref, state_s, qr_s, kr_s, upd_s, st_s):
    tc = z_ref.shape[0]
    n_chunks = tc // B_CHUNK
    k_off, v_off, g_off = B_QK_WIDTH, 2 * B_QK_WIDTH, 2 * B_QK_WIDTH + B_V_WIDTH

    @pl.when(pl.program_id(1) == 0)
    def _():
        state_s[...] = jnp.zeros_like(state_s)

    cos2 = cos_ref[...]
    sin2 = sin_ref[...]
    half = B_QK_DIM // 2

    def rope(t):
        return t * cos2 + pltpu.roll(t, half, 1) * sin2

    for h in range(B_HEADS):
        qs = slice(h * B_QK_DIM, (h + 1) * B_QK_DIM)
        qr_s[:, qs] = rope(z_ref[:, qs])
        kr_s[:, qs] = rope(z_ref[:, k_off + h * B_QK_DIM:k_off + (h + 1) * B_QK_DIM]) * (B_QK_DIM ** -0.5)

    items = [(c, h) for c in range(n_chunks) for h in range(B_HEADS)]
    rows = lambda c: slice(c * B_CHUNK, (c + 1) * B_CHUNK)
    qcol = lambda h: slice(h * B_QK_DIM, (h + 1) * B_QK_DIM)
    vcol = lambda h: slice(v_off + h * B_V_DIM, v_off + (h + 1) * B_V_DIM)

    for c, h in items:
        upd_s[c * B_HEADS + h] = _dot_tn(kr_s[rows(c), qcol(h)] * kd_ref[h], z_ref[rows(c), vcol(h)])
    for h in range(B_HEADS):
        st = state_s[h]
        for c in range(n_chunks):
            st_s[c * B_HEADS + h] = st
            st = cd_ref[h] * st + upd_s[c * B_HEADS + h]
        state_s[h] = st

    for c, h in items:
        qc = qr_s[rows(c), qcol(h)]
        scores = _dot_nt(qc, kr_s[rows(c), qcol(h)]) * dmask_ref[h]
        o = _dot(scores, z_ref[rows(c), vcol(h)]) + _dot(qc * qd_ref[h], st_s[c * B_HEADS + h])
        mu = jnp.mean(o, axis=-1, keepdims=True)
        d = o - mu
        var = jnp.mean(d * d, axis=-1, keepdims=True)
        on = d * lax.rsqrt(var + B_GN_EPS)
        g = z_ref[rows(c), g_off + h * B_V_DIM:g_off + (h + 1) * B_V_DIM]
        y_ref[rows(c), h * B_V_DIM:(h + 1) * B_V_DIM] = g * _sigmoid(g) * on


def _retention(zmix, bsz, seq, tc):
    f32 = F32
    pos = jnp.arange(seq, dtype=f32)
    half = B_QK_DIM // 2
    inv_freq = B_ROPE_BASE ** (-jnp.arange(half, dtype=f32) / half)
    ang = pos[:, None] * inv_freq[None, :]
    cos, sin = jnp.cos(ang), jnp.sin(ang)
    cos2 = jnp.concatenate([cos, cos], axis=1)
    sin2 = jnp.concatenate([-sin, sin], axis=1)
    log_gamma = jnp.log(1.0 - 2.0 ** (-5.0 - jnp.arange(B_HEADS, dtype=f32)))
    idx = jnp.arange(B_CHUNK, dtype=f32)
    rel = idx[:, None] - idx[None, :]
    dmask = jnp.where(rel >= 0, jnp.exp(log_gamma[:, None, None] * jnp.maximum(rel, 0.0)), 0.0)
    qd = jnp.broadcast_to(jnp.exp(log_gamma[:, None] * (idx + 1.0))[:, :, None],
                          (B_HEADS, B_CHUNK, B_QK_DIM))
    kd = jnp.broadcast_to(jnp.exp(log_gamma[:, None] * (B_CHUNK - 1.0 - idx))[:, :, None],
                          (B_HEADS, B_CHUNK, B_QK_DIM))
    cd = jnp.broadcast_to(jnp.exp(log_gamma * B_CHUNK)[:, None, None], (B_HEADS, B_QK_DIM, B_V_DIM))
    tabs = [dmask, qd, kd, cd]
    n_items = (tc // B_CHUNK) * B_HEADS
    per_item = pltpu.VMEM((n_items, B_QK_DIM, B_V_DIM), F32)
    return pl.pallas_call(
        _ret_kernel,
        grid=(bsz, seq // tc),
        in_specs=[pl.BlockSpec((None, tc, RET_COLS), lambda b, i: (b, i, RET_BLK)),
                  pl.BlockSpec((tc, B_QK_DIM), lambda b, i: (i, 0)),
                  pl.BlockSpec((tc, B_QK_DIM), lambda b, i: (i, 0))] + [_full(a.shape) for a in tabs],
        out_specs=pl.BlockSpec((None, tc, B_V_WIDTH), lambda b, i: (b, i, 0)),
        out_shape=jax.ShapeDtypeStruct((bsz, seq, B_V_WIDTH), F32),
        scratch_shapes=[pltpu.VMEM((B_HEADS, B_QK_DIM, B_V_DIM), F32),
                        pltpu.VMEM((tc, B_QK_WIDTH), F32), pltpu.VMEM((tc, B_QK_WIDTH), F32),
                        per_item, per_item],
        compiler_params=_cparams("parallel", "arbitrary"),
        name="retention",
    )(zmix, cos2, sin2, *tabs)


def _s5_kernel(u_ref, bbr_ref, bbi_ref, ar_ref, ai_ref, cr_ref, ci_ref, d_ref, wg_ref, bg_ref,
               y_ref, xr_s, xi_s, sr_s, si_s):
    bsz, tt, _ = u_ref.shape

    @pl.when(pl.program_id(0) == 0)
    def _():
        sr_s[...] = jnp.zeros_like(sr_s)
        si_s[...] = jnp.zeros_like(si_s)

    u = pltpu.einshape("btc->tbc", u_ref[...]).reshape(tt * bsz, C_WIDTH)
    ub = u.astype(BF16)
    blk = [slice(m * C_SLANES, (m + 1) * C_SLANES) for m in range(C_BLOCKS)]

    for m in range(C_BLOCKS):
        um = ub[:, m * C_ULANES:(m + 1) * C_ULANES]
        xr_s[:, blk[m]] = jnp.dot(um, bbr_ref[m], preferred_element_type=F32)
        xi_s[:, blk[m]] = jnp.dot(um, bbi_ref[m], preferred_element_type=F32)

    for m in range(C_BLOCKS):
        cols = blk[m]
        ar = jnp.broadcast_to(ar_ref[:, cols], (bsz, C_SLANES))
        ai = jnp.broadcast_to(ai_ref[:, cols], (bsz, C_SLANES))

        def step(t, carry):
            xr, xi = carry
            rows = pl.ds(pl.multiple_of(t * bsz, bsz), bsz)
            nr = ar * xr - ai * xi + xr_s[rows, cols]
            ni = ar * xi + ai * xr + xi_s[rows, cols]
            xr_s[rows, cols] = nr
            xi_s[rows, cols] = ni
            return nr, ni

        xr, xi = lax.fori_loop(0, tt, step, (sr_s[:, cols], si_s[:, cols]), unroll=4)
        sr_s[:, cols] = xr
        si_s[:, cols] = xi

    parts = [_dot(xr_s[:, blk[m]], cr_ref[m]) - _dot(xi_s[:, blk[m]], ci_ref[m]) for m in range(C_BLOCKS)]
    y = jnp.concatenate(parts, axis=1) + d_ref[...] * u
    y = jax.nn.gelu(y)
    y = y * _sigmoid(_dot(y, wg_ref[...]) + bg_ref[...])
    y_ref[...] = pltpu.einshape("tbc->btc", y.reshape(tt, bsz, C_WIDTH))


def _s5(zmix, bsz, seq, p, tt):
    f32 = F32
    dt = jnp.exp(p["c_log_dt"].astype(f32))[:, None]
    lr, li = p["c_lam_re"].astype(f32), p["c_lam_im"].astype(f32)
    mag = jnp.exp(lr * dt)
    ab_re, ab_im = mag * jnp.cos(li * dt), mag * jnp.sin(li * dt)
    den = lr * lr + li * li
    f_re = ((ab_re - 1.0) * lr + ab_im * li) / den
    f_im = (ab_im * lr - (ab_re - 1.0) * li) / den
    bre, bim = p["c_b_re"].astype(f32), p["c_b_im"].astype(f32)
    bb_re = f_re[..., None] * bre - f_im[..., None] * bim
    bb_im = f_re[..., None] * bim + f_im[..., None] * bre
    eye = jnp.eye(C_GROUPS, dtype=f32)

    def in_blocks(bb):
        full = jnp.einsum("gpc,gh->gchp", bb, eye).reshape(C_WIDTH, C_LANES)
        return jnp.stack([full[m * C_ULANES:(m + 1) * C_ULANES, m * C_SLANES:(m + 1) * C_SLANES]
                          for m in range(C_BLOCKS)]).astype(BF16)

    def out_blocks(cc):
        full = jnp.einsum("gcp,gh->gphc", cc.astype(f32), eye).reshape(C_LANES, C_WIDTH)
        return jnp.stack([full[m * C_SLANES:(m + 1) * C_SLANES, m * C_ULANES:(m + 1) * C_ULANES]
                          for m in range(C_BLOCKS)]).astype(BF16)

    consts = [in_blocks(bb_re), in_blocks(bb_im), ab_re.reshape(1, C_LANES), ab_im.reshape(1, C_LANES),
              out_blocks(p["c_c_re"]), out_blocks(p["c_c_im"]), p["c_d"].reshape(1, C_WIDTH).astype(f32),
              p["c_w_glu"].astype(BF16), p["c_b_glu"].reshape(1, C_WIDTH).astype(f32)]
    rows = tt * bsz
    return pl.pallas_call(
        _s5_kernel,
        grid=(seq // tt,),
        in_specs=[pl.BlockSpec((bsz, tt, C_WIDTH), lambda i: (0, i, U_BLK))] + [_full(a.shape) for a in consts],
        out_specs=pl.BlockSpec((bsz, tt, C_WIDTH), lambda i: (0, i, 0)),
        out_shape=jax.ShapeDtypeStruct((bsz, seq, C_WIDTH), F32),
        scratch_shapes=[pltpu.VMEM((rows, C_LANES), F32), pltpu.VMEM((rows, C_LANES), F32),
                        pltpu.VMEM((bsz, C_LANES), F32), pltpu.VMEM((bsz, C_LANES), F32)],
        compiler_params=_cparams("arbitrary"),
        name="s5",
    )(zmix, *consts)


def _merge_kernel(alpha, x_ref, ya_ref, yb_ref, yc_ref, wgate_ref, bgate_ref, wba_ref, wbb_ref, wbc_ref,
                  wout_ref, g_ref, b_ref, o_ref):
    d = x_ref.shape[1]
    x = x_ref[...]
    gates = _sigmoid(_dot(x, wgate_ref[...]) + bgate_ref[...])
    merged = (gates[:, :d] * _dot(ya_ref[...], wba_ref[...])
              + gates[:, d:2 * d] * _dot(yb_ref[...], wbb_ref[...])
              + gates[:, 2 * d:] * _dot(yc_ref[...], wbc_ref[...]))
    o_ref[...] = _layer_norm(alpha * x + _dot(merged, wout_ref[...]), g_ref[...], b_ref[...])


def _merge(x2d, ya, yb, yc, wgate, bgate, wb, wout, g, b, alpha, tm):
    m, d = x2d.shape
    wba, wbb, wbc = wb[:A_WIDTH], wb[A_WIDTH:A_WIDTH + B_V_WIDTH], wb[A_WIDTH + B_V_WIDTH:]
    consts = [wgate, bgate, wba, wbb, wbc, wout, g, b]
    tile = lambda n: pl.BlockSpec((tm, n), lambda i: (i, 0))
    return pl.pallas_call(
        functools.partial(_merge_kernel, alpha),
        grid=(m // tm,),
        in_specs=[tile(d), tile(A_WIDTH), tile(B_V_WIDTH), tile(C_WIDTH)] + [_resident(a.shape) for a in consts],
        out_specs=tile(d),
        out_shape=jax.ShapeDtypeStruct((m, d), F32),
        compiler_params=_cparams("parallel"),
        name="merge",
    )(x2d, ya, yb, yc, *consts)


def _ffn_kernel(alpha, tf, x_ref, w1_ref, w2_ref, g_ref, b_ref, o_ref):
    x = x_ref[...]
    xb = x.astype(BF16)
    acc = None
    for j in range(w1_ref.shape[1] // tf):
        h = jnp.maximum(jnp.dot(xb, w1_ref[:, j * tf:(j + 1) * tf], preferred_element_type=F32), 0.0)
        part = jnp.dot((h * h).astype(BF16), w2_ref[j * tf:(j + 1) * tf, :], preferred_element_type=F32)
        acc = part if acc is None else acc + part
    o_ref[...] = _layer_norm(alpha * x + acc, g_ref[...], b_ref[...])


def _ffn(x2d, w1, w2, g, b, alpha, tm, tf):
    m, d = x2d.shape
    return pl.pallas_call(
        functools.partial(_ffn_kernel, alpha, tf),
        grid=(m // tm,),
        in_specs=[pl.BlockSpec((tm, d), lambda i: (i, 0)),
                  _resident(w1.shape), _resident(w2.shape), _full(g.shape), _full(b.shape)],
        out_specs=pl.BlockSpec((tm, d), lambda i: (i, 0)),
        out_shape=jax.ShapeDtypeStruct((m, d), F32),
        compiler_params=_cparams("parallel"),
        name="ffn",
    )(x2d, w1, w2, g, b)


def _tile(n, want):
    t = min(n, want)
    assert n % t == 0, (n, want)
    return t


def kernel(x, w_in, b_gate, a_shift, a_w0, a_w2, a_a0, a_a2, a_g2, a_kk, a_ka, a_rk, a_lnx_g, a_lnx_b, c_lam_re, c_lam_im, c_log_dt, c_b_re, c_b_im, c_c_re, c_c_im, c_d, c_w_glu, c_b_glu, w_branch, w_out, ln1_g, ln1_b, w_ff1, w_ff2, ln2_g, ln2_b):
    bsz, seq, d = x.shape
    depth = w_in.shape[0]
    alpha = (2.0 * depth) ** 0.25
    tokens = bsz * seq
    per_layer = dict(a_shift=a_shift, a_w0=a_w0, a_w2=a_w2, a_a0=a_a0, a_a2=a_a2, a_g2=a_g2, a_kk=a_kk,
                     a_ka=a_ka, a_rk=a_rk, a_lnx_g=a_lnx_g, a_lnx_b=a_lnx_b, c_lam_re=c_lam_re,
                     c_lam_im=c_lam_im, c_log_dt=c_log_dt, c_b_re=c_b_re, c_b_im=c_b_im, c_c_re=c_c_re,
                     c_c_im=c_c_im, c_d=c_d, c_w_glu=c_w_glu, c_b_glu=c_b_glu)
    xt = x.reshape(tokens, d)
    for l in range(depth):
        p = {k: v[l] for k, v in per_layer.items()}
        wl = w_in[l]
        w_mix = jnp.concatenate([wl[:, A_PROJ:MIX_COLS], wl[:, :A_PROJ]], axis=1).astype(BF16)
        w_gate = wl[:, MIX_COLS:].astype(BF16)
        zmix = _proj(xt, w_mix, _tile(tokens, 512)).reshape(bsz, seq, MIX_COLS)
        ya = _rwkv(zmix, bsz, seq, p, _tile(seq, 256))
        yb = _retention(zmix, bsz, seq, _tile(seq, 512))
        yc = _s5(zmix, bsz, seq, p, _tile(seq, 64))
        x1 = _merge(xt, ya.reshape(tokens, -1), yb.reshape(tokens, -1), yc.reshape(tokens, -1),
                    w_gate, b_gate[l].reshape(1, -1), w_branch[l].astype(BF16), w_out[l].astype(BF16),
                    ln1_g[l].reshape(1, -1), ln1_b[l].reshape(1, -1), alpha, _tile(tokens, 512))
        xt = _ffn(x1, w_ff1[l].astype(BF16), w_ff2[l].astype(BF16), ln2_g[l].reshape(1, -1),
                  ln2_b[l].reshape(1, -1), alpha, _tile(tokens, 1024), 1024)
    return xt.reshape(bsz, seq, d)
```

```python
import functools
import math

import jax
import jax.numpy as jnp
from jax import lax
from jax.experimental import pallas as pl
from jax.experimental.pallas import tpu as pltpu

F32 = jnp.float32
BF16 = jnp.bfloat16

A_HEADS = 8
A_HEAD_DIM = 64
A_WIDTH = A_HEADS * A_HEAD_DIM
A_DECAY_LORA = 64
A_ICLR_LORA = 64
A_GATE_LORA = 128
A_PROJ = 3 * A_WIDTH + A_DECAY_LORA + A_ICLR_LORA + A_GATE_LORA
A_GN_EPS = 64e-5
A_CHUNK = 64
A_GROUP = 4
A_GLANES = 256

B_HEADS = 4
B_QK_DIM = 128
B_V_DIM = 256
B_QK_WIDTH = B_HEADS * B_QK_DIM
B_V_WIDTH = B_HEADS * B_V_DIM
B_CHUNK = 128
B_ROPE_BASE = 10000.0
B_GN_EPS = 1e-5

C_WIDTH = 512
C_GROUP = 16
C_GROUPS = C_WIDTH // C_GROUP
C_STATE = 64
C_LANES = C_GROUPS * C_STATE
C_BLOCKS = 4
C_ULANES = C_WIDTH // C_BLOCKS
C_SLANES = C_LANES // C_BLOCKS

LN_EPS = 1e-5

RET_COLS = 2 * B_QK_WIDTH + 2 * B_V_WIDTH
MIX_COLS = RET_COLS + C_WIDTH + A_PROJ
RET_BLK = 0
U_BLK = RET_COLS // C_WIDTH
Z_BLK = (RET_COLS + C_WIDTH) // A_PROJ
assert RET_COLS % C_WIDTH == 0 and (RET_COLS + C_WIDTH) % A_PROJ == 0

V7X_VMEM_LIMIT_BYTES = 56 * 1024 * 1024


def _cparams(*sem):
    return pltpu.CompilerParams(dimension_semantics=sem, vmem_limit_bytes=V7X_VMEM_LIMIT_BYTES)


def _full(shape):
    n = len(shape)
    return pl.BlockSpec(shape, lambda *_: (0,) * n)


def _resident(shape):
    n = len(shape)
    return pl.BlockSpec(shape, lambda *_: (0,) * n, pipeline_mode=pl.Buffered(1))


def _dot(a, b):
    return jnp.dot(a.astype(BF16), b.astype(BF16), preferred_element_type=F32)


def _dot_nt(a, b):
    return lax.dot_general(a.astype(BF16), b.astype(BF16), (((1,), (1,)), ((), ())),
                           preferred_element_type=F32)


def _dot_tn(a, b):
    return lax.dot_general(a.astype(BF16), b.astype(BF16), (((0,), (0,)), ((), ())),
                           preferred_element_type=F32)


def _split2(x):
    hi = x.astype(BF16)
    lo = (x - hi.astype(F32)).astype(BF16)
    return hi, lo


def _split3(x):
    hi = x.astype(BF16)
    r1 = x - hi.astype(F32)
    mid = r1.astype(BF16)
    lo = (r1 - mid.astype(F32)).astype(BF16)
    return hi, mid, lo


def _sigmoid(x):
    return 1.0 / (1.0 + jnp.exp(-x))


def _layer_norm(y, g, b):
    mu = jnp.mean(y, axis=-1, keepdims=True)
    d = y - mu
    var = jnp.mean(d * d, axis=-1, keepdims=True)
    return d * lax.rsqrt(var + LN_EPS) * g + b


def _proj_kernel(x_ref, w_ref, o_ref):
    o_ref[...] = jnp.dot(x_ref[...].astype(BF16), w_ref[...], preferred_element_type=F32)


def _proj(x2d, w_bf16, tm):
    m, k = x2d.shape
    n = w_bf16.shape[1]
    return pl.pallas_call(
        _proj_kernel,
        grid=(m // tm,),
        in_specs=[pl.BlockSpec((tm, k), lambda i: (i, 0)), _resident(w_bf16.shape)],
        out_specs=pl.BlockSpec((tm, n), lambda i: (i, 0)),
        out_shape=jax.ShapeDtypeStruct((m, n), F32),
        compiler_params=_cparams("parallel"),
        name="proj",
    )(x2d, w_bf16)


def _chunk_rows(c):
    return pl.ds(c * A_CHUNK, A_CHUNK)


def _rwkv_kernel(z_ref, mu_ref, w0_ref, a0_ref, lora_ref, g2_ref, kkp_ref, kap_ref, rkp_ref,
                 lng_ref, lnb_ref, ones_ref, tri_ref, y_ref,
                 carry_s, state_s, r_s, k_s, v_s, kn_s, ia_s, lw_s, o_s, rk_s, gate_s,
                 gam_s, x_s, m2_s, q_s, op_s):
    tc = z_ref.shape[0]
    n_chunks = tc // A_CHUNK
    w = A_WIDTH

    @pl.when(pl.program_id(1) == 0)
    def _():
        carry_s[...] = jnp.zeros_like(carry_s)
        state_s[...] = jnp.zeros_like(state_s)

    ones = ones_ref[...]

    gw = A_GLANES
    groups = [slice(g * gw, (g + 1) * gw) for g in range(w // gw)]

    def seg_sum(x):
        xb = x.astype(BF16)
        return jnp.concatenate([jnp.dot(xb[:, s], ones, preferred_element_type=F32) for s in groups], axis=1)

    z = z_ref[...]
    rolled = pltpu.roll(z, 1, 0)
    rowid = lax.broadcasted_iota(jnp.int32, z.shape, 0)
    prev = jnp.where(rowid == 0, jnp.broadcast_to(carry_s[0:1, :], z.shape), rolled)
    zs = z + mu_ref[...] * (prev - z)
    carry_s[0:1, :] = z[tc - 1:tc, :]

    lz = zs[:, 3 * w:3 * w + 128]
    lane = lax.broadcasted_iota(jnp.int32, lz.shape, 1)
    lin = jnp.where(lane < A_DECAY_LORA, jnp.tanh(lz), lz)
    wa = _dot(lin, lora_ref[...])
    lw_s[...] = (-math.exp(-0.5)) * _sigmoid(w0_ref[...] + wa[:, :w])
    ia = _sigmoid(a0_ref[...] + wa[:, w:])
    ia_s[...] = ia
    gate_s[...] = _dot(_sigmoid(zs[:, 3 * w + 128:3 * w + 256]), g2_ref[...])

    r = zs[:, :w]
    k = zs[:, w:2 * w]
    v = zs[:, 2 * w:3 * w]
    kk = k * kkp_ref[...]
    kn_s[...] = kk * lax.rsqrt(jnp.maximum(seg_sum(kk * kk), 1e-24))
    kmod = k * (1.0 + (ia - 1.0) * kap_ref[...])
    r_s[...] = r
    k_s[...] = kmod
    v_s[...] = v
    rk_s[...] = seg_sum(r * kmod * rkp_ref[...])

    tri = tri_ref[...]
    hpg = gw // A_HEAD_DIM
    rid = lax.broadcasted_iota(jnp.int32, (A_CHUNK, gw), 0)
    cid = lax.broadcasted_iota(jnp.int32, (A_CHUNK, gw), 1) % A_HEAD_DIM
    strict = rid > cid
    incl = rid >= cid
    eye = (rid == cid).astype(F32)
    brow = lax.broadcasted_iota(jnp.int32, (gw, gw), 0) // A_HEAD_DIM
    bcol = lax.broadcasted_iota(jnp.int32, (gw, gw), 1) // A_HEAD_DIM
    same_head = brow == bcol
    same_head_bf = same_head.astype(BF16)
    n = A_HEAD_DIM

    def bd(x):
        xb = x.astype(BF16)
        return jnp.concatenate([xb] * hpg, axis=0) * same_head_bf


    group = min(A_GROUP, n_chunks)
    assert n_chunks % group == 0

    def state_free_part(gi, carry):
        items = []
        for j in range(group):
            c = gi * group + j
            rows = _chunk_rows(c)
            lw = lw_s[rows, :]
            h3 = _split3(lw)
            cum = (jnp.dot(tri, h3[0], preferred_element_type=F32)
                   + jnp.dot(tri, h3[1], preferred_element_type=F32)
                   + jnp.dot(tri, h3[2], preferred_element_type=F32))
            e_in = jnp.exp(cum)
            e_ex = jnp.exp(cum - lw)
            e_ng = jnp.exp(-cum)
            kn = kn_s[rows, :]
            rt = r_s[rows, :] * e_in
            at = -kn * e_ex
            bt = kn * ia_s[rows, :] * e_ng
            kt = k_s[rows, :] * e_ng
            vv = v_s[rows, :]
            gam_s[pl.ds(c, 1), :] = e_in[A_CHUNK - 1:A_CHUNK, :]
            for g, s in enumerate(groups):
                items.append((c * len(groups) + g, at[:, s], rt[:, s], bt[:, s], kt[:, s], vv[:, s]))
        ids = range(len(items))
        idx = [it[0] for it in items]
        a_ = [it[1] for it in items]
        r_ = [it[2] for it in items]
        b_ = [it[3] for it in items]
        k_ = [it[4] for it in items]
        v_ = [it[5] for it in items]
        ar = [jnp.concatenate([a_[i], r_[i]], axis=0) for i in ids]
        gb = [_dot_nt(ar[i], bd(b_[i])) for i in ids]
        gk = [_dot_nt(ar[i], bd(k_[i])) for i in ids]
        l_ab = [jnp.where(strict, gb[i][:A_CHUNK], 0.0) for i in ids]
        a_qb = [jnp.where(incl, gb[i][A_CHUNK:], 0.0) for i in ids]
        akq = [jnp.concatenate([jnp.where(strict, gk[i][:A_CHUNK], 0.0),
                                jnp.where(incl, gk[i][A_CHUNK:], 0.0)], axis=0) for i in ids]
        akqv = [_dot(akq[i], bd(v_[i])) for i in ids]
        akv = [akqv[i][:A_CHUNK] for i in ids]
        ov = [akqv[i][A_CHUNK:] for i in ids]
        tinv = [eye + l_ab[i] for i in ids]
        p = [_dot(l_ab[i], bd(l_ab[i])) for i in ids]
        for _ in range(4):
            tp = [_dot(jnp.concatenate([tinv[i], p[i]], axis=0), bd(p[i])) for i in ids]
            tinv = [tinv[i] + tp[i][:A_CHUNK] for i in ids]
            p = [tp[i][A_CHUNK:] for i in ids]
        tinv = [tinv[i] + _dot(tinv[i], bd(p[i])) for i in ids]
        wu = [_dot(tinv[i], jnp.concatenate([bd(a_[i]), bd(akv[i])], axis=1)) for i in ids]
        wm = [wu[i][:, :gw] for i in ids]
        uv = [wu[i][:, gw:] for i in ids]
        for i in ids:
            x_s[idx[i]] = jnp.where(same_head, _dot_tn(wm[i], b_[i]), 0.0)
        for i in ids:
            m2 = jnp.where(same_head, _dot_tn(jnp.concatenate([uv[i], v_[i]], axis=0),
                                              jnp.concatenate([b_[i], k_[i]], axis=0)), 0.0)
            m2_s[idx[i]] = sum(m2[h * n:(h + 1) * n] for h in range(1, hpg)) + m2[:n]
        qo = [_dot(a_qb[i], jnp.concatenate([bd(wm[i]), bd(uv[i])], axis=1)) for i in ids]
        for i in ids:
            q_s[idx[i]] = r_[i] + qo[i][:, :gw]
        for i in ids:
            op_s[idx[i]] = qo[i][:, gw:] + ov[i]
        return carry

    def state_part(c, carry):
        rows = _chunk_rows(c)
        gam = gam_s[pl.ds(c, 1), :]
        gids = range(len(groups))
        s0 = [state_s[g] for g in gids]
        sx = [_dot(s0[g], x_s[c * len(groups) + g]) for g in gids]
        o = [_dot_nt(q_s[c * len(groups) + g], bd(s0[g])) + op_s[c * len(groups) + g] for g in gids]
        for g in gids:
            state_s[g] = (s0[g] + sx[g] + m2_s[c * len(groups) + g]) * gam[:, groups[g]]
        o_s[rows, :] = jnp.concatenate(o, axis=1)
        return carry

    for gi in range(n_chunks // group):
        state_free_part(gi, 0)
    for c in range(n_chunks):
        state_part(c, 0)

    o = o_s[...]
    inv_n = 1.0 / n
    mean = seg_sum(o) * inv_n
    d = o - mean
    var = seg_sum(d * d) * inv_n
    on = d * lax.rsqrt(var + A_GN_EPS) * lng_ref[...] + lnb_ref[...]
    on = on + rk_s[...] * v_s[...]
    y_ref[...] = on * gate_s[...]


def _rwkv(zmix, bsz, seq, p, tc):
    zblk = Z_BLK
    row = lambda a: a.reshape(1, -1).astype(F32)
    w = A_WIDTH
    lora = jnp.zeros((128, 2 * w), F32)
    lora = lora.at[:A_DECAY_LORA, :w].set(p["a_w2"]).at[A_DECAY_LORA:, w:].set(p["a_a2"]).astype(BF16)
    hid = jnp.arange(A_GLANES) // A_HEAD_DIM
    ones = (hid[:, None] == hid[None, :]).astype(BF16)
    ti = jnp.arange(A_CHUNK)
    tri = (ti[:, None] >= ti[None, :]).astype(BF16)
    small = [row(p["a_shift"]), row(p["a_w0"]), row(p["a_a0"]), lora, p["a_g2"].astype(BF16),
             row(p["a_kk"]), row(p["a_ka"]), row(p["a_rk"]), row(p["a_lnx_g"]), row(p["a_lnx_b"]),
             ones, tri]
    stage = pltpu.VMEM((tc, w), F32)
    n_chunks = tc // A_CHUNK
    n_groups = A_WIDTH // A_GLANES
    per_sq = pltpu.VMEM((n_chunks * n_groups, A_GLANES, A_GLANES), F32)
    per_row = pltpu.VMEM((n_chunks * n_groups, A_CHUNK, A_GLANES), F32)
    return pl.pallas_call(
        _rwkv_kernel,
        grid=(bsz, seq // tc),
        in_specs=[pl.BlockSpec((None, tc, A_PROJ), lambda b, i: (b, i, zblk))]
                 + [_full(a.shape) for a in small],
        out_specs=pl.BlockSpec((None, tc, w), lambda b, i: (b, i, 0)),
        out_shape=jax.ShapeDtypeStruct((bsz, seq, w), F32),
        scratch_shapes=[pltpu.VMEM((8, A_PROJ), F32),
                        pltpu.VMEM((A_WIDTH // A_GLANES, A_HEAD_DIM, A_GLANES), F32),
                        stage, stage, stage, stage, stage, stage, stage, stage, stage,
                        pltpu.VMEM((max(8, n_chunks), w), F32),
                        per_sq, per_row, per_row, per_row],
        compiler_params=_cparams("parallel", "arbitrary"),
        name="rwkv7",
    )(zmix, *small)


def _ret_kernel(z_ref, cos_ref, sin_ref, dmask_ref, qd_ref, kd_ref, cd_ref,
                y_ref, state_s, qr_s, kr_s, upd_s, st_s):
    tc = z_ref.shape[0]
    n_chunks = tc // B_CHUNK
    k_off, v_off, g_off = B_QK_WIDTH, 2 * B_QK_WIDTH, 2 * B_QK_WIDTH + B_V_WIDTH

    @pl.when(pl.program_id(1) == 0)
    def _():
        state_s[...] = jnp.zeros_like(state_s)

    cos2 = cos_ref[...]
    sin2 = sin_ref[...]
    half = B_QK_DIM // 2

    def rope(t):
        return t * cos2 + pltpu.roll(t, half, 1) * sin2

    for h in range(B_HEADS):
        qs = slice(h * B_QK_DIM, (h + 1) * B_QK_DIM)
        qr_s[:, qs] = rope(z_ref[:, qs])
        kr_s[:, qs] = rope(z_ref[:, k_off + h * B_QK_DIM:k_off + (h + 1) * B_QK_DIM]) * (B_QK_DIM ** -0.5)

    items = [(c, h) for c in range(n_chunks) for h in range(B_HEADS)]
    rows = lambda c: slice(c * B_CHUNK, (c + 1) * B_CHUNK)
    qcol = lambda h: slice(h * B_QK_DIM, (h + 1) * B_QK_DIM)
    vcol = lambda h: slice(v_off + h * B_V_DIM, v_off + (h + 1) * B_V_DIM)

    for c, h in items:
        upd_s[c * B_HEADS + h] = _dot_tn(kr_s[rows(c), qcol(h)] * kd_ref[h], z_ref[rows(c), vcol(h)])
    for h in range(B_HEADS):
        st = state_s[h]
        for c in range(n_chunks):
            st_s[c * B_HEADS + h] = st
            st = cd_ref[h] * st + upd_s[c * B_HEADS + h]
        state_s[h] = st

    for c, h in items:
        qc = qr_s[rows(c), qcol(h)]
        scores = _dot_nt(qc, kr_s[rows(c), qcol(h)]) * dmask_ref[h]
        o = _dot(scores, z_ref[rows(c), vcol(h)]) + _dot(qc * qd_ref[h], st_s[c * B_HEADS + h])
        mu = jnp.mean(o, axis=-1, keepdims=True)
        d = o - mu
        var = jnp.mean(d * d, axis=-1, keepdims=True)
        on = d * lax.rsqrt(var + B_GN_EPS)
        g = z_ref[rows(c), g_off + h * B_V_DIM:g_off + (h + 1) * B_V_DIM]
        y_ref[rows(c), h * B_V_DIM:(h + 1) * B_V_DIM] = g * _sigmoid(g) * on


def _retention(zmix, bsz, seq, tc):
    f32 = F32
    pos = jnp.arange(seq, dtype=f32)
    half = B_QK_DIM // 2
    inv_freq = B_ROPE_BASE ** (-jnp.arange(half, dtype=f32) / half)
    ang = pos[:, None] * inv_freq[None, :]
    cos, sin = jnp.cos(ang), jnp.sin(ang)
    cos2 = jnp.concatenate([cos, cos], axis=1)
    sin2 = jnp.concatenate([-sin, sin], axis=1)
    log_gamma = jnp.log(1.0 - 2.0 ** (-5.0 - jnp.arange(B_HEADS, dtype=f32)))
    idx = jnp.arange(B_CHUNK, dtype=f32)
    rel = idx[:, None] - idx[None, :]
    dmask = jnp.where(rel >= 0, jnp.exp(log_gamma[:, None, None] * jnp.maximum(rel, 0.0)), 0.0)
    qd = jnp.broadcast_to(jnp.exp(log_gamma[:, None] * (idx + 1.0))[:, :, None],
                          (B_HEADS, B_CHUNK, B_QK_DIM))
    kd = jnp.broadcast_to(jnp.exp(log_gamma[:, None] * (B_CHUNK - 1.0 - idx))[:, :, None],
                          (B_HEADS, B_CHUNK, B_QK_DIM))
    cd = jnp.broadcast_to(jnp.exp(log_gamma * B_CHUNK)[:, None, None], (B_HEADS, B_QK_DIM, B_V_DIM))
    tabs = [dmask, qd, kd, cd]
    n_items = (tc // B_CHUNK) * B_HEADS
    per_item = pltpu.VMEM((n_items, B_QK_DIM, B_V_DIM), F32)
    return pl.pallas_call(
        _ret_kernel,
        grid=(bsz, seq // tc),
        in_specs=[pl.BlockSpec((None, tc, RET_COLS), lambda b, i: (b, i, RET_BLK)),
                  pl.BlockSpec((tc, B_QK_DIM), lambda b, i: (i, 0)),
                  pl.BlockSpec((tc, B_QK_DIM), lambda b, i: (i, 0))] + [_full(a.shape) for a in tabs],
        out_specs=pl.BlockSpec((None, tc, B_V_WIDTH), lambda b, i: (b, i, 0)),
        out_shape=jax.ShapeDtypeStruct((bsz, seq, B_V_WIDTH), F32),
        scratch_shapes=[pltpu.VMEM((B_HEADS, B_QK_DIM, B_V_DIM), F32),
                        pltpu.VMEM((tc, B_QK_WIDTH), F32), pltpu.VMEM((tc, B_QK_WIDTH), F32),
                        per_item, per_item],
        compiler_params=_cparams("parallel", "arbitrary"),
        name="retention",
    )(zmix, cos2, sin2, *tabs)


def _s5_kernel(u_ref, bbr_ref, bbi_ref, ar_ref, ai_ref, cr_ref, ci_ref, d_ref, wg_ref, bg_ref,
               y_ref, xr_s, xi_s, sr_s, si_s):
    bsz, tt, _ = u_ref.shape

    @pl.when(pl.program_id(0) == 0)
    def _():
        sr_s[...] = jnp.zeros_like(sr_s)
        si_s[...] = jnp.zeros_like(si_s)

    u = pltpu.einshape("btc->tbc", u_ref[...]).reshape(tt * bsz, C_WIDTH)
    ub = u.astype(BF16)
    blk = [slice(m * C_SLANES, (m + 1) * C_SLANES) for m in range(C_BLOCKS)]

    for m in range(C_BLOCKS):
        um = ub[:, m * C_ULANES:(m + 1) * C_ULANES]
        xr_s[:, blk[m]] = jnp.dot(um, bbr_ref[m], preferred_element_type=F32)
        xi_s[:, blk[m]] = jnp.dot(um, bbi_ref[m], preferred_element_type=F32)

    for m in range(C_BLOCKS):
        cols = blk[m]
        ar = jnp.broadcast_to(ar_ref[:, cols], (bsz, C_SLANES))
        ai = jnp.broadcast_to(ai_ref[:, cols], (bsz, C_SLANES))

        def step(t, carry):
            xr, xi = carry
            rows = pl.ds(pl.multiple_of(t * bsz, bsz), bsz)
            nr = ar * xr - ai * xi + xr_s[rows, cols]
            ni = ar * xi + ai * xr + xi_s[rows, cols]
            xr_s[rows, cols] = nr
            xi_s[rows, cols] = ni
            return nr, ni

        xr, xi = lax.fori_loop(0, tt, step, (sr_s[:, cols], si_s[:, cols]), unroll=4)
        sr_s[:, cols] = xr
        si_s[:, cols] = xi

    parts = [_dot(xr_s[:, blk[m]], cr_ref[m]) - _dot(xi_s[:, blk[m]], ci_ref[m]) for m in range(C_BLOCKS)]
    y = jnp.concatenate(parts, axis=1) + d_ref[...] * u
    y = jax.nn.gelu(y)
    y = y * _sigmoid(_dot(y, wg_ref[...]) + bg_ref[...])
    y_ref[...] = pltpu.einshape("tbc->btc", y.reshape(tt, bsz, C_WIDTH))


def _s5(zmix, bsz, seq, p, tt):
    f32 = F32
    dt = jnp.exp(p["c_log_dt"].astype(f32))[:, None]
    lr, li = p["c_lam_re"].astype(f32), p["c_lam_im"].astype(f32)
    mag = jnp.exp(lr * dt)
    ab_re, ab_im = mag * jnp.cos(li * dt), mag * jnp.sin(li * dt)
    den = lr * lr + li * li
    f_re = ((ab_re - 1.0) * lr + ab_im * li) / den
    f_im = (ab_im * lr - (ab_re - 1.0) * li) / den
    bre, bim = p["c_b_re"].astype(f32), p["c_b_im"].astype(f32)
    bb_re = f_re[..., None] * bre - f_im[..., None] * bim
    bb_im = f_re[..., None] * bim + f_im[..., None] * bre
    eye = jnp.eye(C_GROUPS, dtype=f32)

    def in_blocks(bb):
        full = jnp.einsum("gpc,gh->gchp", bb, eye).reshape(C_WIDTH, C_LANES)
        return jnp.stack([full[m * C_ULANES:(m + 1) * C_ULANES, m * C_SLANES:(m + 1) * C_SLANES]
                          for m in range(C_BLOCKS)]).astype(BF16)

    def out_blocks(cc):
        full = jnp.einsum("gcp,gh->gphc", cc.astype(f32), eye).reshape(C_LANES, C_WIDTH)
        return jnp.stack([full[m * C_SLANES:(m + 1) * C_SLANES, m * C_ULANES:(m + 1) * C_ULANES]
                          for m in range(C_BLOCKS)]).astype(BF16)

    consts = [in_blocks(bb_re), in_blocks(bb_im), ab_re.reshape(1, C_LANES), ab_im.reshape(1, C_LANES),
              out_blocks(p["c_c_re"]), out_blocks(p["c_c_im"]), p["c_d"].reshape(1, C_WIDTH).astype(f32),
              p["c_w_glu"].astype(BF16), p["c_b_glu"].reshape(1, C_WIDTH).astype(f32)]
    rows = tt * bsz
    return pl.pallas_call(
        _s5_kernel,
        grid=(seq // tt,),
        in_specs=[pl.BlockSpec((bsz, tt, C_WIDTH), lambda i: (0, i, U_BLK))] + [_full(a.shape) for a in consts],
        out_specs=pl.BlockSpec((bsz, tt, C_WIDTH), lambda i: (0, i, 0)),
        out_shape=jax.ShapeDtypeStruct((bsz, seq, C_WIDTH), F32),
        scratch_shapes=[pltpu.VMEM((rows, C_LANES), F32), pltpu.VMEM((rows, C_LANES), F32),
                        pltpu.VMEM((bsz, C_LANES), F32), pltpu.VMEM((bsz, C_LANES), F32)],
        compiler_params=_cparams("arbitrary"),
        name="s5",
    )(zmix, *consts)


def _merge_kernel(alpha, x_ref, ya_ref, yb_ref, yc_ref, wgate_ref, bgate_ref, wba_ref, wbb_ref, wbc_ref,
                  wout_ref, g_ref, b_ref, o_ref):
    d = x_ref.shape[1]
    x = x_ref[...]
    gates = _sigmoid(_dot(x, wgate_ref[...]) + bgate_ref[...])
    merged = (gates[:, :d] * _dot(ya_ref[...], wba_ref[...])
              + gates[:, d:2 * d] * _dot(yb_ref[...], wbb_ref[...])
              + gates[:, 2 * d:] * _dot(yc_ref[...], wbc_ref[...]))
    o_ref[...] = _layer_norm(alpha * x + _dot(merged, wout_ref[...]), g_ref[...], b_ref[...])


def _merge(x2d, ya, yb, yc, wgate, bgate, wb, wout, g, b, alpha, tm):
    m, d = x2d.shape
    wba, wbb, wbc = wb[:A_WIDTH], wb[A_WIDTH:A_WIDTH + B_V_WIDTH], wb[A_WIDTH + B_V_WIDTH:]
    consts = [wgate, bgate, wba, wbb, wbc, wout, g, b]
    tile = lambda n: pl.BlockSpec((tm, n), lambda i: (i, 0))
    return pl.pallas_call(
        functools.partial(_merge_kernel, alpha),
        grid=(m // tm,),
        in_specs=[tile(d), tile(A_WIDTH), tile(B_V_WIDTH), tile(C_WIDTH)] + [_resident(a.shape) for a in consts],
        out_specs=tile(d),
        out_shape=jax.ShapeDtypeStruct((m, d), F32),
        compiler_params=_cparams("parallel"),
        name="merge",
    )(x2d, ya, yb, yc, *consts)


def _ffn_kernel(alpha, tf, x_ref, w1_ref, w2_ref, g_ref, b_ref, o_ref):
    x = x_ref[...]
    xb = x.astype(BF16)
    acc = None
    for j in range(w1_ref.shape[1] // tf):
        h = jnp.maximum(jnp.dot(xb, w1_ref[:, j * tf:(j + 1) * tf], preferred_element_type=F32), 0.0)
        part = jnp.dot((h * h).astype(BF16), w2_ref[j * tf:(j + 1) * tf, :], preferred_element_type=F32)
        acc = part if acc is None else acc + part
    o_ref[...] = _layer_norm(alpha * x + acc, g_ref[...], b_ref[...])


def _ffn(x2d, w1, w2, g, b, alpha, tm, tf):
    m, d = x2d.shape
    return pl.pallas_call(
        functools.partial(_ffn_kernel, alpha, tf),
        grid=(m // tm,),
        in_specs=[pl.BlockSpec((tm, d), lambda i: (i, 0)),
                  _resident(w1.shape), _resident(w2.shape), _full(g.shape), _full(b.shape)],
        out_specs=pl.BlockSpec((tm, d), lambda i: (i, 0)),
        out_shape=jax.ShapeDtypeStruct((m, d), F32),
        compiler_params=_cparams("parallel"),
        name="ffn",
    )(x2d, w1, w2, g, b)


def _tile(n, want):
    t = min(n, want)
    assert n % t == 0, (n, want)
    return t


def kernel(x, w_in, b_gate, a_shift, a_w0, a_w2, a_a0, a_a2, a_g2, a_kk, a_ka, a_rk, a_lnx_g, a_lnx_b, c_lam_re, c_lam_im, c_log_dt, c_b_re, c_b_im, c_c_re, c_c_im, c_d, c_w_glu, c_b_glu, w_branch, w_out, ln1_g, ln1_b, w_ff1, w_ff2, ln2_g, ln2_b):
    bsz, seq, d = x.shape
    depth = w_in.shape[0]
    alpha = (2.0 * depth) ** 0.25
    tokens = bsz * seq
    per_layer = dict(a_shift=a_shift, a_w0=a_w0, a_w2=a_w2, a_a0=a_a0, a_a2=a_a2, a_g2=a_g2, a_kk=a_kk,
                     a_ka=a_ka, a_rk=a_rk, a_lnx_g=a_lnx_g, a_lnx_b=a_lnx_b, c_lam_re=c_lam_re,
                     c_lam_im=c_lam_im, c_log_dt=c_log_dt, c_b_re=c_b_re, c_b_im=c_b_im, c_c_re=c_c_re,
                     c_c_im=c_c_im, c_d=c_d, c_w_glu=c_w_glu, c_b_glu=c_b_glu)
    xt = x.reshape(tokens, d)
    for l in range(depth):
        p = {k: v[l] for k, v in per_layer.items()}
        wl = w_in[l]
        w_mix = jnp.concatenate([wl[:, A_PROJ:MIX_COLS], wl[:, :A_PROJ]], axis=1).astype(BF16)
        w_gate = wl[:, MIX_COLS:].astype(BF16)
        zmix = _proj(xt, w_mix, _tile(tokens, 512)).reshape(bsz, seq, MIX_COLS)
        ya = _rwkv(zmix, bsz, seq, p, _tile(seq, 256))
        yb = _retention(zmix, bsz, seq, _tile(seq, 512))
        yc = _s5(zmix, bsz, seq, p, _tile(seq, 64))
        x1 = _merge(xt, ya.reshape(tokens, -1), yb.reshape(tokens, -1), yc.reshape(tokens, -1),
                    w_gate, b_gate[l].reshape(1, -1), w_branch[l].astype(BF16), w_out[l].astype(BF16),
                    ln1_g[l].reshape(1, -1), ln1_b[l].reshape(1, -1), alpha, _tile(tokens, 512))
        xt = _ffn(x1, w_ff1[l].astype(BF16), w_ff2[l].astype(BF16), ln2_g[l].reshape(1, -1),
                  ln2_b[l].reshape(1, -1), alpha, _tile(tokens, 1024), 1024)
    return xt.reshape(bsz, seq, d)
```

```python
import functools
import math

import jax
import jax.numpy as jnp
from jax import lax
from jax.experimental import pallas as pl
from jax.experimental.pallas import tpu as pltpu

F32 = jnp.float32
BF16 = jnp.bfloat16

A_HEADS = 8
A_HEAD_DIM = 64
A_WIDTH = A_HEADS * A_HEAD_DIM
A_DECAY_LORA = 64
A_ICLR_LORA = 64
A_GATE_LORA = 128
A_PROJ = 3 * A_WIDTH + A_DECAY_LORA + A_ICLR_LORA + A_GATE_LORA
A_GN_EPS = 64e-5
A_CHUNK = 64
A_GROUP = 4
A_GLANES = 256

B_HEADS = 4
B_QK_DIM = 128
B_V_DIM = 256
B_QK_WIDTH = B_HEADS * B_QK_DIM
B_V_WIDTH = B_HEADS * B_V_DIM
B_CHUNK = 128
B_ROPE_BASE = 10000.0
B_GN_EPS = 1e-5

C_WIDTH = 512
C_GROUP = 16
C_GROUPS = C_WIDTH // C_GROUP
C_STATE = 64
C_LANES = C_GROUPS * C_STATE
C_BLOCKS = 4
C_ULANES = C_WIDTH // C_BLOCKS
C_SLANES = C_LANES // C_BLOCKS

LN_EPS = 1e-5

RET_COLS = 2 * B_QK_WIDTH + 2 * B_V_WIDTH
MIX_COLS = RET_COLS + C_WIDTH + A_PROJ
RET_BLK = 0
U_BLK = RET_COLS // C_WIDTH
Z_BLK = (RET_COLS + C_WIDTH) // A_PROJ
assert RET_COLS % C_WIDTH == 0 and (RET_COLS + C_WIDTH) % A_PROJ == 0

V7X_VMEM_LIMIT_BYTES = 56 * 1024 * 1024


def _cparams(*sem):
    return pltpu.CompilerParams(dimension_semantics=sem, vmem_limit_bytes=V7X_VMEM_LIMIT_BYTES)


def _full(shape):
    n = len(shape)
    return pl.BlockSpec(shape, lambda *_: (0,) * n)


def _resident(shape):
    n = len(shape)
    return pl.BlockSpec(shape, lambda *_: (0,) * n, pipeline_mode=pl.Buffered(1))


def _dot(a, b):
    return jnp.dot(a.astype(BF16), b.astype(BF16), preferred_element_type=F32)


def _dot_nt(a, b):
    return lax.dot_general(a.astype(BF16), b.astype(BF16), (((1,), (1,)), ((), ())),
                           preferred_element_type=F32)


def _dot_tn(a, b):
    return lax.dot_general(a.astype(BF16), b.astype(BF16), (((0,), (0,)), ((), ())),
                           preferred_element_type=F32)


def _split2(x):
    hi = x.astype(BF16)
    lo = (x - hi.astype(F32)).astype(BF16)
    return hi, lo


def _split3(x):
    hi = x.astype(BF16)
    r1 = x - hi.astype(F32)
    mid = r1.astype(BF16)
    lo = (r1 - mid.astype(F32)).astype(BF16)
    return hi, mid, lo


def _sigmoid(x):
    return 1.0 / (1.0 + jnp.exp(-x))


def _layer_norm(y, g, b):
    mu = jnp.mean(y, axis=-1, keepdims=True)
    d = y - mu
    var = jnp.mean(d * d, axis=-1, keepdims=True)
    return d * lax.rsqrt(var + LN_EPS) * g + b


def _proj_kernel(x_ref, w_ref, o_ref):
    o_ref[...] = jnp.dot(x_ref[...].astype(BF16), w_ref[...], preferred_element_type=F32).astype(o_ref.dtype)


def _proj(x2d, w_bf16, tm):
    m, k = x2d.shape
    n = w_bf16.shape[1]
    return pl.pallas_call(
        _proj_kernel,
        grid=(m // tm,),
        in_specs=[pl.BlockSpec((tm, k), lambda i: (i, 0)), _resident(w_bf16.shape)],
        out_specs=pl.BlockSpec((tm, n), lambda i: (i, 0)),
        out_shape=jax.ShapeDtypeStruct((m, n), BF16),
        compiler_params=_cparams("parallel"),
        name="proj",
    )(x2d, w_bf16)


def _chunk_rows(c):
    return pl.ds(c * A_CHUNK, A_CHUNK)


def _rwkv_kernel(z_ref, mu_ref, w0_ref, a0_ref, lora_ref, g2_ref, kkp_ref, kap_ref, rkp_ref,
                 lng_ref, lnb_ref, ones_ref, tri_ref, y_ref,
                 carry_s, state_s, r_s, k_s, v_s, kn_s, ia_s, lw_s, o_s, rk_s, gate_s,
                 gam_s, x_s, m2_s, q_s, op_s):
    tc = z_ref.shape[0]
    n_chunks = tc // A_CHUNK
    w = A_WIDTH

    @pl.when(pl.program_id(1) == 0)
    def _():
        carry_s[...] = jnp.zeros_like(carry_s)
        state_s[...] = jnp.zeros_like(state_s)

    ones = ones_ref[...]

    gw = A_GLANES
    groups = [slice(g * gw, (g + 1) * gw) for g in range(w // gw)]

    def seg_sum(x):
        xb = x.astype(BF16)
        return jnp.concatenate([jnp.dot(xb[:, s], ones, preferred_element_type=F32) for s in groups], axis=1)

    z = z_ref[...].astype(F32)
    rolled = pltpu.roll(z, 1, 0)
    rowid = lax.broadcasted_iota(jnp.int32, z.shape, 0)
    prev = jnp.where(rowid == 0, jnp.broadcast_to(carry_s[0:1, :], z.shape), rolled)
    zs = z + mu_ref[...] * (prev - z)
    carry_s[0:1, :] = z[tc - 1:tc, :]

    lz = zs[:, 3 * w:3 * w + 128]
    lane = lax.broadcasted_iota(jnp.int32, lz.shape, 1)
    lin = jnp.where(lane < A_DECAY_LORA, jnp.tanh(lz), lz)
    wa = _dot(lin, lora_ref[...])
    lw_s[...] = (-math.exp(-0.5)) * _sigmoid(w0_ref[...] + wa[:, :w])
    ia = _sigmoid(a0_ref[...] + wa[:, w:])
    ia_s[...] = ia
    gate_s[...] = _dot(_sigmoid(zs[:, 3 * w + 128:3 * w + 256]), g2_ref[...])

    r = zs[:, :w]
    k = zs[:, w:2 * w]
    v = zs[:, 2 * w:3 * w]
    kk = k * kkp_ref[...]
    kn_s[...] = kk * lax.rsqrt(jnp.maximum(seg_sum(kk * kk), 1e-24))
    kmod = k * (1.0 + (ia - 1.0) * kap_ref[...])
    r_s[...] = r
    k_s[...] = kmod
    v_s[...] = v
    rk_s[...] = seg_sum(r * kmod * rkp_ref[...])

    tri = tri_ref[...]
    hpg = gw // A_HEAD_DIM
    rid = lax.broadcasted_iota(jnp.int32, (A_CHUNK, gw), 0)
    cid = lax.broadcasted_iota(jnp.int32, (A_CHUNK, gw), 1) % A_HEAD_DIM
    strict = rid > cid
    incl = rid >= cid
    eye = (rid == cid).astype(F32)
    brow = lax.broadcasted_iota(jnp.int32, (gw, gw), 0) // A_HEAD_DIM
    bcol = lax.broadcasted_iota(jnp.int32, (gw, gw), 1) // A_HEAD_DIM
    same_head = brow == bcol
    same_head_bf = same_head.astype(BF16)
    n = A_HEAD_DIM

    def bd(x):
        xb = x.astype(BF16)
        return jnp.concatenate([xb] * hpg, axis=0) * same_head_bf


    group = min(A_GROUP, n_chunks)
    assert n_chunks % group == 0

    def state_free_part(gi, carry):
        items = []
        for j in range(group):
            c = gi * group + j
            rows = _chunk_rows(c)
            lw = lw_s[rows, :]
            h3 = _split3(lw)
            cum = (jnp.dot(tri, h3[0], preferred_element_type=F32)
                   + jnp.dot(tri, h3[1], preferred_element_type=F32)
                   + jnp.dot(tri, h3[2], preferred_element_type=F32))
            e_in = jnp.exp(cum)
            e_ex = jnp.exp(cum - lw)
            e_ng = jnp.exp(-cum)
            kn = kn_s[rows, :]
            rt = r_s[rows, :] * e_in
            at = -kn * e_ex
            bt = kn * ia_s[rows, :] * e_ng
            kt = k_s[rows, :] * e_ng
            vv = v_s[rows, :]
            gam_s[pl.ds(c, 1), :] = e_in[A_CHUNK - 1:A_CHUNK, :]
            for g, s in enumerate(groups):
                items.append((c * len(groups) + g, at[:, s], rt[:, s], bt[:, s], kt[:, s], vv[:, s]))
        ids = range(len(items))
        idx = [it[0] for it in items]
        a_ = [it[1] for it in items]
        r_ = [it[2] for it in items]
        b_ = [it[3] for it in items]
        k_ = [it[4] for it in items]
        v_ = [it[5] for it in items]
        ar = [jnp.concatenate([a_[i], r_[i]], axis=0) for i in ids]
        gb = [_dot_nt(ar[i], bd(b_[i])) for i in ids]
        gk = [_dot_nt(ar[i], bd(k_[i])) for i in ids]
        l_ab = [jnp.where(strict, gb[i][:A_CHUNK], 0.0) for i in ids]
        a_qb = [jnp.where(incl, gb[i][A_CHUNK:], 0.0) for i in ids]
        akq = [jnp.concatenate([jnp.where(strict, gk[i][:A_CHUNK], 0.0),
                                jnp.where(incl, gk[i][A_CHUNK:], 0.0)], axis=0) for i in ids]
        akqv = [_dot(akq[i], bd(v_[i])) for i in ids]
        akv = [akqv[i][:A_CHUNK] for i in ids]
        ov = [akqv[i][A_CHUNK:] for i in ids]
        tinv = [eye + l_ab[i] for i in ids]
        p = [_dot(l_ab[i], bd(l_ab[i])) for i in ids]
        for _ in range(4):
            tp = [_dot(jnp.concatenate([tinv[i], p[i]], axis=0), bd(p[i])) for i in ids]
            tinv = [tinv[i] + tp[i][:A_CHUNK] for i in ids]
            p = [tp[i][A_CHUNK:] for i in ids]
        tinv = [tinv[i] + _dot(tinv[i], bd(p[i])) for i in ids]
        wu = [_dot(tinv[i], jnp.concatenate([bd(a_[i]), bd(akv[i])], axis=1)) for i in ids]
        wm = [wu[i][:, :gw] for i in ids]
        uv = [wu[i][:, gw:] for i in ids]
        for i in ids:
            x_s[idx[i]] = jnp.where(same_head, _dot_tn(wm[i], b_[i]), 0.0)
        for i in ids:
            m2 = jnp.where(same_head, _dot_tn(jnp.concatenate([uv[i], v_[i]], axis=0),
                                              jnp.concatenate([b_[i], k_[i]], axis=0)), 0.0)
            m2_s[idx[i]] = sum(m2[h * n:(h + 1) * n] for h in range(1, hpg)) + m2[:n]
        qo = [_dot(a_qb[i], jnp.concatenate([bd(wm[i]), bd(uv[i])], axis=1)) for i in ids]
        for i in ids:
            q_s[idx[i]] = r_[i] + qo[i][:, :gw]
        for i in ids:
            op_s[idx[i]] = qo[i][:, gw:] + ov[i]
        return carry

    def state_part(c, carry):
        rows = _chunk_rows(c)
        gam = gam_s[pl.ds(c, 1), :]
        gids = range(len(groups))
        s0 = [state_s[g] for g in gids]
        sx = [_dot(s0[g], x_s[c * len(groups) + g]) for g in gids]
        o = [_dot_nt(q_s[c * len(groups) + g], bd(s0[g])) + op_s[c * len(groups) + g] for g in gids]
        for g in gids:
            state_s[g] = (s0[g] + sx[g] + m2_s[c * len(groups) + g]) * gam[:, groups[g]]
        o_s[rows, :] = jnp.concatenate(o, axis=1)
        return carry

    for gi in range(n_chunks // group):
        state_free_part(gi, 0)
    for c in range(n_chunks):
        state_part(c, 0)

    o = o_s[...]
    inv_n = 1.0 / n
    mean = seg_sum(o) * inv_n
    d = o - mean
    var = seg_sum(d * d) * inv_n
    on = d * lax.rsqrt(var + A_GN_EPS) * lng_ref[...] + lnb_ref[...]
    on = on + rk_s[...] * v_s[...]
    y_ref[...] = on * gate_s[...]


def _rwkv(zmix, bsz, seq, p, tc):
    zblk = Z_BLK
    row = lambda a: a.reshape(1, -1).astype(F32)
    w = A_WIDTH
    lora = jnp.zeros((128, 2 * w), F32)
    lora = lora.at[:A_DECAY_LORA, :w].set(p["a_w2"]).at[A_DECAY_LORA:, w:].set(p["a_a2"]).astype(BF16)
    hid = jnp.arange(A_GLANES) // A_HEAD_DIM
    ones = (hid[:, None] == hid[None, :]).astype(BF16)
    ti = jnp.arange(A_CHUNK)
    tri = (ti[:, None] >= ti[None, :]).astype(BF16)
    small = [row(p["a_shift"]), row(p["a_w0"]), row(p["a_a0"]), lora, p["a_g2"].astype(BF16),
             row(p["a_kk"]), row(p["a_ka"]), row(p["a_rk"]), row(p["a_lnx_g"]), row(p["a_lnx_b"]),
             ones, tri]
    stage = pltpu.VMEM((tc, w), F32)
    n_chunks = tc // A_CHUNK
    n_groups = A_WIDTH // A_GLANES
    per_sq = pltpu.VMEM((n_chunks * n_groups, A_GLANES, A_GLANES), F32)
    per_row = pltpu.VMEM((n_chunks * n_groups, A_CHUNK, A_GLANES), F32)
    return pl.pallas_call(
        _rwkv_kernel,
        grid=(bsz, seq // tc),
        in_specs=[pl.BlockSpec((None, tc, A_PROJ), lambda b, i: (b, i, zblk))]
                 + [_full(a.shape) for a in small],
        out_specs=pl.BlockSpec((None, tc, w), lambda b, i: (b, i, 0)),
        out_shape=jax.ShapeDtypeStruct((bsz, seq, w), F32),
        scratch_shapes=[pltpu.VMEM((8, A_PROJ), F32),
                        pltpu.VMEM((A_WIDTH // A_GLANES, A_HEAD_DIM, A_GLANES), F32),
                        stage, stage, stage, stage, stage, stage, stage, stage, stage,
                        pltpu.VMEM((max(8, n_chunks), w), F32),
                        per_sq, per_row, per_row, per_row],
        compiler_params=_cparams("parallel", "arbitrary"),
        name="rwkv7",
    )(zmix, *small)


def _ret_kernel(z_ref, cos_ref, sin_ref, dmask_ref, qd_ref, kd_ref, cd_ref,
                y_ref, state_s, qr_s, kr_s, upd_s, st_s):
    tc = z_ref.shape[0]
    n_chunks = tc // B_CHUNK
    k_off, v_off, g_off = B_QK_WIDTH, 2 * B_QK_WIDTH, 2 * B_QK_WIDTH + B_V_WIDTH

    @pl.when(pl.program_id(1) == 0)
    def _():
        state_s[...] = jnp.zeros_like(state_s)

    cos2 = cos_ref[...]
    sin2 = sin_ref[...]
    half = B_QK_DIM // 2

    def rope(t):
        return t * cos2 + pltpu.roll(t, half, 1) * sin2

    for h in range(B_HEADS):
        qs = slice(h * B_QK_DIM, (h + 1) * B_QK_DIM)
        qr_s[:, qs] = rope(z_ref[:, qs].astype(F32))
        kr_s[:, qs] = rope(z_ref[:, k_off + h * B_QK_DIM:k_off + (h + 1) * B_QK_DIM].astype(F32)) * (B_QK_DIM ** -0.5)

    items = [(c, h) for c in range(n_chunks) for h in range(B_HEADS)]
    rows = lambda c: slice(c * B_CHUNK, (c + 1) * B_CHUNK)
    qcol = lambda h: slice(h * B_QK_DIM, (h + 1) * B_QK_DIM)
    vcol = lambda h: slice(v_off + h * B_V_DIM, v_off + (h + 1) * B_V_DIM)

    for c, h in items:
        upd_s[c * B_HEADS + h] = _dot_tn(kr_s[rows(c), qcol(h)] * kd_ref[h], z_ref[rows(c), vcol(h)])
    for h in range(B_HEADS):
        st = state_s[h]
        for c in range(n_chunks):
            st_s[c * B_HEADS + h] = st
            st = cd_ref[h] * st + upd_s[c * B_HEADS + h]
        state_s[h] = st

    for c, h in items:
        qc = qr_s[rows(c), qcol(h)]
        scores = _dot_nt(qc, kr_s[rows(c), qcol(h)]) * dmask_ref[h]
        o = _dot(scores, z_ref[rows(c), vcol(h)]) + _dot(qc * qd_ref[h], st_s[c * B_HEADS + h])
        mu = jnp.mean(o, axis=-1, keepdims=True)
        d = o - mu
        var = jnp.mean(d * d, axis=-1, keepdims=True)
        on = d * lax.rsqrt(var + B_GN_EPS)
        g = z_ref[rows(c), g_off + h * B_V_DIM:g_off + (h + 1) * B_V_DIM].astype(F32)
        y_ref[rows(c), h * B_V_DIM:(h + 1) * B_V_DIM] = g * _sigmoid(g) * on


def _retention(zmix, bsz, seq, tc):
    f32 = F32
    pos = jnp.arange(seq, dtype=f32)
    half = B_QK_DIM // 2
    inv_freq = B_ROPE_BASE ** (-jnp.arange(half, dtype=f32) / half)
    ang = pos[:, None] * inv_freq[None, :]
    cos, sin = jnp.cos(ang), jnp.sin(ang)
    cos2 = jnp.concatenate([cos, cos], axis=1)
    sin2 = jnp.concatenate([-sin, sin], axis=1)
    log_gamma = jnp.log(1.0 - 2.0 ** (-5.0 - jnp.arange(B_HEADS, dtype=f32)))
    idx = jnp.arange(B_CHUNK, dtype=f32)
    rel = idx[:, None] - idx[None, :]
    dmask = jnp.where(rel >= 0, jnp.exp(log_gamma[:, None, None] * jnp.maximum(rel, 0.0)), 0.0)
    qd = jnp.broadcast_to(jnp.exp(log_gamma[:, None] * (idx + 1.0))[:, :, None],
                          (B_HEADS, B_CHUNK, B_QK_DIM))
    kd = jnp.broadcast_to(jnp.exp(log_gamma[:, None] * (B_CHUNK - 1.0 - idx))[:, :, None],
                          (B_HEADS, B_CHUNK, B_QK_DIM))
    cd = jnp.broadcast_to(jnp.exp(log_gamma * B_CHUNK)[:, None, None], (B_HEADS, B_QK_DIM, B_V_DIM))
    tabs = [dmask, qd, kd, cd]
    n_items = (tc // B_CHUNK) * B_HEADS
    per_item = pltpu.VMEM((n_items, B_QK_DIM, B_V_DIM), F32)
    return pl.pallas_call(
        _ret_kernel,
        grid=(bsz, seq // tc),
        in_specs=[pl.BlockSpec((None, tc, RET_COLS), lambda b, i: (b, i, RET_BLK)),
                  pl.BlockSpec((tc, B_QK_DIM), lambda b, i: (i, 0)),
                  pl.BlockSpec((tc, B_QK_DIM), lambda b, i: (i, 0))] + [_full(a.shape) for a in tabs],
        out_specs=pl.BlockSpec((None, tc, B_V_WIDTH), lambda b, i: (b, i, 0)),
        out_shape=jax.ShapeDtypeStruct((bsz, seq, B_V_WIDTH), F32),
        scratch_shapes=[pltpu.VMEM((B_HEADS, B_QK_DIM, B_V_DIM), F32),
                        pltpu.VMEM((tc, B_QK_WIDTH), F32), pltpu.VMEM((tc, B_QK_WIDTH), F32),
                        per_item, per_item],
        compiler_params=_cparams("parallel", "arbitrary"),
        name="retention",
    )(zmix, cos2, sin2, *tabs)


def _s5_kernel(u_ref, bbr_ref, bbi_ref, ar_ref, ai_ref, cr_ref, ci_ref, d_ref, wg_ref, bg_ref,
               y_ref, xr_s, xi_s, sr_s, si_s):
    bsz, tt, _ = u_ref.shape

    @pl.when(pl.program_id(0) == 0)
    def _():
        sr_s[...] = jnp.zeros_like(sr_s)
        si_s[...] = jnp.zeros_like(si_s)

    u = pltpu.einshape("btc->tbc", u_ref[...].astype(F32)).reshape(tt * bsz, C_WIDTH)
    ub = u.astype(BF16)
    blk = [slice(m * C_SLANES, (m + 1) * C_SLANES) for m in range(C_BLOCKS)]

    for m in range(C_BLOCKS):
        um = ub[:, m * C_ULANES:(m + 1) * C_ULANES]
        xr_s[:, blk[m]] = jnp.dot(um, bbr_ref[m], preferred_element_type=F32)
        xi_s[:, blk[m]] = jnp.dot(um, bbi_ref[m], preferred_element_type=F32)

    for m in range(C_BLOCKS):
        cols = blk[m]
        ar = jnp.broadcast_to(ar_ref[:, cols], (bsz, C_SLANES))
        ai = jnp.broadcast_to(ai_ref[:, cols], (bsz, C_SLANES))

        def step(t, carry):
            xr, xi = carry
            rows = pl.ds(pl.multiple_of(t * bsz, bsz), bsz)
            nr = ar * xr - ai * xi + xr_s[rows, cols]
            ni = ar * xi + ai * xr + xi_s[rows, cols]
            xr_s[rows, cols] = nr
            xi_s[rows, cols] = ni
            return nr, ni

        xr, xi = lax.fori_loop(0, tt, step, (sr_s[:, cols], si_s[:, cols]), unroll=4)
        sr_s[:, cols] = xr
        si_s[:, cols] = xi

    parts = [_dot(xr_s[:, blk[m]], cr_ref[m]) - _dot(xi_s[:, blk[m]], ci_ref[m]) for m in range(C_BLOCKS)]
    y = jnp.concatenate(parts, axis=1) + d_ref[...] * u
    y = jax.nn.gelu(y)
    y = y * _sigmoid(_dot(y, wg_ref[...]) + bg_ref[...])
    y_ref[...] = pltpu.einshape("tbc->btc", y.reshape(tt, bsz, C_WIDTH))


def _s5(zmix, bsz, seq, p, tt):
    f32 = F32
    dt = jnp.exp(p["c_log_dt"].astype(f32))[:, None]
    lr, li = p["c_lam_re"].astype(f32), p["c_lam_im"].astype(f32)
    mag = jnp.exp(lr * dt)
    ab_re, ab_im = mag * jnp.cos(li * dt), mag * jnp.sin(li * dt)
    den = lr * lr + li * li
    f_re = ((ab_re - 1.0) * lr + ab_im * li) / den
    f_im = (ab_im * lr - (ab_re - 1.0) * li) / den
    bre, bim = p["c_b_re"].astype(f32), p["c_b_im"].astype(f32)
    bb_re = f_re[..., None] * bre - f_im[..., None] * bim
    bb_im = f_re[..., None] * bim + f_im[..., None] * bre
    eye = jnp.eye(C_GROUPS, dtype=f32)

    def in_blocks(bb):
        full = jnp.einsum("gpc,gh->gchp", bb, eye).reshape(C_WIDTH, C_LANES)
        return jnp.stack([full[m * C_ULANES:(m + 1) * C_ULANES, m * C_SLANES:(m + 1) * C_SLANES]
                          for m in range(C_BLOCKS)]).astype(BF16)

    def out_blocks(cc):
        full = jnp.einsum("gcp,gh->gphc", cc.astype(f32), eye).reshape(C_LANES, C_WIDTH)
        return jnp.stack([full[m * C_SLANES:(m + 1) * C_SLANES, m * C_ULANES:(m + 1) * C_ULANES]
                          for m in range(C_BLOCKS)]).astype(BF16)

    consts = [in_blocks(bb_re), in_blocks(bb_im), ab_re.reshape(1, C_LANES), ab_im.reshape(1, C_LANES),
              out_blocks(p["c_c_re"]), out_blocks(p["c_c_im"]), p["c_d"].reshape(1, C_WIDTH).astype(f32),
              p["c_w_glu"].astype(BF16), p["c_b_glu"].reshape(1, C_WIDTH).astype(f32)]
    rows = tt * bsz
    return pl.pallas_call(
        _s5_kernel,
        grid=(seq // tt,),
        in_specs=[pl.BlockSpec((bsz, tt, C_WIDTH), lambda i: (0, i, U_BLK))] + [_full(a.shape) for a in consts],
        out_specs=pl.BlockSpec((bsz, tt, C_WIDTH), lambda i: (0, i, 0)),
        out_shape=jax.ShapeDtypeStruct((bsz, seq, C_WIDTH), F32),
        scratch_shapes=[pltpu.VMEM((rows, C_LANES), F32), pltpu.VMEM((rows, C_LANES), F32),
                        pltpu.VMEM((bsz, C_LANES), F32), pltpu.VMEM((bsz, C_LANES), F32)],
        compiler_params=_cparams("arbitrary"),
        name="s5",
    )(zmix, *consts)


def _merge_kernel(alpha, x_ref, ya_ref, yb_ref, yc_ref, wgate_ref, bgate_ref, wba_ref, wbb_ref, wbc_ref,
                  wout_ref, g_ref, b_ref, o_ref):
    d = x_ref.shape[1]
    x = x_ref[...]
    gates = _sigmoid(_dot(x, wgate_ref[...]) + bgate_ref[...])
    merged = (gates[:, :d] * _dot(ya_ref[...], wba_ref[...])
              + gates[:, d:2 * d] * _dot(yb_ref[...], wbb_ref[...])
              + gates[:, 2 * d:] * _dot(yc_ref[...], wbc_ref[...]))
    o_ref[...] = _layer_norm(alpha * x + _dot(merged, wout_ref[...]), g_ref[...], b_ref[...])


def _merge(x2d, ya, yb, yc, wgate, bgate, wb, wout, g, b, alpha, tm):
    m, d = x2d.shape
    wba, wbb, wbc = wb[:A_WIDTH], wb[A_WIDTH:A_WIDTH + B_V_WIDTH], wb[A_WIDTH + B_V_WIDTH:]
    consts = [wgate, bgate, wba, wbb, wbc, wout, g, b]
    tile = lambda n: pl.BlockSpec((tm, n), lambda i: (i, 0))
    return pl.pallas_call(
        functools.partial(_merge_kernel, alpha),
        grid=(m // tm,),
        in_specs=[tile(d), tile(A_WIDTH), tile(B_V_WIDTH), tile(C_WIDTH)] + [_resident(a.shape) for a in consts],
        out_specs=tile(d),
        out_shape=jax.ShapeDtypeStruct((m, d), F32),
        compiler_params=_cparams("parallel"),
        name="merge",
    )(x2d, ya, yb, yc, *consts)


def _ffn_kernel(alpha, tf, x_ref, w1_ref, w2_ref, g_ref, b_ref, o_ref):
    x = x_ref[...]
    xb = x.astype(BF16)
    acc = None
    for j in range(w1_ref.shape[1] // tf):
        h = jnp.maximum(jnp.dot(xb, w1_ref[:, j * tf:(j + 1) * tf], preferred_element_type=F32), 0.0)
        part = jnp.dot((h * h).astype(BF16), w2_ref[j * tf:(j + 1) * tf, :], preferred_element_type=F32)
        acc = part if acc is None else acc + part
    o_ref[...] = _layer_norm(alpha * x + acc, g_ref[...], b_ref[...])


def _ffn(x2d, w1, w2, g, b, alpha, tm, tf):
    m, d = x2d.shape
    return pl.pallas_call(
        functools.partial(_ffn_kernel, alpha, tf),
        grid=(m // tm,),
        in_specs=[pl.BlockSpec((tm, d), lambda i: (i, 0)),
                  _resident(w1.shape), _resident(w2.shape), _full(g.shape), _full(b.shape)],
        out_specs=pl.BlockSpec((tm, d), lambda i: (i, 0)),
        out_shape=jax.ShapeDtypeStruct((m, d), F32),
        compiler_params=_cparams("parallel"),
        name="ffn",
    )(x2d, w1, w2, g, b)


def _tile(n, want):
    t = min(n, want)
    assert n % t == 0, (n, want)
    return t


def kernel(x, w_in, b_gate, a_shift, a_w0, a_w2, a_a0, a_a2, a_g2, a_kk, a_ka, a_rk, a_lnx_g, a_lnx_b, c_lam_re, c_lam_im, c_log_dt, c_b_re, c_b_im, c_c_re, c_c_im, c_d, c_w_glu, c_b_glu, w_branch, w_out, ln1_g, ln1_b, w_ff1, w_ff2, ln2_g, ln2_b):
    bsz, seq, d = x.shape
    depth = w_in.shape[0]
    alpha = (2.0 * depth) ** 0.25
    tokens = bsz * seq
    per_layer = dict(a_shift=a_shift, a_w0=a_w0, a_w2=a_w2, a_a0=a_a0, a_a2=a_a2, a_g2=a_g2, a_kk=a_kk,
                     a_ka=a_ka, a_rk=a_rk, a_lnx_g=a_lnx_g, a_lnx_b=a_lnx_b, c_lam_re=c_lam_re,
                     c_lam_im=c_lam_im, c_log_dt=c_log_dt, c_b_re=c_b_re, c_b_im=c_b_im, c_c_re=c_c_re,
                     c_c_im=c_c_im, c_d=c_d, c_w_glu=c_w_glu, c_b_glu=c_b_glu)
    xt = x.reshape(tokens, d)
    for l in range(depth):
        p = {k: v[l] for k, v in per_layer.items()}
        wl = w_in[l]
        w_mix = jnp.concatenate([wl[:, A_PROJ:MIX_COLS], wl[:, :A_PROJ]], axis=1).astype(BF16)
        w_gate = wl[:, MIX_COLS:].astype(BF16)
        zmix = _proj(xt, w_mix, _tile(tokens, 512)).reshape(bsz, seq, MIX_COLS)
        ya = _rwkv(zmix, bsz, seq, p, _tile(seq, 256))
        yb = _retention(zmix, bsz, seq, _tile(seq, 512))
        yc = _s5(zmix, bsz, seq, p, _tile(seq, 64))
        x1 = _merge(xt, ya.reshape(tokens, -1), yb.reshape(tokens, -1), yc.reshape(tokens, -1),
                    w_gate, b_gate[l].reshape(1, -1), w_branch[l].astype(BF16), w_out[l].astype(BF16),
                    ln1_g[l].reshape(1, -1), ln1_b[l].reshape(1, -1), alpha, _tile(tokens, 512))
        xt = _ffn(x1, w_ff1[l].astype(BF16), w_ff2[l].astype(BF16), ln2_g[l].reshape(1, -1),
                  ln2_b[l].reshape(1, -1), alpha, _tile(tokens, 1024), 1024)
    return xt.reshape(bsz, seq, d)
```

```python
import functools
import math

import jax
import jax.numpy as jnp
from jax import lax
from jax.experimental import pallas as pl
from jax.experimental.pallas import tpu as pltpu

F32 = jnp.float32
BF16 = jnp.bfloat16

A_HEADS = 8
A_HEAD_DIM = 64
A_WIDTH = A_HEADS * A_HEAD_DIM
A_DECAY_LORA = 64
A_ICLR_LORA = 64
A_GATE_LORA = 128
A_PROJ = 3 * A_WIDTH + A_DECAY_LORA + A_ICLR_LORA + A_GATE_LORA
A_GN_EPS = 64e-5
A_CHUNK = 64
A_GROUP = 4
A_GLANES = 256

B_HEADS = 4
B_QK_DIM = 128
B_V_DIM = 256
B_QK_WIDTH = B_HEADS * B_QK_DIM
B_V_WIDTH = B_HEADS * B_V_DIM
B_CHUNK = 128
B_ROPE_BASE = 10000.0
B_GN_EPS = 1e-5

C_WIDTH = 512
C_GROUP = 16
C_GROUPS = C_WIDTH // C_GROUP
C_STATE = 64
C_LANES = C_GROUPS * C_STATE
C_BLOCKS = 4
C_ULANES = C_WIDTH // C_BLOCKS
C_SLANES = C_LANES // C_BLOCKS

LN_EPS = 1e-5

RET_COLS = 2 * B_QK_WIDTH + 2 * B_V_WIDTH
MIX_COLS = RET_COLS + C_WIDTH + A_PROJ
RET_BLK = 0
U_BLK = RET_COLS // C_WIDTH
Z_BLK = (RET_COLS + C_WIDTH) // A_PROJ
assert RET_COLS % C_WIDTH == 0 and (RET_COLS + C_WIDTH) % A_PROJ == 0

V7X_VMEM_LIMIT_BYTES = 56 * 1024 * 1024


def _cparams(*sem):
    return pltpu.CompilerParams(dimension_semantics=sem, vmem_limit_bytes=V7X_VMEM_LIMIT_BYTES)


def _full(shape):
    n = len(shape)
    return pl.BlockSpec(shape, lambda *_: (0,) * n)


def _layer_spec(a, l):
    n = a.ndim - 1
    return pl.BlockSpec((None,) + a.shape[1:], lambda *_: (l,) + (0,) * n, pipeline_mode=pl.Buffered(1))


def _dot(a, b):
    return jnp.dot(a.astype(BF16), b.astype(BF16), preferred_element_type=F32)


def _dot_nt(a, b):
    return lax.dot_general(a.astype(BF16), b.astype(BF16), (((1,), (1,)), ((), ())),
                           preferred_element_type=F32)


def _dot_tn(a, b):
    return lax.dot_general(a.astype(BF16), b.astype(BF16), (((0,), (0,)), ((), ())),
                           preferred_element_type=F32)


def _split3(x):
    hi = x.astype(BF16)
    r1 = x - hi.astype(F32)
    mid = r1.astype(BF16)
    lo = (r1 - mid.astype(F32)).astype(BF16)
    return hi, mid, lo


def _sigmoid(x):
    return 1.0 / (1.0 + jnp.exp(-x))


def _layer_norm(y, g, b):
    mu = jnp.mean(y, axis=-1, keepdims=True)
    d = y - mu
    var = jnp.mean(d * d, axis=-1, keepdims=True)
    return d * lax.rsqrt(var + LN_EPS) * g + b


def _proj_kernel(x_ref, w_ref, o_ref):
    o_ref[...] = jnp.dot(x_ref[...].astype(BF16), w_ref[...], preferred_element_type=F32).astype(o_ref.dtype)


def _proj(x2d, w_bf16, l, tm):
    m, k = x2d.shape
    n = w_bf16.shape[2]
    return pl.pallas_call(
        _proj_kernel,
        grid=(m // tm,),
        in_specs=[pl.BlockSpec((tm, k), lambda i: (i, 0)), _layer_spec(w_bf16, l)],
        out_specs=pl.BlockSpec((tm, n), lambda i: (i, 0)),
        out_shape=jax.ShapeDtypeStruct((m, n), BF16),
        compiler_params=_cparams("parallel"),
        name="proj",
    )(x2d, w_bf16)


def _chunk_rows(c):
    return pl.ds(c * A_CHUNK, A_CHUNK)


def _rwkv_kernel(z_ref, mu_ref, w0_ref, a0_ref, lora_ref, g2_ref, kkp_ref, kap_ref, rkp_ref,
                 lng_ref, lnb_ref, ones_ref, tri_ref, y_ref,
                 carry_s, state_s, r_s, k_s, v_s, kn_s, ia_s, lw_s, o_s, rk_s, gate_s,
                 gam_s, x_s, m2_s, q_s, op_s):
    tc = z_ref.shape[0]
    n_chunks = tc // A_CHUNK
    w = A_WIDTH

    @pl.when(pl.program_id(1) == 0)
    def _():
        carry_s[...] = jnp.zeros_like(carry_s)
        state_s[...] = jnp.zeros_like(state_s)

    ones = ones_ref[...]

    gw = A_GLANES
    groups = [slice(g * gw, (g + 1) * gw) for g in range(w // gw)]

    def seg_sum(x):
        xb = x.astype(BF16)
        return jnp.concatenate([jnp.dot(xb[:, s], ones, preferred_element_type=F32) for s in groups], axis=1)

    z = z_ref[...].astype(F32)
    rolled = pltpu.roll(z, 1, 0)
    rowid = lax.broadcasted_iota(jnp.int32, z.shape, 0)
    prev = jnp.where(rowid == 0, jnp.broadcast_to(carry_s[0:1, :], z.shape), rolled)
    zs = z + mu_ref[...] * (prev - z)
    carry_s[0:1, :] = z[tc - 1:tc, :]

    lz = zs[:, 3 * w:3 * w + 128]
    lane = lax.broadcasted_iota(jnp.int32, lz.shape, 1)
    lin = jnp.where(lane < A_DECAY_LORA, jnp.tanh(lz), lz)
    wa = _dot(lin, lora_ref[...])
    lw_s[...] = (-math.exp(-0.5)) * _sigmoid(w0_ref[...] + wa[:, :w])
    ia = _sigmoid(a0_ref[...] + wa[:, w:])
    ia_s[...] = ia
    gate_s[...] = _dot(_sigmoid(zs[:, 3 * w + 128:3 * w + 256]), g2_ref[...])

    r = zs[:, :w]
    k = zs[:, w:2 * w]
    v = zs[:, 2 * w:3 * w]
    kk = k * kkp_ref[...]
    kn_s[...] = kk * lax.rsqrt(jnp.maximum(seg_sum(kk * kk), 1e-24))
    kmod = k * (1.0 + (ia - 1.0) * kap_ref[...])
    r_s[...] = r
    k_s[...] = kmod
    v_s[...] = v
    rk_s[...] = seg_sum(r * kmod * rkp_ref[...])

    tri = tri_ref[...]
    hpg = gw // A_HEAD_DIM
    rid = lax.broadcasted_iota(jnp.int32, (A_CHUNK, gw), 0)
    cid = lax.broadcasted_iota(jnp.int32, (A_CHUNK, gw), 1) % A_HEAD_DIM
    strict = rid > cid
    incl = rid >= cid
    eye = (rid == cid).astype(F32)
    brow = lax.broadcasted_iota(jnp.int32, (gw, gw), 0) // A_HEAD_DIM
    bcol = lax.broadcasted_iota(jnp.int32, (gw, gw), 1) // A_HEAD_DIM
    same_head = brow == bcol
    same_head_bf = same_head.astype(BF16)
    n = A_HEAD_DIM

    def bd(x):
        xb = x.astype(BF16)
        return jnp.concatenate([xb] * hpg, axis=0) * same_head_bf


    group = min(A_GROUP, n_chunks)
    assert n_chunks % group == 0

    def state_free_part(gi):
        items = []
        for j in range(group):
            c = gi * group + j
            rows = _chunk_rows(c)
            lw = lw_s[rows, :]
            h3 = _split3(lw)
            cum = (jnp.dot(tri, h3[0], preferred_element_type=F32)
                   + jnp.dot(tri, h3[1], preferred_element_type=F32)
                   + jnp.dot(tri, h3[2], preferred_element_type=F32))
            e_in = jnp.exp(cum)
            e_ex = jnp.exp(cum - lw)
            e_ng = jnp.exp(-cum)
            kn = kn_s[rows, :]
            rt = r_s[rows, :] * e_in
            at = -kn * e_ex
            bt = kn * ia_s[rows, :] * e_ng
            kt = k_s[rows, :] * e_ng
            vv = v_s[rows, :]
            gam_s[pl.ds(c, 1), :] = e_in[A_CHUNK - 1:A_CHUNK, :]
            for g, s in enumerate(groups):
                items.append((c * len(groups) + g, at[:, s], rt[:, s], bt[:, s], kt[:, s], vv[:, s]))
        ids = range(len(items))
        idx = [it[0] for it in items]
        a_ = [it[1] for it in items]
        r_ = [it[2] for it in items]
        b_ = [it[3] for it in items]
        k_ = [it[4] for it in items]
        v_ = [it[5] for it in items]
        ar = [jnp.concatenate([a_[i], r_[i]], axis=0) for i in ids]
        gb = [_dot_nt(ar[i], bd(b_[i])) for i in ids]
        gk = [_dot_nt(ar[i], bd(k_[i])) for i in ids]
        l_ab = [jnp.where(strict, gb[i][:A_CHUNK], 0.0) for i in ids]
        a_qb = [jnp.where(incl, gb[i][A_CHUNK:], 0.0) for i in ids]
        akq = [jnp.concatenate([jnp.where(strict, gk[i][:A_CHUNK], 0.0),
                                jnp.where(incl, gk[i][A_CHUNK:], 0.0)], axis=0) for i in ids]
        akqv = [_dot(akq[i], bd(v_[i])) for i in ids]
        akv = [akqv[i][:A_CHUNK] for i in ids]
        ov = [akqv[i][A_CHUNK:] for i in ids]
        tinv = [eye + l_ab[i] for i in ids]
        p = [_dot(l_ab[i], bd(l_ab[i])) for i in ids]
        for _ in range(4):
            tp = [_dot(jnp.concatenate([tinv[i], p[i]], axis=0), bd(p[i])) for i in ids]
            tinv = [tinv[i] + tp[i][:A_CHUNK] for i in ids]
            p = [tp[i][A_CHUNK:] for i in ids]
        tinv = [tinv[i] + _dot(tinv[i], bd(p[i])) for i in ids]
        wu = [_dot(tinv[i], jnp.concatenate([bd(a_[i]), bd(akv[i])], axis=1)) for i in ids]
        wm = [wu[i][:, :gw] for i in ids]
        uv = [wu[i][:, gw:] for i in ids]
        for i in ids:
            x_s[idx[i]] = jnp.where(same_head, _dot_tn(wm[i], b_[i]), 0.0)
        for i in ids:
            m2 = jnp.where(same_head, _dot_tn(jnp.concatenate([uv[i], v_[i]], axis=0),
                                              jnp.concatenate([b_[i], k_[i]], axis=0)), 0.0)
            m2_s[idx[i]] = sum(m2[h * n:(h + 1) * n] for h in range(1, hpg)) + m2[:n]
        qo = [_dot(a_qb[i], jnp.concatenate([bd(wm[i]), bd(uv[i])], axis=1)) for i in ids]
        for i in ids:
            q_s[idx[i]] = r_[i] + qo[i][:, :gw]
        for i in ids:
            op_s[idx[i]] = qo[i][:, gw:] + ov[i]

    def state_part(c):
        rows = _chunk_rows(c)
        gam = gam_s[pl.ds(c, 1), :]
        gids = range(len(groups))
        s0 = [state_s[g] for g in gids]
        sx = [_dot(s0[g], x_s[c * len(groups) + g]) for g in gids]
        o = [_dot_nt(q_s[c * len(groups) + g], bd(s0[g])) + op_s[c * len(groups) + g] for g in gids]
        for g in gids:
            state_s[g] = (s0[g] + sx[g] + m2_s[c * len(groups) + g]) * gam[:, groups[g]]
        o_s[rows, :] = jnp.concatenate(o, axis=1)

    for gi in range(n_chunks // group):
        state_free_part(gi)
    for c in range(n_chunks):
        state_part(c)

    o = o_s[...]
    inv_n = 1.0 / n
    mean = seg_sum(o) * inv_n
    d = o - mean
    var = seg_sum(d * d) * inv_n
    on = d * lax.rsqrt(var + A_GN_EPS) * lng_ref[...] + lnb_ref[...]
    on = on + rk_s[...] * v_s[...]
    y_ref[...] = on * gate_s[...]


def _rwkv_params(p):
    depth = p["a_w0"].shape[0]
    row = lambda a: a.reshape(depth, 1, -1).astype(F32)
    w = A_WIDTH
    lora = jnp.zeros((depth, 128, 2 * w), F32)
    lora = lora.at[:, :A_DECAY_LORA, :w].set(p["a_w2"]).at[:, A_DECAY_LORA:, w:].set(p["a_a2"]).astype(BF16)
    hid = jnp.arange(A_GLANES) // A_HEAD_DIM
    ones = (hid[:, None] == hid[None, :]).astype(BF16)
    ti = jnp.arange(A_CHUNK)
    tri = (ti[:, None] >= ti[None, :]).astype(BF16)
    layered = [row(p["a_shift"]), row(p["a_w0"]), row(p["a_a0"]), lora, p["a_g2"].astype(BF16),
               row(p["a_kk"]), row(p["a_ka"]), row(p["a_rk"]), row(p["a_lnx_g"]), row(p["a_lnx_b"])]
    return layered, [ones, tri]


def _rwkv(zmix, bsz, seq, params, l, tc):
    layered, shared = params
    w = A_WIDTH
    stage = pltpu.VMEM((tc, w), F32)
    n_chunks = tc // A_CHUNK
    n_groups = A_WIDTH // A_GLANES
    per_sq = pltpu.VMEM((n_chunks * n_groups, A_GLANES, A_GLANES), F32)
    per_row = pltpu.VMEM((n_chunks * n_groups, A_CHUNK, A_GLANES), F32)
    return pl.pallas_call(
        _rwkv_kernel,
        grid=(bsz, seq // tc),
        in_specs=[pl.BlockSpec((None, tc, A_PROJ), lambda b, i: (b, i, Z_BLK))]
                 + [_layer_spec(a, l) for a in layered] + [_full(a.shape) for a in shared],
        out_specs=pl.BlockSpec((None, tc, w), lambda b, i: (b, i, 0)),
        out_shape=jax.ShapeDtypeStruct((bsz, seq, w), F32),
        scratch_shapes=[pltpu.VMEM((8, A_PROJ), F32),
                        pltpu.VMEM((n_groups, A_HEAD_DIM, A_GLANES), F32),
                        stage, stage, stage, stage, stage, stage, stage, stage, stage,
                        pltpu.VMEM((max(8, n_chunks), w), F32),
                        per_sq, per_row, per_row, per_row],
        compiler_params=_cparams("parallel", "arbitrary"),
        name="rwkv7",
    )(zmix, *layered, *shared)


def _ret_kernel(z_ref, cos_ref, sin_ref, dmask_ref, qd_ref, kd_ref, cd_ref,
                y_ref, state_s, qr_s, kr_s, upd_s, st_s):
    tc = z_ref.shape[0]
    n_chunks = tc // B_CHUNK
    k_off, v_off, g_off = B_QK_WIDTH, 2 * B_QK_WIDTH, 2 * B_QK_WIDTH + B_V_WIDTH

    @pl.when(pl.program_id(1) == 0)
    def _():
        state_s[...] = jnp.zeros_like(state_s)

    cos2 = cos_ref[...]
    sin2 = sin_ref[...]
    half = B_QK_DIM // 2

    def rope(t):
        return t * cos2 + pltpu.roll(t, half, 1) * sin2

    for h in range(B_HEADS):
        qs = slice(h * B_QK_DIM, (h + 1) * B_QK_DIM)
        ks = slice(k_off + h * B_QK_DIM, k_off + (h + 1) * B_QK_DIM)
        qr_s[:, qs] = rope(z_ref[:, qs].astype(F32))
        kr_s[:, qs] = rope(z_ref[:, ks].astype(F32)) * (B_QK_DIM ** -0.5)

    items = [(c, h) for c in range(n_chunks) for h in range(B_HEADS)]
    rows = lambda c: slice(c * B_CHUNK, (c + 1) * B_CHUNK)
    qcol = lambda h: slice(h * B_QK_DIM, (h + 1) * B_QK_DIM)
    vcol = lambda h: slice(v_off + h * B_V_DIM, v_off + (h + 1) * B_V_DIM)

    for c, h in items:
        upd_s[c * B_HEADS + h] = _dot_tn(kr_s[rows(c), qcol(h)] * kd_ref[h], z_ref[rows(c), vcol(h)])
    for h in range(B_HEADS):
        st = state_s[h]
        for c in range(n_chunks):
            st_s[c * B_HEADS + h] = st
            st = cd_ref[h] * st + upd_s[c * B_HEADS + h]
        state_s[h] = st

    for c, h in items:
        qc = qr_s[rows(c), qcol(h)]
        scores = _dot_nt(qc, kr_s[rows(c), qcol(h)]) * dmask_ref[h]
        o = _dot(scores, z_ref[rows(c), vcol(h)]) + _dot(qc * qd_ref[h], st_s[c * B_HEADS + h])
        mu = jnp.mean(o, axis=-1, keepdims=True)
        d = o - mu
        var = jnp.mean(d * d, axis=-1, keepdims=True)
        on = d * lax.rsqrt(var + B_GN_EPS)
        g = z_ref[rows(c), g_off + h * B_V_DIM:g_off + (h + 1) * B_V_DIM].astype(F32)
        y_ref[rows(c), h * B_V_DIM:(h + 1) * B_V_DIM] = g * _sigmoid(g) * on


def _retention_tables(seq):
    f32 = F32
    pos = jnp.arange(seq, dtype=f32)
    half = B_QK_DIM // 2
    inv_freq = B_ROPE_BASE ** (-jnp.arange(half, dtype=f32) / half)
    ang = pos[:, None] * inv_freq[None, :]
    cos, sin = jnp.cos(ang), jnp.sin(ang)
    cos2 = jnp.concatenate([cos, cos], axis=1)
    sin2 = jnp.concatenate([-sin, sin], axis=1)
    log_gamma = jnp.log(1.0 - 2.0 ** (-5.0 - jnp.arange(B_HEADS, dtype=f32)))
    idx = jnp.arange(B_CHUNK, dtype=f32)
    rel = idx[:, None] - idx[None, :]
    dmask = jnp.where(rel >= 0, jnp.exp(log_gamma[:, None, None] * jnp.maximum(rel, 0.0)), 0.0)
    qd = jnp.broadcast_to(jnp.exp(log_gamma[:, None] * (idx + 1.0))[:, :, None],
                          (B_HEADS, B_CHUNK, B_QK_DIM))
    kd = jnp.broadcast_to(jnp.exp(log_gamma[:, None] * (B_CHUNK - 1.0 - idx))[:, :, None],
                          (B_HEADS, B_CHUNK, B_QK_DIM))
    cd = jnp.broadcast_to(jnp.exp(log_gamma * B_CHUNK)[:, None, None], (B_HEADS, B_QK_DIM, B_V_DIM))
    return [cos2, sin2, dmask, qd, kd, cd]


def _retention(zmix, bsz, seq, tables, tc):
    cos2, sin2 = tables[:2]
    tabs = tables[2:]
    n_items = (tc // B_CHUNK) * B_HEADS
    per_item = pltpu.VMEM((n_items, B_QK_DIM, B_V_DIM), F32)
    return pl.pallas_call(
        _ret_kernel,
        grid=(bsz, seq // tc),
        in_specs=[pl.BlockSpec((None, tc, RET_COLS), lambda b, i: (b, i, RET_BLK)),
                  pl.BlockSpec((tc, B_QK_DIM), lambda b, i: (i, 0)),
                  pl.BlockSpec((tc, B_QK_DIM), lambda b, i: (i, 0))] + [_full(a.shape) for a in tabs],
        out_specs=pl.BlockSpec((None, tc, B_V_WIDTH), lambda b, i: (b, i, 0)),
        out_shape=jax.ShapeDtypeStruct((bsz, seq, B_V_WIDTH), F32),
        scratch_shapes=[pltpu.VMEM((B_HEADS, B_QK_DIM, B_V_DIM), F32),
                        pltpu.VMEM((tc, B_QK_WIDTH), F32), pltpu.VMEM((tc, B_QK_WIDTH), F32),
                        per_item, per_item],
        compiler_params=_cparams("parallel", "arbitrary"),
        name="retention",
    )(zmix, cos2, sin2, *tabs)


def _s5_kernel(u_ref, bbr_ref, bbi_ref, ar_ref, ai_ref, cr_ref, ci_ref, d_ref, wg_ref, bg_ref,
               y_ref, xr_s, xi_s, sr_s, si_s):
    bsz, tt, _ = u_ref.shape

    @pl.when(pl.program_id(0) == 0)
    def _():
        sr_s[...] = jnp.zeros_like(sr_s)
        si_s[...] = jnp.zeros_like(si_s)

    u = pltpu.einshape("btc->tbc", u_ref[...].astype(F32)).reshape(tt * bsz, C_WIDTH)
    ub = u.astype(BF16)
    blk = [slice(m * C_SLANES, (m + 1) * C_SLANES) for m in range(C_BLOCKS)]

    for m in range(C_BLOCKS):
        um = ub[:, m * C_ULANES:(m + 1) * C_ULANES]
        xr_s[:, blk[m]] = jnp.dot(um, bbr_ref[m], preferred_element_type=F32)
        xi_s[:, blk[m]] = jnp.dot(um, bbi_ref[m], preferred_element_type=F32)

    for m in range(C_BLOCKS):
        cols = blk[m]
        ar = jnp.broadcast_to(ar_ref[:, cols], (bsz, C_SLANES))
        ai = jnp.broadcast_to(ai_ref[:, cols], (bsz, C_SLANES))

        def step(t, carry):
            xr, xi = carry
            rows = pl.ds(pl.multiple_of(t * bsz, bsz), bsz)
            nr = ar * xr - ai * xi + xr_s[rows, cols]
            ni = ar * xi + ai * xr + xi_s[rows, cols]
            xr_s[rows, cols] = nr
            xi_s[rows, cols] = ni
            return nr, ni

        xr, xi = lax.fori_loop(0, tt, step, (sr_s[:, cols], si_s[:, cols]), unroll=4)
        sr_s[:, cols] = xr
        si_s[:, cols] = xi

    parts = [_dot(xr_s[:, blk[m]], cr_ref[m]) - _dot(xi_s[:, blk[m]], ci_ref[m]) for m in range(C_BLOCKS)]
    y = jnp.concatenate(parts, axis=1) + d_ref[...] * u
    y = jax.nn.gelu(y)
    y = y * _sigmoid(_dot(y, wg_ref[...]) + bg_ref[...])
    y_ref[...] = pltpu.einshape("tbc->btc", y.reshape(tt, bsz, C_WIDTH))


def _s5_params(p):
    f32 = F32
    depth = p["c_log_dt"].shape[0]
    dt = jnp.exp(p["c_log_dt"].astype(f32))[..., None]
    lr, li = p["c_lam_re"].astype(f32), p["c_lam_im"].astype(f32)
    mag = jnp.exp(lr * dt)
    ab_re, ab_im = mag * jnp.cos(li * dt), mag * jnp.sin(li * dt)
    den = lr * lr + li * li
    f_re = ((ab_re - 1.0) * lr + ab_im * li) / den
    f_im = (ab_im * lr - (ab_re - 1.0) * li) / den
    bre, bim = p["c_b_re"].astype(f32), p["c_b_im"].astype(f32)
    bb_re = f_re[..., None] * bre - f_im[..., None] * bim
    bb_im = f_re[..., None] * bim + f_im[..., None] * bre
    gpb = C_GROUPS // C_BLOCKS
    eye = jnp.eye(gpb, dtype=f32)

    def in_blocks(bb):
        bb = bb.reshape(depth, C_BLOCKS, gpb, C_STATE, C_GROUP)
        return jnp.einsum("lmgpc,gh->lmgchp", bb, eye).reshape(depth, C_BLOCKS, C_ULANES, C_SLANES).astype(BF16)

    def out_blocks(cc):
        cc = cc.astype(f32).reshape(depth, C_BLOCKS, gpb, C_GROUP, C_STATE)
        return jnp.einsum("lmgcp,gh->lmgphc", cc, eye).reshape(depth, C_BLOCKS, C_SLANES, C_ULANES).astype(BF16)

    return [in_blocks(bb_re), in_blocks(bb_im), ab_re.reshape(depth, 1, C_LANES), ab_im.reshape(depth, 1, C_LANES),
            out_blocks(p["c_c_re"]), out_blocks(p["c_c_im"]), p["c_d"].reshape(depth, 1, C_WIDTH).astype(f32),
            p["c_w_glu"].astype(BF16), p["c_b_glu"].reshape(depth, 1, C_WIDTH).astype(f32)]


def _s5(zmix, bsz, seq, consts, l, tt):
    rows = tt * bsz
    return pl.pallas_call(
        _s5_kernel,
        grid=(seq // tt,),
        in_specs=[pl.BlockSpec((bsz, tt, C_WIDTH), lambda i: (0, i, U_BLK))] + [_layer_spec(a, l) for a in consts],
        out_specs=pl.BlockSpec((bsz, tt, C_WIDTH), lambda i: (0, i, 0)),
        out_shape=jax.ShapeDtypeStruct((bsz, seq, C_WIDTH), F32),
        scratch_shapes=[pltpu.VMEM((rows, C_LANES), F32), pltpu.VMEM((rows, C_LANES), F32),
                        pltpu.VMEM((bsz, C_LANES), F32), pltpu.VMEM((bsz, C_LANES), F32)],
        compiler_params=_cparams("arbitrary"),
        name="s5",
    )(zmix, *consts)


def _merge_kernel(alpha, x_ref, ya_ref, yb_ref, yc_ref, wgate_ref, bgate_ref, wb_ref,
                  wout_ref, g_ref, b_ref, o_ref):
    d = x_ref.shape[1]
    x = x_ref[...]
    gates = _sigmoid(_dot(x, wgate_ref[...]) + bgate_ref[...])
    b_lo, c_lo = A_WIDTH, A_WIDTH + B_V_WIDTH
    merged = (gates[:, :d] * _dot(ya_ref[...], wb_ref[:b_lo, :])
              + gates[:, d:2 * d] * _dot(yb_ref[...], wb_ref[b_lo:c_lo, :])
              + gates[:, 2 * d:] * _dot(yc_ref[...], wb_ref[c_lo:, :]))
    o_ref[...] = _layer_norm(alpha * x + _dot(merged, wout_ref[...]), g_ref[...], b_ref[...])


def _merge(x2d, ya, yb, yc, consts, l, alpha, tm):
    m, d = x2d.shape
    tile = lambda n: pl.BlockSpec((tm, n), lambda i: (i, 0))
    return pl.pallas_call(
        functools.partial(_merge_kernel, alpha),
        grid=(m // tm,),
        in_specs=[tile(d), tile(A_WIDTH), tile(B_V_WIDTH), tile(C_WIDTH)] + [_layer_spec(a, l) for a in consts],
        out_specs=tile(d),
        out_shape=jax.ShapeDtypeStruct((m, d), F32),
        compiler_params=_cparams("parallel"),
        name="merge",
    )(x2d, ya, yb, yc, *consts)


def _ffn_kernel(alpha, tf, x_ref, w1_ref, w2_ref, g_ref, b_ref, o_ref):
    x = x_ref[...]
    xb = x.astype(BF16)
    acc = None
    for j in range(w1_ref.shape[1] // tf):
        h = jnp.maximum(jnp.dot(xb, w1_ref[:, j * tf:(j + 1) * tf], preferred_element_type=F32), 0.0)
        part = jnp.dot((h * h).astype(BF16), w2_ref[j * tf:(j + 1) * tf, :], preferred_element_type=F32)
        acc = part if acc is None else acc + part
    o_ref[...] = _layer_norm(alpha * x + acc, g_ref[...], b_ref[...])


def _ffn(x2d, consts, l, alpha, tm, tf):
    m, d = x2d.shape
    return pl.pallas_call(
        functools.partial(_ffn_kernel, alpha, tf),
        grid=(m // tm,),
        in_specs=[pl.BlockSpec((tm, d), lambda i: (i, 0))] + [_layer_spec(a, l) for a in consts],
        out_specs=pl.BlockSpec((tm, d), lambda i: (i, 0)),
        out_shape=jax.ShapeDtypeStruct((m, d), F32),
        compiler_params=_cparams("parallel"),
        name="ffn",
    )(x2d, *consts)


def _tile(n, want):
    t = min(n, want)
    assert n % t == 0, (n, want)
    return t


def kernel(x, w_in, b_gate, a_shift, a_w0, a_w2, a_a0, a_a2, a_g2, a_kk, a_ka, a_rk, a_lnx_g, a_lnx_b, c_lam_re, c_lam_im, c_log_dt, c_b_re, c_b_im, c_c_re, c_c_im, c_d, c_w_glu, c_b_glu, w_branch, w_out, ln1_g, ln1_b, w_ff1, w_ff2, ln2_g, ln2_b):
    bsz, seq, d = x.shape
    depth = w_in.shape[0]
    alpha = (2.0 * depth) ** 0.25
    tokens = bsz * seq
    rows = lambda a: a.reshape(depth, 1, -1)
    rwkv_params = _rwkv_params(dict(a_shift=a_shift, a_w0=a_w0, a_w2=a_w2, a_a0=a_a0, a_a2=a_a2, a_g2=a_g2,
                                    a_kk=a_kk, a_ka=a_ka, a_rk=a_rk, a_lnx_g=a_lnx_g, a_lnx_b=a_lnx_b))
    s5_params = _s5_params(dict(c_lam_re=c_lam_re, c_lam_im=c_lam_im, c_log_dt=c_log_dt, c_b_re=c_b_re,
                                c_b_im=c_b_im, c_c_re=c_c_re, c_c_im=c_c_im, c_d=c_d, c_w_glu=c_w_glu,
                                c_b_glu=c_b_glu))
    ret_tables = _retention_tables(seq)
    w_in_bf = w_in.astype(BF16)
    w_mix = jnp.concatenate([w_in_bf[:, :, A_PROJ:MIX_COLS], w_in_bf[:, :, :A_PROJ]], axis=2)
    merge_consts = [w_in_bf[:, :, MIX_COLS:], rows(b_gate), w_branch.astype(BF16), w_out.astype(BF16),
                    rows(ln1_g), rows(ln1_b)]
    ffn_consts = [w_ff1.astype(BF16), w_ff2.astype(BF16), rows(ln2_g), rows(ln2_b)]
    xt = x.reshape(tokens, d)
    for l in range(depth):
        zmix = _proj(xt, w_mix, l, _tile(tokens, 512)).reshape(bsz, seq, MIX_COLS)
        ya = _rwkv(zmix, bsz, seq, rwkv_params, l, _tile(seq, 512))
        yb = _retention(zmix, bsz, seq, ret_tables, _tile(seq, 512))
        yc = _s5(zmix, bsz, seq, s5_params, l, _tile(seq, 64))
        x1 = _merge(xt, ya.reshape(tokens, -1), yb.reshape(tokens, -1), yc.reshape(tokens, -1),
                    merge_consts, l, alpha, _tile(tokens, 512))
        xt = _ffn(x1, ffn_consts, l, alpha, _tile(tokens, 1024), 1024)
    return xt.reshape(bsz, seq, d)
```

```python
import functools
import math

import jax
import jax.numpy as jnp
from jax import lax
from jax.experimental import pallas as pl
from jax.experimental.pallas import tpu as pltpu

F32 = jnp.float32
BF16 = jnp.bfloat16

A_HEADS = 8
A_HEAD_DIM = 64
A_WIDTH = A_HEADS * A_HEAD_DIM
A_DECAY_LORA = 64
A_ICLR_LORA = 64
A_GATE_LORA = 128
A_PROJ = 3 * A_WIDTH + A_DECAY_LORA + A_ICLR_LORA + A_GATE_LORA
A_GN_EPS = 64e-5
A_CHUNK = 64
A_GROUP = 8
A_GLANES = 256

B_HEADS = 4
B_QK_DIM = 128
B_V_DIM = 256
B_QK_WIDTH = B_HEADS * B_QK_DIM
B_V_WIDTH = B_HEADS * B_V_DIM
B_CHUNK = 128
B_ROPE_BASE = 10000.0
B_GN_EPS = 1e-5

C_WIDTH = 512
C_GROUP = 16
C_GROUPS = C_WIDTH // C_GROUP
C_STATE = 64
C_LANES = C_GROUPS * C_STATE
C_BLOCKS = 4
C_ULANES = C_WIDTH // C_BLOCKS
C_SLANES = C_LANES // C_BLOCKS

LN_EPS = 1e-5

RET_COLS = 2 * B_QK_WIDTH + 2 * B_V_WIDTH
REST_COLS = RET_COLS + C_WIDTH
MIX_COLS = A_PROJ + REST_COLS
U_BLK = RET_COLS // C_WIDTH
assert RET_COLS % C_WIDTH == 0

V7X_VMEM_LIMIT_BYTES = 56 * 1024 * 1024


def _cparams(*sem):
    return pltpu.CompilerParams(dimension_semantics=sem, vmem_limit_bytes=V7X_VMEM_LIMIT_BYTES)


def _full(shape):
    n = len(shape)
    return pl.BlockSpec(shape, lambda *_: (0,) * n)


def _layer_spec(a, l):
    n = a.ndim - 1
    return pl.BlockSpec((None,) + a.shape[1:], lambda *_: (l,) + (0,) * n, pipeline_mode=pl.Buffered(1))


def _dot(a, b):
    return jnp.dot(a.astype(BF16), b.astype(BF16), preferred_element_type=F32)


def _dot_nt(a, b):
    return lax.dot_general(a.astype(BF16), b.astype(BF16), (((1,), (1,)), ((), ())),
                           preferred_element_type=F32)


def _dot_tn(a, b):
    return lax.dot_general(a.astype(BF16), b.astype(BF16), (((0,), (0,)), ((), ())),
                           preferred_element_type=F32)


def _split3(x):
    hi = x.astype(BF16)
    r1 = x - hi.astype(F32)
    mid = r1.astype(BF16)
    lo = (r1 - mid.astype(F32)).astype(BF16)
    return hi, mid, lo


def _sigmoid(x):
    return 1.0 / (1.0 + jnp.exp(-x))


def _layer_norm(y, g, b):
    mu = jnp.mean(y, axis=-1, keepdims=True)
    d = y - mu
    var = jnp.mean(d * d, axis=-1, keepdims=True)
    return d * lax.rsqrt(var + LN_EPS) * g + b


def _proj_kernel(x_ref, wza_ref, wrest_ref, mu_ref, w0_ref, a0_ref, lora_ref, g2_ref, kkp_ref, kap_ref, rkp_ref,
                 ones_ref, rest_ref, r_ref, k_ref, v_ref, kn_ref, b0_ref, lw_ref, gate_ref, bonus_ref, carry_s):
    tm = x_ref.shape[0]
    w = A_WIDTH

    @pl.when(pl.program_id(1) == 0)
    def _():
        carry_s[...] = jnp.zeros_like(carry_s)

    xb = x_ref[...].astype(BF16)
    z = jnp.dot(xb, wza_ref[...], preferred_element_type=F32)
    rest_ref[...] = jnp.dot(xb, wrest_ref[...], preferred_element_type=F32).astype(rest_ref.dtype)

    ones = ones_ref[...]
    groups = [slice(g * A_GLANES, (g + 1) * A_GLANES) for g in range(w // A_GLANES)]

    def seg_sum(t):
        tb = t.astype(BF16)
        return jnp.concatenate([jnp.dot(tb[:, s], ones, preferred_element_type=F32) for s in groups], axis=1)

    rolled = pltpu.roll(z, 1, 0)
    rowid = lax.broadcasted_iota(jnp.int32, z.shape, 0)
    prev = jnp.where(rowid == 0, jnp.broadcast_to(carry_s[0:1, :], z.shape), rolled)
    zs = z + mu_ref[...] * (prev - z)
    carry_s[0:1, :] = z[tm - 1:tm, :]

    lz = zs[:, 3 * w:3 * w + 128]
    lane = lax.broadcasted_iota(jnp.int32, lz.shape, 1)
    lin = jnp.where(lane < A_DECAY_LORA, jnp.tanh(lz), lz)
    wa = _dot(lin, lora_ref[...])
    lw_ref[...] = (-math.exp(-0.5)) * _sigmoid(w0_ref[...] + wa[:, :w])
    ia = _sigmoid(a0_ref[...] + wa[:, w:])
    gate_ref[...] = _dot(_sigmoid(zs[:, 3 * w + 128:3 * w + 256]), g2_ref[...]).astype(gate_ref.dtype)

    r = zs[:, :w]
    k = zs[:, w:2 * w]
    v = zs[:, 2 * w:3 * w]
    kk = k * kkp_ref[...]
    kn = kk * lax.rsqrt(jnp.maximum(seg_sum(kk * kk), 1e-24))
    kmod = k * (1.0 + (ia - 1.0) * kap_ref[...])
    r_ref[...] = r.astype(r_ref.dtype)
    k_ref[...] = kmod.astype(k_ref.dtype)
    v_ref[...] = v.astype(v_ref.dtype)
    kn_ref[...] = kn.astype(kn_ref.dtype)
    b0_ref[...] = (kn * ia).astype(b0_ref.dtype)
    bonus_ref[...] = (seg_sum(r * kmod * rkp_ref[...]) * v).astype(bonus_ref.dtype)


def _proj(x2d, w_za, w_rest, params, bsz, seq, l, tm):
    layered, ones = params
    k = x2d.shape[1]
    per_seq = seq // tm
    wide = lambda dt: jax.ShapeDtypeStruct((bsz * seq, A_WIDTH), dt)
    row_tile = lambda n: pl.BlockSpec((tm, n), lambda b, i: (b * per_seq + i, 0))
    consts = [w_za, w_rest] + layered
    return pl.pallas_call(
        _proj_kernel,
        grid=(bsz, per_seq),
        in_specs=[row_tile(k)] + [_layer_spec(a, l) for a in consts] + [_full(ones.shape)],
        out_specs=[row_tile(REST_COLS)] + [row_tile(A_WIDTH)] * 8,
        out_shape=[jax.ShapeDtypeStruct((bsz * seq, REST_COLS), BF16),
                   wide(BF16), wide(BF16), wide(BF16), wide(BF16), wide(BF16), wide(F32), wide(BF16), wide(BF16)],
        scratch_shapes=[pltpu.VMEM((8, A_PROJ), F32)],
        compiler_params=_cparams("parallel", "arbitrary"),
        name="proj",
    )(x2d, *consts, ones)


def _chunk_rows(c):
    return pl.ds(c * A_CHUNK, A_CHUNK)


def _rwkv_kernel(r_ref, k_ref, v_ref, kn_ref, b0_ref, lw_ref, gate_ref, bonus_ref, lng_ref, lnb_ref,
                 ones_ref, tri_ref, y_ref, state_s, o_s, gam_s, x_s, m2_s, q_s, op_s):
    tc = r_ref.shape[0]
    n_chunks = tc // A_CHUNK
    w = A_WIDTH

    @pl.when(pl.program_id(1) == 0)
    def _():
        state_s[...] = jnp.zeros_like(state_s)

    ones = ones_ref[...]

    gw = A_GLANES
    groups = [slice(g * gw, (g + 1) * gw) for g in range(w // gw)]

    def seg_sum(x):
        xb = x.astype(BF16)
        return jnp.concatenate([jnp.dot(xb[:, s], ones, preferred_element_type=F32) for s in groups], axis=1)

    tri = tri_ref[...]
    hpg = gw // A_HEAD_DIM
    rid = lax.broadcasted_iota(jnp.int32, (A_CHUNK, gw), 0)
    cid = lax.broadcasted_iota(jnp.int32, (A_CHUNK, gw), 1) % A_HEAD_DIM
    strict = rid > cid
    incl = rid >= cid
    eye = (rid == cid).astype(F32)
    brow = lax.broadcasted_iota(jnp.int32, (gw, gw), 0) // A_HEAD_DIM
    bcol = lax.broadcasted_iota(jnp.int32, (gw, gw), 1) // A_HEAD_DIM
    same_head = brow == bcol
    same_head_bf = same_head.astype(BF16)
    n = A_HEAD_DIM

    def bd(x):
        xb = x.astype(BF16)
        return jnp.concatenate([xb] * hpg, axis=0) * same_head_bf


    group = min(A_GROUP, n_chunks)
    assert n_chunks % group == 0

    def state_free_part(gi):
        items = []
        for j in range(group):
            c = gi * group + j
            rows = _chunk_rows(c)
            lw = lw_ref[rows, :]
            h3 = _split3(lw)
            cum = (jnp.dot(tri, h3[0], preferred_element_type=F32)
                   + jnp.dot(tri, h3[1], preferred_element_type=F32)
                   + jnp.dot(tri, h3[2], preferred_element_type=F32))
            e_in = jnp.exp(cum)
            e_ex = jnp.exp(cum - lw)
            e_ng = jnp.exp(-cum)
            rt = r_ref[rows, :].astype(F32) * e_in
            at = -kn_ref[rows, :].astype(F32) * e_ex
            bt = b0_ref[rows, :].astype(F32) * e_ng
            kt = k_ref[rows, :].astype(F32) * e_ng
            vv = v_ref[rows, :]
            gam_s[pl.ds(c, 1), :] = e_in[A_CHUNK - 1:A_CHUNK, :]
            for g, s in enumerate(groups):
                items.append((c * len(groups) + g, at[:, s], rt[:, s], bt[:, s], kt[:, s], vv[:, s]))
        ids = range(len(items))
        idx = [it[0] for it in items]
        a_ = [it[1] for it in items]
        r_ = [it[2] for it in items]
        b_ = [it[3] for it in items]
        k_ = [it[4] for it in items]
        v_ = [it[5] for it in items]
        ar = [jnp.concatenate([a_[i], r_[i]], axis=0) for i in ids]
        gb = [_dot_nt(ar[i], bd(b_[i])) for i in ids]
        gk = [_dot_nt(ar[i], bd(k_[i])) for i in ids]
        l_ab = [jnp.where(strict, gb[i][:A_CHUNK], 0.0) for i in ids]
        a_qb = [jnp.where(incl, gb[i][A_CHUNK:], 0.0) for i in ids]
        akq = [jnp.concatenate([jnp.where(strict, gk[i][:A_CHUNK], 0.0),
                                jnp.where(incl, gk[i][A_CHUNK:], 0.0)], axis=0) for i in ids]
        akqv = [_dot(akq[i], bd(v_[i])) for i in ids]
        akv = [akqv[i][:A_CHUNK] for i in ids]
        ov = [akqv[i][A_CHUNK:] for i in ids]
        tinv = [eye + l_ab[i] for i in ids]
        p = [_dot(l_ab[i], bd(l_ab[i])) for i in ids]
        for _ in range(4):
            tp = [_dot(jnp.concatenate([tinv[i], p[i]], axis=0), bd(p[i])) for i in ids]
            tinv = [tinv[i] + tp[i][:A_CHUNK] for i in ids]
            p = [tp[i][A_CHUNK:] for i in ids]
        tinv = [tinv[i] + _dot(tinv[i], bd(p[i])) for i in ids]
        wu = [_dot(tinv[i], jnp.concatenate([bd(a_[i]), bd(akv[i])], axis=1)) for i in ids]
        wm = [wu[i][:, :gw] for i in ids]
        uv = [wu[i][:, gw:] for i in ids]
        for i in ids:
            x_s[idx[i]] = jnp.where(same_head, _dot_tn(wm[i], b_[i]), 0.0)
        for i in ids:
            m2 = jnp.where(same_head, _dot_tn(jnp.concatenate([uv[i], v_[i]], axis=0),
                                              jnp.concatenate([b_[i], k_[i]], axis=0)), 0.0)
            m2_s[idx[i]] = sum(m2[h * n:(h + 1) * n] for h in range(1, hpg)) + m2[:n]
        qo = [_dot(a_qb[i], jnp.concatenate([bd(wm[i]), bd(uv[i])], axis=1)) for i in ids]
        for i in ids:
            q_s[idx[i]] = r_[i] + qo[i][:, :gw]
        for i in ids:
            op_s[idx[i]] = qo[i][:, gw:] + ov[i]

    def state_part(c):
        rows = _chunk_rows(c)
        gam = gam_s[pl.ds(c, 1), :]
        gids = range(len(groups))
        s0 = [state_s[g] for g in gids]
        sx = [_dot(s0[g], x_s[c * len(groups) + g]) for g in gids]
        o = [_dot_nt(q_s[c * len(groups) + g], bd(s0[g])) + op_s[c * len(groups) + g] for g in gids]
        for g in gids:
            state_s[g] = (s0[g] + sx[g] + m2_s[c * len(groups) + g]) * gam[:, groups[g]]
        o_s[rows, :] = jnp.concatenate(o, axis=1)

    for gi in range(n_chunks // group):
        state_free_part(gi)
    for c in range(n_chunks):
        state_part(c)

    o = o_s[...]
    inv_n = 1.0 / n
    mean = seg_sum(o) * inv_n
    d = o - mean
    var = seg_sum(d * d) * inv_n
    on = d * lax.rsqrt(var + A_GN_EPS) * lng_ref[...] + lnb_ref[...]
    on = on + bonus_ref[...].astype(F32)
    y_ref[...] = on * gate_ref[...].astype(F32)


def _rwkv_params(p):
    depth = p["a_w0"].shape[0]
    row = lambda a: a.reshape(depth, 1, -1).astype(F32)
    w = A_WIDTH
    lora = jnp.zeros((depth, 128, 2 * w), F32)
    lora = lora.at[:, :A_DECAY_LORA, :w].set(p["a_w2"]).at[:, A_DECAY_LORA:, w:].set(p["a_a2"]).astype(BF16)
    hid = jnp.arange(A_GLANES) // A_HEAD_DIM
    ones = (hid[:, None] == hid[None, :]).astype(BF16)
    ti = jnp.arange(A_CHUNK)
    tri = (ti[:, None] >= ti[None, :]).astype(BF16)
    stage = [row(p["a_shift"]), row(p["a_w0"]), row(p["a_a0"]), lora, p["a_g2"].astype(BF16),
             row(p["a_kk"]), row(p["a_ka"]), row(p["a_rk"])]
    return (stage, ones), ([row(p["a_lnx_g"]), row(p["a_lnx_b"])], [ones, tri])


def _rwkv(stage_out, bsz, seq, params, l, tc):
    layered, shared = params
    w = A_WIDTH
    n_chunks = tc // A_CHUNK
    n_groups = A_WIDTH // A_GLANES
    per_seq = seq // tc
    per_sq = pltpu.VMEM((n_chunks * n_groups, A_GLANES, A_GLANES), F32)
    per_row = pltpu.VMEM((n_chunks * n_groups, A_CHUNK, A_GLANES), F32)
    row_tile = pl.BlockSpec((tc, w), lambda b, i: (b * per_seq + i, 0))
    return pl.pallas_call(
        _rwkv_kernel,
        grid=(bsz, per_seq),
        in_specs=[row_tile] * len(stage_out)
                 + [_layer_spec(a, l) for a in layered] + [_full(a.shape) for a in shared],
        out_specs=row_tile,
        out_shape=jax.ShapeDtypeStruct((bsz * seq, w), F32),
        scratch_shapes=[pltpu.VMEM((n_groups, A_HEAD_DIM, A_GLANES), F32),
                        pltpu.VMEM((tc, w), F32),
                        pltpu.VMEM((max(8, n_chunks), w), F32),
                        per_sq, per_row, per_row, per_row],
        compiler_params=_cparams("parallel", "arbitrary"),
        name="rwkv7",
    )(*stage_out, *layered, *shared)


def _ret_kernel(z_ref, cos_ref, sin_ref, dmask_ref, qd_ref, kd_ref, cd_ref,
                y_ref, state_s, qr_s, kr_s, upd_s, st_s):
    tc = z_ref.shape[0]
    n_chunks = tc // B_CHUNK
    k_off, v_off, g_off = B_QK_WIDTH, 2 * B_QK_WIDTH, 2 * B_QK_WIDTH + B_V_WIDTH

    @pl.when(pl.program_id(1) == 0)
    def _():
        state_s[...] = jnp.zeros_like(state_s)

    cos2 = cos_ref[...]
    sin2 = sin_ref[...]
    half = B_QK_DIM // 2

    def rope(t):
        return t * cos2 + pltpu.roll(t, half, 1) * sin2

    for h in range(B_HEADS):
        qs = slice(h * B_QK_DIM, (h + 1) * B_QK_DIM)
        ks = slice(k_off + h * B_QK_DIM, k_off + (h + 1) * B_QK_DIM)
        qr_s[:, qs] = rope(z_ref[:, qs].astype(F32))
        kr_s[:, qs] = rope(z_ref[:, ks].astype(F32)) * (B_QK_DIM ** -0.5)

    items = [(c, h) for c in range(n_chunks) for h in range(B_HEADS)]
    rows = lambda c: slice(c * B_CHUNK, (c + 1) * B_CHUNK)
    qcol = lambda h: slice(h * B_QK_DIM, (h + 1) * B_QK_DIM)
    vcol = lambda h: slice(v_off + h * B_V_DIM, v_off + (h + 1) * B_V_DIM)

    for c, h in items:
        upd_s[c * B_HEADS + h] = _dot_tn(kr_s[rows(c), qcol(h)] * kd_ref[h], z_ref[rows(c), vcol(h)])
    for h in range(B_HEADS):
        st = state_s[h]
        for c in range(n_chunks):
            st_s[c * B_HEADS + h] = st
            st = cd_ref[h] * st + upd_s[c * B_HEADS + h]
        state_s[h] = st

    for c, h in items:
        qc = qr_s[rows(c), qcol(h)]
        scores = _dot_nt(qc, kr_s[rows(c), qcol(h)]) * dmask_ref[h]
        o = _dot(scores, z_ref[rows(c), vcol(h)]) + _dot(qc * qd_ref[h], st_s[c * B_HEADS + h])
        mu = jnp.mean(o, axis=-1, keepdims=True)
        d = o - mu
        var = jnp.mean(d * d, axis=-1, keepdims=True)
        on = d * lax.rsqrt(var + B_GN_EPS)
        g = z_ref[rows(c), g_off + h * B_V_DIM:g_off + (h + 1) * B_V_DIM].astype(F32)
        y_ref[rows(c), h * B_V_DIM:(h + 1) * B_V_DIM] = g * _sigmoid(g) * on


def _retention_tables(seq):
    f32 = F32
    pos = jnp.arange(seq, dtype=f32)
    half = B_QK_DIM // 2
    inv_freq = B_ROPE_BASE ** (-jnp.arange(half, dtype=f32) / half)
    ang = pos[:, None] * inv_freq[None, :]
    cos, sin = jnp.cos(ang), jnp.sin(ang)
    cos2 = jnp.concatenate([cos, cos], axis=1)
    sin2 = jnp.concatenate([-sin, sin], axis=1)
    log_gamma = jnp.log(1.0 - 2.0 ** (-5.0 - jnp.arange(B_HEADS, dtype=f32)))
    idx = jnp.arange(B_CHUNK, dtype=f32)
    rel = idx[:, None] - idx[None, :]
    dmask = jnp.where(rel >= 0, jnp.exp(log_gamma[:, None, None] * jnp.maximum(rel, 0.0)), 0.0)
    qd = jnp.broadcast_to(jnp.exp(log_gamma[:, None] * (idx + 1.0))[:, :, None],
                          (B_HEADS, B_CHUNK, B_QK_DIM))
    kd = jnp.broadcast_to(jnp.exp(log_gamma[:, None] * (B_CHUNK - 1.0 - idx))[:, :, None],
                          (B_HEADS, B_CHUNK, B_QK_DIM))
    cd = jnp.broadcast_to(jnp.exp(log_gamma * B_CHUNK)[:, None, None], (B_HEADS, B_QK_DIM, B_V_DIM))
    return [cos2, sin2, dmask, qd, kd, cd]


def _retention(zmix, bsz, seq, tables, tc):
    cos2, sin2 = tables[:2]
    tabs = tables[2:]
    n_items = (tc // B_CHUNK) * B_HEADS
    per_item = pltpu.VMEM((n_items, B_QK_DIM, B_V_DIM), F32)
    return pl.pallas_call(
        _ret_kernel,
        grid=(bsz, seq // tc),
        in_specs=[pl.BlockSpec((None, tc, RET_COLS), lambda b, i: (b, i, 0)),
                  pl.BlockSpec((tc, B_QK_DIM), lambda b, i: (i, 0)),
                  pl.BlockSpec((tc, B_QK_DIM), lambda b, i: (i, 0))] + [_full(a.shape) for a in tabs],
        out_specs=pl.BlockSpec((None, tc, B_V_WIDTH), lambda b, i: (b, i, 0)),
        out_shape=jax.ShapeDtypeStruct((bsz, seq, B_V_WIDTH), F32),
        scratch_shapes=[pltpu.VMEM((B_HEADS, B_QK_DIM, B_V_DIM), F32),
                        pltpu.VMEM((tc, B_QK_WIDTH), F32), pltpu.VMEM((tc, B_QK_WIDTH), F32),
                        per_item, per_item],
        compiler_params=_cparams("parallel", "arbitrary"),
        name="retention",
    )(zmix, cos2, sin2, *tabs)


def _s5_kernel(u_ref, bbr_ref, bbi_ref, ar_ref, ai_ref, cr_ref, ci_ref, d_ref, wg_ref, bg_ref,
               y_ref, xr_s, xi_s, sr_s, si_s):
    bsz, tt, _ = u_ref.shape

    @pl.when(pl.program_id(0) == 0)
    def _():
        sr_s[...] = jnp.zeros_like(sr_s)
        si_s[...] = jnp.zeros_like(si_s)

    u = pltpu.einshape("btc->tbc", u_ref[...].astype(F32)).reshape(tt * bsz, C_WIDTH)
    ub = u.astype(BF16)
    blk = [slice(m * C_SLANES, (m + 1) * C_SLANES) for m in range(C_BLOCKS)]

    for m in range(C_BLOCKS):
        um = ub[:, m * C_ULANES:(m + 1) * C_ULANES]
        xr_s[:, blk[m]] = jnp.dot(um, bbr_ref[m], preferred_element_type=F32)
        xi_s[:, blk[m]] = jnp.dot(um, bbi_ref[m], preferred_element_type=F32)

    for m in range(C_BLOCKS):
        cols = blk[m]
        ar = jnp.broadcast_to(ar_ref[:, cols], (bsz, C_SLANES))
        ai = jnp.broadcast_to(ai_ref[:, cols], (bsz, C_SLANES))

        def step(t, carry):
            xr, xi = carry
            rows = pl.ds(pl.multiple_of(t * bsz, bsz), bsz)
            nr = ar * xr - ai * xi + xr_s[rows, cols]
            ni = ar * xi + ai * xr + xi_s[rows, cols]
            xr_s[rows, cols] = nr
            xi_s[rows, cols] = ni
            return nr, ni

        xr, xi = lax.fori_loop(0, tt, step, (sr_s[:, cols], si_s[:, cols]), unroll=4)
        sr_s[:, cols] = xr
        si_s[:, cols] = xi

    parts = [_dot(xr_s[:, blk[m]], cr_ref[m]) - _dot(xi_s[:, blk[m]], ci_ref[m]) for m in range(C_BLOCKS)]
    y = jnp.concatenate(parts, axis=1) + d_ref[...] * u
    y = jax.nn.gelu(y)
    y = y * _sigmoid(_dot(y, wg_ref[...]) + bg_ref[...])
    y_ref[...] = pltpu.einshape("tbc->btc", y.reshape(tt, bsz, C_WIDTH))


def _s5_params(p):
    f32 = F32
    depth = p["c_log_dt"].shape[0]
    dt = jnp.exp(p["c_log_dt"].astype(f32))[..., None]
    lr, li = p["c_lam_re"].astype(f32), p["c_lam_im"].astype(f32)
    mag = jnp.exp(lr * dt)
    ab_re, ab_im = mag * jnp.cos(li * dt), mag * jnp.sin(li * dt)
    den = lr * lr + li * li
    f_re = ((ab_re - 1.0) * lr + ab_im * li) / den
    f_im = (ab_im * lr - (ab_re - 1.0) * li) / den
    bre, bim = p["c_b_re"].astype(f32), p["c_b_im"].astype(f32)
    bb_re = f_re[..., None] * bre - f_im[..., None] * bim
    bb_im = f_re[..., None] * bim + f_im[..., None] * bre
    gpb = C_GROUPS // C_BLOCKS
    eye = jnp.eye(gpb, dtype=f32)

    def in_blocks(bb):
        bb = bb.reshape(depth, C_BLOCKS, gpb, C_STATE, C_GROUP)
        return jnp.einsum("lmgpc,gh->lmgchp", bb, eye).reshape(depth, C_BLOCKS, C_ULANES, C_SLANES).astype(BF16)

    def out_blocks(cc):
        cc = cc.astype(f32).reshape(depth, C_BLOCKS, gpb, C_GROUP, C_STATE)
        return jnp.einsum("lmgcp,gh->lmgphc", cc, eye).reshape(depth, C_BLOCKS, C_SLANES, C_ULANES).astype(BF16)

    return [in_blocks(bb_re), in_blocks(bb_im), ab_re.reshape(depth, 1, C_LANES), ab_im.reshape(depth, 1, C_LANES),
            out_blocks(p["c_c_re"]), out_blocks(p["c_c_im"]), p["c_d"].reshape(depth, 1, C_WIDTH).astype(f32),
            p["c_w_glu"].astype(BF16), p["c_b_glu"].reshape(depth, 1, C_WIDTH).astype(f32)]


def _s5(zmix, bsz, seq, consts, l, tt):
    rows = tt * bsz
    return pl.pallas_call(
        _s5_kernel,
        grid=(seq // tt,),
        in_specs=[pl.BlockSpec((bsz, tt, C_WIDTH), lambda i: (0, i, U_BLK))] + [_layer_spec(a, l) for a in consts],
        out_specs=pl.BlockSpec((bsz, tt, C_WIDTH), lambda i: (0, i, 0)),
        out_shape=jax.ShapeDtypeStruct((bsz, seq, C_WIDTH), F32),
        scratch_shapes=[pltpu.VMEM((rows, C_LANES), F32), pltpu.VMEM((rows, C_LANES), F32),
                        pltpu.VMEM((bsz, C_LANES), F32), pltpu.VMEM((bsz, C_LANES), F32)],
        compiler_params=_cparams("arbitrary"),
        name="s5",
    )(zmix, *consts)


def _merge_kernel(alpha, x_ref, ya_ref, yb_ref, yc_ref, wgate_ref, bgate_ref, wb_ref,
                  wout_ref, g_ref, b_ref, o_ref):
    d = x_ref.shape[1]
    x = x_ref[...]
    gates = _sigmoid(_dot(x, wgate_ref[...]) + bgate_ref[...])
    b_lo, c_lo = A_WIDTH, A_WIDTH + B_V_WIDTH
    merged = (gates[:, :d] * _dot(ya_ref[...], wb_ref[:b_lo, :])
              + gates[:, d:2 * d] * _dot(yb_ref[...], wb_ref[b_lo:c_lo, :])
              + gates[:, 2 * d:] * _dot(yc_ref[...], wb_ref[c_lo:, :]))
    o_ref[...] = _layer_norm(alpha * x + _dot(merged, wout_ref[...]), g_ref[...], b_ref[...])


def _merge(x2d, ya, yb, yc, consts, l, alpha, tm):
    m, d = x2d.shape
    tile = lambda n: pl.BlockSpec((tm, n), lambda i: (i, 0))
    return pl.pallas_call(
        functools.partial(_merge_kernel, alpha),
        grid=(m // tm,),
        in_specs=[tile(d), tile(A_WIDTH), tile(B_V_WIDTH), tile(C_WIDTH)] + [_layer_spec(a, l) for a in consts],
        out_specs=tile(d),
        out_shape=jax.ShapeDtypeStruct((m, d), F32),
        compiler_params=_cparams("parallel"),
        name="merge",
    )(x2d, ya, yb, yc, *consts)


def _ffn_kernel(alpha, tf, x_ref, w1_ref, w2_ref, g_ref, b_ref, o_ref):
    x = x_ref[...]
    xb = x.astype(BF16)
    acc = None
    for j in range(w1_ref.shape[1] // tf):
        h = jnp.maximum(jnp.dot(xb, w1_ref[:, j * tf:(j + 1) * tf], preferred_element_type=F32), 0.0)
        part = jnp.dot((h * h).astype(BF16), w2_ref[j * tf:(j + 1) * tf, :], preferred_element_type=F32)
        acc = part if acc is None else acc + part
    o_ref[...] = _layer_norm(alpha * x + acc, g_ref[...], b_ref[...])


def _ffn(x2d, consts, l, alpha, tm, tf):
    m, d = x2d.shape
    return pl.pallas_call(
        functools.partial(_ffn_kernel, alpha, tf),
        grid=(m // tm,),
        in_specs=[pl.BlockSpec((tm, d), lambda i: (i, 0))] + [_layer_spec(a, l) for a in consts],
        out_specs=pl.BlockSpec((tm, d), lambda i: (i, 0)),
        out_shape=jax.ShapeDtypeStruct((m, d), F32),
        compiler_params=_cparams("parallel"),
        name="ffn",
    )(x2d, *consts)


def _tile(n, want):
    t = min(n, want)
    assert n % t == 0, (n, want)
    return t


def kernel(x, w_in, b_gate, a_shift, a_w0, a_w2, a_a0, a_a2, a_g2, a_kk, a_ka, a_rk, a_lnx_g, a_lnx_b, c_lam_re, c_lam_im, c_log_dt, c_b_re, c_b_im, c_c_re, c_c_im, c_d, c_w_glu, c_b_glu, w_branch, w_out, ln1_g, ln1_b, w_ff1, w_ff2, ln2_g, ln2_b):
    bsz, seq, d = x.shape
    depth = w_in.shape[0]
    alpha = (2.0 * depth) ** 0.25
    tokens = bsz * seq
    rows = lambda a: a.reshape(depth, 1, -1)
    rwkv_params = _rwkv_params(dict(a_shift=a_shift, a_w0=a_w0, a_w2=a_w2, a_a0=a_a0, a_a2=a_a2, a_g2=a_g2,
                                    a_kk=a_kk, a_ka=a_ka, a_rk=a_rk, a_lnx_g=a_lnx_g, a_lnx_b=a_lnx_b))
    s5_params = _s5_params(dict(c_lam_re=c_lam_re, c_lam_im=c_lam_im, c_log_dt=c_log_dt, c_b_re=c_b_re,
                                c_b_im=c_b_im, c_c_re=c_c_re, c_c_im=c_c_im, c_d=c_d, c_w_glu=c_w_glu,
                                c_b_glu=c_b_glu))
    ret_tables = _retention_tables(seq)
    w_in_bf = w_in.astype(BF16)
    w_za, w_rest = w_in_bf[:, :, :A_PROJ], w_in_bf[:, :, A_PROJ:MIX_COLS]
    merge_consts = [w_in_bf[:, :, MIX_COLS:], rows(b_gate), w_branch.astype(BF16), w_out.astype(BF16),
                    rows(ln1_g), rows(ln1_b)]
    ffn_consts = [w_ff1.astype(BF16), w_ff2.astype(BF16), rows(ln2_g), rows(ln2_b)]
    xt = x.reshape(tokens, d)
    for l in range(depth):
        zmix, *stage_out = _proj(xt, w_za, w_rest, rwkv_params[0], bsz, seq, l, _tile(seq, 512))
        zmix = zmix.reshape(bsz, seq, REST_COLS)
        ya = _rwkv(stage_out, bsz, seq, rwkv_params[1], l, _tile(seq, 512))
        yb = _retention(zmix, bsz, seq, ret_tables, _tile(seq, 512))
        yc = _s5(zmix, bsz, seq, s5_params, l, _tile(seq, 64))
        x1 = _merge(xt, ya.reshape(tokens, -1), yb.reshape(tokens, -1), yc.reshape(tokens, -1),
                    merge_consts, l, alpha, _tile(tokens, 512))
        xt = _ffn(x1, ffn_consts, l, alpha, _tile(tokens, 1024), 1024)
    return xt.reshape(bsz, seq, d)
```

```python
import functools
import math

import jax
import jax.numpy as jnp
from jax import lax
from jax.experimental import pallas as pl
from jax.experimental.pallas import tpu as pltpu

F32 = jnp.float32
BF16 = jnp.bfloat16

A_HEADS = 8
A_HEAD_DIM = 64
A_WIDTH = A_HEADS * A_HEAD_DIM
A_DECAY_LORA = 64
A_ICLR_LORA = 64
A_GATE_LORA = 128
A_PROJ = 3 * A_WIDTH + A_DECAY_LORA + A_ICLR_LORA + A_GATE_LORA
A_GN_EPS = 64e-5
A_CHUNK = 64
A_GROUP = 8
A_GLANES = 128
A_SUMLANES = 256

B_HEADS = 4
B_QK_DIM = 128
B_V_DIM = 256
B_QK_WIDTH = B_HEADS * B_QK_DIM
B_V_WIDTH = B_HEADS * B_V_DIM
B_CHUNK = 128
B_ROPE_BASE = 10000.0
B_GN_EPS = 1e-5

C_WIDTH = 512
C_GROUP = 16
C_GROUPS = C_WIDTH // C_GROUP
C_STATE = 64
C_LANES = C_GROUPS * C_STATE
C_BLOCKS = 4
C_ULANES = C_WIDTH // C_BLOCKS
C_SLANES = C_LANES // C_BLOCKS

LN_EPS = 1e-5

RET_COLS = 2 * B_QK_WIDTH + 2 * B_V_WIDTH
REST_COLS = RET_COLS + C_WIDTH
MIX_COLS = A_PROJ + REST_COLS
U_BLK = RET_COLS // C_WIDTH
assert RET_COLS % C_WIDTH == 0

V7X_VMEM_LIMIT_BYTES = 56 * 1024 * 1024


def _cparams(*sem):
    return pltpu.CompilerParams(dimension_semantics=sem, vmem_limit_bytes=V7X_VMEM_LIMIT_BYTES)


def _full(shape):
    n = len(shape)
    return pl.BlockSpec(shape, lambda *_: (0,) * n)


def _layer_spec(a, l):
    n = a.ndim - 1
    return pl.BlockSpec((None,) + a.shape[1:], lambda *_: (l,) + (0,) * n, pipeline_mode=pl.Buffered(1))


def _dot(a, b):
    return jnp.dot(a.astype(BF16), b.astype(BF16), preferred_element_type=F32)


def _dot_nt(a, b):
    return lax.dot_general(a.astype(BF16), b.astype(BF16), (((1,), (1,)), ((), ())),
                           preferred_element_type=F32)


def _dot_tn(a, b):
    return lax.dot_general(a.astype(BF16), b.astype(BF16), (((0,), (0,)), ((), ())),
                           preferred_element_type=F32)


def _split3(x):
    hi = x.astype(BF16)
    r1 = x - hi.astype(F32)
    mid = r1.astype(BF16)
    lo = (r1 - mid.astype(F32)).astype(BF16)
    return hi, mid, lo


def _sigmoid(x):
    return 1.0 / (1.0 + jnp.exp(-x))


def _layer_norm(y, g, b):
    mu = jnp.mean(y, axis=-1, keepdims=True)
    d = y - mu
    var = jnp.mean(d * d, axis=-1, keepdims=True)
    return d * lax.rsqrt(var + LN_EPS) * g + b


def _proj_kernel(x_ref, wza_ref, wrest_ref, mu_ref, w0_ref, a0_ref, lora_ref, g2_ref, kkp_ref, kap_ref, rkp_ref,
                 ones_ref, rest_ref, r_ref, k_ref, v_ref, kn_ref, b0_ref, lw_ref, gate_ref, bonus_ref, carry_s):
    tm = x_ref.shape[0]
    w = A_WIDTH

    @pl.when(pl.program_id(1) == 0)
    def _():
        carry_s[...] = jnp.zeros_like(carry_s)

    xb = x_ref[...].astype(BF16)
    z = jnp.dot(xb, wza_ref[...], preferred_element_type=F32)
    rest_ref[...] = jnp.dot(xb, wrest_ref[...], preferred_element_type=F32).astype(rest_ref.dtype)

    ones = ones_ref[...]
    slabs = [slice(g * A_SUMLANES, (g + 1) * A_SUMLANES) for g in range(w // A_SUMLANES)]

    def seg_sum(t):
        tb = t.astype(BF16)
        return jnp.concatenate([jnp.dot(tb[:, s], ones, preferred_element_type=F32) for s in slabs], axis=1)

    rolled = pltpu.roll(z, 1, 0)
    rowid = lax.broadcasted_iota(jnp.int32, z.shape, 0)
    prev = jnp.where(rowid == 0, jnp.broadcast_to(carry_s[0:1, :], z.shape), rolled)
    zs = z + mu_ref[...] * (prev - z)
    carry_s[0:1, :] = z[tm - 1:tm, :]

    lz = zs[:, 3 * w:3 * w + 128]
    lane = lax.broadcasted_iota(jnp.int32, lz.shape, 1)
    lin = jnp.where(lane < A_DECAY_LORA, jnp.tanh(lz), lz)
    wa = _dot(lin, lora_ref[...])
    lw_ref[...] = (-math.exp(-0.5)) * _sigmoid(w0_ref[...] + wa[:, :w])
    ia = _sigmoid(a0_ref[...] + wa[:, w:])
    gate_ref[...] = _dot(_sigmoid(zs[:, 3 * w + 128:3 * w + 256]), g2_ref[...]).astype(gate_ref.dtype)

    r = zs[:, :w]
    k = zs[:, w:2 * w]
    v = zs[:, 2 * w:3 * w]
    kk = k * kkp_ref[...]
    kn = kk * lax.rsqrt(jnp.maximum(seg_sum(kk * kk), 1e-24))
    kmod = k * (1.0 + (ia - 1.0) * kap_ref[...])
    r_ref[...] = r.astype(r_ref.dtype)
    k_ref[...] = kmod.astype(k_ref.dtype)
    v_ref[...] = v.astype(v_ref.dtype)
    kn_ref[...] = kn.astype(kn_ref.dtype)
    b0_ref[...] = (kn * ia).astype(b0_ref.dtype)
    bonus_ref[...] = (seg_sum(r * kmod * rkp_ref[...]) * v).astype(bonus_ref.dtype)


def _proj(x2d, w_za, w_rest, params, bsz, seq, l, tm):
    layered, ones = params
    k = x2d.shape[1]
    per_seq = seq // tm
    wide = lambda dt: jax.ShapeDtypeStruct((bsz * seq, A_WIDTH), dt)
    row_tile = lambda n: pl.BlockSpec((tm, n), lambda b, i: (b * per_seq + i, 0))
    consts = [w_za, w_rest] + layered
    return pl.pallas_call(
        _proj_kernel,
        grid=(bsz, per_seq),
        in_specs=[row_tile(k)] + [_layer_spec(a, l) for a in consts] + [_full(ones.shape)],
        out_specs=[row_tile(REST_COLS)] + [row_tile(A_WIDTH)] * 8,
        out_shape=[jax.ShapeDtypeStruct((bsz * seq, REST_COLS), BF16),
                   wide(BF16), wide(BF16), wide(BF16), wide(BF16), wide(BF16), wide(F32), wide(BF16), wide(BF16)],
        scratch_shapes=[pltpu.VMEM((8, A_PROJ), F32)],
        compiler_params=_cparams("parallel", "arbitrary"),
        name="proj",
    )(x2d, *consts, ones)


def _chunk_rows(c):
    return pl.ds(c * A_CHUNK, A_CHUNK)


def _rwkv_kernel(r_ref, k_ref, v_ref, kn_ref, b0_ref, lw_ref, gate_ref, bonus_ref, lng_ref, lnb_ref,
                 ones_ref, tri_ref, y_ref, state_s, o_s, gam_s, x_s, m2_s, q_s, op_s):
    tc = r_ref.shape[0]
    n_chunks = tc // A_CHUNK
    w = A_WIDTH

    @pl.when(pl.program_id(1) == 0)
    def _():
        state_s[...] = jnp.zeros_like(state_s)

    ones = ones_ref[...]

    gw = A_GLANES
    groups = [slice(g * gw, (g + 1) * gw) for g in range(w // gw)]
    slabs = [slice(g * A_SUMLANES, (g + 1) * A_SUMLANES) for g in range(w // A_SUMLANES)]

    def seg_sum(x):
        xb = x.astype(BF16)
        return jnp.concatenate([jnp.dot(xb[:, s], ones, preferred_element_type=F32) for s in slabs], axis=1)

    tri = tri_ref[...]
    hpg = gw // A_HEAD_DIM
    rid = lax.broadcasted_iota(jnp.int32, (A_CHUNK, gw), 0)
    cid = lax.broadcasted_iota(jnp.int32, (A_CHUNK, gw), 1) % A_HEAD_DIM
    strict = rid > cid
    incl = rid >= cid
    eye = (rid == cid).astype(F32)
    brow = lax.broadcasted_iota(jnp.int32, (gw, gw), 0) // A_HEAD_DIM
    bcol = lax.broadcasted_iota(jnp.int32, (gw, gw), 1) // A_HEAD_DIM
    same_head = brow == bcol
    same_head_bf = same_head.astype(BF16)
    n = A_HEAD_DIM

    def bd(x):
        xb = x.astype(BF16)
        return jnp.concatenate([xb] * hpg, axis=0) * same_head_bf


    group = min(A_GROUP, n_chunks)
    assert n_chunks % group == 0

    def state_free_part(gi):
        items = []
        for j in range(group):
            c = gi * group + j
            rows = _chunk_rows(c)
            lw = lw_ref[rows, :]
            h3 = _split3(lw)
            cum = (jnp.dot(tri, h3[0], preferred_element_type=F32)
                   + jnp.dot(tri, h3[1], preferred_element_type=F32)
                   + jnp.dot(tri, h3[2], preferred_element_type=F32))
            e_in = jnp.exp(cum)
            e_ex = jnp.exp(cum - lw)
            e_ng = jnp.exp(-cum)
            rt = r_ref[rows, :].astype(F32) * e_in
            at = -kn_ref[rows, :].astype(F32) * e_ex
            bt = b0_ref[rows, :].astype(F32) * e_ng
            kt = k_ref[rows, :].astype(F32) * e_ng
            vv = v_ref[rows, :]
            gam_s[pl.ds(c, 1), :] = e_in[A_CHUNK - 1:A_CHUNK, :]
            for g, s in enumerate(groups):
                items.append((c * len(groups) + g, at[:, s], rt[:, s], bt[:, s], kt[:, s], vv[:, s]))
        ids = range(len(items))
        idx = [it[0] for it in items]
        a_ = [it[1] for it in items]
        r_ = [it[2] for it in items]
        b_ = [it[3] for it in items]
        k_ = [it[4] for it in items]
        v_ = [it[5] for it in items]
        ar = [jnp.concatenate([a_[i], r_[i]], axis=0) for i in ids]
        gb = [_dot_nt(ar[i], bd(b_[i])) for i in ids]
        gk = [_dot_nt(ar[i], bd(k_[i])) for i in ids]
        l_ab = [jnp.where(strict, gb[i][:A_CHUNK], 0.0) for i in ids]
        a_qb = [jnp.where(incl, gb[i][A_CHUNK:], 0.0) for i in ids]
        akq = [jnp.concatenate([jnp.where(strict, gk[i][:A_CHUNK], 0.0),
                                jnp.where(incl, gk[i][A_CHUNK:], 0.0)], axis=0) for i in ids]
        akqv = [_dot(akq[i], bd(v_[i])) for i in ids]
        akv = [akqv[i][:A_CHUNK] for i in ids]
        ov = [akqv[i][A_CHUNK:] for i in ids]
        tinv = [eye + l_ab[i] for i in ids]
        p = [_dot(l_ab[i], bd(l_ab[i])) for i in ids]
        for _ in range(4):
            tp = [_dot(jnp.concatenate([tinv[i], p[i]], axis=0), bd(p[i])) for i in ids]
            tinv = [tinv[i] + tp[i][:A_CHUNK] for i in ids]
            p = [tp[i][A_CHUNK:] for i in ids]
        tinv = [tinv[i] + _dot(tinv[i], bd(p[i])) for i in ids]
        wu = [_dot(tinv[i], jnp.concatenate([bd(a_[i]), bd(akv[i])], axis=1)) for i in ids]
        wm = [wu[i][:, :gw] for i in ids]
        uv = [wu[i][:, gw:] for i in ids]
        for i in ids:
            x_s[idx[i]] = jnp.where(same_head, _dot_tn(wm[i], b_[i]), 0.0)
        for i in ids:
            m2 = jnp.where(same_head, _dot_tn(jnp.concatenate([uv[i], v_[i]], axis=0),
                                              jnp.concatenate([b_[i], k_[i]], axis=0)), 0.0)
            m2_s[idx[i]] = sum(m2[h * n:(h + 1) * n] for h in range(1, hpg)) + m2[:n]
        qo = [_dot(a_qb[i], jnp.concatenate([bd(wm[i]), bd(uv[i])], axis=1)) for i in ids]
        for i in ids:
            q_s[idx[i]] = r_[i] + qo[i][:, :gw]
        for i in ids:
            op_s[idx[i]] = qo[i][:, gw:] + ov[i]

    def state_part(c):
        rows = _chunk_rows(c)
        gam = gam_s[pl.ds(c, 1), :]
        gids = range(len(groups))
        s0 = [state_s[g] for g in gids]
        sx = [_dot(s0[g], x_s[c * len(groups) + g]) for g in gids]
        o = [_dot_nt(q_s[c * len(groups) + g], bd(s0[g])) + op_s[c * len(groups) + g] for g in gids]
        for g in gids:
            state_s[g] = (s0[g] + sx[g] + m2_s[c * len(groups) + g]) * gam[:, groups[g]]
        o_s[rows, :] = jnp.concatenate(o, axis=1)

    for gi in range(n_chunks // group):
        state_free_part(gi)
    for c in range(n_chunks):
        state_part(c)

    o = o_s[...]
    inv_n = 1.0 / n
    mean = seg_sum(o) * inv_n
    d = o - mean
    var = seg_sum(d * d) * inv_n
    on = d * lax.rsqrt(var + A_GN_EPS) * lng_ref[...] + lnb_ref[...]
    on = on + bonus_ref[...].astype(F32)
    y_ref[...] = on * gate_ref[...].astype(F32)


def _rwkv_params(p):
    depth = p["a_w0"].shape[0]
    row = lambda a: a.reshape(depth, 1, -1).astype(F32)
    w = A_WIDTH
    lora = jnp.zeros((depth, 128, 2 * w), F32)
    lora = lora.at[:, :A_DECAY_LORA, :w].set(p["a_w2"]).at[:, A_DECAY_LORA:, w:].set(p["a_a2"]).astype(BF16)
    hid = jnp.arange(A_SUMLANES) // A_HEAD_DIM
    ones = (hid[:, None] == hid[None, :]).astype(BF16)
    ti = jnp.arange(A_CHUNK)
    tri = (ti[:, None] >= ti[None, :]).astype(BF16)
    stage = [row(p["a_shift"]), row(p["a_w0"]), row(p["a_a0"]), lora, p["a_g2"].astype(BF16),
             row(p["a_kk"]), row(p["a_ka"]), row(p["a_rk"])]
    return (stage, ones), ([row(p["a_lnx_g"]), row(p["a_lnx_b"])], [ones, tri])


def _rwkv(stage_out, bsz, seq, params, l, tc):
    layered, shared = params
    w = A_WIDTH
    n_chunks = tc // A_CHUNK
    n_groups = A_WIDTH // A_GLANES
    per_seq = seq // tc
    per_sq = pltpu.VMEM((n_chunks * n_groups, A_GLANES, A_GLANES), F32)
    per_row = pltpu.VMEM((n_chunks * n_groups, A_CHUNK, A_GLANES), F32)
    row_tile = pl.BlockSpec((tc, w), lambda b, i: (b * per_seq + i, 0))
    return pl.pallas_call(
        _rwkv_kernel,
        grid=(bsz, per_seq),
        in_specs=[row_tile] * len(stage_out)
                 + [_layer_spec(a, l) for a in layered] + [_full(a.shape) for a in shared],
        out_specs=row_tile,
        out_shape=jax.ShapeDtypeStruct((bsz * seq, w), F32),
        scratch_shapes=[pltpu.VMEM((n_groups, A_HEAD_DIM, A_GLANES), F32),
                        pltpu.VMEM((tc, w), F32),
                        pltpu.VMEM((max(8, n_chunks), w), F32),
                        per_sq, per_row, per_row, per_row],
        compiler_params=_cparams("parallel", "arbitrary"),
        name="rwkv7",
    )(*stage_out, *layered, *shared)


def _ret_kernel(z_ref, cos_ref, sin_ref, dmask_ref, qd_ref, kd_ref, cd_ref,
                y_ref, state_s, qr_s, kr_s, upd_s, st_s):
    tc = z_ref.shape[0]
    n_chunks = tc // B_CHUNK
    k_off, v_off, g_off = B_QK_WIDTH, 2 * B_QK_WIDTH, 2 * B_QK_WIDTH + B_V_WIDTH

    @pl.when(pl.program_id(1) == 0)
    def _():
        state_s[...] = jnp.zeros_like(state_s)

    cos2 = cos_ref[...]
    sin2 = sin_ref[...]
    half = B_QK_DIM // 2

    def rope(t):
        return t * cos2 + pltpu.roll(t, half, 1) * sin2

    for h in range(B_HEADS):
        qs = slice(h * B_QK_DIM, (h + 1) * B_QK_DIM)
        ks = slice(k_off + h * B_QK_DIM, k_off + (h + 1) * B_QK_DIM)
        qr_s[:, qs] = rope(z_ref[:, qs].astype(F32))
        kr_s[:, qs] = rope(z_ref[:, ks].astype(F32)) * (B_QK_DIM ** -0.5)

    items = [(c, h) for c in range(n_chunks) for h in range(B_HEADS)]
    rows = lambda c: slice(c * B_CHUNK, (c + 1) * B_CHUNK)
    qcol = lambda h: slice(h * B_QK_DIM, (h + 1) * B_QK_DIM)
    vcol = lambda h: slice(v_off + h * B_V_DIM, v_off + (h + 1) * B_V_DIM)

    for c, h in items:
        upd_s[c * B_HEADS + h] = _dot_tn(kr_s[rows(c), qcol(h)] * kd_ref[h], z_ref[rows(c), vcol(h)])
    for h in range(B_HEADS):
        st = state_s[h]
        for c in range(n_chunks):
            st_s[c * B_HEADS + h] = st
            st = cd_ref[h] * st + upd_s[c * B_HEADS + h]
        state_s[h] = st

    for c, h in items:
        qc = qr_s[rows(c), qcol(h)]
        scores = _dot_nt(qc, kr_s[rows(c), qcol(h)]) * dmask_ref[h]
        o = _dot(scores, z_ref[rows(c), vcol(h)]) + _dot(qc * qd_ref[h], st_s[c * B_HEADS + h])
        mu = jnp.mean(o, axis=-1, keepdims=True)
        d = o - mu
        var = jnp.mean(d * d, axis=-1, keepdims=True)
        on = d * lax.rsqrt(var + B_GN_EPS)
        g = z_ref[rows(c), g_off + h * B_V_DIM:g_off + (h + 1) * B_V_DIM].astype(F32)
        y_ref[rows(c), h * B_V_DIM:(h + 1) * B_V_DIM] = g * _sigmoid(g) * on


def _retention_tables(seq):
    f32 = F32
    pos = jnp.arange(seq, dtype=f32)
    half = B_QK_DIM // 2
    inv_freq = B_ROPE_BASE ** (-jnp.arange(half, dtype=f32) / half)
    ang = pos[:, None] * inv_freq[None, :]
    cos, sin = jnp.cos(ang), jnp.sin(ang)
    cos2 = jnp.concatenate([cos, cos], axis=1)
    sin2 = jnp.concatenate([-sin, sin], axis=1)
    log_gamma = jnp.log(1.0 - 2.0 ** (-5.0 - jnp.arange(B_HEADS, dtype=f32)))
    idx = jnp.arange(B_CHUNK, dtype=f32)
    rel = idx[:, None] - idx[None, :]
    dmask = jnp.where(rel >= 0, jnp.exp(log_gamma[:, None, None] * jnp.maximum(rel, 0.0)), 0.0)
    qd = jnp.broadcast_to(jnp.exp(log_gamma[:, None] * (idx + 1.0))[:, :, None],
                          (B_HEADS, B_CHUNK, B_QK_DIM))
    kd = jnp.broadcast_to(jnp.exp(log_gamma[:, None] * (B_CHUNK - 1.0 - idx))[:, :, None],
                          (B_HEADS, B_CHUNK, B_QK_DIM))
    cd = jnp.broadcast_to(jnp.exp(log_gamma * B_CHUNK)[:, None, None], (B_HEADS, B_QK_DIM, B_V_DIM))
    return [cos2, sin2, dmask, qd, kd, cd]


def _retention(zmix, bsz, seq, tables, tc):
    cos2, sin2 = tables[:2]
    tabs = tables[2:]
    n_items = (tc // B_CHUNK) * B_HEADS
    per_item = pltpu.VMEM((n_items, B_QK_DIM, B_V_DIM), F32)
    return pl.pallas_call(
        _ret_kernel,
        grid=(bsz, seq // tc),
        in_specs=[pl.BlockSpec((None, tc, RET_COLS), lambda b, i: (b, i, 0)),
                  pl.BlockSpec((tc, B_QK_DIM), lambda b, i: (i, 0)),
                  pl.BlockSpec((tc, B_QK_DIM), lambda b, i: (i, 0))] + [_full(a.shape) for a in tabs],
        out_specs=pl.BlockSpec((None, tc, B_V_WIDTH), lambda b, i: (b, i, 0)),
        out_shape=jax.ShapeDtypeStruct((bsz, seq, B_V_WIDTH), F32),
        scratch_shapes=[pltpu.VMEM((B_HEADS, B_QK_DIM, B_V_DIM), F32),
                        pltpu.VMEM((tc, B_QK_WIDTH), F32), pltpu.VMEM((tc, B_QK_WIDTH), F32),
                        per_item, per_item],
        compiler_params=_cparams("parallel", "arbitrary"),
        name="retention",
    )(zmix, cos2, sin2, *tabs)


def _s5_kernel(u_ref, bbr_ref, bbi_ref, ar_ref, ai_ref, cr_ref, ci_ref, d_ref, wg_ref, bg_ref,
               y_ref, xr_s, xi_s, sr_s, si_s):
    bsz, tt, _ = u_ref.shape

    @pl.when(pl.program_id(0) == 0)
    def _():
        sr_s[...] = jnp.zeros_like(sr_s)
        si_s[...] = jnp.zeros_like(si_s)

    u = pltpu.einshape("btc->tbc", u_ref[...].astype(F32)).reshape(tt * bsz, C_WIDTH)
    ub = u.astype(BF16)
    blk = [slice(m * C_SLANES, (m + 1) * C_SLANES) for m in range(C_BLOCKS)]

    for m in range(C_BLOCKS):
        um = ub[:, m * C_ULANES:(m + 1) * C_ULANES]
        xr_s[:, blk[m]] = jnp.dot(um, bbr_ref[m], preferred_element_type=F32)
        xi_s[:, blk[m]] = jnp.dot(um, bbi_ref[m], preferred_element_type=F32)

    for m in range(C_BLOCKS):
        cols = blk[m]
        ar = jnp.broadcast_to(ar_ref[:, cols], (bsz, C_SLANES))
        ai = jnp.broadcast_to(ai_ref[:, cols], (bsz, C_SLANES))

        def step(t, carry):
            xr, xi = carry
            rows = pl.ds(pl.multiple_of(t * bsz, bsz), bsz)
            nr = ar * xr - ai * xi + xr_s[rows, cols]
            ni = ar * xi + ai * xr + xi_s[rows, cols]
            xr_s[rows, cols] = nr
            xi_s[rows, cols] = ni
            return nr, ni

        xr, xi = lax.fori_loop(0, tt, step, (sr_s[:, cols], si_s[:, cols]), unroll=4)
        sr_s[:, cols] = xr
        si_s[:, cols] = xi

    parts = [_dot(xr_s[:, blk[m]], cr_ref[m]) - _dot(xi_s[:, blk[m]], ci_ref[m]) for m in range(C_BLOCKS)]
    y = jnp.concatenate(parts, axis=1) + d_ref[...] * u
    y = jax.nn.gelu(y)
    y = y * _sigmoid(_dot(y, wg_ref[...]) + bg_ref[...])
    y_ref[...] = pltpu.einshape("tbc->btc", y.reshape(tt, bsz, C_WIDTH))


def _s5_params(p):
    f32 = F32
    depth = p["c_log_dt"].shape[0]
    dt = jnp.exp(p["c_log_dt"].astype(f32))[..., None]
    lr, li = p["c_lam_re"].astype(f32), p["c_lam_im"].astype(f32)
    mag = jnp.exp(lr * dt)
    ab_re, ab_im = mag * jnp.cos(li * dt), mag * jnp.sin(li * dt)
    den = lr * lr + li * li
    f_re = ((ab_re - 1.0) * lr + ab_im * li) / den
    f_im = (ab_im * lr - (ab_re - 1.0) * li) / den
    bre, bim = p["c_b_re"].astype(f32), p["c_b_im"].astype(f32)
    bb_re = f_re[..., None] * bre - f_im[..., None] * bim
    bb_im = f_re[..., None] * bim + f_im[..., None] * bre
    gpb = C_GROUPS // C_BLOCKS
    eye = jnp.eye(gpb, dtype=f32)

    def in_blocks(bb):
        bb = bb.reshape(depth, C_BLOCKS, gpb, C_STATE, C_GROUP)
        return jnp.einsum("lmgpc,gh->lmgchp", bb, eye).reshape(depth, C_BLOCKS, C_ULANES, C_SLANES).astype(BF16)

    def out_blocks(cc):
        cc = cc.astype(f32).reshape(depth, C_BLOCKS, gpb, C_GROUP, C_STATE)
        return jnp.einsum("lmgcp,gh->lmgphc", cc, eye).reshape(depth, C_BLOCKS, C_SLANES, C_ULANES).astype(BF16)

    return [in_blocks(bb_re), in_blocks(bb_im), ab_re.reshape(depth, 1, C_LANES), ab_im.reshape(depth, 1, C_LANES),
            out_blocks(p["c_c_re"]), out_blocks(p["c_c_im"]), p["c_d"].reshape(depth, 1, C_WIDTH).astype(f32),
            p["c_w_glu"].astype(BF16), p["c_b_glu"].reshape(depth, 1, C_WIDTH).astype(f32)]


def _s5(zmix, bsz, seq, consts, l, tt):
    rows = tt * bsz
    return pl.pallas_call(
        _s5_kernel,
        grid=(seq // tt,),
        in_specs=[pl.BlockSpec((bsz, tt, C_WIDTH), lambda i: (0, i, U_BLK))] + [_layer_spec(a, l) for a in consts],
        out_specs=pl.BlockSpec((bsz, tt, C_WIDTH), lambda i: (0, i, 0)),
        out_shape=jax.ShapeDtypeStruct((bsz, seq, C_WIDTH), F32),
        scratch_shapes=[pltpu.VMEM((rows, C_LANES), F32), pltpu.VMEM((rows, C_LANES), F32),
                        pltpu.VMEM((bsz, C_LANES), F32), pltpu.VMEM((bsz, C_LANES), F32)],
        compiler_params=_cparams("arbitrary"),
        name="s5",
    )(zmix, *consts)


def _merge_kernel(alpha, x_ref, ya_ref, yb_ref, yc_ref, wgate_ref, bgate_ref, wb_ref,
                  wout_ref, g_ref, b_ref, o_ref):
    d = x_ref.shape[1]
    x = x_ref[...]
    gates = _sigmoid(_dot(x, wgate_ref[...]) + bgate_ref[...])
    b_lo, c_lo = A_WIDTH, A_WIDTH + B_V_WIDTH
    merged = (gates[:, :d] * _dot(ya_ref[...], wb_ref[:b_lo, :])
              + gates[:, d:2 * d] * _dot(yb_ref[...], wb_ref[b_lo:c_lo, :])
              + gates[:, 2 * d:] * _dot(yc_ref[...], wb_ref[c_lo:, :]))
    o_ref[...] = _layer_norm(alpha * x + _dot(merged, wout_ref[...]), g_ref[...], b_ref[...])


def _merge(x2d, ya, yb, yc, consts, l, alpha, tm):
    m, d = x2d.shape
    tile = lambda n: pl.BlockSpec((tm, n), lambda i: (i, 0))
    return pl.pallas_call(
        functools.partial(_merge_kernel, alpha),
        grid=(m // tm,),
        in_specs=[tile(d), tile(A_WIDTH), tile(B_V_WIDTH), tile(C_WIDTH)] + [_layer_spec(a, l) for a in consts],
        out_specs=tile(d),
        out_shape=jax.ShapeDtypeStruct((m, d), F32),
        compiler_params=_cparams("parallel"),
        name="merge",
    )(x2d, ya, yb, yc, *consts)


def _ffn_kernel(alpha, tf, x_ref, w1_ref, w2_ref, g_ref, b_ref, o_ref):
    x = x_ref[...]
    xb = x.astype(BF16)
    acc = None
    for j in range(w1_ref.shape[1] // tf):
        h = jnp.maximum(jnp.dot(xb, w1_ref[:, j * tf:(j + 1) * tf], preferred_element_type=F32), 0.0)
        part = jnp.dot((h * h).astype(BF16), w2_ref[j * tf:(j + 1) * tf, :], preferred_element_type=F32)
        acc = part if acc is None else acc + part
    o_ref[...] = _layer_norm(alpha * x + acc, g_ref[...], b_ref[...])


def _ffn(x2d, consts, l, alpha, tm, tf):
    m, d = x2d.shape
    return pl.pallas_call(
        functools.partial(_ffn_kernel, alpha, tf),
        grid=(m // tm,),
        in_specs=[pl.BlockSpec((tm, d), lambda i: (i, 0))] + [_layer_spec(a, l) for a in consts],
        out_specs=pl.BlockSpec((tm, d), lambda i: (i, 0)),
        out_shape=jax.ShapeDtypeStruct((m, d), F32),
        compiler_params=_cparams("parallel"),
        name="ffn",
    )(x2d, *consts)


def _tile(n, want):
    t = min(n, want)
    assert n % t == 0, (n, want)
    return t


def kernel(x, w_in, b_gate, a_shift, a_w0, a_w2, a_a0, a_a2, a_g2, a_kk, a_ka, a_rk, a_lnx_g, a_lnx_b, c_lam_re, c_lam_im, c_log_dt, c_b_re, c_b_im, c_c_re, c_c_im, c_d, c_w_glu, c_b_glu, w_branch, w_out, ln1_g, ln1_b, w_ff1, w_ff2, ln2_g, ln2_b):
    bsz, seq, d = x.shape
    depth = w_in.shape[0]
    alpha = (2.0 * depth) ** 0.25
    tokens = bsz * seq
    rows = lambda a: a.reshape(depth, 1, -1)
    rwkv_params = _rwkv_params(dict(a_shift=a_shift, a_w0=a_w0, a_w2=a_w2, a_a0=a_a0, a_a2=a_a2, a_g2=a_g2,
                                    a_kk=a_kk, a_ka=a_ka, a_rk=a_rk, a_lnx_g=a_lnx_g, a_lnx_b=a_lnx_b))
    s5_params = _s5_params(dict(c_lam_re=c_lam_re, c_lam_im=c_lam_im, c_log_dt=c_log_dt, c_b_re=c_b_re,
                                c_b_im=c_b_im, c_c_re=c_c_re, c_c_im=c_c_im, c_d=c_d, c_w_glu=c_w_glu,
                                c_b_glu=c_b_glu))
    ret_tables = _retention_tables(seq)
    w_in_bf = w_in.astype(BF16)
    w_za, w_rest = w_in_bf[:, :, :A_PROJ], w_in_bf[:, :, A_PROJ:MIX_COLS]
    merge_consts = [w_in_bf[:, :, MIX_COLS:], rows(b_gate), w_branch.astype(BF16), w_out.astype(BF16),
                    rows(ln1_g), rows(ln1_b)]
    ffn_consts = [w_ff1.astype(BF16), w_ff2.astype(BF16), rows(ln2_g), rows(ln2_b)]
    xt = x.reshape(tokens, d)
    for l in range(depth):
        zmix, *stage_out = _proj(xt, w_za, w_rest, rwkv_params[0], bsz, seq, l, _tile(seq, 512))
        zmix = zmix.reshape(bsz, seq, REST_COLS)
        ya = _rwkv(stage_out, bsz, seq, rwkv_params[1], l, _tile(seq, 512))
        yb = _retention(zmix, bsz, seq, ret_tables, _tile(seq, 512))
        yc = _s5(zmix, bsz, seq, s5_params, l, _tile(seq, 64))
        x1 = _merge(xt, ya.reshape(tokens, -1), yb.reshape(tokens, -1), yc.reshape(tokens, -1),
                    merge_consts, l, alpha, _tile(tokens, 512))
        xt = _ffn(x1, ffn_consts, l, alpha, _tile(tokens, 1024), 1024)
    return xt.reshape(bsz, seq, d)
```

```python
import functools
import math

import jax
import jax.numpy as jnp
from jax import lax
from jax.experimental import pallas as pl
from jax.experimental.pallas import tpu as pltpu

F32 = jnp.float32
BF16 = jnp.bfloat16

A_HEADS = 8
A_HEAD_DIM = 64
A_WIDTH = A_HEADS * A_HEAD_DIM
A_DECAY_LORA = 64
A_ICLR_LORA = 64
A_GATE_LORA = 128
A_PROJ = 3 * A_WIDTH + A_DECAY_LORA + A_ICLR_LORA + A_GATE_LORA
A_GN_EPS = 64e-5
A_CHUNK = 64
A_GROUP = 8
A_GLANES = 128
A_SUMLANES = 256

B_HEADS = 4
B_QK_DIM = 128
B_V_DIM = 256
B_QK_WIDTH = B_HEADS * B_QK_DIM
B_V_WIDTH = B_HEADS * B_V_DIM
B_CHUNK = 128
B_ROPE_BASE = 10000.0
B_GN_EPS = 1e-5

C_WIDTH = 512
C_GROUP = 16
C_GROUPS = C_WIDTH // C_GROUP
C_STATE = 64
C_LANES = C_GROUPS * C_STATE
C_BLOCKS = 4
C_ULANES = C_WIDTH // C_BLOCKS
C_SLANES = C_LANES // C_BLOCKS
C_PARTS = 4

LN_EPS = 1e-5

RET_COLS = 2 * B_QK_WIDTH + 2 * B_V_WIDTH
REST_COLS = RET_COLS + C_WIDTH
MIX_COLS = A_PROJ + REST_COLS
U_BLK = RET_COLS // C_WIDTH
assert RET_COLS % C_WIDTH == 0

V7X_VMEM_LIMIT_BYTES = 56 * 1024 * 1024


def _cparams(*sem):
    return pltpu.CompilerParams(dimension_semantics=sem, vmem_limit_bytes=V7X_VMEM_LIMIT_BYTES)


def _full(shape):
    n = len(shape)
    return pl.BlockSpec(shape, lambda *_: (0,) * n)


def _layer_spec(a, l):
    n = a.ndim - 1
    return pl.BlockSpec((None,) + a.shape[1:], lambda *_: (l,) + (0,) * n, pipeline_mode=pl.Buffered(1))


def _dot(a, b):
    return jnp.dot(a.astype(BF16), b.astype(BF16), preferred_element_type=F32)


def _dot_nt(a, b):
    return lax.dot_general(a.astype(BF16), b.astype(BF16), (((1,), (1,)), ((), ())),
                           preferred_element_type=F32)


def _dot_tn(a, b):
    return lax.dot_general(a.astype(BF16), b.astype(BF16), (((0,), (0,)), ((), ())),
                           preferred_element_type=F32)


def _split3(x):
    hi = x.astype(BF16)
    r1 = x - hi.astype(F32)
    mid = r1.astype(BF16)
    lo = (r1 - mid.astype(F32)).astype(BF16)
    return hi, mid, lo


def _sigmoid(x):
    return 1.0 / (1.0 + jnp.exp(-x))


def _layer_norm(y, g, b):
    mu = jnp.mean(y, axis=-1, keepdims=True)
    d = y - mu
    var = jnp.mean(d * d, axis=-1, keepdims=True)
    return d * lax.rsqrt(var + LN_EPS) * g + b


def _proj_kernel(x_ref, wza_ref, wrest_ref, mu_ref, w0_ref, a0_ref, lora_ref, g2_ref, kkp_ref, kap_ref, rkp_ref,
                 ones_ref, rest_ref, r_ref, k_ref, v_ref, kn_ref, b0_ref, lw_ref, gate_ref, bonus_ref, carry_s):
    tm = x_ref.shape[0]
    w = A_WIDTH

    @pl.when(pl.program_id(1) == 0)
    def _():
        carry_s[...] = jnp.zeros_like(carry_s)

    xb = x_ref[...].astype(BF16)
    z = jnp.dot(xb, wza_ref[...], preferred_element_type=F32)
    rest_ref[...] = jnp.dot(xb, wrest_ref[...], preferred_element_type=F32).astype(rest_ref.dtype)

    ones = ones_ref[...]
    slabs = [slice(g * A_SUMLANES, (g + 1) * A_SUMLANES) for g in range(w // A_SUMLANES)]

    def seg_sum(t):
        tb = t.astype(BF16)
        return jnp.concatenate([jnp.dot(tb[:, s], ones, preferred_element_type=F32) for s in slabs], axis=1)

    rolled = pltpu.roll(z, 1, 0)
    rowid = lax.broadcasted_iota(jnp.int32, z.shape, 0)
    prev = jnp.where(rowid == 0, jnp.broadcast_to(carry_s[0:1, :], z.shape), rolled)
    zs = z + mu_ref[...] * (prev - z)
    carry_s[0:1, :] = z[tm - 1:tm, :]

    lz = zs[:, 3 * w:3 * w + 128]
    lane = lax.broadcasted_iota(jnp.int32, lz.shape, 1)
    lin = jnp.where(lane < A_DECAY_LORA, jnp.tanh(lz), lz)
    wa = _dot(lin, lora_ref[...])
    lw_ref[...] = (-math.exp(-0.5)) * _sigmoid(w0_ref[...] + wa[:, :w])
    ia = _sigmoid(a0_ref[...] + wa[:, w:])
    gate_ref[...] = _dot(_sigmoid(zs[:, 3 * w + 128:3 * w + 256]), g2_ref[...]).astype(gate_ref.dtype)

    r = zs[:, :w]
    k = zs[:, w:2 * w]
    v = zs[:, 2 * w:3 * w]
    kk = k * kkp_ref[...]
    kn = kk * lax.rsqrt(jnp.maximum(seg_sum(kk * kk), 1e-24))
    kmod = k * (1.0 + (ia - 1.0) * kap_ref[...])
    r_ref[...] = r.astype(r_ref.dtype)
    k_ref[...] = kmod.astype(k_ref.dtype)
    v_ref[...] = v.astype(v_ref.dtype)
    kn_ref[...] = kn.astype(kn_ref.dtype)
    b0_ref[...] = (kn * ia).astype(b0_ref.dtype)
    bonus_ref[...] = (seg_sum(r * kmod * rkp_ref[...]) * v).astype(bonus_ref.dtype)


def _proj(x2d, w_za, w_rest, params, bsz, seq, l, tm):
    layered, ones = params
    k = x2d.shape[1]
    per_seq = seq // tm
    wide = lambda dt: jax.ShapeDtypeStruct((bsz * seq, A_WIDTH), dt)
    row_tile = lambda n: pl.BlockSpec((tm, n), lambda b, i: (b * per_seq + i, 0))
    consts = [w_za, w_rest] + layered
    return pl.pallas_call(
        _proj_kernel,
        grid=(bsz, per_seq),
        in_specs=[row_tile(k)] + [_layer_spec(a, l) for a in consts] + [_full(ones.shape)],
        out_specs=[row_tile(REST_COLS)] + [row_tile(A_WIDTH)] * 8,
        out_shape=[jax.ShapeDtypeStruct((bsz * seq, REST_COLS), BF16),
                   wide(BF16), wide(BF16), wide(BF16), wide(BF16), wide(BF16), wide(F32), wide(BF16), wide(BF16)],
        scratch_shapes=[pltpu.VMEM((8, A_PROJ), F32)],
        compiler_params=_cparams("parallel", "arbitrary"),
        name="proj",
    )(x2d, *consts, ones)


def _chunk_rows(c):
    return pl.ds(c * A_CHUNK, A_CHUNK)


def _rwkv_kernel(r_ref, k_ref, v_ref, kn_ref, b0_ref, lw_ref, gate_ref, bonus_ref, lng_ref, lnb_ref,
                 ones_ref, tri_ref, y_ref, state_s, o_s, gam_s, x_s, m2_s, q_s, op_s):
    tc = r_ref.shape[0]
    n_chunks = tc // A_CHUNK
    w = A_WIDTH

    @pl.when(pl.program_id(1) == 0)
    def _():
        state_s[...] = jnp.zeros_like(state_s)

    ones = ones_ref[...]

    gw = A_GLANES
    groups = [slice(g * gw, (g + 1) * gw) for g in range(w // gw)]
    slabs = [slice(g * A_SUMLANES, (g + 1) * A_SUMLANES) for g in range(w // A_SUMLANES)]

    def seg_sum(x):
        xb = x.astype(BF16)
        return jnp.concatenate([jnp.dot(xb[:, s], ones, preferred_element_type=F32) for s in slabs], axis=1)

    tri = tri_ref[...]
    hpg = gw // A_HEAD_DIM
    rid = lax.broadcasted_iota(jnp.int32, (A_CHUNK, gw), 0)
    cid = lax.broadcasted_iota(jnp.int32, (A_CHUNK, gw), 1) % A_HEAD_DIM
    strict = rid > cid
    incl = rid >= cid
    eye = (rid == cid).astype(F32)
    brow = lax.broadcasted_iota(jnp.int32, (gw, gw), 0) // A_HEAD_DIM
    bcol = lax.broadcasted_iota(jnp.int32, (gw, gw), 1) // A_HEAD_DIM
    same_head = brow == bcol
    same_head_bf = same_head.astype(BF16)
    n = A_HEAD_DIM

    def bd(x):
        xb = x.astype(BF16)
        return jnp.concatenate([xb] * hpg, axis=0) * same_head_bf


    group = min(A_GROUP, n_chunks)
    assert n_chunks % group == 0

    def state_free_part(gi):
        items = []
        for j in range(group):
            c = gi * group + j
            rows = _chunk_rows(c)
            lw = lw_ref[rows, :]
            h3 = _split3(lw)
            cum = (jnp.dot(tri, h3[0], preferred_element_type=F32)
                   + jnp.dot(tri, h3[1], preferred_element_type=F32)
                   + jnp.dot(tri, h3[2], preferred_element_type=F32))
            e_in = jnp.exp(cum)
            e_ex = jnp.exp(cum - lw)
            e_ng = jnp.exp(-cum)
            rt = r_ref[rows, :].astype(F32) * e_in
            at = -kn_ref[rows, :].astype(F32) * e_ex
            bt = b0_ref[rows, :].astype(F32) * e_ng
            kt = k_ref[rows, :].astype(F32) * e_ng
            vv = v_ref[rows, :]
            gam_s[pl.ds(c, 1), :] = e_in[A_CHUNK - 1:A_CHUNK, :]
            for g, s in enumerate(groups):
                items.append((c * len(groups) + g, at[:, s], rt[:, s], bt[:, s], kt[:, s], vv[:, s]))
        ids = range(len(items))
        idx = [it[0] for it in items]
        a_ = [it[1] for it in items]
        r_ = [it[2] for it in items]
        b_ = [it[3] for it in items]
        k_ = [it[4] for it in items]
        v_ = [it[5] for it in items]
        ar = [jnp.concatenate([a_[i], r_[i]], axis=0) for i in ids]
        gb = [_dot_nt(ar[i], bd(b_[i])) for i in ids]
        gk = [_dot_nt(ar[i], bd(k_[i])) for i in ids]
        l_ab = [jnp.where(strict, gb[i][:A_CHUNK], 0.0) for i in ids]
        a_qb = [jnp.where(incl, gb[i][A_CHUNK:], 0.0) for i in ids]
        akq = [jnp.concatenate([jnp.where(strict, gk[i][:A_CHUNK], 0.0),
                                jnp.where(incl, gk[i][A_CHUNK:], 0.0)], axis=0) for i in ids]
        akqv = [_dot(akq[i], bd(v_[i])) for i in ids]
        akv = [akqv[i][:A_CHUNK] for i in ids]
        ov = [akqv[i][A_CHUNK:] for i in ids]
        tinv = [eye + l_ab[i] for i in ids]
        p = [_dot(l_ab[i], bd(l_ab[i])) for i in ids]
        for _ in range(4):
            tp = [_dot(jnp.concatenate([tinv[i], p[i]], axis=0), bd(p[i])) for i in ids]
            tinv = [tinv[i] + tp[i][:A_CHUNK] for i in ids]
            p = [tp[i][A_CHUNK:] for i in ids]
        tinv = [tinv[i] + _dot(tinv[i], bd(p[i])) for i in ids]
        wu = [_dot(tinv[i], jnp.concatenate([bd(a_[i]), bd(akv[i])], axis=1)) for i in ids]
        wm = [wu[i][:, :gw] for i in ids]
        uv = [wu[i][:, gw:] for i in ids]
        for i in ids:
            x_s[idx[i]] = jnp.where(same_head, _dot_tn(wm[i], b_[i]), 0.0)
        for i in ids:
            m2 = jnp.where(same_head, _dot_tn(jnp.concatenate([uv[i], v_[i]], axis=0),
                                              jnp.concatenate([b_[i], k_[i]], axis=0)), 0.0)
            m2_s[idx[i]] = sum(m2[h * n:(h + 1) * n] for h in range(1, hpg)) + m2[:n]
        qo = [_dot(a_qb[i], jnp.concatenate([bd(wm[i]), bd(uv[i])], axis=1)) for i in ids]
        for i in ids:
            q_s[idx[i]] = r_[i] + qo[i][:, :gw]
        for i in ids:
            op_s[idx[i]] = qo[i][:, gw:] + ov[i]

    def state_part(c):
        rows = _chunk_rows(c)
        gam = gam_s[pl.ds(c, 1), :]
        gids = range(len(groups))
        s0 = [state_s[g] for g in gids]
        sx = [_dot(s0[g], x_s[c * len(groups) + g]) for g in gids]
        o = [_dot_nt(q_s[c * len(groups) + g], bd(s0[g])) + op_s[c * len(groups) + g] for g in gids]
        for g in gids:
            state_s[g] = (s0[g] + sx[g] + m2_s[c * len(groups) + g]) * gam[:, groups[g]]
        o_s[rows, :] = jnp.concatenate(o, axis=1)

    for gi in range(n_chunks // group):
        state_free_part(gi)
    for c in range(n_chunks):
        state_part(c)

    o = o_s[...]
    inv_n = 1.0 / n
    mean = seg_sum(o) * inv_n
    d = o - mean
    var = seg_sum(d * d) * inv_n
    on = d * lax.rsqrt(var + A_GN_EPS) * lng_ref[...] + lnb_ref[...]
    on = on + bonus_ref[...].astype(F32)
    y_ref[...] = on * gate_ref[...].astype(F32)


def _rwkv_params(p):
    depth = p["a_w0"].shape[0]
    row = lambda a: a.reshape(depth, 1, -1).astype(F32)
    w = A_WIDTH
    lora = jnp.zeros((depth, 128, 2 * w), F32)
    lora = lora.at[:, :A_DECAY_LORA, :w].set(p["a_w2"]).at[:, A_DECAY_LORA:, w:].set(p["a_a2"]).astype(BF16)
    hid = jnp.arange(A_SUMLANES) // A_HEAD_DIM
    ones = (hid[:, None] == hid[None, :]).astype(BF16)
    ti = jnp.arange(A_CHUNK)
    tri = (ti[:, None] >= ti[None, :]).astype(BF16)
    stage = [row(p["a_shift"]), row(p["a_w0"]), row(p["a_a0"]), lora, p["a_g2"].astype(BF16),
             row(p["a_kk"]), row(p["a_ka"]), row(p["a_rk"])]
    return (stage, ones), ([row(p["a_lnx_g"]), row(p["a_lnx_b"])], [ones, tri])


def _rwkv(stage_out, bsz, seq, params, l, tc):
    layered, shared = params
    w = A_WIDTH
    n_chunks = tc // A_CHUNK
    n_groups = A_WIDTH // A_GLANES
    per_seq = seq // tc
    per_sq = pltpu.VMEM((n_chunks * n_groups, A_GLANES, A_GLANES), F32)
    per_row = pltpu.VMEM((n_chunks * n_groups, A_CHUNK, A_GLANES), F32)
    row_tile = pl.BlockSpec((tc, w), lambda b, i: (b * per_seq + i, 0))
    return pl.pallas_call(
        _rwkv_kernel,
        grid=(bsz, per_seq),
        in_specs=[row_tile] * len(stage_out)
                 + [_layer_spec(a, l) for a in layered] + [_full(a.shape) for a in shared],
        out_specs=row_tile,
        out_shape=jax.ShapeDtypeStruct((bsz * seq, w), F32),
        scratch_shapes=[pltpu.VMEM((n_groups, A_HEAD_DIM, A_GLANES), F32),
                        pltpu.VMEM((tc, w), F32),
                        pltpu.VMEM((max(8, n_chunks), w), F32),
                        per_sq, per_row, per_row, per_row],
        compiler_params=_cparams("parallel", "arbitrary"),
        name="rwkv7",
    )(*stage_out, *layered, *shared)


def _ret_kernel(z_ref, cos_ref, sin_ref, dmask_ref, qd_ref, kd_ref, cd_ref,
                y_ref, state_s, qr_s, kr_s, upd_s, st_s):
    tc = z_ref.shape[0]
    n_chunks = tc // B_CHUNK
    k_off, v_off, g_off = B_QK_WIDTH, 2 * B_QK_WIDTH, 2 * B_QK_WIDTH + B_V_WIDTH

    @pl.when(pl.program_id(1) == 0)
    def _():
        state_s[...] = jnp.zeros_like(state_s)

    cos2 = cos_ref[...]
    sin2 = sin_ref[...]
    half = B_QK_DIM // 2

    def rope(t):
        return t * cos2 + pltpu.roll(t, half, 1) * sin2

    for h in range(B_HEADS):
        qs = slice(h * B_QK_DIM, (h + 1) * B_QK_DIM)
        ks = slice(k_off + h * B_QK_DIM, k_off + (h + 1) * B_QK_DIM)
        qr_s[:, qs] = rope(z_ref[:, qs].astype(F32))
        kr_s[:, qs] = rope(z_ref[:, ks].astype(F32)) * (B_QK_DIM ** -0.5)

    items = [(c, h) for c in range(n_chunks) for h in range(B_HEADS)]
    rows = lambda c: slice(c * B_CHUNK, (c + 1) * B_CHUNK)
    qcol = lambda h: slice(h * B_QK_DIM, (h + 1) * B_QK_DIM)
    vcol = lambda h: slice(v_off + h * B_V_DIM, v_off + (h + 1) * B_V_DIM)

    for c, h in items:
        upd_s[c * B_HEADS + h] = _dot_tn(kr_s[rows(c), qcol(h)] * kd_ref[h], z_ref[rows(c), vcol(h)])
    for h in range(B_HEADS):
        st = state_s[h]
        for c in range(n_chunks):
            st_s[c * B_HEADS + h] = st
            st = cd_ref[h] * st + upd_s[c * B_HEADS + h]
        state_s[h] = st

    for c, h in items:
        qc = qr_s[rows(c), qcol(h)]
        scores = _dot_nt(qc, kr_s[rows(c), qcol(h)]) * dmask_ref[h]
        o = _dot(scores, z_ref[rows(c), vcol(h)]) + _dot(qc * qd_ref[h], st_s[c * B_HEADS + h])
        mu = jnp.mean(o, axis=-1, keepdims=True)
        d = o - mu
        var = jnp.mean(d * d, axis=-1, keepdims=True)
        on = d * lax.rsqrt(var + B_GN_EPS)
        g = z_ref[rows(c), g_off + h * B_V_DIM:g_off + (h + 1) * B_V_DIM].astype(F32)
        y_ref[rows(c), h * B_V_DIM:(h + 1) * B_V_DIM] = g * _sigmoid(g) * on


def _retention_tables(seq):
    f32 = F32
    pos = jnp.arange(seq, dtype=f32)
    half = B_QK_DIM // 2
    inv_freq = B_ROPE_BASE ** (-jnp.arange(half, dtype=f32) / half)
    ang = pos[:, None] * inv_freq[None, :]
    cos, sin = jnp.cos(ang), jnp.sin(ang)
    cos2 = jnp.concatenate([cos, cos], axis=1)
    sin2 = jnp.concatenate([-sin, sin], axis=1)
    log_gamma = jnp.log(1.0 - 2.0 ** (-5.0 - jnp.arange(B_HEADS, dtype=f32)))
    idx = jnp.arange(B_CHUNK, dtype=f32)
    rel = idx[:, None] - idx[None, :]
    dmask = jnp.where(rel >= 0, jnp.exp(log_gamma[:, None, None] * jnp.maximum(rel, 0.0)), 0.0)
    qd = jnp.broadcast_to(jnp.exp(log_gamma[:, None] * (idx + 1.0))[:, :, None],
                          (B_HEADS, B_CHUNK, B_QK_DIM))
    kd = jnp.broadcast_to(jnp.exp(log_gamma[:, None] * (B_CHUNK - 1.0 - idx))[:, :, None],
                          (B_HEADS, B_CHUNK, B_QK_DIM))
    cd = jnp.broadcast_to(jnp.exp(log_gamma * B_CHUNK)[:, None, None], (B_HEADS, B_QK_DIM, B_V_DIM))
    return [cos2, sin2, dmask, qd, kd, cd]


def _retention(zmix, bsz, seq, tables, tc):
    cos2, sin2 = tables[:2]
    tabs = tables[2:]
    n_items = (tc // B_CHUNK) * B_HEADS
    per_item = pltpu.VMEM((n_items, B_QK_DIM, B_V_DIM), F32)
    return pl.pallas_call(
        _ret_kernel,
        grid=(bsz, seq // tc),
        in_specs=[pl.BlockSpec((None, tc, RET_COLS), lambda b, i: (b, i, 0)),
                  pl.BlockSpec((tc, B_QK_DIM), lambda b, i: (i, 0)),
                  pl.BlockSpec((tc, B_QK_DIM), lambda b, i: (i, 0))] + [_full(a.shape) for a in tabs],
        out_specs=pl.BlockSpec((None, tc, B_V_WIDTH), lambda b, i: (b, i, 0)),
        out_shape=jax.ShapeDtypeStruct((bsz, seq, B_V_WIDTH), F32),
        scratch_shapes=[pltpu.VMEM((B_HEADS, B_QK_DIM, B_V_DIM), F32),
                        pltpu.VMEM((tc, B_QK_WIDTH), F32), pltpu.VMEM((tc, B_QK_WIDTH), F32),
                        per_item, per_item],
        compiler_params=_cparams("parallel", "arbitrary"),
        name="retention",
    )(zmix, cos2, sin2, *tabs)


def _s5_kernel(u_ref, bbr_ref, bbi_ref, ar_ref, ai_ref, cr_ref, ci_ref, d_ref, wg_ref, bg_ref,
               y_ref, xr_s, xi_s, u_s, sr_s, si_s):
    bsz, tt, _ = u_ref.shape
    tp = tt // C_PARTS
    rp = tp * bsz

    @pl.when(pl.program_id(0) == 0)
    def _():
        sr_s[...] = jnp.zeros_like(sr_s)
        si_s[...] = jnp.zeros_like(si_s)

    blk = [slice(m * C_SLANES, (m + 1) * C_SLANES) for m in range(C_BLOCKS)]


    def drive(p):
        rows = slice(p * rp, (p + 1) * rp)
        u = pltpu.einshape("btc->tbc", u_ref[:, p * tp:(p + 1) * tp, :].astype(F32)).reshape(rp, C_WIDTH)
        u_s[rows, :] = u
        ub = u.astype(BF16)
        for m in range(C_BLOCKS):
            um = ub[:, m * C_ULANES:(m + 1) * C_ULANES]
            xr_s[rows, blk[m]] = jnp.dot(um, bbr_ref[m], preferred_element_type=F32)
            xi_s[rows, blk[m]] = jnp.dot(um, bbi_ref[m], preferred_element_type=F32)

    def scan(p):
        for m in range(C_BLOCKS):
            cols = blk[m]
            ar = jnp.broadcast_to(ar_ref[:, cols], (bsz, C_SLANES))
            ai = jnp.broadcast_to(ai_ref[:, cols], (bsz, C_SLANES))
            xr, xi = sr_s[:, cols], si_s[:, cols]
            for t in range(p * tp, (p + 1) * tp):
                rows = slice(t * bsz, (t + 1) * bsz)
                xr, xi = (ar * xr - ai * xi + xr_s[rows, cols], ar * xi + ai * xr + xi_s[rows, cols])
                xr_s[rows, cols] = xr
                xi_s[rows, cols] = xi
            sr_s[:, cols] = xr
            si_s[:, cols] = xi

    def readout(p):
        rows = slice(p * rp, (p + 1) * rp)
        parts = [_dot(xr_s[rows, blk[m]], cr_ref[m]) - _dot(xi_s[rows, blk[m]], ci_ref[m])
                 for m in range(C_BLOCKS)]
        y = jnp.concatenate(parts, axis=1) + d_ref[...] * u_s[rows, :]
        y = jax.nn.gelu(y)
        y = y * _sigmoid(_dot(y, wg_ref[...]) + bg_ref[...])
        y_ref[:, p * tp:(p + 1) * tp, :] = pltpu.einshape("tbc->btc", y.reshape(tp, bsz, C_WIDTH))

    drive(0)
    for p in range(C_PARTS):
        if p + 1 < C_PARTS:
            drive(p + 1)
        scan(p)
        readout(p)


def _s5_params(p):
    f32 = F32
    depth = p["c_log_dt"].shape[0]
    dt = jnp.exp(p["c_log_dt"].astype(f32))[..., None]
    lr, li = p["c_lam_re"].astype(f32), p["c_lam_im"].astype(f32)
    mag = jnp.exp(lr * dt)
    ab_re, ab_im = mag * jnp.cos(li * dt), mag * jnp.sin(li * dt)
    den = lr * lr + li * li
    f_re = ((ab_re - 1.0) * lr + ab_im * li) / den
    f_im = (ab_im * lr - (ab_re - 1.0) * li) / den
    bre, bim = p["c_b_re"].astype(f32), p["c_b_im"].astype(f32)
    bb_re = f_re[..., None] * bre - f_im[..., None] * bim
    bb_im = f_re[..., None] * bim + f_im[..., None] * bre
    gpb = C_GROUPS // C_BLOCKS
    eye = jnp.eye(gpb, dtype=f32)

    def in_blocks(bb):
        bb = bb.reshape(depth, C_BLOCKS, gpb, C_STATE, C_GROUP)
        return jnp.einsum("lmgpc,gh->lmgchp", bb, eye).reshape(depth, C_BLOCKS, C_ULANES, C_SLANES).astype(BF16)

    def out_blocks(cc):
        cc = cc.astype(f32).reshape(depth, C_BLOCKS, gpb, C_GROUP, C_STATE)
        return jnp.einsum("lmgcp,gh->lmgphc", cc, eye).reshape(depth, C_BLOCKS, C_SLANES, C_ULANES).astype(BF16)

    return [in_blocks(bb_re), in_blocks(bb_im), ab_re.reshape(depth, 1, C_LANES), ab_im.reshape(depth, 1, C_LANES),
            out_blocks(p["c_c_re"]), out_blocks(p["c_c_im"]), p["c_d"].reshape(depth, 1, C_WIDTH).astype(f32),
            p["c_w_glu"].astype(BF16), p["c_b_glu"].reshape(depth, 1, C_WIDTH).astype(f32)]


def _s5(zmix, bsz, seq, consts, l, tt):
    rows = tt * bsz
    return pl.pallas_call(
        _s5_kernel,
        grid=(seq // tt,),
        in_specs=[pl.BlockSpec((bsz, tt, C_WIDTH), lambda i: (0, i, U_BLK))] + [_layer_spec(a, l) for a in consts],
        out_specs=pl.BlockSpec((bsz, tt, C_WIDTH), lambda i: (0, i, 0)),
        out_shape=jax.ShapeDtypeStruct((bsz, seq, C_WIDTH), F32),
        scratch_shapes=[pltpu.VMEM((rows, C_LANES), F32), pltpu.VMEM((rows, C_LANES), F32),
                        pltpu.VMEM((rows, C_WIDTH), F32),
                        pltpu.VMEM((bsz, C_LANES), F32), pltpu.VMEM((bsz, C_LANES), F32)],
        compiler_params=_cparams("arbitrary"),
        name="s5",
    )(zmix, *consts)


def _merge_kernel(alpha, x_ref, ya_ref, yb_ref, yc_ref, wgate_ref, bgate_ref, wb_ref,
                  wout_ref, g_ref, b_ref, o_ref):
    d = x_ref.shape[1]
    x = x_ref[...]
    gates = _sigmoid(_dot(x, wgate_ref[...]) + bgate_ref[...])
    b_lo, c_lo = A_WIDTH, A_WIDTH + B_V_WIDTH
    merged = (gates[:, :d] * _dot(ya_ref[...], wb_ref[:b_lo, :])
              + gates[:, d:2 * d] * _dot(yb_ref[...], wb_ref[b_lo:c_lo, :])
              + gates[:, 2 * d:] * _dot(yc_ref[...], wb_ref[c_lo:, :]))
    o_ref[...] = _layer_norm(alpha * x + _dot(merged, wout_ref[...]), g_ref[...], b_ref[...])


def _merge(x2d, ya, yb, yc, consts, l, alpha, tm):
    m, d = x2d.shape
    tile = lambda n: pl.BlockSpec((tm, n), lambda i: (i, 0))
    return pl.pallas_call(
        functools.partial(_merge_kernel, alpha),
        grid=(m // tm,),
        in_specs=[tile(d), tile(A_WIDTH), tile(B_V_WIDTH), tile(C_WIDTH)] + [_layer_spec(a, l) for a in consts],
        out_specs=tile(d),
        out_shape=jax.ShapeDtypeStruct((m, d), F32),
        compiler_params=_cparams("parallel"),
        name="merge",
    )(x2d, ya, yb, yc, *consts)


def _ffn_kernel(alpha, tf, x_ref, w1_ref, w2_ref, g_ref, b_ref, o_ref):
    x = x_ref[...]
    xb = x.astype(BF16)
    acc = None
    for j in range(w1_ref.shape[1] // tf):
        h = jnp.maximum(jnp.dot(xb, w1_ref[:, j * tf:(j + 1) * tf], preferred_element_type=F32), 0.0)
        part = jnp.dot((h * h).astype(BF16), w2_ref[j * tf:(j + 1) * tf, :], preferred_element_type=F32)
        acc = part if acc is None else acc + part
    o_ref[...] = _layer_norm(alpha * x + acc, g_ref[...], b_ref[...])


def _ffn(x2d, consts, l, alpha, tm, tf):
    m, d = x2d.shape
    return pl.pallas_call(
        functools.partial(_ffn_kernel, alpha, tf),
        grid=(m // tm,),
        in_specs=[pl.BlockSpec((tm, d), lambda i: (i, 0))] + [_layer_spec(a, l) for a in consts],
        out_specs=pl.BlockSpec((tm, d), lambda i: (i, 0)),
        out_shape=jax.ShapeDtypeStruct((m, d), F32),
        compiler_params=_cparams("parallel"),
        name="ffn",
    )(x2d, *consts)


def _tile(n, want):
    t = min(n, want)
    assert n % t == 0, (n, want)
    return t


def kernel(x, w_in, b_gate, a_shift, a_w0, a_w2, a_a0, a_a2, a_g2, a_kk, a_ka, a_rk, a_lnx_g, a_lnx_b, c_lam_re, c_lam_im, c_log_dt, c_b_re, c_b_im, c_c_re, c_c_im, c_d, c_w_glu, c_b_glu, w_branch, w_out, ln1_g, ln1_b, w_ff1, w_ff2, ln2_g, ln2_b):
    bsz, seq, d = x.shape
    depth = w_in.shape[0]
    alpha = (2.0 * depth) ** 0.25
    tokens = bsz * seq
    rows = lambda a: a.reshape(depth, 1, -1)
    rwkv_params = _rwkv_params(dict(a_shift=a_shift, a_w0=a_w0, a_w2=a_w2, a_a0=a_a0, a_a2=a_a2, a_g2=a_g2,
                                    a_kk=a_kk, a_ka=a_ka, a_rk=a_rk, a_lnx_g=a_lnx_g, a_lnx_b=a_lnx_b))
    s5_params = _s5_params(dict(c_lam_re=c_lam_re, c_lam_im=c_lam_im, c_log_dt=c_log_dt, c_b_re=c_b_re,
                                c_b_im=c_b_im, c_c_re=c_c_re, c_c_im=c_c_im, c_d=c_d, c_w_glu=c_w_glu,
                                c_b_glu=c_b_glu))
    ret_tables = _retention_tables(seq)
    w_in_bf = w_in.astype(BF16)
    w_za, w_rest = w_in_bf[:, :, :A_PROJ], w_in_bf[:, :, A_PROJ:MIX_COLS]
    merge_consts = [w_in_bf[:, :, MIX_COLS:], rows(b_gate), w_branch.astype(BF16), w_out.astype(BF16),
                    rows(ln1_g), rows(ln1_b)]
    ffn_consts = [w_ff1.astype(BF16), w_ff2.astype(BF16), rows(ln2_g), rows(ln2_b)]
    xt = x.reshape(tokens, d)
    for l in range(depth):
        zmix, *stage_out = _proj(xt, w_za, w_rest, rwkv_params[0], bsz, seq, l, _tile(seq, 512))
        zmix = zmix.reshape(bsz, seq, REST_COLS)
        ya = _rwkv(stage_out, bsz, seq, rwkv_params[1], l, _tile(seq, 512))
        yb = _retention(zmix, bsz, seq, ret_tables, _tile(seq, 512))
        yc = _s5(zmix, bsz, seq, s5_params, l, _tile(seq, 64))
        x1 = _merge(xt, ya.reshape(tokens, -1), yb.reshape(tokens, -1), yc.reshape(tokens, -1),
                    merge_consts, l, alpha, _tile(tokens, 512))
        xt = _ffn(x1, ffn_consts, l, alpha, _tile(tokens, 1024), 1024)
    return xt.reshape(bsz, seq, d)
```

```python
import functools
import math

import jax
import jax.numpy as jnp
from jax import lax
from jax.experimental import pallas as pl
from jax.experimental.pallas import tpu as pltpu

F32 = jnp.float32
BF16 = jnp.bfloat16

A_HEADS = 8
A_HEAD_DIM = 64
A_WIDTH = A_HEADS * A_HEAD_DIM
A_DECAY_LORA = 64
A_ICLR_LORA = 64
A_GATE_LORA = 128
A_PROJ = 3 * A_WIDTH + A_DECAY_LORA + A_ICLR_LORA + A_GATE_LORA
A_GN_EPS = 64e-5
A_CHUNK = 64
A_GROUP = 8
A_GLANES = 128
A_SUMLANES = 256

B_HEADS = 4
B_QK_DIM = 128
B_V_DIM = 256
B_QK_WIDTH = B_HEADS * B_QK_DIM
B_V_WIDTH = B_HEADS * B_V_DIM
B_CHUNK = 128
B_ROPE_BASE = 10000.0
B_GN_EPS = 1e-5

C_WIDTH = 512
C_GROUP = 16
C_GROUPS = C_WIDTH // C_GROUP
C_STATE = 64
C_LANES = C_GROUPS * C_STATE
C_BLOCKS = 4
C_ULANES = C_WIDTH // C_BLOCKS
C_SLANES = C_LANES // C_BLOCKS
C_PARTS = 4

LN_EPS = 1e-5

RET_COLS = 2 * B_QK_WIDTH + 2 * B_V_WIDTH
REST_COLS = RET_COLS + C_WIDTH
MIX_COLS = A_PROJ + REST_COLS
U_BLK = RET_COLS // C_WIDTH
assert RET_COLS % C_WIDTH == 0

V7X_VMEM_LIMIT_BYTES = 56 * 1024 * 1024


def _cparams(*sem):
    return pltpu.CompilerParams(dimension_semantics=sem, vmem_limit_bytes=V7X_VMEM_LIMIT_BYTES)


def _full(shape):
    n = len(shape)
    return pl.BlockSpec(shape, lambda *_: (0,) * n)


def _layer_spec(a, l):
    n = a.ndim - 1
    return pl.BlockSpec((None,) + a.shape[1:], lambda *_: (l,) + (0,) * n, pipeline_mode=pl.Buffered(1))


def _dot(a, b):
    return jnp.dot(a.astype(BF16), b.astype(BF16), preferred_element_type=F32)


def _dot_nt(a, b):
    return lax.dot_general(a.astype(BF16), b.astype(BF16), (((1,), (1,)), ((), ())),
                           preferred_element_type=F32)


def _dot_tn(a, b):
    return lax.dot_general(a.astype(BF16), b.astype(BF16), (((0,), (0,)), ((), ())),
                           preferred_element_type=F32)


def _split3(x):
    hi = x.astype(BF16)
    r1 = x - hi.astype(F32)
    mid = r1.astype(BF16)
    lo = (r1 - mid.astype(F32)).astype(BF16)
    return hi, mid, lo


def _sigmoid(x):
    return 1.0 / (1.0 + jnp.exp(-x))


def _layer_norm(y, g, b):
    mu = jnp.mean(y, axis=-1, keepdims=True)
    d = y - mu
    var = jnp.mean(d * d, axis=-1, keepdims=True)
    return d * lax.rsqrt(var + LN_EPS) * g + b


def _proj_kernel(x_ref, wza_ref, wrest_ref, mu_ref, w0_ref, a0_ref, lora_ref, g2_ref, kkp_ref, kap_ref, rkp_ref,
                 ones_ref, cos_ref, sin_ref,
                 rest_ref, r_ref, k_ref, v_ref, kn_ref, b0_ref, lw_ref, gate_ref, bonus_ref, carry_s):
    tm = x_ref.shape[0]
    w = A_WIDTH

    @pl.when(pl.program_id(1) == 0)
    def _():
        carry_s[...] = jnp.zeros_like(carry_s)

    xb = x_ref[...].astype(BF16)
    z = jnp.dot(xb, wza_ref[...], preferred_element_type=F32)

    cos2 = cos_ref[...]
    sin2 = sin_ref[...]

    def rope(t):
        return t * cos2 + pltpu.roll(t, B_QK_DIM // 2, 1) * sin2

    def rest_cols(lo, hi):
        return jnp.dot(xb, wrest_ref[:, lo:hi], preferred_element_type=F32)

    k_off, v_off, g_off = B_QK_WIDTH, 2 * B_QK_WIDTH, 2 * B_QK_WIDTH + B_V_WIDTH
    q = rest_cols(0, k_off)
    kq = rest_cols(k_off, v_off)
    for h in range(B_HEADS):
        hs = slice(h * B_QK_DIM, (h + 1) * B_QK_DIM)
        rest_ref[:, hs] = rope(q[:, hs]).astype(rest_ref.dtype)
        rest_ref[:, k_off + h * B_QK_DIM:k_off + (h + 1) * B_QK_DIM] = (
            rope(kq[:, hs]) * (B_QK_DIM ** -0.5)).astype(rest_ref.dtype)
    rest_ref[:, v_off:g_off] = rest_cols(v_off, g_off).astype(rest_ref.dtype)
    g = rest_cols(g_off, RET_COLS)
    rest_ref[:, g_off:RET_COLS] = (g * _sigmoid(g)).astype(rest_ref.dtype)
    rest_ref[:, RET_COLS:] = rest_cols(RET_COLS, REST_COLS).astype(rest_ref.dtype)

    ones = ones_ref[...]
    slabs = [slice(g * A_SUMLANES, (g + 1) * A_SUMLANES) for g in range(w // A_SUMLANES)]

    def seg_sum(t):
        tb = t.astype(BF16)
        return jnp.concatenate([jnp.dot(tb[:, s], ones, preferred_element_type=F32) for s in slabs], axis=1)

    rolled = pltpu.roll(z, 1, 0)
    rowid = lax.broadcasted_iota(jnp.int32, z.shape, 0)
    prev = jnp.where(rowid == 0, jnp.broadcast_to(carry_s[0:1, :], z.shape), rolled)
    zs = z + mu_ref[...] * (prev - z)
    carry_s[0:1, :] = z[tm - 1:tm, :]

    lz = zs[:, 3 * w:3 * w + 128]
    lane = lax.broadcasted_iota(jnp.int32, lz.shape, 1)
    lin = jnp.where(lane < A_DECAY_LORA, jnp.tanh(lz), lz)
    wa = _dot(lin, lora_ref[...])
    lw_ref[...] = (-math.exp(-0.5)) * _sigmoid(w0_ref[...] + wa[:, :w])
    ia = _sigmoid(a0_ref[...] + wa[:, w:])
    gate_ref[...] = _dot(_sigmoid(zs[:, 3 * w + 128:3 * w + 256]), g2_ref[...]).astype(gate_ref.dtype)

    r = zs[:, :w]
    k = zs[:, w:2 * w]
    v = zs[:, 2 * w:3 * w]
    kk = k * kkp_ref[...]
    kn = kk * lax.rsqrt(jnp.maximum(seg_sum(kk * kk), 1e-24))
    kmod = k * (1.0 + (ia - 1.0) * kap_ref[...])
    r_ref[...] = r.astype(r_ref.dtype)
    k_ref[...] = kmod.astype(k_ref.dtype)
    v_ref[...] = v.astype(v_ref.dtype)
    kn_ref[...] = kn.astype(kn_ref.dtype)
    b0_ref[...] = (kn * ia).astype(b0_ref.dtype)
    bonus_ref[...] = (seg_sum(r * kmod * rkp_ref[...]) * v).astype(bonus_ref.dtype)


def _proj(x2d, w_za, w_rest, params, rope_tabs, bsz, seq, l, tm):
    layered, ones = params
    cos2, sin2 = rope_tabs
    k = x2d.shape[1]
    per_seq = seq // tm
    wide = lambda dt: jax.ShapeDtypeStruct((bsz * seq, A_WIDTH), dt)
    row_tile = lambda n: pl.BlockSpec((tm, n), lambda b, i: (b * per_seq + i, 0))
    consts = [w_za, w_rest] + layered
    return pl.pallas_call(
        _proj_kernel,
        grid=(bsz, per_seq),
        in_specs=[row_tile(k)] + [_layer_spec(a, l) for a in consts] + [_full(ones.shape)]
                 + [pl.BlockSpec((tm, B_QK_DIM), lambda b, i: (i, 0))] * 2,
        out_specs=[row_tile(REST_COLS)] + [row_tile(A_WIDTH)] * 8,
        out_shape=[jax.ShapeDtypeStruct((bsz * seq, REST_COLS), BF16),
                   wide(BF16), wide(BF16), wide(BF16), wide(BF16), wide(BF16), wide(F32), wide(BF16), wide(BF16)],
        scratch_shapes=[pltpu.VMEM((8, A_PROJ), F32)],
        compiler_params=_cparams("parallel", "arbitrary"),
        name="proj",
    )(x2d, *consts, ones, cos2, sin2)


def _chunk_rows(c):
    return pl.ds(c * A_CHUNK, A_CHUNK)


def _rwkv_kernel(r_ref, k_ref, v_ref, kn_ref, b0_ref, lw_ref, gate_ref, bonus_ref, lng_ref, lnb_ref,
                 ones_ref, tri_ref, y_ref, state_s, o_s, gam_s, x_s, m2_s, q_s, op_s):
    tc = r_ref.shape[0]
    n_chunks = tc // A_CHUNK
    w = A_WIDTH

    @pl.when(pl.program_id(1) == 0)
    def _():
        state_s[...] = jnp.zeros_like(state_s)

    ones = ones_ref[...]

    gw = A_GLANES
    groups = [slice(g * gw, (g + 1) * gw) for g in range(w // gw)]
    slabs = [slice(g * A_SUMLANES, (g + 1) * A_SUMLANES) for g in range(w // A_SUMLANES)]

    def seg_sum(x):
        xb = x.astype(BF16)
        return jnp.concatenate([jnp.dot(xb[:, s], ones, preferred_element_type=F32) for s in slabs], axis=1)

    tri = tri_ref[...]
    hpg = gw // A_HEAD_DIM
    rid = lax.broadcasted_iota(jnp.int32, (A_CHUNK, gw), 0)
    cid = lax.broadcasted_iota(jnp.int32, (A_CHUNK, gw), 1) % A_HEAD_DIM
    strict = rid > cid
    incl = rid >= cid
    eye = (rid == cid).astype(F32)
    brow = lax.broadcasted_iota(jnp.int32, (gw, gw), 0) // A_HEAD_DIM
    bcol = lax.broadcasted_iota(jnp.int32, (gw, gw), 1) // A_HEAD_DIM
    same_head = brow == bcol
    same_head_bf = same_head.astype(BF16)
    n = A_HEAD_DIM

    def bd(x):
        xb = x.astype(BF16)
        return jnp.concatenate([xb] * hpg, axis=0) * same_head_bf


    group = min(A_GROUP, n_chunks)
    assert n_chunks % group == 0

    def state_free_part(gi):
        items = []
        for j in range(group):
            c = gi * group + j
            rows = _chunk_rows(c)
            lw = lw_ref[rows, :]
            h3 = _split3(lw)
            cum = (jnp.dot(tri, h3[0], preferred_element_type=F32)
                   + jnp.dot(tri, h3[1], preferred_element_type=F32)
                   + jnp.dot(tri, h3[2], preferred_element_type=F32))
            e_in = jnp.exp(cum)
            e_ex = jnp.exp(cum - lw)
            e_ng = jnp.exp(-cum)
            rt = r_ref[rows, :].astype(F32) * e_in
            at = -kn_ref[rows, :].astype(F32) * e_ex
            bt = b0_ref[rows, :].astype(F32) * e_ng
            kt = k_ref[rows, :].astype(F32) * e_ng
            vv = v_ref[rows, :]
            gam_s[pl.ds(c, 1), :] = e_in[A_CHUNK - 1:A_CHUNK, :]
            for g, s in enumerate(groups):
                items.append((c * len(groups) + g, at[:, s], rt[:, s], bt[:, s], kt[:, s], vv[:, s]))
        ids = range(len(items))
        idx = [it[0] for it in items]
        a_ = [it[1] for it in items]
        r_ = [it[2] for it in items]
        b_ = [it[3] for it in items]
        k_ = [it[4] for it in items]
        v_ = [it[5] for it in items]
        ar = [jnp.concatenate([a_[i], r_[i]], axis=0) for i in ids]
        gb = [_dot_nt(ar[i], bd(b_[i])) for i in ids]
        gk = [_dot_nt(ar[i], bd(k_[i])) for i in ids]
        l_ab = [jnp.where(strict, gb[i][:A_CHUNK], 0.0) for i in ids]
        a_qb = [jnp.where(incl, gb[i][A_CHUNK:], 0.0) for i in ids]
        akq = [jnp.concatenate([jnp.where(strict, gk[i][:A_CHUNK], 0.0),
                                jnp.where(incl, gk[i][A_CHUNK:], 0.0)], axis=0) for i in ids]
        akqv = [_dot(akq[i], bd(v_[i])) for i in ids]
        akv = [akqv[i][:A_CHUNK] for i in ids]
        ov = [akqv[i][A_CHUNK:] for i in ids]
        tinv = [eye + l_ab[i] for i in ids]
        p = [_dot(l_ab[i], bd(l_ab[i])) for i in ids]
        for _ in range(4):
            tp = [_dot(jnp.concatenate([tinv[i], p[i]], axis=0), bd(p[i])) for i in ids]
            tinv = [tinv[i] + tp[i][:A_CHUNK] for i in ids]
            p = [tp[i][A_CHUNK:] for i in ids]
        tinv = [tinv[i] + _dot(tinv[i], bd(p[i])) for i in ids]
        wu = [_dot(tinv[i], jnp.concatenate([bd(a_[i]), bd(akv[i])], axis=1)) for i in ids]
        wm = [wu[i][:, :gw] for i in ids]
        uv = [wu[i][:, gw:] for i in ids]
        for i in ids:
            x_s[idx[i]] = jnp.where(same_head, _dot_tn(wm[i], b_[i]), 0.0)
        for i in ids:
            m2 = jnp.where(same_head, _dot_tn(jnp.concatenate([uv[i], v_[i]], axis=0),
                                              jnp.concatenate([b_[i], k_[i]], axis=0)), 0.0)
            m2_s[idx[i]] = sum(m2[h * n:(h + 1) * n] for h in range(1, hpg)) + m2[:n]
        qo = [_dot(a_qb[i], jnp.concatenate([bd(wm[i]), bd(uv[i])], axis=1)) for i in ids]
        for i in ids:
            q_s[idx[i]] = r_[i] + qo[i][:, :gw]
        for i in ids:
            op_s[idx[i]] = qo[i][:, gw:] + ov[i]

    def state_part(c):
        rows = _chunk_rows(c)
        gam = gam_s[pl.ds(c, 1), :]
        gids = range(len(groups))
        s0 = [state_s[g] for g in gids]
        sx = [_dot(s0[g], x_s[c * len(groups) + g]) for g in gids]
        o = [_dot_nt(q_s[c * len(groups) + g], bd(s0[g])) + op_s[c * len(groups) + g] for g in gids]
        for g in gids:
            state_s[g] = (s0[g] + sx[g] + m2_s[c * len(groups) + g]) * gam[:, groups[g]]
        o_s[rows, :] = jnp.concatenate(o, axis=1)

    for gi in range(n_chunks // group):
        state_free_part(gi)
    for c in range(n_chunks):
        state_part(c)

    o = o_s[...]
    inv_n = 1.0 / n
    mean = seg_sum(o) * inv_n
    d = o - mean
    var = seg_sum(d * d) * inv_n
    on = d * lax.rsqrt(var + A_GN_EPS) * lng_ref[...] + lnb_ref[...]
    on = on + bonus_ref[...].astype(F32)
    y_ref[...] = on * gate_ref[...].astype(F32)


def _rwkv_params(p):
    depth = p["a_w0"].shape[0]
    row = lambda a: a.reshape(depth, 1, -1).astype(F32)
    w = A_WIDTH
    lora = jnp.zeros((depth, 128, 2 * w), F32)
    lora = lora.at[:, :A_DECAY_LORA, :w].set(p["a_w2"]).at[:, A_DECAY_LORA:, w:].set(p["a_a2"]).astype(BF16)
    hid = jnp.arange(A_SUMLANES) // A_HEAD_DIM
    ones = (hid[:, None] == hid[None, :]).astype(BF16)
    ti = jnp.arange(A_CHUNK)
    tri = (ti[:, None] >= ti[None, :]).astype(BF16)
    stage = [row(p["a_shift"]), row(p["a_w0"]), row(p["a_a0"]), lora, p["a_g2"].astype(BF16),
             row(p["a_kk"]), row(p["a_ka"]), row(p["a_rk"])]
    return (stage, ones), ([row(p["a_lnx_g"]), row(p["a_lnx_b"])], [ones, tri])


def _rwkv(stage_out, bsz, seq, params, l, tc):
    layered, shared = params
    w = A_WIDTH
    n_chunks = tc // A_CHUNK
    n_groups = A_WIDTH // A_GLANES
    per_seq = seq // tc
    per_sq = pltpu.VMEM((n_chunks * n_groups, A_GLANES, A_GLANES), F32)
    per_row = pltpu.VMEM((n_chunks * n_groups, A_CHUNK, A_GLANES), F32)
    row_tile = pl.BlockSpec((tc, w), lambda b, i: (b * per_seq + i, 0))
    return pl.pallas_call(
        _rwkv_kernel,
        grid=(bsz, per_seq),
        in_specs=[row_tile] * len(stage_out)
                 + [_layer_spec(a, l) for a in layered] + [_full(a.shape) for a in shared],
        out_specs=row_tile,
        out_shape=jax.ShapeDtypeStruct((bsz * seq, w), F32),
        scratch_shapes=[pltpu.VMEM((n_groups, A_HEAD_DIM, A_GLANES), F32),
                        pltpu.VMEM((tc, w), F32),
                        pltpu.VMEM((max(8, n_chunks), w), F32),
                        per_sq, per_row, per_row, per_row],
        compiler_params=_cparams("parallel", "arbitrary"),
        name="rwkv7",
    )(*stage_out, *layered, *shared)


def _ret_kernel(z_ref, dmask_ref, qd_ref, kd_ref, cd_ref, y_ref, state_s, upd_s, st_s):
    tc = z_ref.shape[0]
    n_chunks = tc // B_CHUNK
    k_off, v_off, g_off = B_QK_WIDTH, 2 * B_QK_WIDTH, 2 * B_QK_WIDTH + B_V_WIDTH

    @pl.when(pl.program_id(1) == 0)
    def _():
        state_s[...] = jnp.zeros_like(state_s)

    items = [(c, h) for c in range(n_chunks) for h in range(B_HEADS)]
    rows = lambda c: slice(c * B_CHUNK, (c + 1) * B_CHUNK)
    qcol = lambda h: slice(h * B_QK_DIM, (h + 1) * B_QK_DIM)
    kcol = lambda h: slice(k_off + h * B_QK_DIM, k_off + (h + 1) * B_QK_DIM)
    vcol = lambda h: slice(v_off + h * B_V_DIM, v_off + (h + 1) * B_V_DIM)
    gcol = lambda h: slice(g_off + h * B_V_DIM, g_off + (h + 1) * B_V_DIM)

    for c, h in items:
        upd_s[c * B_HEADS + h] = _dot_tn(z_ref[rows(c), kcol(h)] * kd_ref[h], z_ref[rows(c), vcol(h)])
    for h in range(B_HEADS):
        st = state_s[h]
        for c in range(n_chunks):
            st_s[c * B_HEADS + h] = st
            st = cd_ref[h] * st + upd_s[c * B_HEADS + h]
        state_s[h] = st

    for c, h in items:
        qc = z_ref[rows(c), qcol(h)]
        scores = _dot_nt(qc, z_ref[rows(c), kcol(h)]) * dmask_ref[h]
        o = _dot(scores, z_ref[rows(c), vcol(h)]) + _dot(qc * qd_ref[h], st_s[c * B_HEADS + h])
        mu = jnp.mean(o, axis=-1, keepdims=True)
        d = o - mu
        var = jnp.mean(d * d, axis=-1, keepdims=True)
        on = d * lax.rsqrt(var + B_GN_EPS)
        y_ref[rows(c), h * B_V_DIM:(h + 1) * B_V_DIM] = z_ref[rows(c), gcol(h)].astype(F32) * on


def _retention_tables(seq):
    f32 = F32
    pos = jnp.arange(seq, dtype=f32)
    half = B_QK_DIM // 2
    inv_freq = B_ROPE_BASE ** (-jnp.arange(half, dtype=f32) / half)
    ang = pos[:, None] * inv_freq[None, :]
    cos, sin = jnp.cos(ang), jnp.sin(ang)
    cos2 = jnp.concatenate([cos, cos], axis=1)
    sin2 = jnp.concatenate([-sin, sin], axis=1)
    log_gamma = jnp.log(1.0 - 2.0 ** (-5.0 - jnp.arange(B_HEADS, dtype=f32)))
    idx = jnp.arange(B_CHUNK, dtype=f32)
    rel = idx[:, None] - idx[None, :]
    dmask = jnp.where(rel >= 0, jnp.exp(log_gamma[:, None, None] * jnp.maximum(rel, 0.0)), 0.0)
    qd = jnp.broadcast_to(jnp.exp(log_gamma[:, None] * (idx + 1.0))[:, :, None],
                          (B_HEADS, B_CHUNK, B_QK_DIM))
    kd = jnp.broadcast_to(jnp.exp(log_gamma[:, None] * (B_CHUNK - 1.0 - idx))[:, :, None],
                          (B_HEADS, B_CHUNK, B_QK_DIM))
    cd = jnp.broadcast_to(jnp.exp(log_gamma * B_CHUNK)[:, None, None], (B_HEADS, B_QK_DIM, B_V_DIM))
    return (cos2, sin2), [dmask, qd.astype(BF16), kd.astype(BF16), cd]


def _retention(zmix, bsz, seq, tabs, tc):
    n_items = (tc // B_CHUNK) * B_HEADS
    per_item = pltpu.VMEM((n_items, B_QK_DIM, B_V_DIM), F32)
    return pl.pallas_call(
        _ret_kernel,
        grid=(bsz, seq // tc),
        in_specs=[pl.BlockSpec((None, tc, RET_COLS), lambda b, i: (b, i, 0))] + [_full(a.shape) for a in tabs],
        out_specs=pl.BlockSpec((None, tc, B_V_WIDTH), lambda b, i: (b, i, 0)),
        out_shape=jax.ShapeDtypeStruct((bsz, seq, B_V_WIDTH), F32),
        scratch_shapes=[pltpu.VMEM((B_HEADS, B_QK_DIM, B_V_DIM), F32), per_item, per_item],
        compiler_params=_cparams("parallel", "arbitrary"),
        name="retention",
    )(zmix, *tabs)


def _s5_kernel(u_ref, bbr_ref, bbi_ref, ar_ref, ai_ref, cr_ref, ci_ref, d_ref, wg_ref, bg_ref,
               y_ref, xr_s, xi_s, u_s, sr_s, si_s):
    bsz, tt, _ = u_ref.shape
    tp = tt // C_PARTS
    rp = tp * bsz

    @pl.when(pl.program_id(0) == 0)
    def _():
        sr_s[...] = jnp.zeros_like(sr_s)
        si_s[...] = jnp.zeros_like(si_s)

    blk = [slice(m * C_SLANES, (m + 1) * C_SLANES) for m in range(C_BLOCKS)]


    def drive(p):
        rows = slice(p * rp, (p + 1) * rp)
        u = pltpu.einshape("btc->tbc", u_ref[:, p * tp:(p + 1) * tp, :].astype(F32)).reshape(rp, C_WIDTH)
        u_s[rows, :] = u
        ub = u.astype(BF16)
        for m in range(C_BLOCKS):
            um = ub[:, m * C_ULANES:(m + 1) * C_ULANES]
            xr_s[rows, blk[m]] = jnp.dot(um, bbr_ref[m], preferred_element_type=F32)
            xi_s[rows, blk[m]] = jnp.dot(um, bbi_ref[m], preferred_element_type=F32)

    def scan(p):
        for m in range(C_BLOCKS):
            cols = blk[m]
            ar = jnp.broadcast_to(ar_ref[:, cols], (bsz, C_SLANES))
            ai = jnp.broadcast_to(ai_ref[:, cols], (bsz, C_SLANES))
            xr, xi = sr_s[:, cols], si_s[:, cols]
            for t in range(p * tp, (p + 1) * tp):
                rows = slice(t * bsz, (t + 1) * bsz)
                xr, xi = (ar * xr - ai * xi + xr_s[rows, cols], ar * xi + ai * xr + xi_s[rows, cols])
                xr_s[rows, cols] = xr
                xi_s[rows, cols] = xi
            sr_s[:, cols] = xr
            si_s[:, cols] = xi

    def readout(p):
        rows = slice(p * rp, (p + 1) * rp)
        parts = [_dot(xr_s[rows, blk[m]], cr_ref[m]) - _dot(xi_s[rows, blk[m]], ci_ref[m])
                 for m in range(C_BLOCKS)]
        y = jnp.concatenate(parts, axis=1) + d_ref[...] * u_s[rows, :]
        y = jax.nn.gelu(y)
        y = y * _sigmoid(_dot(y, wg_ref[...]) + bg_ref[...])
        y_ref[:, p * tp:(p + 1) * tp, :] = pltpu.einshape("tbc->btc", y.reshape(tp, bsz, C_WIDTH))

    drive(0)
    for p in range(C_PARTS):
        if p + 1 < C_PARTS:
            drive(p + 1)
        scan(p)
        readout(p)


def _s5_params(p):
    f32 = F32
    depth = p["c_log_dt"].shape[0]
    dt = jnp.exp(p["c_log_dt"].astype(f32))[..., None]
    lr, li = p["c_lam_re"].astype(f32), p["c_lam_im"].astype(f32)
    mag = jnp.exp(lr * dt)
    ab_re, ab_im = mag * jnp.cos(li * dt), mag * jnp.sin(li * dt)
    den = lr * lr + li * li
    f_re = ((ab_re - 1.0) * lr + ab_im * li) / den
    f_im = (ab_im * lr - (ab_re - 1.0) * li) / den
    bre, bim = p["c_b_re"].astype(f32), p["c_b_im"].astype(f32)
    bb_re = f_re[..., None] * bre - f_im[..., None] * bim
    bb_im = f_re[..., None] * bim + f_im[..., None] * bre
    gpb = C_GROUPS // C_BLOCKS
    eye = jnp.eye(gpb, dtype=f32)

    def in_blocks(bb):
        bb = bb.reshape(depth, C_BLOCKS, gpb, C_STATE, C_GROUP)
        return jnp.einsum("lmgpc,gh->lmgchp", bb, eye).reshape(depth, C_BLOCKS, C_ULANES, C_SLANES).astype(BF16)

    def out_blocks(cc):
        cc = cc.astype(f32).reshape(depth, C_BLOCKS, gpb, C_GROUP, C_STATE)
        return jnp.einsum("lmgcp,gh->lmgphc", cc, eye).reshape(depth, C_BLOCKS, C_SLANES, C_ULANES).astype(BF16)

    return [in_blocks(bb_re), in_blocks(bb_im), ab_re.reshape(depth, 1, C_LANES), ab_im.reshape(depth, 1, C_LANES),
            out_blocks(p["c_c_re"]), out_blocks(p["c_c_im"]), p["c_d"].reshape(depth, 1, C_WIDTH).astype(f32),
            p["c_w_glu"].astype(BF16), p["c_b_glu"].reshape(depth, 1, C_WIDTH).astype(f32)]


def _s5(zmix, bsz, seq, consts, l, tt):
    rows = tt * bsz
    return pl.pallas_call(
        _s5_kernel,
        grid=(seq // tt,),
        in_specs=[pl.BlockSpec((bsz, tt, C_WIDTH), lambda i: (0, i, U_BLK))] + [_layer_spec(a, l) for a in consts],
        out_specs=pl.BlockSpec((bsz, tt, C_WIDTH), lambda i: (0, i, 0)),
        out_shape=jax.ShapeDtypeStruct((bsz, seq, C_WIDTH), F32),
        scratch_shapes=[pltpu.VMEM((rows, C_LANES), F32), pltpu.VMEM((rows, C_LANES), F32),
                        pltpu.VMEM((rows, C_WIDTH), F32),
                        pltpu.VMEM((bsz, C_LANES), F32), pltpu.VMEM((bsz, C_LANES), F32)],
        compiler_params=_cparams("arbitrary"),
        name="s5",
    )(zmix, *consts)


def _merge_kernel(alpha, x_ref, ya_ref, yb_ref, yc_ref, wgate_ref, bgate_ref, wb_ref,
                  wout_ref, g_ref, b_ref, o_ref):
    d = x_ref.shape[1]
    x = x_ref[...]
    gates = _sigmoid(_dot(x, wgate_ref[...]) + bgate_ref[...])
    b_lo, c_lo = A_WIDTH, A_WIDTH + B_V_WIDTH
    merged = (gates[:, :d] * _dot(ya_ref[...], wb_ref[:b_lo, :])
              + gates[:, d:2 * d] * _dot(yb_ref[...], wb_ref[b_lo:c_lo, :])
              + gates[:, 2 * d:] * _dot(yc_ref[...], wb_ref[c_lo:, :]))
    o_ref[...] = _layer_norm(alpha * x + _dot(merged, wout_ref[...]), g_ref[...], b_ref[...])


def _merge(x2d, ya, yb, yc, consts, l, alpha, tm):
    m, d = x2d.shape
    tile = lambda n: pl.BlockSpec((tm, n), lambda i: (i, 0))
    return pl.pallas_call(
        functools.partial(_merge_kernel, alpha),
        grid=(m // tm,),
        in_specs=[tile(d), tile(A_WIDTH), tile(B_V_WIDTH), tile(C_WIDTH)] + [_layer_spec(a, l) for a in consts],
        out_specs=tile(d),
        out_shape=jax.ShapeDtypeStruct((m, d), F32),
        compiler_params=_cparams("parallel"),
        name="merge",
    )(x2d, ya, yb, yc, *consts)


def _ffn_kernel(alpha, tf, x_ref, w1_ref, w2_ref, g_ref, b_ref, o_ref):
    x = x_ref[...]
    xb = x.astype(BF16)
    acc = None
    for j in range(w1_ref.shape[1] // tf):
        h = jnp.maximum(jnp.dot(xb, w1_ref[:, j * tf:(j + 1) * tf], preferred_element_type=F32), 0.0)
        part = jnp.dot((h * h).astype(BF16), w2_ref[j * tf:(j + 1) * tf, :], preferred_element_type=F32)
        acc = part if acc is None else acc + part
    o_ref[...] = _layer_norm(alpha * x + acc, g_ref[...], b_ref[...])


def _ffn(x2d, consts, l, alpha, tm, tf):
    m, d = x2d.shape
    return pl.pallas_call(
        functools.partial(_ffn_kernel, alpha, tf),
        grid=(m // tm,),
        in_specs=[pl.BlockSpec((tm, d), lambda i: (i, 0))] + [_layer_spec(a, l) for a in consts],
        out_specs=pl.BlockSpec((tm, d), lambda i: (i, 0)),
        out_shape=jax.ShapeDtypeStruct((m, d), F32),
        compiler_params=_cparams("parallel"),
        name="ffn",
    )(x2d, *consts)


def _tile(n, want):
    t = min(n, want)
    assert n % t == 0, (n, want)
    return t


def kernel(x, w_in, b_gate, a_shift, a_w0, a_w2, a_a0, a_a2, a_g2, a_kk, a_ka, a_rk, a_lnx_g, a_lnx_b, c_lam_re, c_lam_im, c_log_dt, c_b_re, c_b_im, c_c_re, c_c_im, c_d, c_w_glu, c_b_glu, w_branch, w_out, ln1_g, ln1_b, w_ff1, w_ff2, ln2_g, ln2_b):
    bsz, seq, d = x.shape
    depth = w_in.shape[0]
    alpha = (2.0 * depth) ** 0.25
    tokens = bsz * seq
    rows = lambda a: a.reshape(depth, 1, -1)
    rwkv_params = _rwkv_params(dict(a_shift=a_shift, a_w0=a_w0, a_w2=a_w2, a_a0=a_a0, a_a2=a_a2, a_g2=a_g2,
                                    a_kk=a_kk, a_ka=a_ka, a_rk=a_rk, a_lnx_g=a_lnx_g, a_lnx_b=a_lnx_b))
    s5_params = _s5_params(dict(c_lam_re=c_lam_re, c_lam_im=c_lam_im, c_log_dt=c_log_dt, c_b_re=c_b_re,
                                c_b_im=c_b_im, c_c_re=c_c_re, c_c_im=c_c_im, c_d=c_d, c_w_glu=c_w_glu,
                                c_b_glu=c_b_glu))
    rope_tabs, ret_tables = _retention_tables(seq)
    w_in_bf = w_in.astype(BF16)
    w_za, w_rest = w_in_bf[:, :, :A_PROJ], w_in_bf[:, :, A_PROJ:MIX_COLS]
    merge_consts = [w_in_bf[:, :, MIX_COLS:], rows(b_gate), w_branch.astype(BF16), w_out.astype(BF16),
                    rows(ln1_g), rows(ln1_b)]
    ffn_consts = [w_ff1.astype(BF16), w_ff2.astype(BF16), rows(ln2_g), rows(ln2_b)]
    xt = x.reshape(tokens, d)
    for l in range(depth):
        zmix, *stage_out = _proj(xt, w_za, w_rest, rwkv_params[0], rope_tabs, bsz, seq, l, _tile(seq, 512))
        zmix = zmix.reshape(bsz, seq, REST_COLS)
        ya = _rwkv(stage_out, bsz, seq, rwkv_params[1], l, _tile(seq, 512))
        yb = _retention(zmix, bsz, seq, ret_tables, _tile(seq, 512))
        yc = _s5(zmix, bsz, seq, s5_params, l, _tile(seq, 64))
        x1 = _merge(xt, ya.reshape(tokens, -1), yb.reshape(tokens, -1), yc.reshape(tokens, -1),
                    merge_consts, l, alpha, _tile(tokens, 512))
        xt = _ffn(x1, ffn_consts, l, alpha, _tile(tokens, 1024), 1024)
    return xt.reshape(bsz, seq, d)
```

```python
import functools
import math

import jax
import jax.numpy as jnp
from jax import lax
from jax.experimental import pallas as pl
from jax.experimental.pallas import tpu as pltpu

F32 = jnp.float32
BF16 = jnp.bfloat16

A_HEADS = 8
A_HEAD_DIM = 64
A_WIDTH = A_HEADS * A_HEAD_DIM
A_DECAY_LORA = 64
A_ICLR_LORA = 64
A_GATE_LORA = 128
A_PROJ = 3 * A_WIDTH + A_DECAY_LORA + A_ICLR_LORA + A_GATE_LORA
A_GN_EPS = 64e-5
A_CHUNK = 64
A_GROUP = 8
A_GLANES = 128
A_SUMLANES = 256

B_HEADS = 4
B_QK_DIM = 128
B_V_DIM = 256
B_QK_WIDTH = B_HEADS * B_QK_DIM
B_V_WIDTH = B_HEADS * B_V_DIM
B_CHUNK = 128
B_ROPE_BASE = 10000.0
B_GN_EPS = 1e-5

C_WIDTH = 512
C_GROUP = 16
C_GROUPS = C_WIDTH // C_GROUP
C_STATE = 64
C_LANES = C_GROUPS * C_STATE
C_BLOCKS = 4
C_ULANES = C_WIDTH // C_BLOCKS
C_SLANES = C_LANES // C_BLOCKS
C_PARTS = 4

LN_EPS = 1e-5

RET_COLS = 2 * B_QK_WIDTH + 2 * B_V_WIDTH
REST_COLS = RET_COLS + C_WIDTH
MIX_COLS = A_PROJ + REST_COLS
U_BLK = RET_COLS // C_WIDTH
assert RET_COLS % C_WIDTH == 0

V7X_VMEM_LIMIT_BYTES = 56 * 1024 * 1024


def _cparams(*sem):
    return pltpu.CompilerParams(dimension_semantics=sem, vmem_limit_bytes=V7X_VMEM_LIMIT_BYTES)


def _full(shape):
    n = len(shape)
    return pl.BlockSpec(shape, lambda *_: (0,) * n)


def _layer_spec(a, l):
    n = a.ndim - 1
    return pl.BlockSpec((None,) + a.shape[1:], lambda *_: (l,) + (0,) * n, pipeline_mode=pl.Buffered(1))


def _dot(a, b):
    return jnp.dot(a.astype(BF16), b.astype(BF16), preferred_element_type=F32)


def _dot_nt(a, b):
    return lax.dot_general(a.astype(BF16), b.astype(BF16), (((1,), (1,)), ((), ())),
                           preferred_element_type=F32)


def _dot_tn(a, b):
    return lax.dot_general(a.astype(BF16), b.astype(BF16), (((0,), (0,)), ((), ())),
                           preferred_element_type=F32)


def _split3(x):
    hi = x.astype(BF16)
    r1 = x - hi.astype(F32)
    mid = r1.astype(BF16)
    lo = (r1 - mid.astype(F32)).astype(BF16)
    return hi, mid, lo


def _sigmoid(x):
    return 1.0 / (1.0 + jnp.exp(-x))


def _layer_norm(y, g, b):
    mu = jnp.mean(y, axis=-1, keepdims=True)
    d = y - mu
    var = jnp.mean(d * d, axis=-1, keepdims=True)
    return d * lax.rsqrt(var + LN_EPS) * g + b


def _proj_kernel(x_ref, wza_ref, wrest_ref, mu_ref, w0_ref, a0_ref, lora_ref, g2_ref, kkp_ref, kap_ref, rkp_ref,
                 ones_ref, cos_ref, sin_ref,
                 rest_ref, r_ref, k_ref, v_ref, kn_ref, b0_ref, lw_ref, gate_ref, bonus_ref, carry_s):
    tm = x_ref.shape[0]
    w = A_WIDTH

    @pl.when(pl.program_id(1) == 0)
    def _():
        carry_s[...] = jnp.zeros_like(carry_s)

    xb = x_ref[...].astype(BF16)
    z = jnp.dot(xb, wza_ref[...], preferred_element_type=F32)

    cos2 = cos_ref[...]
    sin2 = sin_ref[...]

    def rope(t):
        return t * cos2 + pltpu.roll(t, B_QK_DIM // 2, 1) * sin2

    def rest_cols(lo, hi):
        return jnp.dot(xb, wrest_ref[:, lo:hi], preferred_element_type=F32)

    k_off, v_off, g_off = B_QK_WIDTH, 2 * B_QK_WIDTH, 2 * B_QK_WIDTH + B_V_WIDTH
    q = rest_cols(0, k_off)
    kq = rest_cols(k_off, v_off)
    for h in range(B_HEADS):
        hs = slice(h * B_QK_DIM, (h + 1) * B_QK_DIM)
        rest_ref[:, hs] = rope(q[:, hs]).astype(rest_ref.dtype)
        rest_ref[:, k_off + h * B_QK_DIM:k_off + (h + 1) * B_QK_DIM] = (
            rope(kq[:, hs]) * (B_QK_DIM ** -0.5)).astype(rest_ref.dtype)
    rest_ref[:, v_off:g_off] = rest_cols(v_off, g_off).astype(rest_ref.dtype)
    g = rest_cols(g_off, RET_COLS)
    rest_ref[:, g_off:RET_COLS] = (g * _sigmoid(g)).astype(rest_ref.dtype)
    rest_ref[:, RET_COLS:] = rest_cols(RET_COLS, REST_COLS).astype(rest_ref.dtype)

    ones = ones_ref[...]
    slabs = [slice(g * A_SUMLANES, (g + 1) * A_SUMLANES) for g in range(w // A_SUMLANES)]

    def seg_sum(t):
        tb = t.astype(BF16)
        return jnp.concatenate([jnp.dot(tb[:, s], ones, preferred_element_type=F32) for s in slabs], axis=1)

    rolled = pltpu.roll(z, 1, 0)
    rowid = lax.broadcasted_iota(jnp.int32, z.shape, 0)
    prev = jnp.where(rowid == 0, jnp.broadcast_to(carry_s[0:1, :], z.shape), rolled)
    zs = z + mu_ref[...] * (prev - z)
    carry_s[0:1, :] = z[tm - 1:tm, :]

    lz = zs[:, 3 * w:3 * w + 128]
    lane = lax.broadcasted_iota(jnp.int32, lz.shape, 1)
    lin = jnp.where(lane < A_DECAY_LORA, jnp.tanh(lz), lz)
    wa = _dot(lin, lora_ref[...])
    lw_ref[...] = (-math.exp(-0.5)) * _sigmoid(w0_ref[...] + wa[:, :w])
    ia = _sigmoid(a0_ref[...] + wa[:, w:])
    gate_ref[...] = _dot(_sigmoid(zs[:, 3 * w + 128:3 * w + 256]), g2_ref[...]).astype(gate_ref.dtype)

    r = zs[:, :w]
    k = zs[:, w:2 * w]
    v = zs[:, 2 * w:3 * w]
    kk = k * kkp_ref[...]
    kn = kk * lax.rsqrt(jnp.maximum(seg_sum(kk * kk), 1e-24))
    kmod = k * (1.0 + (ia - 1.0) * kap_ref[...])
    r_ref[...] = r.astype(r_ref.dtype)
    k_ref[...] = kmod.astype(k_ref.dtype)
    v_ref[...] = v.astype(v_ref.dtype)
    kn_ref[...] = kn.astype(kn_ref.dtype)
    b0_ref[...] = (kn * ia).astype(b0_ref.dtype)
    bonus_ref[...] = (seg_sum(r * kmod * rkp_ref[...]) * v).astype(bonus_ref.dtype)


def _proj(x2d, w_za, w_rest, params, rope_tabs, bsz, seq, l, tm):
    layered, ones = params
    cos2, sin2 = rope_tabs
    k = x2d.shape[1]
    per_seq = seq // tm
    wide = lambda dt: jax.ShapeDtypeStruct((bsz * seq, A_WIDTH), dt)
    row_tile = lambda n: pl.BlockSpec((tm, n), lambda b, i: (b * per_seq + i, 0))
    consts = [w_za, w_rest] + layered
    return pl.pallas_call(
        _proj_kernel,
        grid=(bsz, per_seq),
        in_specs=[row_tile(k)] + [_layer_spec(a, l) for a in consts] + [_full(ones.shape)]
                 + [pl.BlockSpec((tm, B_QK_DIM), lambda b, i: (i, 0))] * 2,
        out_specs=[row_tile(REST_COLS)] + [row_tile(A_WIDTH)] * 8,
        out_shape=[jax.ShapeDtypeStruct((bsz * seq, REST_COLS), BF16),
                   wide(BF16), wide(BF16), wide(BF16), wide(BF16), wide(BF16), wide(F32), wide(BF16), wide(BF16)],
        scratch_shapes=[pltpu.VMEM((8, A_PROJ), F32)],
        compiler_params=_cparams("parallel", "arbitrary"),
        name="proj",
    )(x2d, *consts, ones, cos2, sin2)


def _chunk_rows(c):
    return pl.ds(c * A_CHUNK, A_CHUNK)


def _rwkv_kernel(r_ref, k_ref, v_ref, kn_ref, b0_ref, lw_ref, gate_ref, bonus_ref, lng_ref, lnb_ref,
                 ones_ref, tri_ref, y_ref, state_s, o_s, gam_s, x_s, m2_s, q_s, op_s):
    tc = r_ref.shape[0]
    n_chunks = tc // A_CHUNK
    w = A_WIDTH

    @pl.when(pl.program_id(1) == 0)
    def _():
        state_s[...] = jnp.zeros_like(state_s)

    ones = ones_ref[...]

    gw = A_GLANES
    groups = [slice(g * gw, (g + 1) * gw) for g in range(w // gw)]
    slabs = [slice(g * A_SUMLANES, (g + 1) * A_SUMLANES) for g in range(w // A_SUMLANES)]

    def seg_sum(x):
        xb = x.astype(BF16)
        return jnp.concatenate([jnp.dot(xb[:, s], ones, preferred_element_type=F32) for s in slabs], axis=1)

    tri = tri_ref[...]
    hpg = gw // A_HEAD_DIM
    rid = lax.broadcasted_iota(jnp.int32, (A_CHUNK, gw), 0)
    cid = lax.broadcasted_iota(jnp.int32, (A_CHUNK, gw), 1) % A_HEAD_DIM
    strict = rid > cid
    incl = rid >= cid
    eye = (rid == cid).astype(F32)
    brow = lax.broadcasted_iota(jnp.int32, (gw, gw), 0) // A_HEAD_DIM
    bcol = lax.broadcasted_iota(jnp.int32, (gw, gw), 1) // A_HEAD_DIM
    same_head = brow == bcol
    same_head_bf = same_head.astype(BF16)
    n = A_HEAD_DIM

    def bd(x):
        xb = x.astype(BF16)
        return jnp.concatenate([xb] * hpg, axis=0) * same_head_bf


    group = min(A_GROUP, n_chunks)
    assert n_chunks % group == 0

    def state_free_part(gi):
        items = []
        for j in range(group):
            c = gi * group + j
            rows = _chunk_rows(c)
            lw = lw_ref[rows, :]
            h3 = _split3(lw)
            cum = (jnp.dot(tri, h3[0], preferred_element_type=F32)
                   + jnp.dot(tri, h3[1], preferred_element_type=F32)
                   + jnp.dot(tri, h3[2], preferred_element_type=F32))
            e_in = jnp.exp(cum)
            e_ex = jnp.exp(cum - lw)
            e_ng = jnp.exp(-cum)
            rt = r_ref[rows, :].astype(F32) * e_in
            at = -kn_ref[rows, :].astype(F32) * e_ex
            bt = b0_ref[rows, :].astype(F32) * e_ng
            kt = k_ref[rows, :].astype(F32) * e_ng
            vv = v_ref[rows, :]
            gam_s[pl.ds(c, 1), :] = e_in[A_CHUNK - 1:A_CHUNK, :]
            for g, s in enumerate(groups):
                items.append((c * len(groups) + g, at[:, s], rt[:, s], bt[:, s], kt[:, s], vv[:, s]))
        ids = range(len(items))
        idx = [it[0] for it in items]
        a_ = [it[1] for it in items]
        r_ = [it[2] for it in items]
        b_ = [it[3] for it in items]
        k_ = [it[4] for it in items]
        v_ = [it[5] for it in items]
        ar = [jnp.concatenate([a_[i], r_[i]], axis=0) for i in ids]
        gb = [_dot_nt(ar[i], bd(b_[i])) for i in ids]
        gk = [_dot_nt(ar[i], bd(k_[i])) for i in ids]
        l_ab = [jnp.where(strict, gb[i][:A_CHUNK], 0.0) for i in ids]
        a_qb = [jnp.where(incl, gb[i][A_CHUNK:], 0.0) for i in ids]
        akq = [jnp.concatenate([jnp.where(strict, gk[i][:A_CHUNK], 0.0),
                                jnp.where(incl, gk[i][A_CHUNK:], 0.0)], axis=0) for i in ids]
        akqv = [_dot(akq[i], bd(v_[i])) for i in ids]
        akv = [akqv[i][:A_CHUNK] for i in ids]
        ov = [akqv[i][A_CHUNK:] for i in ids]
        tinv = [eye + l_ab[i] for i in ids]
        p = [_dot(l_ab[i], bd(l_ab[i])) for i in ids]
        for _ in range(4):
            tp = [_dot(jnp.concatenate([tinv[i], p[i]], axis=0), bd(p[i])) for i in ids]
            tinv = [tinv[i] + tp[i][:A_CHUNK] for i in ids]
            p = [tp[i][A_CHUNK:] for i in ids]
        tinv = [tinv[i] + _dot(tinv[i], bd(p[i])) for i in ids]
        wu = [_dot(tinv[i], jnp.concatenate([bd(a_[i]), bd(akv[i])], axis=1)) for i in ids]
        wm = [wu[i][:, :gw] for i in ids]
        uv = [wu[i][:, gw:] for i in ids]
        for i in ids:
            x_s[idx[i]] = jnp.where(same_head, _dot_tn(wm[i], b_[i]), 0.0)
        for i in ids:
            m2 = jnp.where(same_head, _dot_tn(jnp.concatenate([uv[i], v_[i]], axis=0),
                                              jnp.concatenate([b_[i], k_[i]], axis=0)), 0.0)
            m2_s[idx[i]] = sum(m2[h * n:(h + 1) * n] for h in range(1, hpg)) + m2[:n]
        qo = [_dot(a_qb[i], jnp.concatenate([bd(wm[i]), bd(uv[i])], axis=1)) for i in ids]
        for i in ids:
            q_s[idx[i]] = r_[i] + qo[i][:, :gw]
        for i in ids:
            op_s[idx[i]] = qo[i][:, gw:] + ov[i]

    def state_part(c):
        rows = _chunk_rows(c)
        gam = gam_s[pl.ds(c, 1), :]
        gids = range(len(groups))
        s0 = [state_s[g] for g in gids]
        sx = [_dot(s0[g], x_s[c * len(groups) + g]) for g in gids]
        o = [_dot_nt(q_s[c * len(groups) + g], bd(s0[g])) + op_s[c * len(groups) + g] for g in gids]
        for g in gids:
            state_s[g] = (s0[g] + sx[g] + m2_s[c * len(groups) + g]) * gam[:, groups[g]]
        o_s[rows, :] = jnp.concatenate(o, axis=1)

    for gi in range(n_chunks // group):
        state_free_part(gi)
    for c in range(n_chunks):
        state_part(c)

    o = o_s[...]
    inv_n = 1.0 / n
    mean = seg_sum(o) * inv_n
    d = o - mean
    var = seg_sum(d * d) * inv_n
    on = d * lax.rsqrt(var + A_GN_EPS) * lng_ref[...] + lnb_ref[...]
    on = on + bonus_ref[...].astype(F32)
    y_ref[...] = (on * gate_ref[...].astype(F32)).astype(y_ref.dtype)


def _rwkv_params(p):
    depth = p["a_w0"].shape[0]
    row = lambda a: a.reshape(depth, 1, -1).astype(F32)
    w = A_WIDTH
    lora = jnp.zeros((depth, 128, 2 * w), F32)
    lora = lora.at[:, :A_DECAY_LORA, :w].set(p["a_w2"]).at[:, A_DECAY_LORA:, w:].set(p["a_a2"]).astype(BF16)
    hid = jnp.arange(A_SUMLANES) // A_HEAD_DIM
    ones = (hid[:, None] == hid[None, :]).astype(BF16)
    ti = jnp.arange(A_CHUNK)
    tri = (ti[:, None] >= ti[None, :]).astype(BF16)
    stage = [row(p["a_shift"]), row(p["a_w0"]), row(p["a_a0"]), lora, p["a_g2"].astype(BF16),
             row(p["a_kk"]), row(p["a_ka"]), row(p["a_rk"])]
    return (stage, ones), ([row(p["a_lnx_g"]), row(p["a_lnx_b"])], [ones, tri])


def _rwkv(stage_out, bsz, seq, params, l, tc):
    layered, shared = params
    w = A_WIDTH
    n_chunks = tc // A_CHUNK
    n_groups = A_WIDTH // A_GLANES
    per_seq = seq // tc
    per_sq = pltpu.VMEM((n_chunks * n_groups, A_GLANES, A_GLANES), F32)
    per_row = pltpu.VMEM((n_chunks * n_groups, A_CHUNK, A_GLANES), F32)
    row_tile = pl.BlockSpec((tc, w), lambda b, i: (b * per_seq + i, 0))
    return pl.pallas_call(
        _rwkv_kernel,
        grid=(bsz, per_seq),
        in_specs=[row_tile] * len(stage_out)
                 + [_layer_spec(a, l) for a in layered] + [_full(a.shape) for a in shared],
        out_specs=row_tile,
        out_shape=jax.ShapeDtypeStruct((bsz * seq, w), BF16),
        scratch_shapes=[pltpu.VMEM((n_groups, A_HEAD_DIM, A_GLANES), F32),
                        pltpu.VMEM((tc, w), F32),
                        pltpu.VMEM((max(8, n_chunks), w), F32),
                        per_sq, per_row, per_row, per_row],
        compiler_params=_cparams("parallel", "arbitrary"),
        name="rwkv7",
    )(*stage_out, *layered, *shared)


def _ret_kernel(z_ref, dmask_ref, qd_ref, kd_ref, cd_ref, y_ref, state_s, upd_s, st_s):
    tc = z_ref.shape[0]
    n_chunks = tc // B_CHUNK
    k_off, v_off, g_off = B_QK_WIDTH, 2 * B_QK_WIDTH, 2 * B_QK_WIDTH + B_V_WIDTH

    @pl.when(pl.program_id(1) == 0)
    def _():
        state_s[...] = jnp.zeros_like(state_s)

    items = [(c, h) for c in range(n_chunks) for h in range(B_HEADS)]
    rows = lambda c: slice(c * B_CHUNK, (c + 1) * B_CHUNK)
    qcol = lambda h: slice(h * B_QK_DIM, (h + 1) * B_QK_DIM)
    kcol = lambda h: slice(k_off + h * B_QK_DIM, k_off + (h + 1) * B_QK_DIM)
    vcol = lambda h: slice(v_off + h * B_V_DIM, v_off + (h + 1) * B_V_DIM)
    gcol = lambda h: slice(g_off + h * B_V_DIM, g_off + (h + 1) * B_V_DIM)

    for c, h in items:
        upd_s[c * B_HEADS + h] = _dot_tn(z_ref[rows(c), kcol(h)] * kd_ref[h], z_ref[rows(c), vcol(h)])
    for h in range(B_HEADS):
        st = state_s[h]
        for c in range(n_chunks):
            st_s[c * B_HEADS + h] = st
            st = cd_ref[h] * st + upd_s[c * B_HEADS + h]
        state_s[h] = st

    for c, h in items:
        qc = z_ref[rows(c), qcol(h)]
        scores = _dot_nt(qc, z_ref[rows(c), kcol(h)]) * dmask_ref[h]
        o = _dot(scores, z_ref[rows(c), vcol(h)]) + _dot(qc * qd_ref[h], st_s[c * B_HEADS + h])
        mu = jnp.mean(o, axis=-1, keepdims=True)
        d = o - mu
        var = jnp.mean(d * d, axis=-1, keepdims=True)
        on = d * lax.rsqrt(var + B_GN_EPS)
        y_ref[rows(c), h * B_V_DIM:(h + 1) * B_V_DIM] = (
            z_ref[rows(c), gcol(h)].astype(F32) * on).astype(y_ref.dtype)


def _retention_tables(seq):
    f32 = F32
    pos = jnp.arange(seq, dtype=f32)
    half = B_QK_DIM // 2
    inv_freq = B_ROPE_BASE ** (-jnp.arange(half, dtype=f32) / half)
    ang = pos[:, None] * inv_freq[None, :]
    cos, sin = jnp.cos(ang), jnp.sin(ang)
    cos2 = jnp.concatenate([cos, cos], axis=1)
    sin2 = jnp.concatenate([-sin, sin], axis=1)
    log_gamma = jnp.log(1.0 - 2.0 ** (-5.0 - jnp.arange(B_HEADS, dtype=f32)))
    idx = jnp.arange(B_CHUNK, dtype=f32)
    rel = idx[:, None] - idx[None, :]
    dmask = jnp.where(rel >= 0, jnp.exp(log_gamma[:, None, None] * jnp.maximum(rel, 0.0)), 0.0)
    qd = jnp.broadcast_to(jnp.exp(log_gamma[:, None] * (idx + 1.0))[:, :, None],
                          (B_HEADS, B_CHUNK, B_QK_DIM))
    kd = jnp.broadcast_to(jnp.exp(log_gamma[:, None] * (B_CHUNK - 1.0 - idx))[:, :, None],
                          (B_HEADS, B_CHUNK, B_QK_DIM))
    cd = jnp.broadcast_to(jnp.exp(log_gamma * B_CHUNK)[:, None, None], (B_HEADS, B_QK_DIM, B_V_DIM))
    return (cos2, sin2), [dmask, qd.astype(BF16), kd.astype(BF16), cd]


def _retention(zmix, bsz, seq, tabs, tc):
    n_items = (tc // B_CHUNK) * B_HEADS
    per_item = pltpu.VMEM((n_items, B_QK_DIM, B_V_DIM), F32)
    return pl.pallas_call(
        _ret_kernel,
        grid=(bsz, seq // tc),
        in_specs=[pl.BlockSpec((None, tc, RET_COLS), lambda b, i: (b, i, 0))] + [_full(a.shape) for a in tabs],
        out_specs=pl.BlockSpec((None, tc, B_V_WIDTH), lambda b, i: (b, i, 0)),
        out_shape=jax.ShapeDtypeStruct((bsz, seq, B_V_WIDTH), BF16),
        scratch_shapes=[pltpu.VMEM((B_HEADS, B_QK_DIM, B_V_DIM), F32), per_item, per_item],
        compiler_params=_cparams("parallel", "arbitrary"),
        name="retention",
    )(zmix, *tabs)


def _s5_kernel(u_ref, bbr_ref, bbi_ref, ar_ref, ai_ref, cr_ref, ci_ref, d_ref, wg_ref, bg_ref,
               y_ref, xr_s, xi_s, u_s, sr_s, si_s):
    bsz, tt, _ = u_ref.shape
    tp = tt // C_PARTS
    rp = tp * bsz

    @pl.when(pl.program_id(0) == 0)
    def _():
        sr_s[...] = jnp.zeros_like(sr_s)
        si_s[...] = jnp.zeros_like(si_s)

    blk = [slice(m * C_SLANES, (m + 1) * C_SLANES) for m in range(C_BLOCKS)]


    def drive(p):
        rows = slice(p * rp, (p + 1) * rp)
        u = pltpu.einshape("btc->tbc", u_ref[:, p * tp:(p + 1) * tp, :].astype(F32)).reshape(rp, C_WIDTH)
        u_s[rows, :] = u
        ub = u.astype(BF16)
        for m in range(C_BLOCKS):
            um = ub[:, m * C_ULANES:(m + 1) * C_ULANES]
            xr_s[rows, blk[m]] = jnp.dot(um, bbr_ref[m], preferred_element_type=F32)
            xi_s[rows, blk[m]] = jnp.dot(um, bbi_ref[m], preferred_element_type=F32)

    def scan(p):
        for m in range(C_BLOCKS):
            cols = blk[m]
            ar = jnp.broadcast_to(ar_ref[:, cols], (bsz, C_SLANES))
            ai = jnp.broadcast_to(ai_ref[:, cols], (bsz, C_SLANES))
            xr, xi = sr_s[:, cols], si_s[:, cols]
            for t in range(p * tp, (p + 1) * tp):
                rows = slice(t * bsz, (t + 1) * bsz)
                xr, xi = (ar * xr - ai * xi + xr_s[rows, cols], ar * xi + ai * xr + xi_s[rows, cols])
                xr_s[rows, cols] = xr
                xi_s[rows, cols] = xi
            sr_s[:, cols] = xr
            si_s[:, cols] = xi

    def readout(p):
        rows = slice(p * rp, (p + 1) * rp)
        parts = [_dot(xr_s[rows, blk[m]], cr_ref[m]) - _dot(xi_s[rows, blk[m]], ci_ref[m])
                 for m in range(C_BLOCKS)]
        y = jnp.concatenate(parts, axis=1) + d_ref[...] * u_s[rows, :]
        y = jax.nn.gelu(y)
        y = y * _sigmoid(_dot(y, wg_ref[...]) + bg_ref[...])
        y_ref[:, p * tp:(p + 1) * tp, :] = pltpu.einshape(
            "tbc->btc", y.reshape(tp, bsz, C_WIDTH)).astype(y_ref.dtype)

    drive(0)
    for p in range(C_PARTS):
        if p + 1 < C_PARTS:
            drive(p + 1)
        scan(p)
        readout(p)


def _s5_params(p):
    f32 = F32
    depth = p["c_log_dt"].shape[0]
    dt = jnp.exp(p["c_log_dt"].astype(f32))[..., None]
    lr, li = p["c_lam_re"].astype(f32), p["c_lam_im"].astype(f32)
    mag = jnp.exp(lr * dt)
    ab_re, ab_im = mag * jnp.cos(li * dt), mag * jnp.sin(li * dt)
    den = lr * lr + li * li
    f_re = ((ab_re - 1.0) * lr + ab_im * li) / den
    f_im = (ab_im * lr - (ab_re - 1.0) * li) / den
    bre, bim = p["c_b_re"].astype(f32), p["c_b_im"].astype(f32)
    bb_re = f_re[..., None] * bre - f_im[..., None] * bim
    bb_im = f_re[..., None] * bim + f_im[..., None] * bre
    gpb = C_GROUPS // C_BLOCKS
    eye = jnp.eye(gpb, dtype=f32)

    def in_blocks(bb):
        bb = bb.reshape(depth, C_BLOCKS, gpb, C_STATE, C_GROUP)
        return jnp.einsum("lmgpc,gh->lmgchp", bb, eye).reshape(depth, C_BLOCKS, C_ULANES, C_SLANES).astype(BF16)

    def out_blocks(cc):
        cc = cc.astype(f32).reshape(depth, C_BLOCKS, gpb, C_GROUP, C_STATE)
        return jnp.einsum("lmgcp,gh->lmgphc", cc, eye).reshape(depth, C_BLOCKS, C_SLANES, C_ULANES).astype(BF16)

    return [in_blocks(bb_re), in_blocks(bb_im), ab_re.reshape(depth, 1, C_LANES), ab_im.reshape(depth, 1, C_LANES),
            out_blocks(p["c_c_re"]), out_blocks(p["c_c_im"]), p["c_d"].reshape(depth, 1, C_WIDTH).astype(f32),
            p["c_w_glu"].astype(BF16), p["c_b_glu"].reshape(depth, 1, C_WIDTH).astype(f32)]


def _s5(zmix, bsz, seq, consts, l, tt):
    rows = tt * bsz
    return pl.pallas_call(
        _s5_kernel,
        grid=(seq // tt,),
        in_specs=[pl.BlockSpec((bsz, tt, C_WIDTH), lambda i: (0, i, U_BLK))] + [_layer_spec(a, l) for a in consts],
        out_specs=pl.BlockSpec((bsz, tt, C_WIDTH), lambda i: (0, i, 0)),
        out_shape=jax.ShapeDtypeStruct((bsz, seq, C_WIDTH), BF16),
        scratch_shapes=[pltpu.VMEM((rows, C_LANES), F32), pltpu.VMEM((rows, C_LANES), F32),
                        pltpu.VMEM((rows, C_WIDTH), F32),
                        pltpu.VMEM((bsz, C_LANES), F32), pltpu.VMEM((bsz, C_LANES), F32)],
        compiler_params=_cparams("arbitrary"),
        name="s5",
    )(zmix, *consts)


def _merge_kernel(alpha, x_ref, ya_ref, yb_ref, yc_ref, wgate_ref, bgate_ref, wb_ref,
                  wout_ref, g_ref, b_ref, o_ref):
    d = x_ref.shape[1]
    x = x_ref[...]
    gates = _sigmoid(_dot(x, wgate_ref[...]) + bgate_ref[...])
    b_lo, c_lo = A_WIDTH, A_WIDTH + B_V_WIDTH
    merged = (gates[:, :d] * _dot(ya_ref[...], wb_ref[:b_lo, :])
              + gates[:, d:2 * d] * _dot(yb_ref[...], wb_ref[b_lo:c_lo, :])
              + gates[:, 2 * d:] * _dot(yc_ref[...], wb_ref[c_lo:, :]))
    o_ref[...] = _layer_norm(alpha * x + _dot(merged, wout_ref[...]), g_ref[...], b_ref[...])


def _merge(x2d, ya, yb, yc, consts, l, alpha, tm):
    m, d = x2d.shape
    tile = lambda n: pl.BlockSpec((tm, n), lambda i: (i, 0))
    return pl.pallas_call(
        functools.partial(_merge_kernel, alpha),
        grid=(m // tm,),
        in_specs=[tile(d), tile(A_WIDTH), tile(B_V_WIDTH), tile(C_WIDTH)] + [_layer_spec(a, l) for a in consts],
        out_specs=tile(d),
        out_shape=jax.ShapeDtypeStruct((m, d), F32),
        compiler_params=_cparams("parallel"),
        name="merge",
    )(x2d, ya, yb, yc, *consts)


def _ffn_kernel(alpha, tf, x_ref, w1_ref, w2_ref, g_ref, b_ref, o_ref):
    x = x_ref[...]
    xb = x.astype(BF16)
    acc = None
    for j in range(w1_ref.shape[1] // tf):
        h = jnp.maximum(jnp.dot(xb, w1_ref[:, j * tf:(j + 1) * tf], preferred_element_type=F32), 0.0)
        part = jnp.dot((h * h).astype(BF16), w2_ref[j * tf:(j + 1) * tf, :], preferred_element_type=F32)
        acc = part if acc is None else acc + part
    o_ref[...] = _layer_norm(alpha * x + acc, g_ref[...], b_ref[...])


def _ffn(x2d, consts, l, alpha, tm, tf):
    m, d = x2d.shape
    return pl.pallas_call(
        functools.partial(_ffn_kernel, alpha, tf),
        grid=(m // tm,),
        in_specs=[pl.BlockSpec((tm, d), lambda i: (i, 0))] + [_layer_spec(a, l) for a in consts],
        out_specs=pl.BlockSpec((tm, d), lambda i: (i, 0)),
        out_shape=jax.ShapeDtypeStruct((m, d), F32),
        compiler_params=_cparams("parallel"),
        name="ffn",
    )(x2d, *consts)


def _tile(n, want):
    t = min(n, want)
    assert n % t == 0, (n, want)
    return t


def kernel(x, w_in, b_gate, a_shift, a_w0, a_w2, a_a0, a_a2, a_g2, a_kk, a_ka, a_rk, a_lnx_g, a_lnx_b, c_lam_re, c_lam_im, c_log_dt, c_b_re, c_b_im, c_c_re, c_c_im, c_d, c_w_glu, c_b_glu, w_branch, w_out, ln1_g, ln1_b, w_ff1, w_ff2, ln2_g, ln2_b):
    bsz, seq, d = x.shape
    depth = w_in.shape[0]
    alpha = (2.0 * depth) ** 0.25
    tokens = bsz * seq
    rows = lambda a: a.reshape(depth, 1, -1)
    rwkv_params = _rwkv_params(dict(a_shift=a_shift, a_w0=a_w0, a_w2=a_w2, a_a0=a_a0, a_a2=a_a2, a_g2=a_g2,
                                    a_kk=a_kk, a_ka=a_ka, a_rk=a_rk, a_lnx_g=a_lnx_g, a_lnx_b=a_lnx_b))
    s5_params = _s5_params(dict(c_lam_re=c_lam_re, c_lam_im=c_lam_im, c_log_dt=c_log_dt, c_b_re=c_b_re,
                                c_b_im=c_b_im, c_c_re=c_c_re, c_c_im=c_c_im, c_d=c_d, c_w_glu=c_w_glu,
                                c_b_glu=c_b_glu))
    rope_tabs, ret_tables = _retention_tables(seq)
    w_in_bf = w_in.astype(BF16)
    w_za, w_rest = w_in_bf[:, :, :A_PROJ], w_in_bf[:, :, A_PROJ:MIX_COLS]
    merge_consts = [w_in_bf[:, :, MIX_COLS:], rows(b_gate), w_branch.astype(BF16), w_out.astype(BF16),
                    rows(ln1_g), rows(ln1_b)]
    ffn_consts = [w_ff1.astype(BF16), w_ff2.astype(BF16), rows(ln2_g), rows(ln2_b)]
    xt = x.reshape(tokens, d)
    for l in range(depth):
        zmix, *stage_out = _proj(xt, w_za, w_rest, rwkv_params[0], rope_tabs, bsz, seq, l, _tile(seq, 512))
        zmix = zmix.reshape(bsz, seq, REST_COLS)
        ya = _rwkv(stage_out, bsz, seq, rwkv_params[1], l, _tile(seq, 512))
        yb = _retention(zmix, bsz, seq, ret_tables, _tile(seq, 512))
        yc = _s5(zmix, bsz, seq, s5_params, l, _tile(seq, 64))
        x1 = _merge(xt, ya.reshape(tokens, -1), yb.reshape(tokens, -1), yc.reshape(tokens, -1),
                    merge_consts, l, alpha, _tile(tokens, 512))
        xt = _ffn(x1, ffn_consts, l, alpha, _tile(tokens, 1024), 1024)
    return xt.reshape(bsz, seq, d)
```

```python
import functools
import math

import jax
import jax.numpy as jnp
from jax import lax
from jax.experimental import pallas as pl
from jax.experimental.pallas import tpu as pltpu

F32 = jnp.float32
BF16 = jnp.bfloat16

A_HEADS = 8
A_HEAD_DIM = 64
A_WIDTH = A_HEADS * A_HEAD_DIM
A_DECAY_LORA = 64
A_ICLR_LORA = 64
A_GATE_LORA = 128
A_PROJ = 3 * A_WIDTH + A_DECAY_LORA + A_ICLR_LORA + A_GATE_LORA
A_GN_EPS = 64e-5
A_CHUNK = 64
A_GROUP = 8
A_GLANES = 128
A_SUMLANES = 256
A_LORA_IN = A_DECAY_LORA + A_ICLR_LORA
PROJ_CHUNK = 256

B_HEADS = 4
B_QK_DIM = 128
B_V_DIM = 256
B_QK_WIDTH = B_HEADS * B_QK_DIM
B_V_WIDTH = B_HEADS * B_V_DIM
B_CHUNK = 128
B_ROPE_BASE = 10000.0
B_GN_EPS = 1e-5

C_WIDTH = 512
C_GROUP = 16
C_GROUPS = C_WIDTH // C_GROUP
C_STATE = 64
C_LANES = C_GROUPS * C_STATE
C_BLOCKS = 4
C_ULANES = C_WIDTH // C_BLOCKS
C_SLANES = C_LANES // C_BLOCKS
C_PARTS = 4

LN_EPS = 1e-5

RET_COLS = 2 * B_QK_WIDTH + 2 * B_V_WIDTH
REST_COLS = RET_COLS + C_WIDTH
MIX_COLS = A_PROJ + REST_COLS
U_BLK = RET_COLS // C_WIDTH
assert RET_COLS % C_WIDTH == 0

V7X_VMEM_LIMIT_BYTES = 56 * 1024 * 1024


def _cparams(*sem):
    return pltpu.CompilerParams(dimension_semantics=sem, vmem_limit_bytes=V7X_VMEM_LIMIT_BYTES)


def _full(shape):
    n = len(shape)
    return pl.BlockSpec(shape, lambda *_: (0,) * n)


def _layer_spec(a, l):
    n = a.ndim - 1
    return pl.BlockSpec((None,) + a.shape[1:], lambda *_: (l,) + (0,) * n, pipeline_mode=pl.Buffered(1))


def _dot(a, b):
    return jnp.dot(a.astype(BF16), b.astype(BF16), preferred_element_type=F32)


def _dot_nt(a, b):
    return lax.dot_general(a.astype(BF16), b.astype(BF16), (((1,), (1,)), ((), ())),
                           preferred_element_type=F32)


def _dot_tn(a, b):
    return lax.dot_general(a.astype(BF16), b.astype(BF16), (((0,), (0,)), ((), ())),
                           preferred_element_type=F32)


def _split3(x):
    hi = x.astype(BF16)
    r1 = x - hi.astype(F32)
    mid = r1.astype(BF16)
    lo = (r1 - mid.astype(F32)).astype(BF16)
    return hi, mid, lo


def _sigmoid(x):
    return 1.0 / (1.0 + jnp.exp(-x))


def _row_parts(n, parts=2):
    step = n // parts
    return [slice(i * step, (i + 1) * step) for i in range(parts)]


def _layer_norm(y, g, b):
    mu = jnp.mean(y, axis=-1, keepdims=True)
    d = y - mu
    var = jnp.mean(d * d, axis=-1, keepdims=True)
    return d * lax.rsqrt(var + LN_EPS) * g + b


def _chunk_rows(c):
    return pl.ds(c * A_CHUNK, A_CHUNK)


def _proj_rwkv_kernel(x_ref, wza_ref, wrest_ref, mu_ref, w0_ref, a0_ref, lora_ref, g2_ref, kkp_ref, kap_ref,
                      rkp_ref, lng_ref, lnb_ref, ones_ref, tri_ref, cos_ref, sin_ref,
                      rest_ref, y_ref,
                      xb_s, carry_s, state_s, r_s, k_s, v_s, kn_s, b0_s, lw_s, gate_s, bonus_s, o_s,
                      gam_s, x_s, m2_s, q_s, op_s):
    tm = x_ref.shape[0]
    n_chunks = tm // A_CHUNK
    w = A_WIDTH

    @pl.when(pl.program_id(1) == 0)
    def _():
        carry_s[...] = jnp.zeros_like(carry_s)
        state_s[...] = jnp.zeros_like(state_s)

    xb_s[...] = x_ref[...].astype(BF16)
    z = jnp.dot(xb_s[...], wza_ref[...], preferred_element_type=F32)

    cos2 = cos_ref[...]
    sin2 = sin_ref[...]

    def rope(t):
        return t * cos2 + pltpu.roll(t, B_QK_DIM // 2, 1) * sin2

    k_off, v_off, g_off = B_QK_WIDTH, 2 * B_QK_WIDTH, 2 * B_QK_WIDTH + B_V_WIDTH

    def piece(lo):
        hi = lo + PROJ_CHUNK
        val = jnp.dot(xb_s[...], wrest_ref[:, lo:hi], preferred_element_type=F32)
        if lo < v_off:
            scale = 1.0 if lo < k_off else B_QK_DIM ** -0.5
            for j in range(PROJ_CHUNK // B_QK_DIM):
                hs = slice(j * B_QK_DIM, (j + 1) * B_QK_DIM)
                rest_ref[:, lo + j * B_QK_DIM:lo + (j + 1) * B_QK_DIM] = (
                    rope(val[:, hs]) * scale).astype(rest_ref.dtype)
        elif g_off <= lo < RET_COLS:
            rest_ref[:, lo:hi] = (val * _sigmoid(val)).astype(rest_ref.dtype)
        else:
            rest_ref[:, lo:hi] = val.astype(rest_ref.dtype)

    pending = list(range(0, REST_COLS, PROJ_CHUNK))

    def fill(n=1):
        for _ in range(n):
            if pending:
                piece(pending.pop(0))

    ones = ones_ref[...]
    gw = A_GLANES
    groups = [slice(g * gw, (g + 1) * gw) for g in range(w // gw)]
    slabs = [slice(g * A_SUMLANES, (g + 1) * A_SUMLANES) for g in range(w // A_SUMLANES)]

    def seg_sum(t):
        tb = t.astype(BF16)
        return jnp.concatenate([jnp.dot(tb[:, s], ones, preferred_element_type=F32) for s in slabs], axis=1)

    rolled = pltpu.roll(z, 1, 0)
    rowid = lax.broadcasted_iota(jnp.int32, z.shape, 0)
    prev = jnp.where(rowid == 0, jnp.broadcast_to(carry_s[0:1, :], z.shape), rolled)
    zs = z + mu_ref[...] * (prev - z)
    carry_s[0:1, :] = z[tm - 1:tm, :]
    fill()

    lz = zs[:, 3 * w:3 * w + A_LORA_IN]
    lane = lax.broadcasted_iota(jnp.int32, lz.shape, 1)
    lin = jnp.where(lane < A_DECAY_LORA, jnp.tanh(lz), lz)
    wa = _dot(lin, lora_ref[...])
    lw_s[...] = (-math.exp(-0.5)) * _sigmoid(w0_ref[...] + wa[:, :w])
    ia = _sigmoid(a0_ref[...] + wa[:, w:])
    gate_s[...] = _dot(_sigmoid(zs[:, 3 * w + A_LORA_IN:3 * w + A_LORA_IN + A_GATE_LORA]), g2_ref[...])

    r = zs[:, :w]
    k = zs[:, w:2 * w]
    v = zs[:, 2 * w:3 * w]
    kk = k * kkp_ref[...]
    kn = kk * lax.rsqrt(jnp.maximum(seg_sum(kk * kk), 1e-24))
    kmod = k * (1.0 + (ia - 1.0) * kap_ref[...])
    r_s[...] = r
    k_s[...] = kmod
    v_s[...] = v
    kn_s[...] = kn
    b0_s[...] = kn * ia
    bonus_s[...] = seg_sum(r * kmod * rkp_ref[...]) * v
    fill()

    tri = tri_ref[...]
    hpg = gw // A_HEAD_DIM
    rid = lax.broadcasted_iota(jnp.int32, (A_CHUNK, gw), 0)
    cid = lax.broadcasted_iota(jnp.int32, (A_CHUNK, gw), 1) % A_HEAD_DIM
    strict = rid > cid
    incl = rid >= cid
    eye = (rid == cid).astype(F32)
    brow = lax.broadcasted_iota(jnp.int32, (gw, gw), 0) // A_HEAD_DIM
    bcol = lax.broadcasted_iota(jnp.int32, (gw, gw), 1) // A_HEAD_DIM
    same_head = brow == bcol
    same_head_bf = same_head.astype(BF16)
    n = A_HEAD_DIM

    def bd(x):
        xb = x.astype(BF16)
        return jnp.concatenate([xb] * hpg, axis=0) * same_head_bf


    group = min(A_GROUP, n_chunks)
    assert n_chunks % group == 0

    def state_free_part(gi):
        items = []
        for j in range(group):
            c = gi * group + j
            rows = _chunk_rows(c)
            lw = lw_s[rows, :]
            h3 = _split3(lw)
            cum = (jnp.dot(tri, h3[0], preferred_element_type=F32)
                   + jnp.dot(tri, h3[1], preferred_element_type=F32)
                   + jnp.dot(tri, h3[2], preferred_element_type=F32))
            e_in = jnp.exp(cum)
            e_ex = jnp.exp(cum - lw)
            e_ng = jnp.exp(-cum)
            rt = r_s[rows, :] * e_in
            at = -kn_s[rows, :] * e_ex
            bt = b0_s[rows, :] * e_ng
            kt = k_s[rows, :] * e_ng
            vv = v_s[rows, :]
            gam_s[pl.ds(c, 1), :] = e_in[A_CHUNK - 1:A_CHUNK, :]
            for g, s in enumerate(groups):
                items.append((c * len(groups) + g, at[:, s], rt[:, s], bt[:, s], kt[:, s], vv[:, s]))
            if j % 2 == 1:
                fill()
        ids = range(len(items))
        idx = [it[0] for it in items]
        a_ = [it[1] for it in items]
        r_ = [it[2] for it in items]
        b_ = [it[3] for it in items]
        k_ = [it[4] for it in items]
        v_ = [it[5] for it in items]
        ar = [jnp.concatenate([a_[i], r_[i]], axis=0) for i in ids]
        gb = [_dot_nt(ar[i], bd(b_[i])) for i in ids]
        fill()
        gk = [_dot_nt(ar[i], bd(k_[i])) for i in ids]
        fill()
        l_ab = [jnp.where(strict, gb[i][:A_CHUNK], 0.0) for i in ids]
        a_qb = [jnp.where(incl, gb[i][A_CHUNK:], 0.0) for i in ids]
        akq = [jnp.concatenate([jnp.where(strict, gk[i][:A_CHUNK], 0.0),
                                jnp.where(incl, gk[i][A_CHUNK:], 0.0)], axis=0) for i in ids]
        akqv = [_dot(akq[i], bd(v_[i])) for i in ids]
        fill()
        akv = [akqv[i][:A_CHUNK] for i in ids]
        ov = [akqv[i][A_CHUNK:] for i in ids]
        tinv = [eye + l_ab[i] for i in ids]
        p = [_dot(l_ab[i], bd(l_ab[i])) for i in ids]
        fill()
        for _ in range(4):
            tp = [_dot(jnp.concatenate([tinv[i], p[i]], axis=0), bd(p[i])) for i in ids]
            fill()
            tinv = [tinv[i] + tp[i][:A_CHUNK] for i in ids]
            p = [tp[i][A_CHUNK:] for i in ids]
        tinv = [tinv[i] + _dot(tinv[i], bd(p[i])) for i in ids]
        fill()
        wu = [_dot(tinv[i], jnp.concatenate([bd(a_[i]), bd(akv[i])], axis=1)) for i in ids]
        fill()
        wm = [wu[i][:, :gw] for i in ids]
        uv = [wu[i][:, gw:] for i in ids]
        for i in ids:
            x_s[idx[i]] = jnp.where(same_head, _dot_tn(wm[i], b_[i]), 0.0)
        fill()
        for i in ids:
            m2 = jnp.where(same_head, _dot_tn(jnp.concatenate([uv[i], v_[i]], axis=0),
                                              jnp.concatenate([b_[i], k_[i]], axis=0)), 0.0)
            m2_s[idx[i]] = sum(m2[h * n:(h + 1) * n] for h in range(1, hpg)) + m2[:n]
        fill()
        qo = [_dot(a_qb[i], jnp.concatenate([bd(wm[i]), bd(uv[i])], axis=1)) for i in ids]
        fill()
        for i in ids:
            q_s[idx[i]] = r_[i] + qo[i][:, :gw]
        for i in ids:
            op_s[idx[i]] = qo[i][:, gw:] + ov[i]

    def state_part(c):
        rows = _chunk_rows(c)
        gam = gam_s[pl.ds(c, 1), :]
        gids = range(len(groups))
        s0 = [state_s[g] for g in gids]
        sx = [_dot(s0[g], x_s[c * len(groups) + g]) for g in gids]
        o = [_dot_nt(q_s[c * len(groups) + g], bd(s0[g])) + op_s[c * len(groups) + g] for g in gids]
        for g in gids:
            state_s[g] = (s0[g] + sx[g] + m2_s[c * len(groups) + g]) * gam[:, groups[g]]
        o_s[rows, :] = jnp.concatenate(o, axis=1)
        fill()

    for gi in range(n_chunks // group):
        state_free_part(gi)
    for c in range(n_chunks):
        state_part(c)
    fill(len(pending))

    o = o_s[...]
    inv_n = 1.0 / n
    mean = seg_sum(o) * inv_n
    d = o - mean
    var = seg_sum(d * d) * inv_n
    on = d * lax.rsqrt(var + A_GN_EPS) * lng_ref[...] + lnb_ref[...]
    y_ref[...] = ((on + bonus_s[...]) * gate_s[...]).astype(y_ref.dtype)


def _rwkv_params(p):
    depth = p["a_w0"].shape[0]
    row = lambda a: a.reshape(depth, 1, -1).astype(F32)
    w = A_WIDTH
    lora = jnp.zeros((depth, A_LORA_IN, 2 * w), F32)
    lora = lora.at[:, :A_DECAY_LORA, :w].set(p["a_w2"]).at[:, A_DECAY_LORA:, w:].set(p["a_a2"]).astype(BF16)
    hid = jnp.arange(A_SUMLANES) // A_HEAD_DIM
    ones = (hid[:, None] == hid[None, :]).astype(BF16)
    ti = jnp.arange(A_CHUNK)
    tri = (ti[:, None] >= ti[None, :]).astype(BF16)
    layered = [row(p["a_shift"]), row(p["a_w0"]), row(p["a_a0"]), lora, p["a_g2"].astype(BF16),
               row(p["a_kk"]), row(p["a_ka"]), row(p["a_rk"]), row(p["a_lnx_g"]), row(p["a_lnx_b"])]
    return layered, [ones, tri]


def _proj_rwkv(x2d, w_za, w_rest, params, rope_tabs, bsz, seq, l, tm):
    layered, shared = params
    cos2, sin2 = rope_tabs
    k = x2d.shape[1]
    w = A_WIDTH
    per_seq = seq // tm
    n_chunks = tm // A_CHUNK
    n_groups = w // A_GLANES
    row_tile = lambda n: pl.BlockSpec((tm, n), lambda b, i: (b * per_seq + i, 0))
    consts = [w_za, w_rest] + layered
    stage = pltpu.VMEM((tm, w), F32)
    per_sq = pltpu.VMEM((n_chunks * n_groups, A_GLANES, A_GLANES), F32)
    per_row = pltpu.VMEM((n_chunks * n_groups, A_CHUNK, A_GLANES), F32)
    return pl.pallas_call(
        _proj_rwkv_kernel,
        grid=(bsz, per_seq),
        in_specs=[row_tile(k)] + [_layer_spec(a, l) for a in consts] + [_full(a.shape) for a in shared]
                 + [pl.BlockSpec((tm, B_QK_DIM), lambda b, i: (i, 0))] * 2,
        out_specs=[row_tile(REST_COLS), row_tile(w)],
        out_shape=[jax.ShapeDtypeStruct((bsz * seq, REST_COLS), BF16),
                   jax.ShapeDtypeStruct((bsz * seq, w), BF16)],
        scratch_shapes=[pltpu.VMEM((tm, k), BF16),
                        pltpu.VMEM((8, A_PROJ), F32),
                        pltpu.VMEM((n_groups, A_HEAD_DIM, A_GLANES), F32),
                        stage, stage, stage, stage, stage, stage, stage, stage, stage,
                        pltpu.VMEM((max(8, n_chunks), w), F32),
                        per_sq, per_row, per_row, per_row],
        compiler_params=_cparams("parallel", "arbitrary"),
        name="proj_rwkv7",
    )(x2d, *consts, *shared, cos2, sin2)


def _ret_kernel(z_ref, dmask_ref, qd_ref, kd_ref, cd_ref, y_ref, state_s, upd_s, st_s):
    tc = z_ref.shape[0]
    n_chunks = tc // B_CHUNK
    k_off, v_off, g_off = B_QK_WIDTH, 2 * B_QK_WIDTH, 2 * B_QK_WIDTH + B_V_WIDTH

    @pl.when(pl.program_id(1) == 0)
    def _():
        state_s[...] = jnp.zeros_like(state_s)

    items = [(c, h) for c in range(n_chunks) for h in range(B_HEADS)]
    rows = lambda c: slice(c * B_CHUNK, (c + 1) * B_CHUNK)
    qcol = lambda h: slice(h * B_QK_DIM, (h + 1) * B_QK_DIM)
    kcol = lambda h: slice(k_off + h * B_QK_DIM, k_off + (h + 1) * B_QK_DIM)
    vcol = lambda h: slice(v_off + h * B_V_DIM, v_off + (h + 1) * B_V_DIM)
    gcol = lambda h: slice(g_off + h * B_V_DIM, g_off + (h + 1) * B_V_DIM)

    for c, h in items:
        upd_s[c * B_HEADS + h] = _dot_tn(z_ref[rows(c), kcol(h)] * kd_ref[h], z_ref[rows(c), vcol(h)])
    for h in range(B_HEADS):
        st = state_s[h]
        for c in range(n_chunks):
            st_s[c * B_HEADS + h] = st
            st = cd_ref[h] * st + upd_s[c * B_HEADS + h]
        state_s[h] = st

    for c, h in items:
        qc = z_ref[rows(c), qcol(h)]
        scores = _dot_nt(qc, z_ref[rows(c), kcol(h)]) * dmask_ref[h]
        o = _dot(scores, z_ref[rows(c), vcol(h)]) + _dot(qc * qd_ref[h], st_s[c * B_HEADS + h])
        mu = jnp.mean(o, axis=-1, keepdims=True)
        d = o - mu
        var = jnp.mean(d * d, axis=-1, keepdims=True)
        on = d * lax.rsqrt(var + B_GN_EPS)
        y_ref[rows(c), h * B_V_DIM:(h + 1) * B_V_DIM] = (
            z_ref[rows(c), gcol(h)].astype(F32) * on).astype(y_ref.dtype)


def _retention_tables(seq):
    f32 = F32
    pos = jnp.arange(seq, dtype=f32)
    half = B_QK_DIM // 2
    inv_freq = B_ROPE_BASE ** (-jnp.arange(half, dtype=f32) / half)
    ang = pos[:, None] * inv_freq[None, :]
    cos, sin = jnp.cos(ang), jnp.sin(ang)
    cos2 = jnp.concatenate([cos, cos], axis=1)
    sin2 = jnp.concatenate([-sin, sin], axis=1)
    log_gamma = jnp.log(1.0 - 2.0 ** (-5.0 - jnp.arange(B_HEADS, dtype=f32)))
    idx = jnp.arange(B_CHUNK, dtype=f32)
    rel = idx[:, None] - idx[None, :]
    dmask = jnp.where(rel >= 0, jnp.exp(log_gamma[:, None, None] * jnp.maximum(rel, 0.0)), 0.0)
    qd = jnp.broadcast_to(jnp.exp(log_gamma[:, None] * (idx + 1.0))[:, :, None],
                          (B_HEADS, B_CHUNK, B_QK_DIM))
    kd = jnp.broadcast_to(jnp.exp(log_gamma[:, None] * (B_CHUNK - 1.0 - idx))[:, :, None],
                          (B_HEADS, B_CHUNK, B_QK_DIM))
    cd = jnp.broadcast_to(jnp.exp(log_gamma * B_CHUNK)[:, None, None], (B_HEADS, B_QK_DIM, B_V_DIM))
    return (cos2, sin2), [dmask, qd.astype(BF16), kd.astype(BF16), cd]


def _retention(zmix, bsz, seq, tabs, tc):
    n_items = (tc // B_CHUNK) * B_HEADS
    per_item = pltpu.VMEM((n_items, B_QK_DIM, B_V_DIM), F32)
    return pl.pallas_call(
        _ret_kernel,
        grid=(bsz, seq // tc),
        in_specs=[pl.BlockSpec((None, tc, RET_COLS), lambda b, i: (b, i, 0))] + [_full(a.shape) for a in tabs],
        out_specs=pl.BlockSpec((None, tc, B_V_WIDTH), lambda b, i: (b, i, 0)),
        out_shape=jax.ShapeDtypeStruct((bsz, seq, B_V_WIDTH), BF16),
        scratch_shapes=[pltpu.VMEM((B_HEADS, B_QK_DIM, B_V_DIM), F32), per_item, per_item],
        compiler_params=_cparams("parallel", "arbitrary"),
        name="retention",
    )(zmix, *tabs)


def _s5_kernel(u_ref, bbr_ref, bbi_ref, ar_ref, ai_ref, cr_ref, ci_ref, d_ref, wg_ref, bg_ref,
               y_ref, xr_s, xi_s, u_s, sr_s, si_s):
    bsz, tt, _ = u_ref.shape
    tp = tt // C_PARTS
    rp = tp * bsz

    @pl.when(pl.program_id(0) == 0)
    def _():
        sr_s[...] = jnp.zeros_like(sr_s)
        si_s[...] = jnp.zeros_like(si_s)

    blk = [slice(m * C_SLANES, (m + 1) * C_SLANES) for m in range(C_BLOCKS)]


    def drive(p):
        rows = slice(p * rp, (p + 1) * rp)
        u = pltpu.einshape("btc->tbc", u_ref[:, p * tp:(p + 1) * tp, :].astype(F32)).reshape(rp, C_WIDTH)
        u_s[rows, :] = u
        ub = u.astype(BF16)
        for m in range(C_BLOCKS):
            um = ub[:, m * C_ULANES:(m + 1) * C_ULANES]
            xr_s[rows, blk[m]] = jnp.dot(um, bbr_ref[m], preferred_element_type=F32)
            xi_s[rows, blk[m]] = jnp.dot(um, bbi_ref[m], preferred_element_type=F32)

    def scan(p):
        for m in range(C_BLOCKS):
            cols = blk[m]
            ar = jnp.broadcast_to(ar_ref[:, cols], (bsz, C_SLANES))
            ai = jnp.broadcast_to(ai_ref[:, cols], (bsz, C_SLANES))
            xr, xi = sr_s[:, cols], si_s[:, cols]
            for t in range(p * tp, (p + 1) * tp):
                rows = slice(t * bsz, (t + 1) * bsz)
                xr, xi = (ar * xr - ai * xi + xr_s[rows, cols], ar * xi + ai * xr + xi_s[rows, cols])
                xr_s[rows, cols] = xr
                xi_s[rows, cols] = xi
            sr_s[:, cols] = xr
            si_s[:, cols] = xi

    def readout(p):
        rows = slice(p * rp, (p + 1) * rp)
        parts = [_dot(xr_s[rows, blk[m]], cr_ref[m]) - _dot(xi_s[rows, blk[m]], ci_ref[m])
                 for m in range(C_BLOCKS)]
        y = jnp.concatenate(parts, axis=1) + d_ref[...] * u_s[rows, :]
        y = jax.nn.gelu(y)
        y = y * _sigmoid(_dot(y, wg_ref[...]) + bg_ref[...])
        y_ref[:, p * tp:(p + 1) * tp, :] = pltpu.einshape(
            "tbc->btc", y.reshape(tp, bsz, C_WIDTH)).astype(y_ref.dtype)

    drive(0)
    for p in range(C_PARTS):
        if p + 1 < C_PARTS:
            drive(p + 1)
        scan(p)
        readout(p)


def _s5_params(p):
    f32 = F32
    depth = p["c_log_dt"].shape[0]
    dt = jnp.exp(p["c_log_dt"].astype(f32))[..., None]
    lr, li = p["c_lam_re"].astype(f32), p["c_lam_im"].astype(f32)
    mag = jnp.exp(lr * dt)
    ab_re, ab_im = mag * jnp.cos(li * dt), mag * jnp.sin(li * dt)
    den = lr * lr + li * li
    f_re = ((ab_re - 1.0) * lr + ab_im * li) / den
    f_im = (ab_im * lr - (ab_re - 1.0) * li) / den
    bre, bim = p["c_b_re"].astype(f32), p["c_b_im"].astype(f32)
    bb_re = f_re[..., None] * bre - f_im[..., None] * bim
    bb_im = f_re[..., None] * bim + f_im[..., None] * bre
    gpb = C_GROUPS // C_BLOCKS
    eye = jnp.eye(gpb, dtype=f32)

    def in_blocks(bb):
        bb = bb.reshape(depth, C_BLOCKS, gpb, C_STATE, C_GROUP)
        return jnp.einsum("lmgpc,gh->lmgchp", bb, eye).reshape(depth, C_BLOCKS, C_ULANES, C_SLANES).astype(BF16)

    def out_blocks(cc):
        cc = cc.astype(f32).reshape(depth, C_BLOCKS, gpb, C_GROUP, C_STATE)
        return jnp.einsum("lmgcp,gh->lmgphc", cc, eye).reshape(depth, C_BLOCKS, C_SLANES, C_ULANES).astype(BF16)

    return [in_blocks(bb_re), in_blocks(bb_im), ab_re.reshape(depth, 1, C_LANES), ab_im.reshape(depth, 1, C_LANES),
            out_blocks(p["c_c_re"]), out_blocks(p["c_c_im"]), p["c_d"].reshape(depth, 1, C_WIDTH).astype(f32),
            p["c_w_glu"].astype(BF16), p["c_b_glu"].reshape(depth, 1, C_WIDTH).astype(f32)]


def _s5(zmix, bsz, seq, consts, l, tt):
    rows = tt * bsz
    return pl.pallas_call(
        _s5_kernel,
        grid=(seq // tt,),
        in_specs=[pl.BlockSpec((bsz, tt, C_WIDTH), lambda i: (0, i, U_BLK))] + [_layer_spec(a, l) for a in consts],
        out_specs=pl.BlockSpec((bsz, tt, C_WIDTH), lambda i: (0, i, 0)),
        out_shape=jax.ShapeDtypeStruct((bsz, seq, C_WIDTH), BF16),
        scratch_shapes=[pltpu.VMEM((rows, C_LANES), F32), pltpu.VMEM((rows, C_LANES), F32),
                        pltpu.VMEM((rows, C_WIDTH), F32),
                        pltpu.VMEM((bsz, C_LANES), F32), pltpu.VMEM((bsz, C_LANES), F32)],
        compiler_params=_cparams("arbitrary"),
        name="s5",
    )(zmix, *consts)


def _merge_kernel(alpha, x_ref, ya_ref, yb_ref, yc_ref, wgate_ref, bgate_ref, wb_ref,
                  wout_ref, g_ref, b_ref, o_ref):
    tm, d = x_ref.shape
    b_lo, c_lo = A_WIDTH, A_WIDTH + B_V_WIDTH
    for rows in _row_parts(tm):
        x = x_ref[rows, :]
        gates = _sigmoid(_dot(x, wgate_ref[...]) + bgate_ref[...])
        merged = (gates[:, :d] * _dot(ya_ref[rows, :], wb_ref[:b_lo, :])
                  + gates[:, d:2 * d] * _dot(yb_ref[rows, :], wb_ref[b_lo:c_lo, :])
                  + gates[:, 2 * d:] * _dot(yc_ref[rows, :], wb_ref[c_lo:, :]))
        o_ref[rows, :] = _layer_norm(alpha * x + _dot(merged, wout_ref[...]), g_ref[...], b_ref[...])


def _merge(x2d, ya, yb, yc, consts, l, alpha, tm):
    m, d = x2d.shape
    tile = lambda n: pl.BlockSpec((tm, n), lambda i: (i, 0))
    return pl.pallas_call(
        functools.partial(_merge_kernel, alpha),
        grid=(m // tm,),
        in_specs=[tile(d), tile(A_WIDTH), tile(B_V_WIDTH), tile(C_WIDTH)] + [_layer_spec(a, l) for a in consts],
        out_specs=tile(d),
        out_shape=jax.ShapeDtypeStruct((m, d), F32),
        compiler_params=_cparams("parallel"),
        name="merge",
    )(x2d, ya, yb, yc, *consts)


def _ffn_kernel(alpha, tf, x_ref, w1_ref, w2_ref, g_ref, b_ref, o_ref):
    for rows in _row_parts(x_ref.shape[0]):
        x = x_ref[rows, :]
        xb = x.astype(BF16)
        acc = None
        for j in range(w1_ref.shape[1] // tf):
            h = jnp.maximum(jnp.dot(xb, w1_ref[:, j * tf:(j + 1) * tf], preferred_element_type=F32), 0.0)
            part = jnp.dot((h * h).astype(BF16), w2_ref[j * tf:(j + 1) * tf, :], preferred_element_type=F32)
            acc = part if acc is None else acc + part
        o_ref[rows, :] = _layer_norm(alpha * x + acc, g_ref[...], b_ref[...])


def _ffn(x2d, consts, l, alpha, tm, tf):
    m, d = x2d.shape
    return pl.pallas_call(
        functools.partial(_ffn_kernel, alpha, tf),
        grid=(m // tm,),
        in_specs=[pl.BlockSpec((tm, d), lambda i: (i, 0))] + [_layer_spec(a, l) for a in consts],
        out_specs=pl.BlockSpec((tm, d), lambda i: (i, 0)),
        out_shape=jax.ShapeDtypeStruct((m, d), F32),
        compiler_params=_cparams("parallel"),
        name="ffn",
    )(x2d, *consts)


def _tile(n, want):
    t = min(n, want)
    assert n % t == 0, (n, want)
    return t


def kernel(x, w_in, b_gate, a_shift, a_w0, a_w2, a_a0, a_a2, a_g2, a_kk, a_ka, a_rk, a_lnx_g, a_lnx_b, c_lam_re, c_lam_im, c_log_dt, c_b_re, c_b_im, c_c_re, c_c_im, c_d, c_w_glu, c_b_glu, w_branch, w_out, ln1_g, ln1_b, w_ff1, w_ff2, ln2_g, ln2_b):
    bsz, seq, d = x.shape
    depth = w_in.shape[0]
    alpha = (2.0 * depth) ** 0.25
    tokens = bsz * seq
    rows = lambda a: a.reshape(depth, 1, -1)
    rwkv_params = _rwkv_params(dict(a_shift=a_shift, a_w0=a_w0, a_w2=a_w2, a_a0=a_a0, a_a2=a_a2, a_g2=a_g2,
                                    a_kk=a_kk, a_ka=a_ka, a_rk=a_rk, a_lnx_g=a_lnx_g, a_lnx_b=a_lnx_b))
    s5_params = _s5_params(dict(c_lam_re=c_lam_re, c_lam_im=c_lam_im, c_log_dt=c_log_dt, c_b_re=c_b_re,
                                c_b_im=c_b_im, c_c_re=c_c_re, c_c_im=c_c_im, c_d=c_d, c_w_glu=c_w_glu,
                                c_b_glu=c_b_glu))
    rope_tabs, ret_tables = _retention_tables(seq)
    w_in_bf = w_in.astype(BF16)
    w_za, w_rest = w_in_bf[:, :, :A_PROJ], w_in_bf[:, :, A_PROJ:MIX_COLS]
    merge_consts = [w_in_bf[:, :, MIX_COLS:], rows(b_gate), w_branch.astype(BF16), w_out.astype(BF16),
                    rows(ln1_g), rows(ln1_b)]
    ffn_consts = [w_ff1.astype(BF16), w_ff2.astype(BF16), rows(ln2_g), rows(ln2_b)]
    xt = x.reshape(tokens, d)
    for l in range(depth):
        zmix, ya = _proj_rwkv(xt, w_za, w_rest, rwkv_params, rope_tabs, bsz, seq, l, _tile(seq, 512))
        zmix = zmix.reshape(bsz, seq, REST_COLS)
        yb = _retention(zmix, bsz, seq, ret_tables, _tile(seq, 512))
        yc = _s5(zmix, bsz, seq, s5_params, l, _tile(seq, 64))
        x1 = _merge(xt, ya.reshape(tokens, -1), yb.reshape(tokens, -1), yc.reshape(tokens, -1),
                    merge_consts, l, alpha, _tile(tokens, 512))
        xt = _ffn(x1, ffn_consts, l, alpha, _tile(tokens, 1024), 1024)
    return xt.reshape(bsz, seq, d)
```

```python
import functools
import math

import jax
import jax.numpy as jnp
from jax import lax
from jax.experimental import pallas as pl
from jax.experimental.pallas import tpu as pltpu

F32 = jnp.float32
BF16 = jnp.bfloat16

A_HEADS = 8
A_HEAD_DIM = 64
A_WIDTH = A_HEADS * A_HEAD_DIM
A_DECAY_LORA = 64
A_ICLR_LORA = 64
A_GATE_LORA = 128
A_PROJ = 3 * A_WIDTH + A_DECAY_LORA + A_ICLR_LORA + A_GATE_LORA
A_GN_EPS = 64e-5
A_CHUNK = 64
A_GROUP = 8
A_GLANES = 128
A_SUMLANES = 256
A_LORA_IN = A_DECAY_LORA + A_ICLR_LORA
PROJ_CHUNK = 512
PROJ_EVERY = 4

B_HEADS = 4
B_QK_DIM = 128
B_V_DIM = 256
B_QK_WIDTH = B_HEADS * B_QK_DIM
B_V_WIDTH = B_HEADS * B_V_DIM
B_CHUNK = 128
B_ROPE_BASE = 10000.0
B_GN_EPS = 1e-5

C_WIDTH = 512
C_GROUP = 16
C_GROUPS = C_WIDTH // C_GROUP
C_STATE = 64
C_LANES = C_GROUPS * C_STATE
C_BLOCKS = 4
C_ULANES = C_WIDTH // C_BLOCKS
C_SLANES = C_LANES // C_BLOCKS
C_PARTS = 4

LN_EPS = 1e-5

RET_COLS = 2 * B_QK_WIDTH + 2 * B_V_WIDTH
REST_COLS = RET_COLS + C_WIDTH
MIX_COLS = A_PROJ + REST_COLS
U_BLK = RET_COLS // C_WIDTH
assert RET_COLS % C_WIDTH == 0

V7X_VMEM_LIMIT_BYTES = 56 * 1024 * 1024


def _cparams(*sem):
    return pltpu.CompilerParams(dimension_semantics=sem, vmem_limit_bytes=V7X_VMEM_LIMIT_BYTES)


def _full(shape):
    n = len(shape)
    return pl.BlockSpec(shape, lambda *_: (0,) * n)


def _layer_spec(a, l):
    n = a.ndim - 1
    return pl.BlockSpec((None,) + a.shape[1:], lambda *_: (l,) + (0,) * n, pipeline_mode=pl.Buffered(1))


def _dot(a, b):
    return jnp.dot(a.astype(BF16), b.astype(BF16), preferred_element_type=F32)


def _dot_nt(a, b):
    return lax.dot_general(a.astype(BF16), b.astype(BF16), (((1,), (1,)), ((), ())),
                           preferred_element_type=F32)


def _dot_tn(a, b):
    return lax.dot_general(a.astype(BF16), b.astype(BF16), (((0,), (0,)), ((), ())),
                           preferred_element_type=F32)


def _split3(x):
    hi = x.astype(BF16)
    r1 = x - hi.astype(F32)
    mid = r1.astype(BF16)
    lo = (r1 - mid.astype(F32)).astype(BF16)
    return hi, mid, lo


def _sigmoid(x):
    return 1.0 / (1.0 + jnp.exp(-x))


def _row_parts(n, parts=2):
    step = n // parts
    return [slice(i * step, (i + 1) * step) for i in range(parts)]


def _layer_norm(y, g, b):
    mu = jnp.mean(y, axis=-1, keepdims=True)
    d = y - mu
    var = jnp.mean(d * d, axis=-1, keepdims=True)
    return d * lax.rsqrt(var + LN_EPS) * g + b


def _chunk_rows(c):
    return pl.ds(c * A_CHUNK, A_CHUNK)


def _proj_rwkv_kernel(x_ref, wza_ref, wrest_ref, mu_ref, w0_ref, a0_ref, lora_ref, g2_ref, kkp_ref, kap_ref,
                      rkp_ref, lng_ref, lnb_ref, ones_ref, tri_ref, cos_ref, sin_ref,
                      rest_ref, y_ref,
                      xb_s, carry_s, state_s, r_s, k_s, v_s, kn_s, b0_s, lw_s, gate_s, bonus_s, o_s,
                      gam_s, x_s, m2_s, q_s, op_s):
    tm = x_ref.shape[0]
    n_chunks = tm // A_CHUNK
    w = A_WIDTH

    @pl.when(pl.program_id(1) == 0)
    def _():
        carry_s[...] = jnp.zeros_like(carry_s)
        state_s[...] = jnp.zeros_like(state_s)

    xb_s[...] = x_ref[...].astype(BF16)
    z = jnp.dot(xb_s[...], wza_ref[...], preferred_element_type=F32)

    cos2 = cos_ref[...]
    sin2 = sin_ref[...]

    def rope(t):
        return t * cos2 + pltpu.roll(t, B_QK_DIM // 2, 1) * sin2

    k_off, v_off, g_off = B_QK_WIDTH, 2 * B_QK_WIDTH, 2 * B_QK_WIDTH + B_V_WIDTH

    def piece(lo):
        hi = lo + PROJ_CHUNK
        val = jnp.dot(xb_s[...], wrest_ref[:, lo:hi], preferred_element_type=F32)
        if lo < v_off:
            scale = 1.0 if lo < k_off else B_QK_DIM ** -0.5
            for j in range(PROJ_CHUNK // B_QK_DIM):
                hs = slice(j * B_QK_DIM, (j + 1) * B_QK_DIM)
                rest_ref[:, lo + j * B_QK_DIM:lo + (j + 1) * B_QK_DIM] = (
                    rope(val[:, hs]) * scale).astype(rest_ref.dtype)
        elif g_off <= lo < RET_COLS:
            rest_ref[:, lo:hi] = (val * _sigmoid(val)).astype(rest_ref.dtype)
        else:
            rest_ref[:, lo:hi] = val.astype(rest_ref.dtype)

    pending = list(range(0, REST_COLS, PROJ_CHUNK))
    calls = [0]

    def fill(n=1):
        for _ in range(n):
            if pending and calls[0] % PROJ_EVERY == 0:
                piece(pending.pop(0))
            calls[0] += 1

    def flush():
        while pending:
            piece(pending.pop(0))

    ones = ones_ref[...]
    gw = A_GLANES
    groups = [slice(g * gw, (g + 1) * gw) for g in range(w // gw)]
    slabs = [slice(g * A_SUMLANES, (g + 1) * A_SUMLANES) for g in range(w // A_SUMLANES)]

    def seg_sum(t):
        tb = t.astype(BF16)
        return jnp.concatenate([jnp.dot(tb[:, s], ones, preferred_element_type=F32) for s in slabs], axis=1)

    rolled = pltpu.roll(z, 1, 0)
    rowid = lax.broadcasted_iota(jnp.int32, z.shape, 0)
    prev = jnp.where(rowid == 0, jnp.broadcast_to(carry_s[0:1, :], z.shape), rolled)
    zs = z + mu_ref[...] * (prev - z)
    carry_s[0:1, :] = z[tm - 1:tm, :]
    fill()

    lz = zs[:, 3 * w:3 * w + A_LORA_IN]
    lane = lax.broadcasted_iota(jnp.int32, lz.shape, 1)
    lin = jnp.where(lane < A_DECAY_LORA, jnp.tanh(lz), lz)
    wa = _dot(lin, lora_ref[...])
    lw_s[...] = (-math.exp(-0.5)) * _sigmoid(w0_ref[...] + wa[:, :w])
    ia = _sigmoid(a0_ref[...] + wa[:, w:])
    gate_s[...] = _dot(_sigmoid(zs[:, 3 * w + A_LORA_IN:3 * w + A_LORA_IN + A_GATE_LORA]), g2_ref[...])

    r = zs[:, :w]
    k = zs[:, w:2 * w]
    v = zs[:, 2 * w:3 * w]
    kk = k * kkp_ref[...]
    kn = kk * lax.rsqrt(jnp.maximum(seg_sum(kk * kk), 1e-24))
    kmod = k * (1.0 + (ia - 1.0) * kap_ref[...])
    r_s[...] = r
    k_s[...] = kmod
    v_s[...] = v
    kn_s[...] = kn
    b0_s[...] = kn * ia
    bonus_s[...] = seg_sum(r * kmod * rkp_ref[...]) * v
    fill()

    tri = tri_ref[...]
    hpg = gw // A_HEAD_DIM
    rid = lax.broadcasted_iota(jnp.int32, (A_CHUNK, gw), 0)
    cid = lax.broadcasted_iota(jnp.int32, (A_CHUNK, gw), 1) % A_HEAD_DIM
    strict = rid > cid
    incl = rid >= cid
    eye = (rid == cid).astype(F32)
    brow = lax.broadcasted_iota(jnp.int32, (gw, gw), 0) // A_HEAD_DIM
    bcol = lax.broadcasted_iota(jnp.int32, (gw, gw), 1) // A_HEAD_DIM
    same_head = brow == bcol
    same_head_bf = same_head.astype(BF16)
    n = A_HEAD_DIM

    def bd(x):
        xb = x.astype(BF16)
        return jnp.concatenate([xb] * hpg, axis=0) * same_head_bf


    group = min(A_GROUP, n_chunks)
    assert n_chunks % group == 0

    def state_free_part(gi):
        items = []
        for j in range(group):
            c = gi * group + j
            rows = _chunk_rows(c)
            lw = lw_s[rows, :]
            h3 = _split3(lw)
            cum = (jnp.dot(tri, h3[0], preferred_element_type=F32)
                   + jnp.dot(tri, h3[1], preferred_element_type=F32)
                   + jnp.dot(tri, h3[2], preferred_element_type=F32))
            e_in = jnp.exp(cum)
            e_ex = jnp.exp(cum - lw)
            e_ng = jnp.exp(-cum)
            rt = r_s[rows, :] * e_in
            at = -kn_s[rows, :] * e_ex
            bt = b0_s[rows, :] * e_ng
            kt = k_s[rows, :] * e_ng
            vv = v_s[rows, :]
            gam_s[pl.ds(c, 1), :] = e_in[A_CHUNK - 1:A_CHUNK, :]
            for g, s in enumerate(groups):
                items.append((c * len(groups) + g, at[:, s], rt[:, s], bt[:, s], kt[:, s], vv[:, s]))
            if j % 2 == 1:
                fill()
        ids = range(len(items))
        idx = [it[0] for it in items]
        a_ = [it[1] for it in items]
        r_ = [it[2] for it in items]
        b_ = [it[3] for it in items]
        k_ = [it[4] for it in items]
        v_ = [it[5] for it in items]
        ar = [jnp.concatenate([a_[i], r_[i]], axis=0) for i in ids]
        gb = [_dot_nt(ar[i], bd(b_[i])) for i in ids]
        fill()
        gk = [_dot_nt(ar[i], bd(k_[i])) for i in ids]
        fill()
        l_ab = [jnp.where(strict, gb[i][:A_CHUNK], 0.0) for i in ids]
        a_qb = [jnp.where(incl, gb[i][A_CHUNK:], 0.0) for i in ids]
        akq = [jnp.concatenate([jnp.where(strict, gk[i][:A_CHUNK], 0.0),
                                jnp.where(incl, gk[i][A_CHUNK:], 0.0)], axis=0) for i in ids]
        akqv = [_dot(akq[i], bd(v_[i])) for i in ids]
        fill()
        akv = [akqv[i][:A_CHUNK] for i in ids]
        ov = [akqv[i][A_CHUNK:] for i in ids]
        tinv = [eye + l_ab[i] for i in ids]
        p = [_dot(l_ab[i], bd(l_ab[i])) for i in ids]
        fill()
        for _ in range(4):
            tp = [_dot(jnp.concatenate([tinv[i], p[i]], axis=0), bd(p[i])) for i in ids]
            fill()
            tinv = [tinv[i] + tp[i][:A_CHUNK] for i in ids]
            p = [tp[i][A_CHUNK:] for i in ids]
        tinv = [tinv[i] + _dot(tinv[i], bd(p[i])) for i in ids]
        fill()
        wu = [_dot(tinv[i], jnp.concatenate([bd(a_[i]), bd(akv[i])], axis=1)) for i in ids]
        fill()
        wm = [wu[i][:, :gw] for i in ids]
        uv = [wu[i][:, gw:] for i in ids]
        for i in ids:
            x_s[idx[i]] = jnp.where(same_head, _dot_tn(wm[i], b_[i]), 0.0)
        fill()
        for i in ids:
            m2 = jnp.where(same_head, _dot_tn(jnp.concatenate([uv[i], v_[i]], axis=0),
                                              jnp.concatenate([b_[i], k_[i]], axis=0)), 0.0)
            m2_s[idx[i]] = sum(m2[h * n:(h + 1) * n] for h in range(1, hpg)) + m2[:n]
        fill()
        qo = [_dot(a_qb[i], jnp.concatenate([bd(wm[i]), bd(uv[i])], axis=1)) for i in ids]
        fill()
        for i in ids:
            q_s[idx[i]] = r_[i] + qo[i][:, :gw]
        for i in ids:
            op_s[idx[i]] = qo[i][:, gw:] + ov[i]

    def state_part(c):
        rows = _chunk_rows(c)
        gam = gam_s[pl.ds(c, 1), :]
        gids = range(len(groups))
        s0 = [state_s[g] for g in gids]
        sx = [_dot(s0[g], x_s[c * len(groups) + g]) for g in gids]
        o = [_dot_nt(q_s[c * len(groups) + g], bd(s0[g])) + op_s[c * len(groups) + g] for g in gids]
        for g in gids:
            state_s[g] = (s0[g] + sx[g] + m2_s[c * len(groups) + g]) * gam[:, groups[g]]
        o_s[rows, :] = jnp.concatenate(o, axis=1)
        fill()

    for gi in range(n_chunks // group):
        state_free_part(gi)
    for c in range(n_chunks):
        state_part(c)
    flush()

    o = o_s[...]
    inv_n = 1.0 / n
    mean = seg_sum(o) * inv_n
    d = o - mean
    var = seg_sum(d * d) * inv_n
    on = d * lax.rsqrt(var + A_GN_EPS) * lng_ref[...] + lnb_ref[...]
    y_ref[...] = ((on + bonus_s[...]) * gate_s[...]).astype(y_ref.dtype)


def _rwkv_params(p):
    depth = p["a_w0"].shape[0]
    row = lambda a: a.reshape(depth, 1, -1).astype(F32)
    w = A_WIDTH
    lora = jnp.zeros((depth, A_LORA_IN, 2 * w), F32)
    lora = lora.at[:, :A_DECAY_LORA, :w].set(p["a_w2"]).at[:, A_DECAY_LORA:, w:].set(p["a_a2"]).astype(BF16)
    hid = jnp.arange(A_SUMLANES) // A_HEAD_DIM
    ones = (hid[:, None] == hid[None, :]).astype(BF16)
    ti = jnp.arange(A_CHUNK)
    tri = (ti[:, None] >= ti[None, :]).astype(BF16)
    layered = [row(p["a_shift"]), row(p["a_w0"]), row(p["a_a0"]), lora, p["a_g2"].astype(BF16),
               row(p["a_kk"]), row(p["a_ka"]), row(p["a_rk"]), row(p["a_lnx_g"]), row(p["a_lnx_b"])]
    return layered, [ones, tri]


def _proj_rwkv(x2d, w_za, w_rest, params, rope_tabs, bsz, seq, l, tm):
    layered, shared = params
    cos2, sin2 = rope_tabs
    k = x2d.shape[1]
    w = A_WIDTH
    per_seq = seq // tm
    n_chunks = tm // A_CHUNK
    n_groups = w // A_GLANES
    row_tile = lambda n: pl.BlockSpec((tm, n), lambda b, i: (b * per_seq + i, 0))
    consts = [w_za, w_rest] + layered
    stage = pltpu.VMEM((tm, w), F32)
    per_sq = pltpu.VMEM((n_chunks * n_groups, A_GLANES, A_GLANES), F32)
    per_row = pltpu.VMEM((n_chunks * n_groups, A_CHUNK, A_GLANES), F32)
    return pl.pallas_call(
        _proj_rwkv_kernel,
        grid=(bsz, per_seq),
        in_specs=[row_tile(k)] + [_layer_spec(a, l) for a in consts] + [_full(a.shape) for a in shared]
                 + [pl.BlockSpec((tm, B_QK_DIM), lambda b, i: (i, 0))] * 2,
        out_specs=[row_tile(REST_COLS), row_tile(w)],
        out_shape=[jax.ShapeDtypeStruct((bsz * seq, REST_COLS), BF16),
                   jax.ShapeDtypeStruct((bsz * seq, w), BF16)],
        scratch_shapes=[pltpu.VMEM((tm, k), BF16),
                        pltpu.VMEM((8, A_PROJ), F32),
                        pltpu.VMEM((n_groups, A_HEAD_DIM, A_GLANES), F32),
                        stage, stage, stage, stage, stage, stage, stage, stage, stage,
                        pltpu.VMEM((max(8, n_chunks), w), F32),
                        per_sq, per_row, per_row, per_row],
        compiler_params=_cparams("parallel", "arbitrary"),
        name="proj_rwkv7",
    )(x2d, *consts, *shared, cos2, sin2)


def _ret_kernel(z_ref, dmask_ref, qd_ref, kd_ref, cd_ref, y_ref, state_s, upd_s, st_s):
    tc = z_ref.shape[0]
    n_chunks = tc // B_CHUNK
    k_off, v_off, g_off = B_QK_WIDTH, 2 * B_QK_WIDTH, 2 * B_QK_WIDTH + B_V_WIDTH

    @pl.when(pl.program_id(1) == 0)
    def _():
        state_s[...] = jnp.zeros_like(state_s)

    items = [(c, h) for c in range(n_chunks) for h in range(B_HEADS)]
    rows = lambda c: slice(c * B_CHUNK, (c + 1) * B_CHUNK)
    qcol = lambda h: slice(h * B_QK_DIM, (h + 1) * B_QK_DIM)
    kcol = lambda h: slice(k_off + h * B_QK_DIM, k_off + (h + 1) * B_QK_DIM)
    vcol = lambda h: slice(v_off + h * B_V_DIM, v_off + (h + 1) * B_V_DIM)
    gcol = lambda h: slice(g_off + h * B_V_DIM, g_off + (h + 1) * B_V_DIM)

    for c, h in items:
        upd_s[c * B_HEADS + h] = _dot_tn(z_ref[rows(c), kcol(h)] * kd_ref[h], z_ref[rows(c), vcol(h)])
    for h in range(B_HEADS):
        st = state_s[h]
        for c in range(n_chunks):
            st_s[c * B_HEADS + h] = st
            st = cd_ref[h] * st + upd_s[c * B_HEADS + h]
        state_s[h] = st

    for c, h in items:
        qc = z_ref[rows(c), qcol(h)]
        scores = _dot_nt(qc, z_ref[rows(c), kcol(h)]) * dmask_ref[h]
        o = _dot(scores, z_ref[rows(c), vcol(h)]) + _dot(qc * qd_ref[h], st_s[c * B_HEADS + h])
        mu = jnp.mean(o, axis=-1, keepdims=True)
        d = o - mu
        var = jnp.mean(d * d, axis=-1, keepdims=True)
        on = d * lax.rsqrt(var + B_GN_EPS)
        y_ref[rows(c), h * B_V_DIM:(h + 1) * B_V_DIM] = (
            z_ref[rows(c), gcol(h)].astype(F32) * on).astype(y_ref.dtype)


def _retention_tables(seq):
    f32 = F32
    pos = jnp.arange(seq, dtype=f32)
    half = B_QK_DIM // 2
    inv_freq = B_ROPE_BASE ** (-jnp.arange(half, dtype=f32) / half)
    ang = pos[:, None] * inv_freq[None, :]
    cos, sin = jnp.cos(ang), jnp.sin(ang)
    cos2 = jnp.concatenate([cos, cos], axis=1)
    sin2 = jnp.concatenate([-sin, sin], axis=1)
    log_gamma = jnp.log(1.0 - 2.0 ** (-5.0 - jnp.arange(B_HEADS, dtype=f32)))
    idx = jnp.arange(B_CHUNK, dtype=f32)
    rel = idx[:, None] - idx[None, :]
    dmask = jnp.where(rel >= 0, jnp.exp(log_gamma[:, None, None] * jnp.maximum(rel, 0.0)), 0.0)
    qd = jnp.broadcast_to(jnp.exp(log_gamma[:, None] * (idx + 1.0))[:, :, None],
                          (B_HEADS, B_CHUNK, B_QK_DIM))
    kd = jnp.broadcast_to(jnp.exp(log_gamma[:, None] * (B_CHUNK - 1.0 - idx))[:, :, None],
                          (B_HEADS, B_CHUNK, B_QK_DIM))
    cd = jnp.broadcast_to(jnp.exp(log_gamma * B_CHUNK)[:, None, None], (B_HEADS, B_QK_DIM, B_V_DIM))
    return (cos2, sin2), [dmask, qd.astype(BF16), kd.astype(BF16), cd]


def _retention(zmix, bsz, seq, tabs, tc):
    n_items = (tc // B_CHUNK) * B_HEADS
    per_item = pltpu.VMEM((n_items, B_QK_DIM, B_V_DIM), F32)
    return pl.pallas_call(
        _ret_kernel,
        grid=(bsz, seq // tc),
        in_specs=[pl.BlockSpec((None, tc, RET_COLS), lambda b, i: (b, i, 0))] + [_full(a.shape) for a in tabs],
        out_specs=pl.BlockSpec((None, tc, B_V_WIDTH), lambda b, i: (b, i, 0)),
        out_shape=jax.ShapeDtypeStruct((bsz, seq, B_V_WIDTH), BF16),
        scratch_shapes=[pltpu.VMEM((B_HEADS, B_QK_DIM, B_V_DIM), F32), per_item, per_item],
        compiler_params=_cparams("parallel", "arbitrary"),
        name="retention",
    )(zmix, *tabs)


def _s5_kernel(u_ref, bbr_ref, bbi_ref, ar_ref, ai_ref, cr_ref, ci_ref, d_ref, wg_ref, bg_ref,
               y_ref, xr_s, xi_s, u_s, sr_s, si_s):
    bsz, tt, _ = u_ref.shape
    tp = tt // C_PARTS
    rp = tp * bsz

    @pl.when(pl.program_id(0) == 0)
    def _():
        sr_s[...] = jnp.zeros_like(sr_s)
        si_s[...] = jnp.zeros_like(si_s)

    blk = [slice(m * C_SLANES, (m + 1) * C_SLANES) for m in range(C_BLOCKS)]


    def drive(p):
        rows = slice(p * rp, (p + 1) * rp)
        u = pltpu.einshape("btc->tbc", u_ref[:, p * tp:(p + 1) * tp, :].astype(F32)).reshape(rp, C_WIDTH)
        u_s[rows, :] = u
        ub = u.astype(BF16)
        for m in range(C_BLOCKS):
            um = ub[:, m * C_ULANES:(m + 1) * C_ULANES]
            xr_s[rows, blk[m]] = jnp.dot(um, bbr_ref[m], preferred_element_type=F32)
            xi_s[rows, blk[m]] = jnp.dot(um, bbi_ref[m], preferred_element_type=F32)

    def scan(p):
        for m in range(C_BLOCKS):
            cols = blk[m]
            ar = jnp.broadcast_to(ar_ref[:, cols], (bsz, C_SLANES))
            ai = jnp.broadcast_to(ai_ref[:, cols], (bsz, C_SLANES))
            xr, xi = sr_s[:, cols], si_s[:, cols]
            for t in range(p * tp, (p + 1) * tp):
                rows = slice(t * bsz, (t + 1) * bsz)
                xr, xi = (ar * xr - ai * xi + xr_s[rows, cols], ar * xi + ai * xr + xi_s[rows, cols])
                xr_s[rows, cols] = xr
                xi_s[rows, cols] = xi
            sr_s[:, cols] = xr
            si_s[:, cols] = xi

    def readout(p):
        rows = slice(p * rp, (p + 1) * rp)
        parts = [_dot(xr_s[rows, blk[m]], cr_ref[m]) - _dot(xi_s[rows, blk[m]], ci_ref[m])
                 for m in range(C_BLOCKS)]
        y = jnp.concatenate(parts, axis=1) + d_ref[...] * u_s[rows, :]
        y = jax.nn.gelu(y)
        y = y * _sigmoid(_dot(y, wg_ref[...]) + bg_ref[...])
        y_ref[:, p * tp:(p + 1) * tp, :] = pltpu.einshape(
            "tbc->btc", y.reshape(tp, bsz, C_WIDTH)).astype(y_ref.dtype)

    drive(0)
    for p in range(C_PARTS):
        if p + 1 < C_PARTS:
            drive(p + 1)
        scan(p)
        readout(p)


def _s5_params(p):
    f32 = F32
    depth = p["c_log_dt"].shape[0]
    dt = jnp.exp(p["c_log_dt"].astype(f32))[..., None]
    lr, li = p["c_lam_re"].astype(f32), p["c_lam_im"].astype(f32)
    mag = jnp.exp(lr * dt)
    ab_re, ab_im = mag * jnp.cos(li * dt), mag * jnp.sin(li * dt)
    den = lr * lr + li * li
    f_re = ((ab_re - 1.0) * lr + ab_im * li) / den
    f_im = (ab_im * lr - (ab_re - 1.0) * li) / den
    bre, bim = p["c_b_re"].astype(f32), p["c_b_im"].astype(f32)
    bb_re = f_re[..., None] * bre - f_im[..., None] * bim
    bb_im = f_re[..., None] * bim + f_im[..., None] * bre
    gpb = C_GROUPS // C_BLOCKS
    eye = jnp.eye(gpb, dtype=f32)

    def in_blocks(bb):
        bb = bb.reshape(depth, C_BLOCKS, gpb, C_STATE, C_GROUP)
        return jnp.einsum("lmgpc,gh->lmgchp", bb, eye).reshape(depth, C_BLOCKS, C_ULANES, C_SLANES).astype(BF16)

    def out_blocks(cc):
        cc = cc.astype(f32).reshape(depth, C_BLOCKS, gpb, C_GROUP, C_STATE)
        return jnp.einsum("lmgcp,gh->lmgphc", cc, eye).reshape(depth, C_BLOCKS, C_SLANES, C_ULANES).astype(BF16)

    return [in_blocks(bb_re), in_blocks(bb_im), ab_re.reshape(depth, 1, C_LANES), ab_im.reshape(depth, 1, C_LANES),
            out_blocks(p["c_c_re"]), out_blocks(p["c_c_im"]), p["c_d"].reshape(depth, 1, C_WIDTH).astype(f32),
            p["c_w_glu"].astype(BF16), p["c_b_glu"].reshape(depth, 1, C_WIDTH).astype(f32)]


def _s5(zmix, bsz, seq, consts, l, tt):
    rows = tt * bsz
    return pl.pallas_call(
        _s5_kernel,
        grid=(seq // tt,),
        in_specs=[pl.BlockSpec((bsz, tt, C_WIDTH), lambda i: (0, i, U_BLK))] + [_layer_spec(a, l) for a in consts],
        out_specs=pl.BlockSpec((bsz, tt, C_WIDTH), lambda i: (0, i, 0)),
        out_shape=jax.ShapeDtypeStruct((bsz, seq, C_WIDTH), BF16),
        scratch_shapes=[pltpu.VMEM((rows, C_LANES), F32), pltpu.VMEM((rows, C_LANES), F32),
                        pltpu.VMEM((rows, C_WIDTH), F32),
                        pltpu.VMEM((bsz, C_LANES), F32), pltpu.VMEM((bsz, C_LANES), F32)],
        compiler_params=_cparams("arbitrary"),
        name="s5",
    )(zmix, *consts)


def _merge_kernel(alpha, x_ref, ya_ref, yb_ref, yc_ref, wgate_ref, bgate_ref, wb_ref,
                  wout_ref, g_ref, b_ref, o_ref):
    tm, d = x_ref.shape
    b_lo, c_lo = A_WIDTH, A_WIDTH + B_V_WIDTH
    for rows in _row_parts(tm):
        x = x_ref[rows, :]
        gates = _sigmoid(_dot(x, wgate_ref[...]) + bgate_ref[...])
        merged = (gates[:, :d] * _dot(ya_ref[rows, :], wb_ref[:b_lo, :])
                  + gates[:, d:2 * d] * _dot(yb_ref[rows, :], wb_ref[b_lo:c_lo, :])
                  + gates[:, 2 * d:] * _dot(yc_ref[rows, :], wb_ref[c_lo:, :]))
        o_ref[rows, :] = _layer_norm(alpha * x + _dot(merged, wout_ref[...]), g_ref[...], b_ref[...])


def _merge(x2d, ya, yb, yc, consts, l, alpha, tm):
    m, d = x2d.shape
    tile = lambda n: pl.BlockSpec((tm, n), lambda i: (i, 0))
    return pl.pallas_call(
        functools.partial(_merge_kernel, alpha),
        grid=(m // tm,),
        in_specs=[tile(d), tile(A_WIDTH), tile(B_V_WIDTH), tile(C_WIDTH)] + [_layer_spec(a, l) for a in consts],
        out_specs=tile(d),
        out_shape=jax.ShapeDtypeStruct((m, d), F32),
        compiler_params=_cparams("parallel"),
        name="merge",
    )(x2d, ya, yb, yc, *consts)


def _ffn_kernel(alpha, tf, x_ref, w1_ref, w2_ref, g_ref, b_ref, o_ref):
    x = x_ref[...]
    xb = x.astype(BF16)
    acc = None
    for j in range(w1_ref.shape[1] // tf):
        h = jnp.maximum(jnp.dot(xb, w1_ref[:, j * tf:(j + 1) * tf], preferred_element_type=F32), 0.0)
        part = jnp.dot((h * h).astype(BF16), w2_ref[j * tf:(j + 1) * tf, :], preferred_element_type=F32)
        acc = part if acc is None else acc + part
    o_ref[...] = _layer_norm(alpha * x + acc, g_ref[...], b_ref[...])


def _ffn(x2d, consts, l, alpha, tm, tf):
    m, d = x2d.shape
    return pl.pallas_call(
        functools.partial(_ffn_kernel, alpha, tf),
        grid=(m // tm,),
        in_specs=[pl.BlockSpec((tm, d), lambda i: (i, 0))] + [_layer_spec(a, l) for a in consts],
        out_specs=pl.BlockSpec((tm, d), lambda i: (i, 0)),
        out_shape=jax.ShapeDtypeStruct((m, d), F32),
        compiler_params=_cparams("parallel"),
        name="ffn",
    )(x2d, *consts)


def _tile(n, want):
    t = min(n, want)
    assert n % t == 0, (n, want)
    return t


def kernel(x, w_in, b_gate, a_shift, a_w0, a_w2, a_a0, a_a2, a_g2, a_kk, a_ka, a_rk, a_lnx_g, a_lnx_b, c_lam_re, c_lam_im, c_log_dt, c_b_re, c_b_im, c_c_re, c_c_im, c_d, c_w_glu, c_b_glu, w_branch, w_out, ln1_g, ln1_b, w_ff1, w_ff2, ln2_g, ln2_b):
    bsz, seq, d = x.shape
    depth = w_in.shape[0]
    alpha = (2.0 * depth) ** 0.25
    tokens = bsz * seq
    rows = lambda a: a.reshape(depth, 1, -1)
    rwkv_params = _rwkv_params(dict(a_shift=a_shift, a_w0=a_w0, a_w2=a_w2, a_a0=a_a0, a_a2=a_a2, a_g2=a_g2,
                                    a_kk=a_kk, a_ka=a_ka, a_rk=a_rk, a_lnx_g=a_lnx_g, a_lnx_b=a_lnx_b))
    s5_params = _s5_params(dict(c_lam_re=c_lam_re, c_lam_im=c_lam_im, c_log_dt=c_log_dt, c_b_re=c_b_re,
                                c_b_im=c_b_im, c_c_re=c_c_re, c_c_im=c_c_im, c_d=c_d, c_w_glu=c_w_glu,
                                c_b_glu=c_b_glu))
    rope_tabs, ret_tables = _retention_tables(seq)
    w_za = w_in[:, :, :A_PROJ].astype(BF16)
    w_rest = w_in[:, :, A_PROJ:MIX_COLS].astype(BF16)
    merge_consts = [w_in[:, :, MIX_COLS:].astype(BF16), rows(b_gate), w_branch.astype(BF16), w_out.astype(BF16),
                    rows(ln1_g), rows(ln1_b)]
    ffn_consts = [w_ff1.astype(BF16), w_ff2.astype(BF16), rows(ln2_g), rows(ln2_b)]
    xt = x.reshape(tokens, d)
    for l in range(depth):
        zmix, ya = _proj_rwkv(xt, w_za, w_rest, rwkv_params, rope_tabs, bsz, seq, l, _tile(seq, 512))
        zmix = zmix.reshape(bsz, seq, REST_COLS)
        yb = _retention(zmix, bsz, seq, ret_tables, _tile(seq, 1024))
        yc = _s5(zmix, bsz, seq, s5_params, l, _tile(seq, 64))
        x1 = _merge(xt, ya.reshape(tokens, -1), yb.reshape(tokens, -1), yc.reshape(tokens, -1),
                    merge_consts, l, alpha, _tile(tokens, 512))
        xt = _ffn(x1, ffn_consts, l, alpha, _tile(tokens, 1024), 1024)
    return xt.reshape(bsz, seq, d)
```

```python
import functools
import math

import jax
import jax.numpy as jnp
from jax import lax
from jax.experimental import pallas as pl
from jax.experimental.pallas import tpu as pltpu

F32 = jnp.float32
BF16 = jnp.bfloat16

A_HEADS = 8
A_HEAD_DIM = 64
A_WIDTH = A_HEADS * A_HEAD_DIM
A_DECAY_LORA = 64
A_ICLR_LORA = 64
A_GATE_LORA = 128
A_PROJ = 3 * A_WIDTH + A_DECAY_LORA + A_ICLR_LORA + A_GATE_LORA
A_GN_EPS = 64e-5
A_CHUNK = 64
A_GROUP = 8
A_GLANES = 128
A_SUMLANES = 256
A_LORA_IN = A_DECAY_LORA + A_ICLR_LORA
PROJ_CHUNK = 512
PROJ_EVERY = 4

B_HEADS = 4
B_QK_DIM = 128
B_V_DIM = 256
B_QK_WIDTH = B_HEADS * B_QK_DIM
B_V_WIDTH = B_HEADS * B_V_DIM
B_CHUNK = 128
B_ROPE_BASE = 10000.0
B_GN_EPS = 1e-5

C_WIDTH = 512
C_GROUP = 16
C_GROUPS = C_WIDTH // C_GROUP
C_STATE = 64
C_LANES = C_GROUPS * C_STATE
C_BLOCKS = 4
C_ULANES = C_WIDTH // C_BLOCKS
C_SLANES = C_LANES // C_BLOCKS
C_PARTS = 4

LN_EPS = 1e-5

RET_COLS = 2 * B_QK_WIDTH + 2 * B_V_WIDTH
REST_COLS = RET_COLS + C_WIDTH
MIX_COLS = A_PROJ + REST_COLS
U_BLK = RET_COLS // C_WIDTH
assert RET_COLS % C_WIDTH == 0

V7X_VMEM_LIMIT_BYTES = 56 * 1024 * 1024


def _cparams(*sem):
    return pltpu.CompilerParams(dimension_semantics=sem, vmem_limit_bytes=V7X_VMEM_LIMIT_BYTES)


def _full(shape):
    n = len(shape)
    return pl.BlockSpec(shape, lambda *_: (0,) * n)


def _layer_spec(a, l):
    n = a.ndim - 1
    return pl.BlockSpec((None,) + a.shape[1:], lambda *_: (l,) + (0,) * n, pipeline_mode=pl.Buffered(1))


def _dot(a, b):
    return jnp.dot(a.astype(BF16), b.astype(BF16), preferred_element_type=F32)


def _dot_nt(a, b):
    return lax.dot_general(a.astype(BF16), b.astype(BF16), (((1,), (1,)), ((), ())),
                           preferred_element_type=F32)


def _dot_tn(a, b):
    return lax.dot_general(a.astype(BF16), b.astype(BF16), (((0,), (0,)), ((), ())),
                           preferred_element_type=F32)


def _split3(x):
    hi = x.astype(BF16)
    r1 = x - hi.astype(F32)
    mid = r1.astype(BF16)
    lo = (r1 - mid.astype(F32)).astype(BF16)
    return hi, mid, lo


def _sigmoid(x):
    return 1.0 / (1.0 + jnp.exp(-x))


def _row_parts(n, parts=2):
    step = n // parts
    return [slice(i * step, (i + 1) * step) for i in range(parts)]


def _layer_norm(y, g, b):
    mu = jnp.mean(y, axis=-1, keepdims=True)
    d = y - mu
    var = jnp.mean(d * d, axis=-1, keepdims=True)
    return d * lax.rsqrt(var + LN_EPS) * g + b


def _chunk_rows(c):
    return pl.ds(c * A_CHUNK, A_CHUNK)


def _proj_rwkv_kernel(x_ref, win_ref, mu_ref, w0_ref, a0_ref, lora_ref, g2_ref, kkp_ref, kap_ref,
                      rkp_ref, lng_ref, lnb_ref, ones_ref, tri_ref, cos_ref, sin_ref,
                      rest_ref, y_ref,
                      xb_s, carry_s, state_s, r_s, k_s, v_s, kn_s, b0_s, lw_s, gate_s, bonus_s, o_s,
                      gam_s, x_s, m2_s, q_s, op_s):
    tm = x_ref.shape[0]
    n_chunks = tm // A_CHUNK
    w = A_WIDTH

    @pl.when(pl.program_id(1) == 0)
    def _():
        carry_s[...] = jnp.zeros_like(carry_s)
        state_s[...] = jnp.zeros_like(state_s)

    xb_s[...] = x_ref[...].astype(BF16)
    z = jnp.dot(xb_s[...], win_ref[:, :A_PROJ], preferred_element_type=F32)

    cos2 = cos_ref[...]
    sin2 = sin_ref[...]

    def rope(t):
        return t * cos2 + pltpu.roll(t, B_QK_DIM // 2, 1) * sin2

    k_off, v_off, g_off = B_QK_WIDTH, 2 * B_QK_WIDTH, 2 * B_QK_WIDTH + B_V_WIDTH

    def piece(lo):
        hi = lo + PROJ_CHUNK
        val = jnp.dot(xb_s[...], win_ref[:, A_PROJ + lo:A_PROJ + hi], preferred_element_type=F32)
        if lo < v_off:
            scale = 1.0 if lo < k_off else B_QK_DIM ** -0.5
            for j in range(PROJ_CHUNK // B_QK_DIM):
                hs = slice(j * B_QK_DIM, (j + 1) * B_QK_DIM)
                rest_ref[:, lo + j * B_QK_DIM:lo + (j + 1) * B_QK_DIM] = (
                    rope(val[:, hs]) * scale).astype(rest_ref.dtype)
        elif g_off <= lo < RET_COLS:
            rest_ref[:, lo:hi] = (val * _sigmoid(val)).astype(rest_ref.dtype)
        else:
            rest_ref[:, lo:hi] = val.astype(rest_ref.dtype)

    pending = list(range(0, REST_COLS, PROJ_CHUNK))
    calls = [0]

    def fill(n=1):
        for _ in range(n):
            if pending and calls[0] % PROJ_EVERY == 0:
                piece(pending.pop(0))
            calls[0] += 1

    def flush():
        while pending:
            piece(pending.pop(0))

    ones = ones_ref[...]
    gw = A_GLANES
    groups = [slice(g * gw, (g + 1) * gw) for g in range(w // gw)]
    slabs = [slice(g * A_SUMLANES, (g + 1) * A_SUMLANES) for g in range(w // A_SUMLANES)]

    def seg_sum(t):
        tb = t.astype(BF16)
        return jnp.concatenate([jnp.dot(tb[:, s], ones, preferred_element_type=F32) for s in slabs], axis=1)

    rolled = pltpu.roll(z, 1, 0)
    rowid = lax.broadcasted_iota(jnp.int32, z.shape, 0)
    prev = jnp.where(rowid == 0, jnp.broadcast_to(carry_s[0:1, :], z.shape), rolled)
    zs = z + mu_ref[...] * (prev - z)
    carry_s[0:1, :] = z[tm - 1:tm, :]
    fill()

    lz = zs[:, 3 * w:3 * w + A_LORA_IN]
    lane = lax.broadcasted_iota(jnp.int32, lz.shape, 1)
    lin = jnp.where(lane < A_DECAY_LORA, jnp.tanh(lz), lz)
    wa = _dot(lin, lora_ref[...])
    lw_s[...] = (-math.exp(-0.5)) * _sigmoid(w0_ref[...] + wa[:, :w])
    ia = _sigmoid(a0_ref[...] + wa[:, w:])
    gate_s[...] = _dot(_sigmoid(zs[:, 3 * w + A_LORA_IN:3 * w + A_LORA_IN + A_GATE_LORA]), g2_ref[...])

    r = zs[:, :w]
    k = zs[:, w:2 * w]
    v = zs[:, 2 * w:3 * w]
    kk = k * kkp_ref[...]
    kn = kk * lax.rsqrt(jnp.maximum(seg_sum(kk * kk), 1e-24))
    kmod = k * (1.0 + (ia - 1.0) * kap_ref[...])
    r_s[...] = r
    k_s[...] = kmod
    v_s[...] = v
    kn_s[...] = kn
    b0_s[...] = kn * ia
    bonus_s[...] = seg_sum(r * kmod * rkp_ref[...]) * v
    fill()

    tri = tri_ref[...]
    hpg = gw // A_HEAD_DIM
    rid = lax.broadcasted_iota(jnp.int32, (A_CHUNK, gw), 0)
    cid = lax.broadcasted_iota(jnp.int32, (A_CHUNK, gw), 1) % A_HEAD_DIM
    strict = rid > cid
    incl = rid >= cid
    eye = (rid == cid).astype(F32)
    brow = lax.broadcasted_iota(jnp.int32, (gw, gw), 0) // A_HEAD_DIM
    bcol = lax.broadcasted_iota(jnp.int32, (gw, gw), 1) // A_HEAD_DIM
    same_head = brow == bcol
    same_head_bf = same_head.astype(BF16)
    n = A_HEAD_DIM

    def bd(x):
        xb = x.astype(BF16)
        return jnp.concatenate([xb] * hpg, axis=0) * same_head_bf


    group = min(A_GROUP, n_chunks)
    assert n_chunks % group == 0

    def state_free_part(gi):
        items = []
        for j in range(group):
            c = gi * group + j
            rows = _chunk_rows(c)
            lw = lw_s[rows, :]
            h3 = _split3(lw)
            cum = (jnp.dot(tri, h3[0], preferred_element_type=F32)
                   + jnp.dot(tri, h3[1], preferred_element_type=F32)
                   + jnp.dot(tri, h3[2], preferred_element_type=F32))
            e_in = jnp.exp(cum)
            e_ex = jnp.exp(cum - lw)
            e_ng = jnp.exp(-cum)
            rt = r_s[rows, :] * e_in
            at = -kn_s[rows, :] * e_ex
            bt = b0_s[rows, :] * e_ng
            kt = k_s[rows, :] * e_ng
            vv = v_s[rows, :]
            gam_s[pl.ds(c, 1), :] = e_in[A_CHUNK - 1:A_CHUNK, :]
            for g, s in enumerate(groups):
                items.append((c * len(groups) + g, at[:, s], rt[:, s], bt[:, s], kt[:, s], vv[:, s]))
            if j % 2 == 1:
                fill()
        ids = range(len(items))
        idx = [it[0] for it in items]
        a_ = [it[1] for it in items]
        r_ = [it[2] for it in items]
        b_ = [it[3] for it in items]
        k_ = [it[4] for it in items]
        v_ = [it[5] for it in items]
        ar = [jnp.concatenate([a_[i], r_[i]], axis=0) for i in ids]
        gb = [_dot_nt(ar[i], bd(b_[i])) for i in ids]
        fill()
        gk = [_dot_nt(ar[i], bd(k_[i])) for i in ids]
        fill()
        l_ab = [jnp.where(strict, gb[i][:A_CHUNK], 0.0) for i in ids]
        a_qb = [jnp.where(incl, gb[i][A_CHUNK:], 0.0) for i in ids]
        akq = [jnp.concatenate([jnp.where(strict, gk[i][:A_CHUNK], 0.0),
                                jnp.where(incl, gk[i][A_CHUNK:], 0.0)], axis=0) for i in ids]
        akqv = [_dot(akq[i], bd(v_[i])) for i in ids]
        fill()
        akv = [akqv[i][:A_CHUNK] for i in ids]
        ov = [akqv[i][A_CHUNK:] for i in ids]
        tinv = [eye + l_ab[i] for i in ids]
        p = [_dot(l_ab[i], bd(l_ab[i])) for i in ids]
        fill()
        for _ in range(4):
            tp = [_dot(jnp.concatenate([tinv[i], p[i]], axis=0), bd(p[i])) for i in ids]
            fill()
            tinv = [tinv[i] + tp[i][:A_CHUNK] for i in ids]
            p = [tp[i][A_CHUNK:] for i in ids]
        tinv = [tinv[i] + _dot(tinv[i], bd(p[i])) for i in ids]
        fill()
        wu = [_dot(tinv[i], jnp.concatenate([bd(a_[i]), bd(akv[i])], axis=1)) for i in ids]
        fill()
        wm = [wu[i][:, :gw] for i in ids]
        uv = [wu[i][:, gw:] for i in ids]
        for i in ids:
            x_s[idx[i]] = jnp.where(same_head, _dot_tn(wm[i], b_[i]), 0.0)
        fill()
        for i in ids:
            m2 = jnp.where(same_head, _dot_tn(jnp.concatenate([uv[i], v_[i]], axis=0),
                                              jnp.concatenate([b_[i], k_[i]], axis=0)), 0.0)
            m2_s[idx[i]] = sum(m2[h * n:(h + 1) * n] for h in range(1, hpg)) + m2[:n]
        fill()
        qo = [_dot(a_qb[i], jnp.concatenate([bd(wm[i]), bd(uv[i])], axis=1)) for i in ids]
        fill()
        for i in ids:
            q_s[idx[i]] = r_[i] + qo[i][:, :gw]
        for i in ids:
            op_s[idx[i]] = qo[i][:, gw:] + ov[i]

    def state_part(c):
        rows = _chunk_rows(c)
        gam = gam_s[pl.ds(c, 1), :]
        gids = range(len(groups))
        s0 = [state_s[g] for g in gids]
        sx = [_dot(s0[g], x_s[c * len(groups) + g]) for g in gids]
        o = [_dot_nt(q_s[c * len(groups) + g], bd(s0[g])) + op_s[c * len(groups) + g] for g in gids]
        for g in gids:
            state_s[g] = (s0[g] + sx[g] + m2_s[c * len(groups) + g]) * gam[:, groups[g]]
        o_s[rows, :] = jnp.concatenate(o, axis=1)
        fill()

    for gi in range(n_chunks // group):
        state_free_part(gi)
    for c in range(n_chunks):
        state_part(c)
    flush()

    o = o_s[...]
    inv_n = 1.0 / n
    mean = seg_sum(o) * inv_n
    d = o - mean
    var = seg_sum(d * d) * inv_n
    on = d * lax.rsqrt(var + A_GN_EPS) * lng_ref[...] + lnb_ref[...]
    y_ref[...] = ((on + bonus_s[...]) * gate_s[...]).astype(y_ref.dtype)


def _rwkv_params(p):
    depth = p["a_w0"].shape[0]
    row = lambda a: a.reshape(depth, 1, -1).astype(F32)
    w = A_WIDTH
    lora = jnp.zeros((depth, A_LORA_IN, 2 * w), F32)
    lora = lora.at[:, :A_DECAY_LORA, :w].set(p["a_w2"]).at[:, A_DECAY_LORA:, w:].set(p["a_a2"]).astype(BF16)
    hid = jnp.arange(A_SUMLANES) // A_HEAD_DIM
    ones = (hid[:, None] == hid[None, :]).astype(BF16)
    ti = jnp.arange(A_CHUNK)
    tri = (ti[:, None] >= ti[None, :]).astype(BF16)
    layered = [row(p["a_shift"]), row(p["a_w0"]), row(p["a_a0"]), lora, p["a_g2"].astype(BF16),
               row(p["a_kk"]), row(p["a_ka"]), row(p["a_rk"]), row(p["a_lnx_g"]), row(p["a_lnx_b"])]
    return layered, [ones, tri]


def _proj_rwkv(x2d, w_in_bf, params, rope_tabs, bsz, seq, l, tm):
    layered, shared = params
    cos2, sin2 = rope_tabs
    k = x2d.shape[1]
    w = A_WIDTH
    per_seq = seq // tm
    n_chunks = tm // A_CHUNK
    n_groups = w // A_GLANES
    row_tile = lambda n: pl.BlockSpec((tm, n), lambda b, i: (b * per_seq + i, 0))
    consts = [w_in_bf] + layered
    stage = pltpu.VMEM((tm, w), F32)
    per_sq = pltpu.VMEM((n_chunks * n_groups, A_GLANES, A_GLANES), F32)
    per_row = pltpu.VMEM((n_chunks * n_groups, A_CHUNK, A_GLANES), F32)
    return pl.pallas_call(
        _proj_rwkv_kernel,
        grid=(bsz, per_seq),
        in_specs=[row_tile(k)] + [_layer_spec(a, l) for a in consts] + [_full(a.shape) for a in shared]
                 + [pl.BlockSpec((tm, B_QK_DIM), lambda b, i: (i, 0))] * 2,
        out_specs=[row_tile(REST_COLS), row_tile(w)],
        out_shape=[jax.ShapeDtypeStruct((bsz * seq, REST_COLS), BF16),
                   jax.ShapeDtypeStruct((bsz * seq, w), BF16)],
        scratch_shapes=[pltpu.VMEM((tm, k), BF16),
                        pltpu.VMEM((8, A_PROJ), F32),
                        pltpu.VMEM((n_groups, A_HEAD_DIM, A_GLANES), F32),
                        stage, stage, stage, stage, stage, stage, stage, stage, stage,
                        pltpu.VMEM((max(8, n_chunks), w), F32),
                        per_sq, per_row, per_row, per_row],
        compiler_params=_cparams("parallel", "arbitrary"),
        name="proj_rwkv7",
    )(x2d, *consts, *shared, cos2, sin2)


def _ret_kernel(z_ref, dmask_ref, qd_ref, kd_ref, cd_ref, y_ref, state_s, upd_s, st_s):
    tc = z_ref.shape[0]
    n_chunks = tc // B_CHUNK
    k_off, v_off, g_off = B_QK_WIDTH, 2 * B_QK_WIDTH, 2 * B_QK_WIDTH + B_V_WIDTH

    @pl.when(pl.program_id(1) == 0)
    def _():
        state_s[...] = jnp.zeros_like(state_s)

    items = [(c, h) for c in range(n_chunks) for h in range(B_HEADS)]
    rows = lambda c: slice(c * B_CHUNK, (c + 1) * B_CHUNK)
    qcol = lambda h: slice(h * B_QK_DIM, (h + 1) * B_QK_DIM)
    kcol = lambda h: slice(k_off + h * B_QK_DIM, k_off + (h + 1) * B_QK_DIM)
    vcol = lambda h: slice(v_off + h * B_V_DIM, v_off + (h + 1) * B_V_DIM)
    gcol = lambda h: slice(g_off + h * B_V_DIM, g_off + (h + 1) * B_V_DIM)

    for c, h in items:
        upd_s[c * B_HEADS + h] = _dot_tn(z_ref[rows(c), kcol(h)] * kd_ref[h], z_ref[rows(c), vcol(h)])
    for h in range(B_HEADS):
        st = state_s[h]
        for c in range(n_chunks):
            st_s[c * B_HEADS + h] = st
            st = cd_ref[h] * st + upd_s[c * B_HEADS + h]
        state_s[h] = st

    for c, h in items:
        qc = z_ref[rows(c), qcol(h)]
        scores = _dot_nt(qc, z_ref[rows(c), kcol(h)]) * dmask_ref[h]
        o = _dot(scores, z_ref[rows(c), vcol(h)]) + _dot(qc * qd_ref[h], st_s[c * B_HEADS + h])
        mu = jnp.mean(o, axis=-1, keepdims=True)
        d = o - mu
        var = jnp.mean(d * d, axis=-1, keepdims=True)
        on = d * lax.rsqrt(var + B_GN_EPS)
        y_ref[rows(c), h * B_V_DIM:(h + 1) * B_V_DIM] = (
            z_ref[rows(c), gcol(h)].astype(F32) * on).astype(y_ref.dtype)


def _retention_tables(seq):
    f32 = F32
    pos = jnp.arange(seq, dtype=f32)
    half = B_QK_DIM // 2
    inv_freq = B_ROPE_BASE ** (-jnp.arange(half, dtype=f32) / half)
    ang = pos[:, None] * inv_freq[None, :]
    cos, sin = jnp.cos(ang), jnp.sin(ang)
    cos2 = jnp.concatenate([cos, cos], axis=1)
    sin2 = jnp.concatenate([-sin, sin], axis=1)
    log_gamma = jnp.log(1.0 - 2.0 ** (-5.0 - jnp.arange(B_HEADS, dtype=f32)))
    idx = jnp.arange(B_CHUNK, dtype=f32)
    rel = idx[:, None] - idx[None, :]
    dmask = jnp.where(rel >= 0, jnp.exp(log_gamma[:, None, None] * jnp.maximum(rel, 0.0)), 0.0)
    qd = jnp.broadcast_to(jnp.exp(log_gamma[:, None] * (idx + 1.0))[:, :, None],
                          (B_HEADS, B_CHUNK, B_QK_DIM))
    kd = jnp.broadcast_to(jnp.exp(log_gamma[:, None] * (B_CHUNK - 1.0 - idx))[:, :, None],
                          (B_HEADS, B_CHUNK, B_QK_DIM))
    cd = jnp.broadcast_to(jnp.exp(log_gamma * B_CHUNK)[:, None, None], (B_HEADS, B_QK_DIM, B_V_DIM))
    return (cos2, sin2), [dmask, qd.astype(BF16), kd.astype(BF16), cd]


def _retention(zmix, bsz, seq, tabs, tc):
    n_items = (tc // B_CHUNK) * B_HEADS
    per_item = pltpu.VMEM((n_items, B_QK_DIM, B_V_DIM), F32)
    return pl.pallas_call(
        _ret_kernel,
        grid=(bsz, seq // tc),
        in_specs=[pl.BlockSpec((None, tc, RET_COLS), lambda b, i: (b, i, 0))] + [_full(a.shape) for a in tabs],
        out_specs=pl.BlockSpec((None, tc, B_V_WIDTH), lambda b, i: (b, i, 0)),
        out_shape=jax.ShapeDtypeStruct((bsz, seq, B_V_WIDTH), BF16),
        scratch_shapes=[pltpu.VMEM((B_HEADS, B_QK_DIM, B_V_DIM), F32), per_item, per_item],
        compiler_params=_cparams("parallel", "arbitrary"),
        name="retention",
    )(zmix, *tabs)


def _s5_kernel(u_ref, bbr_ref, bbi_ref, ar_ref, ai_ref, cr_ref, ci_ref, d_ref, wg_ref, bg_ref,
               y_ref, xr_s, xi_s, u_s, sr_s, si_s):
    bsz, tt, _ = u_ref.shape
    tp = tt // C_PARTS
    rp = tp * bsz

    @pl.when(pl.program_id(0) == 0)
    def _():
        sr_s[...] = jnp.zeros_like(sr_s)
        si_s[...] = jnp.zeros_like(si_s)

    blk = [slice(m * C_SLANES, (m + 1) * C_SLANES) for m in range(C_BLOCKS)]


    def drive(p):
        rows = slice(p * rp, (p + 1) * rp)
        u = pltpu.einshape("btc->tbc", u_ref[:, p * tp:(p + 1) * tp, :].astype(F32)).reshape(rp, C_WIDTH)
        u_s[rows, :] = u
        ub = u.astype(BF16)
        for m in range(C_BLOCKS):
            um = ub[:, m * C_ULANES:(m + 1) * C_ULANES]
            xr_s[rows, blk[m]] = jnp.dot(um, bbr_ref[m], preferred_element_type=F32)
            xi_s[rows, blk[m]] = jnp.dot(um, bbi_ref[m], preferred_element_type=F32)

    def scan(p):
        for m in range(C_BLOCKS):
            cols = blk[m]
            ar = jnp.broadcast_to(ar_ref[:, cols], (bsz, C_SLANES))
            ai = jnp.broadcast_to(ai_ref[:, cols], (bsz, C_SLANES))
            xr, xi = sr_s[:, cols], si_s[:, cols]
            for t in range(p * tp, (p + 1) * tp):
                rows = slice(t * bsz, (t + 1) * bsz)
                xr, xi = (ar * xr - ai * xi + xr_s[rows, cols], ar * xi + ai * xr + xi_s[rows, cols])
                xr_s[rows, cols] = xr
                xi_s[rows, cols] = xi
            sr_s[:, cols] = xr
            si_s[:, cols] = xi

    def readout(p):
        rows = slice(p * rp, (p + 1) * rp)
        parts = [_dot(xr_s[rows, blk[m]], cr_ref[m]) - _dot(xi_s[rows, blk[m]], ci_ref[m])
                 for m in range(C_BLOCKS)]
        y = jnp.concatenate(parts, axis=1) + d_ref[...] * u_s[rows, :]
        y = jax.nn.gelu(y)
        y = y * _sigmoid(_dot(y, wg_ref[...]) + bg_ref[...])
        y_ref[:, p * tp:(p + 1) * tp, :] = pltpu.einshape(
            "tbc->btc", y.reshape(tp, bsz, C_WIDTH)).astype(y_ref.dtype)

    drive(0)
    for p in range(C_PARTS):
        if p + 1 < C_PARTS:
            drive(p + 1)
        scan(p)
        readout(p)


def _s5_params(p):
    f32 = F32
    depth = p["c_log_dt"].shape[0]
    dt = jnp.exp(p["c_log_dt"].astype(f32))[..., None]
    lr, li = p["c_lam_re"].astype(f32), p["c_lam_im"].astype(f32)
    mag = jnp.exp(lr * dt)
    ab_re, ab_im = mag * jnp.cos(li * dt), mag * jnp.sin(li * dt)
    den = lr * lr + li * li
    f_re = ((ab_re - 1.0) * lr + ab_im * li) / den
    f_im = (ab_im * lr - (ab_re - 1.0) * li) / den
    bre, bim = p["c_b_re"].astype(f32), p["c_b_im"].astype(f32)
    bb_re = f_re[..., None] * bre - f_im[..., None] * bim
    bb_im = f_re[..., None] * bim + f_im[..., None] * bre
    gpb = C_GROUPS // C_BLOCKS
    eye = jnp.eye(gpb, dtype=f32)

    def in_blocks(bb):
        bb = bb.reshape(depth, C_BLOCKS, gpb, C_STATE, C_GROUP)
        return jnp.einsum("lmgpc,gh->lmgchp", bb, eye).reshape(depth, C_BLOCKS, C_ULANES, C_SLANES).astype(BF16)

    def out_blocks(cc):
        cc = cc.astype(f32).reshape(depth, C_BLOCKS, gpb, C_GROUP, C_STATE)
        return jnp.einsum("lmgcp,gh->lmgphc", cc, eye).reshape(depth, C_BLOCKS, C_SLANES, C_ULANES).astype(BF16)

    return [in_blocks(bb_re), in_blocks(bb_im), ab_re.reshape(depth, 1, C_LANES), ab_im.reshape(depth, 1, C_LANES),
            out_blocks(p["c_c_re"]), out_blocks(p["c_c_im"]), p["c_d"].reshape(depth, 1, C_WIDTH).astype(f32),
            p["c_w_glu"].astype(BF16), p["c_b_glu"].reshape(depth, 1, C_WIDTH).astype(f32)]


def _s5(zmix, bsz, seq, consts, l, tt):
    rows = tt * bsz
    return pl.pallas_call(
        _s5_kernel,
        grid=(seq // tt,),
        in_specs=[pl.BlockSpec((bsz, tt, C_WIDTH), lambda i: (0, i, U_BLK))] + [_layer_spec(a, l) for a in consts],
        out_specs=pl.BlockSpec((bsz, tt, C_WIDTH), lambda i: (0, i, 0)),
        out_shape=jax.ShapeDtypeStruct((bsz, seq, C_WIDTH), BF16),
        scratch_shapes=[pltpu.VMEM((rows, C_LANES), F32), pltpu.VMEM((rows, C_LANES), F32),
                        pltpu.VMEM((rows, C_WIDTH), F32),
                        pltpu.VMEM((bsz, C_LANES), F32), pltpu.VMEM((bsz, C_LANES), F32)],
        compiler_params=_cparams("arbitrary"),
        name="s5",
    )(zmix, *consts)


def _merge_kernel(alpha, x_ref, ya_ref, yb_ref, yc_ref, win_ref, bgate_ref, wb_ref,
                  wout_ref, g_ref, b_ref, o_ref):
    tm, d = x_ref.shape
    b_lo, c_lo = A_WIDTH, A_WIDTH + B_V_WIDTH
    for rows in _row_parts(tm):
        x = x_ref[rows, :]
        gates = _sigmoid(_dot(x, win_ref[:, MIX_COLS:]) + bgate_ref[...])
        merged = (gates[:, :d] * _dot(ya_ref[rows, :], wb_ref[:b_lo, :])
                  + gates[:, d:2 * d] * _dot(yb_ref[rows, :], wb_ref[b_lo:c_lo, :])
                  + gates[:, 2 * d:] * _dot(yc_ref[rows, :], wb_ref[c_lo:, :]))
        o_ref[rows, :] = _layer_norm(alpha * x + _dot(merged, wout_ref[...]), g_ref[...], b_ref[...])


def _merge(x2d, ya, yb, yc, consts, l, alpha, tm):
    m, d = x2d.shape
    tile = lambda n: pl.BlockSpec((tm, n), lambda i: (i, 0))
    return pl.pallas_call(
        functools.partial(_merge_kernel, alpha),
        grid=(m // tm,),
        in_specs=[tile(d), tile(A_WIDTH), tile(B_V_WIDTH), tile(C_WIDTH)] + [_layer_spec(a, l) for a in consts],
        out_specs=tile(d),
        out_shape=jax.ShapeDtypeStruct((m, d), F32),
        compiler_params=_cparams("parallel"),
        name="merge",
    )(x2d, ya, yb, yc, *consts)


def _ffn_kernel(alpha, tf, x_ref, w1_ref, w2_ref, g_ref, b_ref, o_ref):
    x = x_ref[...]
    xb = x.astype(BF16)
    acc = None
    for j in range(w1_ref.shape[1] // tf):
        h = jnp.maximum(jnp.dot(xb, w1_ref[:, j * tf:(j + 1) * tf], preferred_element_type=F32), 0.0)
        part = jnp.dot((h * h).astype(BF16), w2_ref[j * tf:(j + 1) * tf, :], preferred_element_type=F32)
        acc = part if acc is None else acc + part
    o_ref[...] = _layer_norm(alpha * x + acc, g_ref[...], b_ref[...])


def _ffn(x2d, consts, l, alpha, tm, tf):
    m, d = x2d.shape
    return pl.pallas_call(
        functools.partial(_ffn_kernel, alpha, tf),
        grid=(m // tm,),
        in_specs=[pl.BlockSpec((tm, d), lambda i: (i, 0))] + [_layer_spec(a, l) for a in consts],
        out_specs=pl.BlockSpec((tm, d), lambda i: (i, 0)),
        out_shape=jax.ShapeDtypeStruct((m, d), F32),
        compiler_params=_cparams("parallel"),
        name="ffn",
    )(x2d, *consts)


def _tile(n, want):
    t = min(n, want)
    assert n % t == 0, (n, want)
    return t


def kernel(x, w_in, b_gate, a_shift, a_w0, a_w2, a_a0, a_a2, a_g2, a_kk, a_ka, a_rk, a_lnx_g, a_lnx_b, c_lam_re, c_lam_im, c_log_dt, c_b_re, c_b_im, c_c_re, c_c_im, c_d, c_w_glu, c_b_glu, w_branch, w_out, ln1_g, ln1_b, w_ff1, w_ff2, ln2_g, ln2_b):
    bsz, seq, d = x.shape
    depth = w_in.shape[0]
    alpha = (2.0 * depth) ** 0.25
    tokens = bsz * seq
    rows = lambda a: a.reshape(depth, 1, -1)
    rwkv_params = _rwkv_params(dict(a_shift=a_shift, a_w0=a_w0, a_w2=a_w2, a_a0=a_a0, a_a2=a_a2, a_g2=a_g2,
                                    a_kk=a_kk, a_ka=a_ka, a_rk=a_rk, a_lnx_g=a_lnx_g, a_lnx_b=a_lnx_b))
    s5_params = _s5_params(dict(c_lam_re=c_lam_re, c_lam_im=c_lam_im, c_log_dt=c_log_dt, c_b_re=c_b_re,
                                c_b_im=c_b_im, c_c_re=c_c_re, c_c_im=c_c_im, c_d=c_d, c_w_glu=c_w_glu,
                                c_b_glu=c_b_glu))
    rope_tabs, ret_tables = _retention_tables(seq)
    w_in_bf = w_in.astype(BF16)
    merge_consts = [w_in_bf, rows(b_gate), w_branch.astype(BF16), w_out.astype(BF16),
                    rows(ln1_g), rows(ln1_b)]
    ffn_consts = [w_ff1.astype(BF16), w_ff2.astype(BF16), rows(ln2_g), rows(ln2_b)]
    xt = x.reshape(tokens, d)
    for l in range(depth):
        zmix, ya = _proj_rwkv(xt, w_in_bf, rwkv_params, rope_tabs, bsz, seq, l, _tile(seq, 512))
        zmix = zmix.reshape(bsz, seq, REST_COLS)
        yb = _retention(zmix, bsz, seq, ret_tables, _tile(seq, 1024))
        yc = _s5(zmix, bsz, seq, s5_params, l, _tile(seq, 64))
        x1 = _merge(xt, ya.reshape(tokens, -1), yb.reshape(tokens, -1), yc.reshape(tokens, -1),
                    merge_consts, l, alpha, _tile(tokens, 512))
        xt = _ffn(x1, ffn_consts, l, alpha, _tile(tokens, 1024), 1024)
    return xt.reshape(bsz, seq, d)
```

```python
import functools
import math

import jax
import jax.numpy as jnp
from jax import lax
from jax.experimental import pallas as pl
from jax.experimental.pallas import tpu as pltpu

F32 = jnp.float32
BF16 = jnp.bfloat16

A_HEADS = 8
A_HEAD_DIM = 64
A_WIDTH = A_HEADS * A_HEAD_DIM
A_DECAY_LORA = 64
A_ICLR_LORA = 64
A_GATE_LORA = 128
A_PROJ = 3 * A_WIDTH + A_DECAY_LORA + A_ICLR_LORA + A_GATE_LORA
A_GN_EPS = 64e-5
A_CHUNK = 64
A_GROUP = 8
A_GLANES = 128
A_SUMLANES = 256
A_LORA_IN = A_DECAY_LORA + A_ICLR_LORA
PROJ_CHUNK = 512
PROJ_EVERY = 4

B_HEADS = 4
B_QK_DIM = 128
B_V_DIM = 256
B_QK_WIDTH = B_HEADS * B_QK_DIM
B_V_WIDTH = B_HEADS * B_V_DIM
B_CHUNK = 128
B_ROPE_BASE = 10000.0
B_GN_EPS = 1e-5

C_WIDTH = 512
C_GROUP = 16
C_GROUPS = C_WIDTH // C_GROUP
C_STATE = 64
C_LANES = C_GROUPS * C_STATE
C_BLOCKS = 4
C_ULANES = C_WIDTH // C_BLOCKS
C_SLANES = C_LANES // C_BLOCKS
C_PARTS = 4

LN_EPS = 1e-5

RET_COLS = 2 * B_QK_WIDTH + 2 * B_V_WIDTH
REST_COLS = RET_COLS + C_WIDTH
MIX_COLS = A_PROJ + REST_COLS
U_BLK = RET_COLS // C_WIDTH
assert RET_COLS % C_WIDTH == 0

V7X_VMEM_LIMIT_BYTES = 56 * 1024 * 1024

TM_PROJ_RWKV = 512
TC_RETENTION = 1024
TT_S5 = 64
TM_MERGE = 512
TM_FFN = 1024
TF_FFN = 1024


def _cparams(*sem):
    return pltpu.CompilerParams(dimension_semantics=sem, vmem_limit_bytes=V7X_VMEM_LIMIT_BYTES)


def _full(shape):
    n = len(shape)
    return pl.BlockSpec(shape, lambda *_: (0,) * n)


def _layer_spec(a, l):
    n = a.ndim - 1
    return pl.BlockSpec((None,) + a.shape[1:], lambda *_: (l,) + (0,) * n, pipeline_mode=pl.Buffered(1))


def _dot(a, b):
    return jnp.dot(a.astype(BF16), b.astype(BF16), preferred_element_type=F32)


def _dot_nt(a, b):
    return lax.dot_general(a.astype(BF16), b.astype(BF16), (((1,), (1,)), ((), ())),
                           preferred_element_type=F32)


def _dot_tn(a, b):
    return lax.dot_general(a.astype(BF16), b.astype(BF16), (((0,), (0,)), ((), ())),
                           preferred_element_type=F32)


def _split2(x):
    hi = x.astype(BF16)
    lo = (x - hi.astype(F32)).astype(BF16)
    return hi, lo


def _sigmoid(x):
    return 1.0 / (1.0 + jnp.exp(-x))


def _row_parts(n, parts=2):
    step = n // parts
    return [slice(i * step, (i + 1) * step) for i in range(parts)]


def _layer_norm(y, g, b):
    mu = jnp.mean(y, axis=-1, keepdims=True)
    d = y - mu
    var = jnp.mean(d * d, axis=-1, keepdims=True)
    return d * lax.rsqrt(var + LN_EPS) * g + b


def _chunk_rows(c):
    return pl.ds(c * A_CHUNK, A_CHUNK)


def _proj_rwkv_kernel(x_ref, win_ref, mu_ref, w0_ref, a0_ref, lora_ref, g2_ref, kkp_ref, kap_ref,
                      rkp_ref, lng_ref, lnb_ref, ones_ref, tri_ref, cos_ref, sin_ref,
                      rest_ref, y_ref,
                      xb_s, carry_s, state_s, r_s, k_s, v_s, kn_s, b0_s, lw_s, gate_s, bonus_s, o_s,
                      gam_s, x_s, m2_s, q_s, op_s):
    tm = x_ref.shape[0]
    n_chunks = tm // A_CHUNK
    w = A_WIDTH

    @pl.when(pl.program_id(1) == 0)
    def _():
        carry_s[...] = jnp.zeros_like(carry_s)
        state_s[...] = jnp.zeros_like(state_s)

    xb_s[...] = x_ref[...].astype(BF16)
    z = jnp.dot(xb_s[...], win_ref[:, :A_PROJ], preferred_element_type=F32)

    cos2 = cos_ref[...]
    sin2 = sin_ref[...]

    def rope(t):
        return t * cos2 + pltpu.roll(t, B_QK_DIM // 2, 1) * sin2

    k_off, v_off, g_off = B_QK_WIDTH, 2 * B_QK_WIDTH, 2 * B_QK_WIDTH + B_V_WIDTH

    def piece(lo):
        hi = lo + PROJ_CHUNK
        val = jnp.dot(xb_s[...], win_ref[:, A_PROJ + lo:A_PROJ + hi], preferred_element_type=F32)
        if lo < v_off:
            scale = 1.0 if lo < k_off else B_QK_DIM ** -0.5
            for j in range(PROJ_CHUNK // B_QK_DIM):
                hs = slice(j * B_QK_DIM, (j + 1) * B_QK_DIM)
                rest_ref[:, lo + j * B_QK_DIM:lo + (j + 1) * B_QK_DIM] = (
                    rope(val[:, hs]) * scale).astype(rest_ref.dtype)
        elif g_off <= lo < RET_COLS:
            rest_ref[:, lo:hi] = (val * _sigmoid(val)).astype(rest_ref.dtype)
        else:
            rest_ref[:, lo:hi] = val.astype(rest_ref.dtype)

    pending = list(range(0, REST_COLS, PROJ_CHUNK))
    calls = [0]

    def fill(n=1):
        for _ in range(n):
            if pending and calls[0] % PROJ_EVERY == 0:
                piece(pending.pop(0))
            calls[0] += 1

    def flush():
        while pending:
            piece(pending.pop(0))

    ones = ones_ref[...]
    gw = A_GLANES
    groups = [slice(g * gw, (g + 1) * gw) for g in range(w // gw)]
    slabs = [slice(g * A_SUMLANES, (g + 1) * A_SUMLANES) for g in range(w // A_SUMLANES)]

    def seg_sum(t):
        tb = t.astype(BF16)
        return jnp.concatenate([jnp.dot(tb[:, s], ones, preferred_element_type=F32) for s in slabs], axis=1)

    rolled = pltpu.roll(z, 1, 0)
    rowid = lax.broadcasted_iota(jnp.int32, z.shape, 0)
    prev = jnp.where(rowid == 0, jnp.broadcast_to(carry_s[0:1, :], z.shape), rolled)
    zs = z + mu_ref[...] * (prev - z)
    carry_s[0:1, :] = z[tm - 1:tm, :]
    fill()

    lz = zs[:, 3 * w:3 * w + A_LORA_IN]
    lane = lax.broadcasted_iota(jnp.int32, lz.shape, 1)
    lin = jnp.where(lane < A_DECAY_LORA, jnp.tanh(lz), lz)
    wa = _dot(lin, lora_ref[...])
    lw_s[...] = (-math.exp(-0.5)) * _sigmoid(w0_ref[...] + wa[:, :w])
    ia = _sigmoid(a0_ref[...] + wa[:, w:])
    gate_s[...] = _dot(_sigmoid(zs[:, 3 * w + A_LORA_IN:3 * w + A_LORA_IN + A_GATE_LORA]), g2_ref[...])

    r = zs[:, :w]
    k = zs[:, w:2 * w]
    v = zs[:, 2 * w:3 * w]
    kk = k * kkp_ref[...]
    kn = kk * lax.rsqrt(jnp.maximum(seg_sum(kk * kk), 1e-24))
    kmod = k * (1.0 + (ia - 1.0) * kap_ref[...])
    r_s[...] = r
    k_s[...] = kmod
    v_s[...] = v
    kn_s[...] = kn
    b0_s[...] = kn * ia
    bonus_s[...] = seg_sum(r * kmod * rkp_ref[...]) * v
    fill()

    tri = tri_ref[...]
    hpg = gw // A_HEAD_DIM
    rid = lax.broadcasted_iota(jnp.int32, (A_CHUNK, gw), 0)
    cid = lax.broadcasted_iota(jnp.int32, (A_CHUNK, gw), 1) % A_HEAD_DIM
    strict = rid > cid
    incl = rid >= cid
    eye = (rid == cid).astype(F32)
    brow = lax.broadcasted_iota(jnp.int32, (gw, gw), 0) // A_HEAD_DIM
    bcol = lax.broadcasted_iota(jnp.int32, (gw, gw), 1) // A_HEAD_DIM
    same_head = brow == bcol
    same_head_bf = same_head.astype(BF16)
    n = A_HEAD_DIM

    def bd(x):
        xb = x.astype(BF16)
        return jnp.concatenate([xb] * hpg, axis=0) * same_head_bf


    group = min(A_GROUP, n_chunks)
    assert n_chunks % group == 0

    def state_free_part(gi):
        items = []
        for j in range(group):
            c = gi * group + j
            rows = _chunk_rows(c)
            lw = lw_s[rows, :]
            hi, lo = _split2(lw)
            cum = (jnp.dot(tri, hi, preferred_element_type=F32)
                   + jnp.dot(tri, lo, preferred_element_type=F32))
            e_in = jnp.exp(cum)
            e_ex = jnp.exp(cum - lw)
            e_ng = jnp.exp(-cum)
            rt = r_s[rows, :] * e_in
            at = -kn_s[rows, :] * e_ex
            bt = b0_s[rows, :] * e_ng
            kt = k_s[rows, :] * e_ng
            vv = v_s[rows, :]
            gam_s[pl.ds(c, 1), :] = e_in[A_CHUNK - 1:A_CHUNK, :]
            for g, s in enumerate(groups):
                items.append((c * len(groups) + g, at[:, s], rt[:, s], bt[:, s], kt[:, s], vv[:, s]))
            if j % 2 == 1:
                fill()
        ids = range(len(items))
        idx = [it[0] for it in items]
        a_ = [it[1] for it in items]
        r_ = [it[2] for it in items]
        b_ = [it[3] for it in items]
        k_ = [it[4] for it in items]
        v_ = [it[5] for it in items]
        ar = [jnp.concatenate([a_[i], r_[i]], axis=0) for i in ids]
        gb = [_dot_nt(ar[i], bd(b_[i])) for i in ids]
        fill()
        gk = [_dot_nt(ar[i], bd(k_[i])) for i in ids]
        fill()
        l_ab = [jnp.where(strict, gb[i][:A_CHUNK], 0.0) for i in ids]
        a_qb = [jnp.where(incl, gb[i][A_CHUNK:], 0.0) for i in ids]
        akq = [jnp.concatenate([jnp.where(strict, gk[i][:A_CHUNK], 0.0),
                                jnp.where(incl, gk[i][A_CHUNK:], 0.0)], axis=0) for i in ids]
        akqv = [_dot(akq[i], bd(v_[i])) for i in ids]
        fill()
        akv = [akqv[i][:A_CHUNK] for i in ids]
        ov = [akqv[i][A_CHUNK:] for i in ids]
        tinv = [eye + l_ab[i] for i in ids]
        p = [_dot(l_ab[i], bd(l_ab[i])) for i in ids]
        fill()
        for _ in range(4):
            tp = [_dot(jnp.concatenate([tinv[i], p[i]], axis=0), bd(p[i])) for i in ids]
            fill()
            tinv = [tinv[i] + tp[i][:A_CHUNK] for i in ids]
            p = [tp[i][A_CHUNK:] for i in ids]
        tinv = [tinv[i] + _dot(tinv[i], bd(p[i])) for i in ids]
        fill()
        wu = [_dot(tinv[i], jnp.concatenate([bd(a_[i]), bd(akv[i])], axis=1)) for i in ids]
        fill()
        wm = [wu[i][:, :gw] for i in ids]
        uv = [wu[i][:, gw:] for i in ids]
        for i in ids:
            x_s[idx[i]] = jnp.where(same_head, _dot_tn(wm[i], b_[i]), 0.0)
        fill()
        for i in ids:
            m2 = jnp.where(same_head, _dot_tn(jnp.concatenate([uv[i], v_[i]], axis=0),
                                              jnp.concatenate([b_[i], k_[i]], axis=0)), 0.0)
            m2_s[idx[i]] = sum(m2[h * n:(h + 1) * n] for h in range(1, hpg)) + m2[:n]
        fill()
        qo = [_dot(a_qb[i], jnp.concatenate([bd(wm[i]), bd(uv[i])], axis=1)) for i in ids]
        fill()
        for i in ids:
            q_s[idx[i]] = r_[i] + qo[i][:, :gw]
        for i in ids:
            op_s[idx[i]] = qo[i][:, gw:] + ov[i]

    def state_part(c):
        rows = _chunk_rows(c)
        gam = gam_s[pl.ds(c, 1), :]
        gids = range(len(groups))
        s0 = [state_s[g] for g in gids]
        sx = [_dot(s0[g], x_s[c * len(groups) + g]) for g in gids]
        o = [_dot_nt(q_s[c * len(groups) + g], bd(s0[g])) + op_s[c * len(groups) + g] for g in gids]
        for g in gids:
            state_s[g] = (s0[g] + sx[g] + m2_s[c * len(groups) + g]) * gam[:, groups[g]]
        o_s[rows, :] = jnp.concatenate(o, axis=1)
        fill()

    for gi in range(n_chunks // group):
        state_free_part(gi)
    for c in range(n_chunks):
        state_part(c)
    flush()

    o = o_s[...]
    inv_n = 1.0 / n
    mean = seg_sum(o) * inv_n
    d = o - mean
    var = seg_sum(d * d) * inv_n
    on = d * lax.rsqrt(var + A_GN_EPS) * lng_ref[...] + lnb_ref[...]
    y_ref[...] = ((on + bonus_s[...]) * gate_s[...]).astype(y_ref.dtype)


def _rwkv_params(p):
    depth = p["a_w0"].shape[0]
    row = lambda a: a.reshape(depth, 1, -1).astype(F32)
    w = A_WIDTH
    lora = jnp.zeros((depth, A_LORA_IN, 2 * w), F32)
    lora = lora.at[:, :A_DECAY_LORA, :w].set(p["a_w2"]).at[:, A_DECAY_LORA:, w:].set(p["a_a2"]).astype(BF16)
    hid = jnp.arange(A_SUMLANES) // A_HEAD_DIM
    ones = (hid[:, None] == hid[None, :]).astype(BF16)
    ti = jnp.arange(A_CHUNK)
    tri = (ti[:, None] >= ti[None, :]).astype(BF16)
    layered = [row(p["a_shift"]), row(p["a_w0"]), row(p["a_a0"]), lora, p["a_g2"].astype(BF16),
               row(p["a_kk"]), row(p["a_ka"]), row(p["a_rk"]), row(p["a_lnx_g"]), row(p["a_lnx_b"])]
    return layered, [ones, tri]


def _proj_rwkv(x2d, w_in_bf, params, rope_tabs, bsz, seq, l, tm):
    layered, shared = params
    cos2, sin2 = rope_tabs
    k = x2d.shape[1]
    w = A_WIDTH
    per_seq = seq // tm
    n_chunks = tm // A_CHUNK
    n_groups = w // A_GLANES
    row_tile = lambda n: pl.BlockSpec((tm, n), lambda b, i: (b * per_seq + i, 0))
    consts = [w_in_bf] + layered
    stage = pltpu.VMEM((tm, w), F32)
    per_sq = pltpu.VMEM((n_chunks * n_groups, A_GLANES, A_GLANES), F32)
    per_row = pltpu.VMEM((n_chunks * n_groups, A_CHUNK, A_GLANES), F32)
    return pl.pallas_call(
        _proj_rwkv_kernel,
        grid=(bsz, per_seq),
        in_specs=[row_tile(k)] + [_layer_spec(a, l) for a in consts] + [_full(a.shape) for a in shared]
                 + [pl.BlockSpec((tm, B_QK_DIM), lambda b, i: (i, 0))] * 2,
        out_specs=[row_tile(REST_COLS), row_tile(w)],
        out_shape=[jax.ShapeDtypeStruct((bsz * seq, REST_COLS), BF16),
                   jax.ShapeDtypeStruct((bsz * seq, w), BF16)],
        scratch_shapes=[pltpu.VMEM((tm, k), BF16),
                        pltpu.VMEM((8, A_PROJ), F32),
                        pltpu.VMEM((n_groups, A_HEAD_DIM, A_GLANES), F32),
                        stage, stage, stage, stage, stage, stage, stage, stage, stage,
                        pltpu.VMEM((max(8, n_chunks), w), F32),
                        per_sq, per_row, per_row, per_row],
        compiler_params=_cparams("parallel", "arbitrary"),
        name="proj_rwkv7",
    )(x2d, *consts, *shared, cos2, sin2)


def _ret_kernel(z_ref, dmask_ref, qd_ref, kd_ref, cd_ref, y_ref, state_s, upd_s, st_s):
    tc = z_ref.shape[0]
    n_chunks = tc // B_CHUNK
    k_off, v_off, g_off = B_QK_WIDTH, 2 * B_QK_WIDTH, 2 * B_QK_WIDTH + B_V_WIDTH

    @pl.when(pl.program_id(1) == 0)
    def _():
        state_s[...] = jnp.zeros_like(state_s)

    items = [(c, h) for c in range(n_chunks) for h in range(B_HEADS)]
    rows = lambda c: slice(c * B_CHUNK, (c + 1) * B_CHUNK)
    qcol = lambda h: slice(h * B_QK_DIM, (h + 1) * B_QK_DIM)
    kcol = lambda h: slice(k_off + h * B_QK_DIM, k_off + (h + 1) * B_QK_DIM)
    vcol = lambda h: slice(v_off + h * B_V_DIM, v_off + (h + 1) * B_V_DIM)
    gcol = lambda h: slice(g_off + h * B_V_DIM, g_off + (h + 1) * B_V_DIM)

    for c, h in items:
        upd_s[c * B_HEADS + h] = _dot_tn(z_ref[rows(c), kcol(h)] * kd_ref[h], z_ref[rows(c), vcol(h)])
    for h in range(B_HEADS):
        st = state_s[h]
        for c in range(n_chunks):
            st_s[c * B_HEADS + h] = st
            st = cd_ref[h] * st + upd_s[c * B_HEADS + h]
        state_s[h] = st

    for c, h in items:
        qc = z_ref[rows(c), qcol(h)]
        scores = _dot_nt(qc, z_ref[rows(c), kcol(h)]) * dmask_ref[h]
        o = _dot(scores, z_ref[rows(c), vcol(h)]) + _dot(qc * qd_ref[h], st_s[c * B_HEADS + h])
        mu = jnp.mean(o, axis=-1, keepdims=True)
        d = o - mu
        var = jnp.mean(d * d, axis=-1, keepdims=True)
        on = d * lax.rsqrt(var + B_GN_EPS)
        y_ref[rows(c), h * B_V_DIM:(h + 1) * B_V_DIM] = (
            z_ref[rows(c), gcol(h)].astype(F32) * on).astype(y_ref.dtype)


def _retention_tables(seq):
    f32 = F32
    pos = jnp.arange(seq, dtype=f32)
    half = B_QK_DIM // 2
    inv_freq = B_ROPE_BASE ** (-jnp.arange(half, dtype=f32) / half)
    ang = pos[:, None] * inv_freq[None, :]
    cos, sin = jnp.cos(ang), jnp.sin(ang)
    cos2 = jnp.concatenate([cos, cos], axis=1)
    sin2 = jnp.concatenate([-sin, sin], axis=1)
    log_gamma = jnp.log(1.0 - 2.0 ** (-5.0 - jnp.arange(B_HEADS, dtype=f32)))
    idx = jnp.arange(B_CHUNK, dtype=f32)
    rel = idx[:, None] - idx[None, :]
    dmask = jnp.where(rel >= 0, jnp.exp(log_gamma[:, None, None] * jnp.maximum(rel, 0.0)), 0.0)
    qd = jnp.broadcast_to(jnp.exp(log_gamma[:, None] * (idx + 1.0))[:, :, None],
                          (B_HEADS, B_CHUNK, B_QK_DIM))
    kd = jnp.broadcast_to(jnp.exp(log_gamma[:, None] * (B_CHUNK - 1.0 - idx))[:, :, None],
                          (B_HEADS, B_CHUNK, B_QK_DIM))
    cd = jnp.broadcast_to(jnp.exp(log_gamma * B_CHUNK)[:, None, None], (B_HEADS, B_QK_DIM, B_V_DIM))
    return (cos2, sin2), [dmask, qd.astype(BF16), kd.astype(BF16), cd]


def _retention(zmix, bsz, seq, tabs, tc):
    n_items = (tc // B_CHUNK) * B_HEADS
    per_item = pltpu.VMEM((n_items, B_QK_DIM, B_V_DIM), F32)
    return pl.pallas_call(
        _ret_kernel,
        grid=(bsz, seq // tc),
        in_specs=[pl.BlockSpec((None, tc, RET_COLS), lambda b, i: (b, i, 0))] + [_full(a.shape) for a in tabs],
        out_specs=pl.BlockSpec((None, tc, B_V_WIDTH), lambda b, i: (b, i, 0)),
        out_shape=jax.ShapeDtypeStruct((bsz, seq, B_V_WIDTH), BF16),
        scratch_shapes=[pltpu.VMEM((B_HEADS, B_QK_DIM, B_V_DIM), F32), per_item, per_item],
        compiler_params=_cparams("parallel", "arbitrary"),
        name="retention",
    )(zmix, *tabs)


def _s5_kernel(u_ref, bbr_ref, bbi_ref, ar_ref, ai_ref, cr_ref, ci_ref, d_ref, wg_ref, bg_ref,
               y_ref, xr_s, xi_s, u_s, sr_s, si_s):
    bsz, tt, _ = u_ref.shape
    tp = tt // C_PARTS
    rp = tp * bsz

    @pl.when(pl.program_id(0) == 0)
    def _():
        sr_s[...] = jnp.zeros_like(sr_s)
        si_s[...] = jnp.zeros_like(si_s)

    blk = [slice(m * C_SLANES, (m + 1) * C_SLANES) for m in range(C_BLOCKS)]


    def drive(p):
        rows = slice(p * rp, (p + 1) * rp)
        u = pltpu.einshape("btc->tbc", u_ref[:, p * tp:(p + 1) * tp, :].astype(F32)).reshape(rp, C_WIDTH)
        u_s[rows, :] = u
        ub = u.astype(BF16)
        for m in range(C_BLOCKS):
            um = ub[:, m * C_ULANES:(m + 1) * C_ULANES]
            xr_s[rows, blk[m]] = jnp.dot(um, bbr_ref[m], preferred_element_type=F32)
            xi_s[rows, blk[m]] = jnp.dot(um, bbi_ref[m], preferred_element_type=F32)

    def scan(p):
        for m in range(C_BLOCKS):
            cols = blk[m]
            ar = jnp.broadcast_to(ar_ref[:, cols], (bsz, C_SLANES))
            ai = jnp.broadcast_to(ai_ref[:, cols], (bsz, C_SLANES))
            xr, xi = sr_s[:, cols], si_s[:, cols]
            for t in range(p * tp, (p + 1) * tp):
                rows = slice(t * bsz, (t + 1) * bsz)
                xr, xi = (ar * xr - ai * xi + xr_s[rows, cols], ar * xi + ai * xr + xi_s[rows, cols])
                xr_s[rows, cols] = xr
                xi_s[rows, cols] = xi
            sr_s[:, cols] = xr
            si_s[:, cols] = xi

    def readout(p):
        rows = slice(p * rp, (p + 1) * rp)
        parts = [_dot(xr_s[rows, blk[m]], cr_ref[m]) - _dot(xi_s[rows, blk[m]], ci_ref[m])
                 for m in range(C_BLOCKS)]
        y = jnp.concatenate(parts, axis=1) + d_ref[...] * u_s[rows, :]
        y = jax.nn.gelu(y)
        y = y * _sigmoid(_dot(y, wg_ref[...]) + bg_ref[...])
        y_ref[:, p * tp:(p + 1) * tp, :] = pltpu.einshape(
            "tbc->btc", y.reshape(tp, bsz, C_WIDTH)).astype(y_ref.dtype)

    drive(0)
    for p in range(C_PARTS):
        if p + 1 < C_PARTS:
            drive(p + 1)
        scan(p)
        readout(p)


def _s5_params(p):
    f32 = F32
    depth = p["c_log_dt"].shape[0]
    dt = jnp.exp(p["c_log_dt"].astype(f32))[..., None]
    lr, li = p["c_lam_re"].astype(f32), p["c_lam_im"].astype(f32)
    mag = jnp.exp(lr * dt)
    ab_re, ab_im = mag * jnp.cos(li * dt), mag * jnp.sin(li * dt)
    den = lr * lr + li * li
    f_re = ((ab_re - 1.0) * lr + ab_im * li) / den
    f_im = (ab_im * lr - (ab_re - 1.0) * li) / den
    bre, bim = p["c_b_re"].astype(f32), p["c_b_im"].astype(f32)
    bb_re = f_re[..., None] * bre - f_im[..., None] * bim
    bb_im = f_re[..., None] * bim + f_im[..., None] * bre
    gpb = C_GROUPS // C_BLOCKS
    eye = jnp.eye(gpb, dtype=f32)

    def in_blocks(bb):
        bb = bb.reshape(depth, C_BLOCKS, gpb, C_STATE, C_GROUP)
        return jnp.einsum("lmgpc,gh->lmgchp", bb, eye).reshape(depth, C_BLOCKS, C_ULANES, C_SLANES).astype(BF16)

    def out_blocks(cc):
        cc = cc.astype(f32).reshape(depth, C_BLOCKS, gpb, C_GROUP, C_STATE)
        return jnp.einsum("lmgcp,gh->lmgphc", cc, eye).reshape(depth, C_BLOCKS, C_SLANES, C_ULANES).astype(BF16)

    return [in_blocks(bb_re), in_blocks(bb_im), ab_re.reshape(depth, 1, C_LANES), ab_im.reshape(depth, 1, C_LANES),
            out_blocks(p["c_c_re"]), out_blocks(p["c_c_im"]), p["c_d"].reshape(depth, 1, C_WIDTH).astype(f32),
            p["c_w_glu"].astype(BF16), p["c_b_glu"].reshape(depth, 1, C_WIDTH).astype(f32)]


def _s5(zmix, bsz, seq, consts, l, tt):
    rows = tt * bsz
    return pl.pallas_call(
        _s5_kernel,
        grid=(seq // tt,),
        in_specs=[pl.BlockSpec((bsz, tt, C_WIDTH), lambda i: (0, i, U_BLK))] + [_layer_spec(a, l) for a in consts],
        out_specs=pl.BlockSpec((bsz, tt, C_WIDTH), lambda i: (0, i, 0)),
        out_shape=jax.ShapeDtypeStruct((bsz, seq, C_WIDTH), BF16),
        scratch_shapes=[pltpu.VMEM((rows, C_LANES), F32), pltpu.VMEM((rows, C_LANES), F32),
                        pltpu.VMEM((rows, C_WIDTH), F32),
                        pltpu.VMEM((bsz, C_LANES), F32), pltpu.VMEM((bsz, C_LANES), F32)],
        compiler_params=_cparams("arbitrary"),
        name="s5",
    )(zmix, *consts)


def _merge_kernel(alpha, x_ref, ya_ref, yb_ref, yc_ref, win_ref, bgate_ref, wb_ref,
                  wout_ref, g_ref, b_ref, o_ref):
    tm, d = x_ref.shape
    b_lo, c_lo = A_WIDTH, A_WIDTH + B_V_WIDTH
    for rows in _row_parts(tm):
        x = x_ref[rows, :]
        gates = _sigmoid(_dot(x, win_ref[:, MIX_COLS:]) + bgate_ref[...])
        merged = (gates[:, :d] * _dot(ya_ref[rows, :], wb_ref[:b_lo, :])
                  + gates[:, d:2 * d] * _dot(yb_ref[rows, :], wb_ref[b_lo:c_lo, :])
                  + gates[:, 2 * d:] * _dot(yc_ref[rows, :], wb_ref[c_lo:, :]))
        o_ref[rows, :] = _layer_norm(alpha * x + _dot(merged, wout_ref[...]), g_ref[...], b_ref[...])


def _merge(x2d, ya, yb, yc, consts, l, alpha, tm):
    m, d = x2d.shape
    tile = lambda n: pl.BlockSpec((tm, n), lambda i: (i, 0))
    return pl.pallas_call(
        functools.partial(_merge_kernel, alpha),
        grid=(m // tm,),
        in_specs=[tile(d), tile(A_WIDTH), tile(B_V_WIDTH), tile(C_WIDTH)] + [_layer_spec(a, l) for a in consts],
        out_specs=tile(d),
        out_shape=jax.ShapeDtypeStruct((m, d), F32),
        compiler_params=_cparams("parallel"),
        name="merge",
    )(x2d, ya, yb, yc, *consts)


def _ffn_kernel(alpha, tf, x_ref, w1_ref, w2_ref, g_ref, b_ref, o_ref):
    x = x_ref[...]
    xb = x.astype(BF16)
    acc = None
    for j in range(w1_ref.shape[1] // tf):
        h = jnp.maximum(jnp.dot(xb, w1_ref[:, j * tf:(j + 1) * tf], preferred_element_type=F32), 0.0)
        part = jnp.dot((h * h).astype(BF16), w2_ref[j * tf:(j + 1) * tf, :], preferred_element_type=F32)
        acc = part if acc is None else acc + part
    o_ref[...] = _layer_norm(alpha * x + acc, g_ref[...], b_ref[...])


def _ffn(x2d, consts, l, alpha, tm, tf):
    m, d = x2d.shape
    return pl.pallas_call(
        functools.partial(_ffn_kernel, alpha, tf),
        grid=(m // tm,),
        in_specs=[pl.BlockSpec((tm, d), lambda i: (i, 0))] + [_layer_spec(a, l) for a in consts],
        out_specs=pl.BlockSpec((tm, d), lambda i: (i, 0)),
        out_shape=jax.ShapeDtypeStruct((m, d), F32),
        compiler_params=_cparams("parallel"),
        name="ffn",
    )(x2d, *consts)


def _tile(n, want):
    t = min(n, want)
    assert n % t == 0, (n, want)
    return t


def kernel(x, w_in, b_gate, a_shift, a_w0, a_w2, a_a0, a_a2, a_g2, a_kk, a_ka, a_rk, a_lnx_g, a_lnx_b, c_lam_re, c_lam_im, c_log_dt, c_b_re, c_b_im, c_c_re, c_c_im, c_d, c_w_glu, c_b_glu, w_branch, w_out, ln1_g, ln1_b, w_ff1, w_ff2, ln2_g, ln2_b):
    bsz, seq, d = x.shape
    depth = w_in.shape[0]
    alpha = (2.0 * depth) ** 0.25
    tokens = bsz * seq
    rows = lambda a: a.reshape(depth, 1, -1)
    rwkv_params = _rwkv_params(dict(a_shift=a_shift, a_w0=a_w0, a_w2=a_w2, a_a0=a_a0, a_a2=a_a2, a_g2=a_g2,
                                    a_kk=a_kk, a_ka=a_ka, a_rk=a_rk, a_lnx_g=a_lnx_g, a_lnx_b=a_lnx_b))
    s5_params = _s5_params(dict(c_lam_re=c_lam_re, c_lam_im=c_lam_im, c_log_dt=c_log_dt, c_b_re=c_b_re,
                                c_b_im=c_b_im, c_c_re=c_c_re, c_c_im=c_c_im, c_d=c_d, c_w_glu=c_w_glu,
                                c_b_glu=c_b_glu))
    rope_tabs, ret_tables = _retention_tables(seq)
    w_in_bf = w_in.astype(BF16)
    merge_consts = [w_in_bf, rows(b_gate), w_branch.astype(BF16), w_out.astype(BF16),
                    rows(ln1_g), rows(ln1_b)]
    ffn_consts = [w_ff1.astype(BF16), w_ff2.astype(BF16), rows(ln2_g), rows(ln2_b)]
    xt = x.reshape(tokens, d)
    for l in range(depth):
        zmix, ya = _proj_rwkv(xt, w_in_bf, rwkv_params, rope_tabs, bsz, seq, l, _tile(seq, TM_PROJ_RWKV))
        zmix = zmix.reshape(bsz, seq, REST_COLS)
        yb = _retention(zmix, bsz, seq, ret_tables, _tile(seq, TC_RETENTION))
        yc = _s5(zmix, bsz, seq, s5_params, l, _tile(seq, TT_S5))
        x1 = _merge(xt, ya.reshape(tokens, -1), yb.reshape(tokens, -1), yc.reshape(tokens, -1),
                    merge_consts, l, alpha, _tile(tokens, TM_MERGE))
        xt = _ffn(x1, ffn_consts, l, alpha, _tile(tokens, TM_FFN), TF_FFN)
    return xt.reshape(bsz, seq, d)
```

```python
import functools
import math

import jax
import jax.numpy as jnp
from jax import lax
from jax.experimental import pallas as pl
from jax.experimental.pallas import tpu as pltpu

F32 = jnp.float32
BF16 = jnp.bfloat16

A_HEADS = 8
A_HEAD_DIM = 64
A_WIDTH = A_HEADS * A_HEAD_DIM
A_DECAY_LORA = 64
A_ICLR_LORA = 64
A_GATE_LORA = 128
A_PROJ = 3 * A_WIDTH + A_DECAY_LORA + A_ICLR_LORA + A_GATE_LORA
A_GN_EPS = 64e-5
A_CHUNK = 64
A_GROUP = 8
A_GLANES = 128
A_SUMLANES = 256
A_LORA_IN = A_DECAY_LORA + A_ICLR_LORA
PROJ_CHUNK = 512
PROJ_EVERY = 4

B_HEADS = 4
B_QK_DIM = 128
B_V_DIM = 256
B_QK_WIDTH = B_HEADS * B_QK_DIM
B_V_WIDTH = B_HEADS * B_V_DIM
B_CHUNK = 128
B_ROPE_BASE = 10000.0
B_GN_EPS = 1e-5

C_WIDTH = 512
C_GROUP = 16
C_GROUPS = C_WIDTH // C_GROUP
C_STATE = 64
C_LANES = C_GROUPS * C_STATE
C_BLOCKS = 4
C_ULANES = C_WIDTH // C_BLOCKS
C_SLANES = C_LANES // C_BLOCKS
C_PARTS = 4

LN_EPS = 1e-5

RET_COLS = 2 * B_QK_WIDTH + 2 * B_V_WIDTH
REST_COLS = RET_COLS + C_WIDTH
MIX_COLS = A_PROJ + REST_COLS
QKV_COLS = 2 * B_QK_WIDTH + B_V_WIDTH
U_BLK = RET_COLS // C_WIDTH
G_BLK = QKV_COLS // B_V_WIDTH
assert RET_COLS % C_WIDTH == 0 and QKV_COLS % B_V_WIDTH == 0

V7X_VMEM_LIMIT_BYTES = 56 * 1024 * 1024

TM_PROJ_RWKV = 512
TC_RETENTION = 1024
TT_S5 = 64
TM_MERGE = 512
TM_FFN = 1024
TF_FFN = 1024


def _cparams(*sem):
    return pltpu.CompilerParams(dimension_semantics=sem, vmem_limit_bytes=V7X_VMEM_LIMIT_BYTES)


def _full(shape):
    n = len(shape)
    return pl.BlockSpec(shape, lambda *_: (0,) * n)


def _layer_spec(a, l):
    n = a.ndim - 1
    return pl.BlockSpec((None,) + a.shape[1:], lambda *_: (l,) + (0,) * n, pipeline_mode=pl.Buffered(1))


def _dot(a, b):
    return jnp.dot(a.astype(BF16), b.astype(BF16), preferred_element_type=F32)


def _dot_nt(a, b):
    return lax.dot_general(a.astype(BF16), b.astype(BF16), (((1,), (1,)), ((), ())),
                           preferred_element_type=F32)


def _dot_tn(a, b):
    return lax.dot_general(a.astype(BF16), b.astype(BF16), (((0,), (0,)), ((), ())),
                           preferred_element_type=F32)


def _split2(x):
    hi = x.astype(BF16)
    lo = (x - hi.astype(F32)).astype(BF16)
    return hi, lo


def _sigmoid(x):
    return 1.0 / (1.0 + jnp.exp(-x))


def _row_parts(n, parts=2):
    step = n // parts
    return [slice(i * step, (i + 1) * step) for i in range(parts)]


def _layer_norm(y, g, b):
    mu = jnp.mean(y, axis=-1, keepdims=True)
    d = y - mu
    var = jnp.mean(d * d, axis=-1, keepdims=True)
    return d * lax.rsqrt(var + LN_EPS) * g + b


def _chunk_rows(c):
    return pl.ds(c * A_CHUNK, A_CHUNK)


def _proj_rwkv_kernel(x_ref, win_ref, mu_ref, w0_ref, a0_ref, lora_ref, g2_ref, kkp_ref, kap_ref,
                      rkp_ref, lng_ref, lnb_ref, ones_ref, tri_ref, cos_ref, sin_ref,
                      rest_ref, y_ref,
                      xb_s, carry_s, state_s, r_s, k_s, v_s, kn_s, b0_s, lw_s, gate_s, bonus_s, o_s,
                      gam_s, x_s, m2_s, q_s, op_s):
    tm = x_ref.shape[0]
    n_chunks = tm // A_CHUNK
    w = A_WIDTH

    @pl.when(pl.program_id(1) == 0)
    def _():
        carry_s[...] = jnp.zeros_like(carry_s)
        state_s[...] = jnp.zeros_like(state_s)

    xb_s[...] = x_ref[...].astype(BF16)
    z = jnp.dot(xb_s[...], win_ref[:, :A_PROJ], preferred_element_type=F32)

    cos2 = cos_ref[...]
    sin2 = sin_ref[...]

    def rope(t):
        return t * cos2 + pltpu.roll(t, B_QK_DIM // 2, 1) * sin2

    k_off, v_off, g_off = B_QK_WIDTH, 2 * B_QK_WIDTH, 2 * B_QK_WIDTH + B_V_WIDTH

    def piece(lo):
        hi = lo + PROJ_CHUNK
        val = jnp.dot(xb_s[...], win_ref[:, A_PROJ + lo:A_PROJ + hi], preferred_element_type=F32)
        if lo < v_off:
            scale = 1.0 if lo < k_off else B_QK_DIM ** -0.5
            for j in range(PROJ_CHUNK // B_QK_DIM):
                hs = slice(j * B_QK_DIM, (j + 1) * B_QK_DIM)
                rest_ref[:, lo + j * B_QK_DIM:lo + (j + 1) * B_QK_DIM] = (
                    rope(val[:, hs]) * scale).astype(rest_ref.dtype)
        elif g_off <= lo < RET_COLS:
            rest_ref[:, lo:hi] = (val * _sigmoid(val)).astype(rest_ref.dtype)
        else:
            rest_ref[:, lo:hi] = val.astype(rest_ref.dtype)

    pending = list(range(0, REST_COLS, PROJ_CHUNK))
    calls = [0]

    def fill(n=1):
        for _ in range(n):
            if pending and calls[0] % PROJ_EVERY == 0:
                piece(pending.pop(0))
            calls[0] += 1

    def flush():
        while pending:
            piece(pending.pop(0))

    ones = ones_ref[...]
    gw = A_GLANES
    groups = [slice(g * gw, (g + 1) * gw) for g in range(w // gw)]
    slabs = [slice(g * A_SUMLANES, (g + 1) * A_SUMLANES) for g in range(w // A_SUMLANES)]

    def seg_sum(t):
        tb = t.astype(BF16)
        return jnp.concatenate([jnp.dot(tb[:, s], ones, preferred_element_type=F32) for s in slabs], axis=1)

    rolled = pltpu.roll(z, 1, 0)
    rowid = lax.broadcasted_iota(jnp.int32, z.shape, 0)
    prev = jnp.where(rowid == 0, jnp.broadcast_to(carry_s[0:1, :], z.shape), rolled)
    zs = z + mu_ref[...] * (prev - z)
    carry_s[0:1, :] = z[tm - 1:tm, :]
    fill()

    lz = zs[:, 3 * w:3 * w + A_LORA_IN]
    lane = lax.broadcasted_iota(jnp.int32, lz.shape, 1)
    lin = jnp.where(lane < A_DECAY_LORA, jnp.tanh(lz), lz)
    wa = _dot(lin, lora_ref[...])
    lw_s[...] = (-math.exp(-0.5)) * _sigmoid(w0_ref[...] + wa[:, :w])
    ia = _sigmoid(a0_ref[...] + wa[:, w:])
    gate_s[...] = _dot(_sigmoid(zs[:, 3 * w + A_LORA_IN:3 * w + A_LORA_IN + A_GATE_LORA]), g2_ref[...])

    r = zs[:, :w]
    k = zs[:, w:2 * w]
    v = zs[:, 2 * w:3 * w]
    kk = k * kkp_ref[...]
    kn = kk * lax.rsqrt(jnp.maximum(seg_sum(kk * kk), 1e-24))
    kmod = k * (1.0 + (ia - 1.0) * kap_ref[...])
    r_s[...] = r
    k_s[...] = kmod
    v_s[...] = v
    kn_s[...] = kn
    b0_s[...] = kn * ia
    bonus_s[...] = seg_sum(r * kmod * rkp_ref[...]) * v
    fill()

    tri = tri_ref[...]
    hpg = gw // A_HEAD_DIM
    rid = lax.broadcasted_iota(jnp.int32, (A_CHUNK, gw), 0)
    cid = lax.broadcasted_iota(jnp.int32, (A_CHUNK, gw), 1) % A_HEAD_DIM
    strict = rid > cid
    incl = rid >= cid
    eye = (rid == cid).astype(F32)
    brow = lax.broadcasted_iota(jnp.int32, (gw, gw), 0) // A_HEAD_DIM
    bcol = lax.broadcasted_iota(jnp.int32, (gw, gw), 1) // A_HEAD_DIM
    same_head = brow == bcol
    same_head_bf = same_head.astype(BF16)
    n = A_HEAD_DIM

    def bd(x):
        xb = x.astype(BF16)
        return jnp.concatenate([xb] * hpg, axis=0) * same_head_bf


    group = min(A_GROUP, n_chunks)
    assert n_chunks % group == 0

    def state_free_part(gi):
        items = []
        for j in range(group):
            c = gi * group + j
            rows = _chunk_rows(c)
            lw = lw_s[rows, :]
            hi, lo = _split2(lw)
            cum = (jnp.dot(tri, hi, preferred_element_type=F32)
                   + jnp.dot(tri, lo, preferred_element_type=F32))
            e_in = jnp.exp(cum)
            e_ex = jnp.exp(cum - lw)
            e_ng = jnp.exp(-cum)
            rt = r_s[rows, :] * e_in
            at = -kn_s[rows, :] * e_ex
            bt = b0_s[rows, :] * e_ng
            kt = k_s[rows, :] * e_ng
            vv = v_s[rows, :]
            gam_s[pl.ds(c, 1), :] = e_in[A_CHUNK - 1:A_CHUNK, :]
            for g, s in enumerate(groups):
                items.append((c * len(groups) + g, at[:, s], rt[:, s], bt[:, s], kt[:, s], vv[:, s]))
            if j % 2 == 1:
                fill()
        ids = range(len(items))
        idx = [it[0] for it in items]
        a_ = [it[1] for it in items]
        r_ = [it[2] for it in items]
        b_ = [it[3] for it in items]
        k_ = [it[4] for it in items]
        v_ = [it[5] for it in items]
        ar = [jnp.concatenate([a_[i], r_[i]], axis=0) for i in ids]
        gb = [_dot_nt(ar[i], bd(b_[i])) for i in ids]
        fill()
        gk = [_dot_nt(ar[i], bd(k_[i])) for i in ids]
        fill()
        l_ab = [jnp.where(strict, gb[i][:A_CHUNK], 0.0) for i in ids]
        a_qb = [jnp.where(incl, gb[i][A_CHUNK:], 0.0) for i in ids]
        akq = [jnp.concatenate([jnp.where(strict, gk[i][:A_CHUNK], 0.0),
                                jnp.where(incl, gk[i][A_CHUNK:], 0.0)], axis=0) for i in ids]
        akqv = [_dot(akq[i], bd(v_[i])) for i in ids]
        fill()
        akv = [akqv[i][:A_CHUNK] for i in ids]
        ov = [akqv[i][A_CHUNK:] for i in ids]
        tinv = [eye + l_ab[i] for i in ids]
        p = [_dot(l_ab[i], bd(l_ab[i])) for i in ids]
        fill()
        for _ in range(4):
            tp = [_dot(jnp.concatenate([tinv[i], p[i]], axis=0), bd(p[i])) for i in ids]
            fill()
            tinv = [tinv[i] + tp[i][:A_CHUNK] for i in ids]
            p = [tp[i][A_CHUNK:] for i in ids]
        tinv = [tinv[i] + _dot(tinv[i], bd(p[i])) for i in ids]
        fill()
        wu = [_dot(tinv[i], jnp.concatenate([bd(a_[i]), bd(akv[i])], axis=1)) for i in ids]
        fill()
        wm = [wu[i][:, :gw] for i in ids]
        uv = [wu[i][:, gw:] for i in ids]
        for i in ids:
            x_s[idx[i]] = jnp.where(same_head, _dot_tn(wm[i], b_[i]), 0.0)
        fill()
        for i in ids:
            m2 = jnp.where(same_head, _dot_tn(jnp.concatenate([uv[i], v_[i]], axis=0),
                                              jnp.concatenate([b_[i], k_[i]], axis=0)), 0.0)
            m2_s[idx[i]] = sum(m2[h * n:(h + 1) * n] for h in range(1, hpg)) + m2[:n]
        fill()
        qo = [_dot(a_qb[i], jnp.concatenate([bd(wm[i]), bd(uv[i])], axis=1)) for i in ids]
        fill()
        for i in ids:
            q_s[idx[i]] = r_[i] + qo[i][:, :gw]
        for i in ids:
            op_s[idx[i]] = qo[i][:, gw:] + ov[i]

    def state_part(c):
        rows = _chunk_rows(c)
        gam = gam_s[pl.ds(c, 1), :]
        gids = range(len(groups))
        s0 = [state_s[g] for g in gids]
        sx = [_dot(s0[g], x_s[c * len(groups) + g]) for g in gids]
        o = [_dot_nt(q_s[c * len(groups) + g], bd(s0[g])) + op_s[c * len(groups) + g] for g in gids]
        for g in gids:
            state_s[g] = (s0[g] + sx[g] + m2_s[c * len(groups) + g]) * gam[:, groups[g]]
        o_s[rows, :] = jnp.concatenate(o, axis=1)
        fill()

    for gi in range(n_chunks // group):
        state_free_part(gi)
    for c in range(n_chunks):
        state_part(c)
    flush()

    o = o_s[...]
    inv_n = 1.0 / n
    mean = seg_sum(o) * inv_n
    d = o - mean
    var = seg_sum(d * d) * inv_n
    on = d * lax.rsqrt(var + A_GN_EPS) * lng_ref[...] + lnb_ref[...]
    y_ref[...] = ((on + bonus_s[...]) * gate_s[...]).astype(y_ref.dtype)


def _rwkv_params(p):
    depth = p["a_w0"].shape[0]
    row = lambda a: a.reshape(depth, 1, -1).astype(F32)
    w = A_WIDTH
    lora = jnp.zeros((depth, A_LORA_IN, 2 * w), F32)
    lora = lora.at[:, :A_DECAY_LORA, :w].set(p["a_w2"]).at[:, A_DECAY_LORA:, w:].set(p["a_a2"]).astype(BF16)
    hid = jnp.arange(A_SUMLANES) // A_HEAD_DIM
    ones = (hid[:, None] == hid[None, :]).astype(BF16)
    ti = jnp.arange(A_CHUNK)
    tri = (ti[:, None] >= ti[None, :]).astype(BF16)
    layered = [row(p["a_shift"]), row(p["a_w0"]), row(p["a_a0"]), lora, p["a_g2"].astype(BF16),
               row(p["a_kk"]), row(p["a_ka"]), row(p["a_rk"]), row(p["a_lnx_g"]), row(p["a_lnx_b"])]
    return layered, [ones, tri]


def _proj_rwkv(x2d, w_in_bf, params, rope_tabs, bsz, seq, l, tm):
    layered, shared = params
    cos2, sin2 = rope_tabs
    k = x2d.shape[1]
    w = A_WIDTH
    per_seq = seq // tm
    n_chunks = tm // A_CHUNK
    n_groups = w // A_GLANES
    row_tile = lambda n: pl.BlockSpec((tm, n), lambda b, i: (b * per_seq + i, 0))
    consts = [w_in_bf] + layered
    stage = pltpu.VMEM((tm, w), F32)
    per_sq = pltpu.VMEM((n_chunks * n_groups, A_GLANES, A_GLANES), F32)
    per_row = pltpu.VMEM((n_chunks * n_groups, A_CHUNK, A_GLANES), F32)
    return pl.pallas_call(
        _proj_rwkv_kernel,
        grid=(bsz, per_seq),
        in_specs=[row_tile(k)] + [_layer_spec(a, l) for a in consts] + [_full(a.shape) for a in shared]
                 + [pl.BlockSpec((tm, B_QK_DIM), lambda b, i: (i, 0))] * 2,
        out_specs=[row_tile(REST_COLS), row_tile(w)],
        out_shape=[jax.ShapeDtypeStruct((bsz * seq, REST_COLS), BF16),
                   jax.ShapeDtypeStruct((bsz * seq, w), BF16)],
        scratch_shapes=[pltpu.VMEM((tm, k), BF16),
                        pltpu.VMEM((8, A_PROJ), F32),
                        pltpu.VMEM((n_groups, A_HEAD_DIM, A_GLANES), F32),
                        stage, stage, stage, stage, stage, stage, stage, stage, stage,
                        pltpu.VMEM((max(8, n_chunks), w), F32),
                        per_sq, per_row, per_row, per_row],
        compiler_params=_cparams("parallel", "arbitrary"),
        name="proj_rwkv7",
    )(x2d, *consts, *shared, cos2, sin2)


def _ret_kernel(z_ref, dmask_ref, qd_ref, kd_ref, cd_ref, y_ref, state_s, upd_s, st_s):
    tc = z_ref.shape[0]
    n_chunks = tc // B_CHUNK
    k_off, v_off = B_QK_WIDTH, 2 * B_QK_WIDTH

    @pl.when(pl.program_id(1) == 0)
    def _():
        state_s[...] = jnp.zeros_like(state_s)

    items = [(c, h) for c in range(n_chunks) for h in range(B_HEADS)]
    rows = lambda c: slice(c * B_CHUNK, (c + 1) * B_CHUNK)
    qcol = lambda h: slice(h * B_QK_DIM, (h + 1) * B_QK_DIM)
    kcol = lambda h: slice(k_off + h * B_QK_DIM, k_off + (h + 1) * B_QK_DIM)
    vcol = lambda h: slice(v_off + h * B_V_DIM, v_off + (h + 1) * B_V_DIM)

    for c, h in items:
        upd_s[c * B_HEADS + h] = _dot_tn(z_ref[rows(c), kcol(h)] * kd_ref[h], z_ref[rows(c), vcol(h)])
    for h in range(B_HEADS):
        st = state_s[h]
        for c in range(n_chunks):
            st_s[c * B_HEADS + h] = st
            st = cd_ref[h] * st + upd_s[c * B_HEADS + h]
        state_s[h] = st

    for c, h in items:
        qc = z_ref[rows(c), qcol(h)]
        scores = _dot_nt(qc, z_ref[rows(c), kcol(h)]) * dmask_ref[h]
        o = _dot(scores, z_ref[rows(c), vcol(h)]) + _dot(qc * qd_ref[h], st_s[c * B_HEADS + h])
        mu = jnp.mean(o, axis=-1, keepdims=True)
        d = o - mu
        var = jnp.mean(d * d, axis=-1, keepdims=True)
        y_ref[rows(c), h * B_V_DIM:(h + 1) * B_V_DIM] = (d * lax.rsqrt(var + B_GN_EPS)).astype(y_ref.dtype)


def _retention_tables(seq):
    f32 = F32
    pos = jnp.arange(seq, dtype=f32)
    half = B_QK_DIM // 2
    inv_freq = B_ROPE_BASE ** (-jnp.arange(half, dtype=f32) / half)
    ang = pos[:, None] * inv_freq[None, :]
    cos, sin = jnp.cos(ang), jnp.sin(ang)
    cos2 = jnp.concatenate([cos, cos], axis=1)
    sin2 = jnp.concatenate([-sin, sin], axis=1)
    log_gamma = jnp.log(1.0 - 2.0 ** (-5.0 - jnp.arange(B_HEADS, dtype=f32)))
    idx = jnp.arange(B_CHUNK, dtype=f32)
    rel = idx[:, None] - idx[None, :]
    dmask = jnp.where(rel >= 0, jnp.exp(log_gamma[:, None, None] * jnp.maximum(rel, 0.0)), 0.0)
    qd = jnp.broadcast_to(jnp.exp(log_gamma[:, None] * (idx + 1.0))[:, :, None],
                          (B_HEADS, B_CHUNK, B_QK_DIM))
    kd = jnp.broadcast_to(jnp.exp(log_gamma[:, None] * (B_CHUNK - 1.0 - idx))[:, :, None],
                          (B_HEADS, B_CHUNK, B_QK_DIM))
    cd = jnp.broadcast_to(jnp.exp(log_gamma * B_CHUNK)[:, None, None], (B_HEADS, B_QK_DIM, B_V_DIM))
    return (cos2, sin2), [dmask, qd.astype(BF16), kd.astype(BF16), cd]


def _retention(zmix, bsz, seq, tabs, tc):
    n_items = (tc // B_CHUNK) * B_HEADS
    per_item = pltpu.VMEM((n_items, B_QK_DIM, B_V_DIM), F32)
    return pl.pallas_call(
        _ret_kernel,
        grid=(bsz, seq // tc),
        in_specs=[pl.BlockSpec((None, tc, QKV_COLS), lambda b, i: (b, i, 0))] + [_full(a.shape) for a in tabs],
        out_specs=pl.BlockSpec((None, tc, B_V_WIDTH), lambda b, i: (b, i, 0)),
        out_shape=jax.ShapeDtypeStruct((bsz, seq, B_V_WIDTH), BF16),
        scratch_shapes=[pltpu.VMEM((B_HEADS, B_QK_DIM, B_V_DIM), F32), per_item, per_item],
        compiler_params=_cparams("parallel", "arbitrary"),
        name="retention",
    )(zmix, *tabs)


def _s5_kernel(u_ref, bbr_ref, bbi_ref, ar_ref, ai_ref, cr_ref, ci_ref, d_ref, wg_ref, bg_ref,
               y_ref, xr_s, xi_s, u_s, sr_s, si_s):
    bsz, tt, _ = u_ref.shape
    tp = tt // C_PARTS
    rp = tp * bsz

    @pl.when(pl.program_id(0) == 0)
    def _():
        sr_s[...] = jnp.zeros_like(sr_s)
        si_s[...] = jnp.zeros_like(si_s)

    blk = [slice(m * C_SLANES, (m + 1) * C_SLANES) for m in range(C_BLOCKS)]


    def drive(p):
        rows = slice(p * rp, (p + 1) * rp)
        u = pltpu.einshape("btc->tbc", u_ref[:, p * tp:(p + 1) * tp, :].astype(F32)).reshape(rp, C_WIDTH)
        u_s[rows, :] = u
        ub = u.astype(BF16)
        for m in range(C_BLOCKS):
            um = ub[:, m * C_ULANES:(m + 1) * C_ULANES]
            xr_s[rows, blk[m]] = jnp.dot(um, bbr_ref[m], preferred_element_type=F32)
            xi_s[rows, blk[m]] = jnp.dot(um, bbi_ref[m], preferred_element_type=F32)

    def scan(p):
        for m in range(C_BLOCKS):
            cols = blk[m]
            ar = jnp.broadcast_to(ar_ref[:, cols], (bsz, C_SLANES))
            ai = jnp.broadcast_to(ai_ref[:, cols], (bsz, C_SLANES))
            xr, xi = sr_s[:, cols], si_s[:, cols]
            for t in range(p * tp, (p + 1) * tp):
                rows = slice(t * bsz, (t + 1) * bsz)
                xr, xi = (ar * xr - ai * xi + xr_s[rows, cols], ar * xi + ai * xr + xi_s[rows, cols])
                xr_s[rows, cols] = xr
                xi_s[rows, cols] = xi
            sr_s[:, cols] = xr
            si_s[:, cols] = xi

    def readout(p):
        rows = slice(p * rp, (p + 1) * rp)
        parts = [_dot(xr_s[rows, blk[m]], cr_ref[m]) - _dot(xi_s[rows, blk[m]], ci_ref[m])
                 for m in range(C_BLOCKS)]
        y = jnp.concatenate(parts, axis=1) + d_ref[...] * u_s[rows, :]
        y = jax.nn.gelu(y)
        y = y * _sigmoid(_dot(y, wg_ref[...]) + bg_ref[...])
        y_ref[:, p * tp:(p + 1) * tp, :] = pltpu.einshape(
            "tbc->btc", y.reshape(tp, bsz, C_WIDTH)).astype(y_ref.dtype)

    drive(0)
    for p in range(C_PARTS):
        if p + 1 < C_PARTS:
            drive(p + 1)
        scan(p)
        readout(p)


def _s5_params(p):
    f32 = F32
    depth = p["c_log_dt"].shape[0]
    dt = jnp.exp(p["c_log_dt"].astype(f32))[..., None]
    lr, li = p["c_lam_re"].astype(f32), p["c_lam_im"].astype(f32)
    mag = jnp.exp(lr * dt)
    ab_re, ab_im = mag * jnp.cos(li * dt), mag * jnp.sin(li * dt)
    den = lr * lr + li * li
    f_re = ((ab_re - 1.0) * lr + ab_im * li) / den
    f_im = (ab_im * lr - (ab_re - 1.0) * li) / den
    bre, bim = p["c_b_re"].astype(f32), p["c_b_im"].astype(f32)
    bb_re = f_re[..., None] * bre - f_im[..., None] * bim
    bb_im = f_re[..., None] * bim + f_im[..., None] * bre
    gpb = C_GROUPS // C_BLOCKS
    eye = jnp.eye(gpb, dtype=f32)

    def in_blocks(bb):
        bb = bb.reshape(depth, C_BLOCKS, gpb, C_STATE, C_GROUP)
        return jnp.einsum("lmgpc,gh->lmgchp", bb, eye).reshape(depth, C_BLOCKS, C_ULANES, C_SLANES).astype(BF16)

    def out_blocks(cc):
        cc = cc.astype(f32).reshape(depth, C_BLOCKS, gpb, C_GROUP, C_STATE)
        return jnp.einsum("lmgcp,gh->lmgphc", cc, eye).reshape(depth, C_BLOCKS, C_SLANES, C_ULANES).astype(BF16)

    return [in_blocks(bb_re), in_blocks(bb_im), ab_re.reshape(depth, 1, C_LANES), ab_im.reshape(depth, 1, C_LANES),
            out_blocks(p["c_c_re"]), out_blocks(p["c_c_im"]), p["c_d"].reshape(depth, 1, C_WIDTH).astype(f32),
            p["c_w_glu"].astype(BF16), p["c_b_glu"].reshape(depth, 1, C_WIDTH).astype(f32)]


def _s5(zmix, bsz, seq, consts, l, tt):
    rows = tt * bsz
    return pl.pallas_call(
        _s5_kernel,
        grid=(seq // tt,),
        in_specs=[pl.BlockSpec((bsz, tt, C_WIDTH), lambda i: (0, i, U_BLK))] + [_layer_spec(a, l) for a in consts],
        out_specs=pl.BlockSpec((bsz, tt, C_WIDTH), lambda i: (0, i, 0)),
        out_shape=jax.ShapeDtypeStruct((bsz, seq, C_WIDTH), BF16),
        scratch_shapes=[pltpu.VMEM((rows, C_LANES), F32), pltpu.VMEM((rows, C_LANES), F32),
                        pltpu.VMEM((rows, C_WIDTH), F32),
                        pltpu.VMEM((bsz, C_LANES), F32), pltpu.VMEM((bsz, C_LANES), F32)],
        compiler_params=_cparams("arbitrary"),
        name="s5",
    )(zmix, *consts)


def _merge_kernel(alpha, x_ref, ya_ref, yb_ref, gb_ref, yc_ref, win_ref, bgate_ref, wb_ref,
                  wout_ref, g_ref, b_ref, o_ref):
    tm, d = x_ref.shape
    b_lo, c_lo = A_WIDTH, A_WIDTH + B_V_WIDTH
    for rows in _row_parts(tm):
        x = x_ref[rows, :]
        gates = _sigmoid(_dot(x, win_ref[:, MIX_COLS:]) + bgate_ref[...])
        yb = yb_ref[rows, :].astype(F32) * gb_ref[rows, :].astype(F32)
        merged = (gates[:, :d] * _dot(ya_ref[rows, :], wb_ref[:b_lo, :])
                  + gates[:, d:2 * d] * _dot(yb, wb_ref[b_lo:c_lo, :])
                  + gates[:, 2 * d:] * _dot(yc_ref[rows, :], wb_ref[c_lo:, :]))
        o_ref[rows, :] = _layer_norm(alpha * x + _dot(merged, wout_ref[...]), g_ref[...], b_ref[...])


def _merge(x2d, ya, yb, zmix2d, yc, consts, l, alpha, tm):
    m, d = x2d.shape
    tile = lambda n: pl.BlockSpec((tm, n), lambda i: (i, 0))
    gate_tile = pl.BlockSpec((tm, B_V_WIDTH), lambda i: (i, G_BLK))
    return pl.pallas_call(
        functools.partial(_merge_kernel, alpha),
        grid=(m // tm,),
        in_specs=[tile(d), tile(A_WIDTH), tile(B_V_WIDTH), gate_tile, tile(C_WIDTH)]
                 + [_layer_spec(a, l) for a in consts],
        out_specs=tile(d),
        out_shape=jax.ShapeDtypeStruct((m, d), F32),
        compiler_params=_cparams("parallel"),
        name="merge",
    )(x2d, ya, yb, zmix2d, yc, *consts)


def _ffn_kernel(alpha, tf, x_ref, w1_ref, w2_ref, g_ref, b_ref, o_ref):
    x = x_ref[...]
    xb = x.astype(BF16)
    acc = None
    for j in range(w1_ref.shape[1] // tf):
        h = jnp.maximum(jnp.dot(xb, w1_ref[:, j * tf:(j + 1) * tf], preferred_element_type=F32), 0.0)
        part = jnp.dot((h * h).astype(BF16), w2_ref[j * tf:(j + 1) * tf, :], preferred_element_type=F32)
        acc = part if acc is None else acc + part
    o_ref[...] = _layer_norm(alpha * x + acc, g_ref[...], b_ref[...])


def _ffn(x2d, consts, l, alpha, tm, tf):
    m, d = x2d.shape
    return pl.pallas_call(
        functools.partial(_ffn_kernel, alpha, tf),
        grid=(m // tm,),
        in_specs=[pl.BlockSpec((tm, d), lambda i: (i, 0))] + [_layer_spec(a, l) for a in consts],
        out_specs=pl.BlockSpec((tm, d), lambda i: (i, 0)),
        out_shape=jax.ShapeDtypeStruct((m, d), F32),
        compiler_params=_cparams("parallel"),
        name="ffn",
    )(x2d, *consts)


def _tile(n, want):
    t = min(n, want)
    assert n % t == 0, (n, want)
    return t


def kernel(x, w_in, b_gate, a_shift, a_w0, a_w2, a_a0, a_a2, a_g2, a_kk, a_ka, a_rk, a_lnx_g, a_lnx_b, c_lam_re, c_lam_im, c_log_dt, c_b_re, c_b_im, c_c_re, c_c_im, c_d, c_w_glu, c_b_glu, w_branch, w_out, ln1_g, ln1_b, w_ff1, w_ff2, ln2_g, ln2_b):
    bsz, seq, d = x.shape
    depth = w_in.shape[0]
    alpha = (2.0 * depth) ** 0.25
    tokens = bsz * seq
    rows = lambda a: a.reshape(depth, 1, -1)
    rwkv_params = _rwkv_params(dict(a_shift=a_shift, a_w0=a_w0, a_w2=a_w2, a_a0=a_a0, a_a2=a_a2, a_g2=a_g2,
                                    a_kk=a_kk, a_ka=a_ka, a_rk=a_rk, a_lnx_g=a_lnx_g, a_lnx_b=a_lnx_b))
    s5_params = _s5_params(dict(c_lam_re=c_lam_re, c_lam_im=c_lam_im, c_log_dt=c_log_dt, c_b_re=c_b_re,
                                c_b_im=c_b_im, c_c_re=c_c_re, c_c_im=c_c_im, c_d=c_d, c_w_glu=c_w_glu,
                                c_b_glu=c_b_glu))
    rope_tabs, ret_tables = _retention_tables(seq)
    w_in_bf = w_in.astype(BF16)
    merge_consts = [w_in_bf, rows(b_gate), w_branch.astype(BF16), w_out.astype(BF16),
                    rows(ln1_g), rows(ln1_b)]
    ffn_consts = [w_ff1.astype(BF16), w_ff2.astype(BF16), rows(ln2_g), rows(ln2_b)]
    xt = x.reshape(tokens, d)
    for l in range(depth):
        zmix, ya = _proj_rwkv(xt, w_in_bf, rwkv_params, rope_tabs, bsz, seq, l, _tile(seq, TM_PROJ_RWKV))
        zmix = zmix.reshape(bsz, seq, REST_COLS)
        yb = _retention(zmix, bsz, seq, ret_tables, _tile(seq, TC_RETENTION))
        yc = _s5(zmix, bsz, seq, s5_params, l, _tile(seq, TT_S5))
        x1 = _merge(xt, ya.reshape(tokens, -1), yb.reshape(tokens, -1), zmix.reshape(tokens, REST_COLS),
                    yc.reshape(tokens, -1), merge_consts, l, alpha, _tile(tokens, TM_MERGE))
        xt = _ffn(x1, ffn_consts, l, alpha, _tile(tokens, TM_FFN), TF_FFN)
    return xt.reshape(bsz, seq, d)
```

```python
import functools
import math

import jax
import jax.numpy as jnp
from jax import lax
from jax.experimental import pallas as pl
from jax.experimental.pallas import tpu as pltpu

F32 = jnp.float32
BF16 = jnp.bfloat16

A_HEADS = 8
A_HEAD_DIM = 64
A_WIDTH = A_HEADS * A_HEAD_DIM
A_DECAY_LORA = 64
A_ICLR_LORA = 64
A_GATE_LORA = 128
A_PROJ = 3 * A_WIDTH + A_DECAY_LORA + A_ICLR_LORA + A_GATE_LORA
A_GN_EPS = 64e-5
A_CHUNK = 64
A_GROUP = 8
A_GLANES = 128
A_SUMLANES = 256
A_LORA_IN = A_DECAY_LORA + A_ICLR_LORA
PROJ_CHUNK = 512
PROJ_EVERY = 4

B_HEADS = 4
B_QK_DIM = 128
B_V_DIM = 256
B_QK_WIDTH = B_HEADS * B_QK_DIM
B_V_WIDTH = B_HEADS * B_V_DIM
B_CHUNK = 128
B_ROPE_BASE = 10000.0
B_GN_EPS = 1e-5

C_WIDTH = 512
C_GROUP = 16
C_GROUPS = C_WIDTH // C_GROUP
C_STATE = 64
C_LANES = C_GROUPS * C_STATE
C_BLOCKS = 4
C_ULANES = C_WIDTH // C_BLOCKS
C_SLANES = C_LANES // C_BLOCKS
C_PARTS = 4

LN_EPS = 1e-5

RET_COLS = 2 * B_QK_WIDTH + 2 * B_V_WIDTH
REST_COLS = RET_COLS + C_WIDTH
MIX_COLS = A_PROJ + REST_COLS
QKV_COLS = 2 * B_QK_WIDTH + B_V_WIDTH
U_BLK = RET_COLS // C_WIDTH
G_BLK = QKV_COLS // B_V_WIDTH
assert RET_COLS % C_WIDTH == 0 and QKV_COLS % B_V_WIDTH == 0

V7X_VMEM_LIMIT_BYTES = 56 * 1024 * 1024

TM_PROJ_RWKV = 512
TC_RETENTION = 1024
TT_S5 = 64
TM_MERGE = 512
TM_FFN = 1024
TF_FFN = 1024


def _cparams(*sem):
    return pltpu.CompilerParams(dimension_semantics=sem, vmem_limit_bytes=V7X_VMEM_LIMIT_BYTES)


def _full(shape):
    n = len(shape)
    return pl.BlockSpec(shape, lambda *_: (0,) * n)


def _layer_spec(a, l):
    n = a.ndim - 1
    return pl.BlockSpec((None,) + a.shape[1:], lambda *_: (l,) + (0,) * n, pipeline_mode=pl.Buffered(1))


def _dot(a, b):
    return jnp.dot(a.astype(BF16), b.astype(BF16), preferred_element_type=F32)


def _dot_nt(a, b):
    return lax.dot_general(a.astype(BF16), b.astype(BF16), (((1,), (1,)), ((), ())),
                           preferred_element_type=F32)


def _dot_tn(a, b):
    return lax.dot_general(a.astype(BF16), b.astype(BF16), (((0,), (0,)), ((), ())),
                           preferred_element_type=F32)


def _split2(x):
    hi = x.astype(BF16)
    lo = (x - hi.astype(F32)).astype(BF16)
    return hi, lo


def _sigmoid(x):
    return 1.0 / (1.0 + jnp.exp(-x))


def _row_parts(n, parts=2):
    step = n // parts
    return [slice(i * step, (i + 1) * step) for i in range(parts)]


def _layer_norm(y, g, b):
    mu = jnp.mean(y, axis=-1, keepdims=True)
    d = y - mu
    var = jnp.mean(d * d, axis=-1, keepdims=True)
    return d * lax.rsqrt(var + LN_EPS) * g + b


def _chunk_rows(c):
    return pl.ds(c * A_CHUNK, A_CHUNK)


def _proj_rwkv_kernel(x_ref, win_ref, mu_ref, w0_ref, a0_ref, lora_ref, g2_ref, kkp_ref, kap_ref,
                      rkp_ref, lng_ref, lnb_ref, ones_ref, tri_ref, cos_ref, sin_ref,
                      rest_ref, y_ref,
                      xb_s, carry_s, state_s, r_s, k_s, v_s, kn_s, b0_s, lw_s, gate_s, bonus_s, o_s,
                      gam_s, x_s, m2_s, q_s, op_s):
    tm = x_ref.shape[0]
    n_chunks = tm // A_CHUNK
    w = A_WIDTH

    @pl.when(pl.program_id(1) == 0)
    def _():
        carry_s[...] = jnp.zeros_like(carry_s)
        state_s[...] = jnp.zeros_like(state_s)

    xb_s[...] = x_ref[...].astype(BF16)
    z = jnp.dot(xb_s[...], win_ref[:, :A_PROJ], preferred_element_type=F32)

    cos2 = cos_ref[...]
    sin2 = sin_ref[...]

    def rope(t):
        return t * cos2 + pltpu.roll(t, B_QK_DIM // 2, 1) * sin2

    k_off, v_off, g_off = B_QK_WIDTH, 2 * B_QK_WIDTH, 2 * B_QK_WIDTH + B_V_WIDTH

    def piece(lo):
        hi = lo + PROJ_CHUNK
        val = jnp.dot(xb_s[...], win_ref[:, A_PROJ + lo:A_PROJ + hi], preferred_element_type=F32)
        if lo < v_off:
            scale = 1.0 if lo < k_off else B_QK_DIM ** -0.5
            for j in range(PROJ_CHUNK // B_QK_DIM):
                hs = slice(j * B_QK_DIM, (j + 1) * B_QK_DIM)
                rest_ref[:, lo + j * B_QK_DIM:lo + (j + 1) * B_QK_DIM] = (
                    rope(val[:, hs]) * scale).astype(rest_ref.dtype)
        elif g_off <= lo < RET_COLS:
            rest_ref[:, lo:hi] = (val * _sigmoid(val)).astype(rest_ref.dtype)
        else:
            rest_ref[:, lo:hi] = val.astype(rest_ref.dtype)

    pending = list(range(0, REST_COLS, PROJ_CHUNK))
    calls = [0]

    def fill(n=1):
        for _ in range(n):
            if pending and calls[0] % PROJ_EVERY == 0:
                piece(pending.pop(0))
            calls[0] += 1

    def flush():
        while pending:
            piece(pending.pop(0))

    ones = ones_ref[...]
    gw = A_GLANES
    groups = [slice(g * gw, (g + 1) * gw) for g in range(w // gw)]
    slabs = [slice(g * A_SUMLANES, (g + 1) * A_SUMLANES) for g in range(w // A_SUMLANES)]

    def seg_sum(t):
        tb = t.astype(BF16)
        return jnp.concatenate([jnp.dot(tb[:, s], ones, preferred_element_type=F32) for s in slabs], axis=1)

    rolled = pltpu.roll(z, 1, 0)
    rowid = lax.broadcasted_iota(jnp.int32, z.shape, 0)
    prev = jnp.where(rowid == 0, jnp.broadcast_to(carry_s[0:1, :], z.shape), rolled)
    zs = z + mu_ref[...] * (prev - z)
    carry_s[0:1, :] = z[tm - 1:tm, :]
    fill()

    lz = zs[:, 3 * w:3 * w + A_LORA_IN]
    lane = lax.broadcasted_iota(jnp.int32, lz.shape, 1)
    lin = jnp.where(lane < A_DECAY_LORA, jnp.tanh(lz), lz)
    wa = _dot(lin, lora_ref[...])
    lw_s[...] = (-math.exp(-0.5)) * _sigmoid(w0_ref[...] + wa[:, :w])
    ia = _sigmoid(a0_ref[...] + wa[:, w:])
    gate_s[...] = _dot(_sigmoid(zs[:, 3 * w + A_LORA_IN:3 * w + A_LORA_IN + A_GATE_LORA]), g2_ref[...])

    r = zs[:, :w]
    k = zs[:, w:2 * w]
    v = zs[:, 2 * w:3 * w]
    kk = k * kkp_ref[...]
    kn = kk * lax.rsqrt(jnp.maximum(seg_sum(kk * kk), 1e-24))
    kmod = k * (1.0 + (ia - 1.0) * kap_ref[...])
    r_s[...] = r
    k_s[...] = kmod
    v_s[...] = v
    kn_s[...] = kn
    b0_s[...] = kn * ia
    bonus_s[...] = seg_sum(r * kmod * rkp_ref[...]) * v
    fill()

    tri = tri_ref[...]
    hpg = gw // A_HEAD_DIM
    rid = lax.broadcasted_iota(jnp.int32, (A_CHUNK, gw), 0)
    cid = lax.broadcasted_iota(jnp.int32, (A_CHUNK, gw), 1) % A_HEAD_DIM
    strict = rid > cid
    incl = rid >= cid
    eye = (rid == cid).astype(F32)
    brow = lax.broadcasted_iota(jnp.int32, (gw, gw), 0) // A_HEAD_DIM
    bcol = lax.broadcasted_iota(jnp.int32, (gw, gw), 1) // A_HEAD_DIM
    same_head = brow == bcol
    same_head_bf = same_head.astype(BF16)
    n = A_HEAD_DIM

    def bd(x):
        xb = x.astype(BF16)
        return jnp.concatenate([xb] * hpg, axis=0) * same_head_bf


    group = min(A_GROUP, n_chunks)
    assert n_chunks % group == 0

    def state_free_part(gi):
        items = []
        for j in range(group):
            c = gi * group + j
            rows = _chunk_rows(c)
            lw = lw_s[rows, :]
            hi, lo = _split2(lw)
            cum = (jnp.dot(tri, hi, preferred_element_type=F32)
                   + jnp.dot(tri, lo, preferred_element_type=F32))
            e_in = jnp.exp(cum)
            e_ex = jnp.exp(cum - lw)
            e_ng = jnp.exp(-cum)
            rt = r_s[rows, :] * e_in
            at = -kn_s[rows, :] * e_ex
            bt = b0_s[rows, :] * e_ng
            kt = k_s[rows, :] * e_ng
            vv = v_s[rows, :]
            gam_s[pl.ds(c, 1), :] = e_in[A_CHUNK - 1:A_CHUNK, :]
            for g, s in enumerate(groups):
                items.append((c * len(groups) + g, at[:, s], rt[:, s], bt[:, s], kt[:, s], vv[:, s]))
            if j % 2 == 1:
                fill()
        ids = range(len(items))
        idx = [it[0] for it in items]
        a_ = [it[1] for it in items]
        r_ = [it[2] for it in items]
        b_ = [it[3] for it in items]
        k_ = [it[4] for it in items]
        v_ = [it[5] for it in items]
        ar = [jnp.concatenate([a_[i], r_[i]], axis=0) for i in ids]
        gb = [_dot_nt(ar[i], bd(b_[i])) for i in ids]
        fill()
        gk = [_dot_nt(ar[i], bd(k_[i])) for i in ids]
        fill()
        l_ab = [jnp.where(strict, gb[i][:A_CHUNK], 0.0) for i in ids]
        a_qb = [jnp.where(incl, gb[i][A_CHUNK:], 0.0) for i in ids]
        akq = [jnp.concatenate([jnp.where(strict, gk[i][:A_CHUNK], 0.0),
                                jnp.where(incl, gk[i][A_CHUNK:], 0.0)], axis=0) for i in ids]
        akqv = [_dot(akq[i], bd(v_[i])) for i in ids]
        fill()
        akv = [akqv[i][:A_CHUNK] for i in ids]
        ov = [akqv[i][A_CHUNK:] for i in ids]
        tinv = [eye + l_ab[i] for i in ids]
        p = [_dot(l_ab[i], bd(l_ab[i])) for i in ids]
        fill()
        for _ in range(4):
            tp = [_dot(jnp.concatenate([tinv[i], p[i]], axis=0), bd(p[i])) for i in ids]
            fill()
            tinv = [tinv[i] + tp[i][:A_CHUNK] for i in ids]
            p = [tp[i][A_CHUNK:] for i in ids]
        tinv = [tinv[i] + _dot(tinv[i], bd(p[i])) for i in ids]
        fill()
        wu = [_dot(tinv[i], jnp.concatenate([bd(a_[i]), bd(akv[i])], axis=1)) for i in ids]
        fill()
        wm = [wu[i][:, :gw] for i in ids]
        uv = [wu[i][:, gw:] for i in ids]
        for i in ids:
            x_s[idx[i]] = jnp.where(same_head, _dot_tn(wm[i], b_[i]), 0.0)
        fill()
        for i in ids:
            m2 = jnp.where(same_head, _dot_tn(jnp.concatenate([uv[i], v_[i]], axis=0),
                                              jnp.concatenate([b_[i], k_[i]], axis=0)), 0.0)
            m2_s[idx[i]] = sum(m2[h * n:(h + 1) * n] for h in range(1, hpg)) + m2[:n]
        fill()
        qo = [_dot(a_qb[i], jnp.concatenate([bd(wm[i]), bd(uv[i])], axis=1)) for i in ids]
        fill()
        for i in ids:
            q_s[idx[i]] = r_[i] + qo[i][:, :gw]
        for i in ids:
            op_s[idx[i]] = qo[i][:, gw:] + ov[i]

    def state_part(c):
        rows = _chunk_rows(c)
        gam = gam_s[pl.ds(c, 1), :]
        gids = range(len(groups))
        s0 = [state_s[g] for g in gids]
        sx = [_dot(s0[g], x_s[c * len(groups) + g]) for g in gids]
        o = [_dot_nt(q_s[c * len(groups) + g], bd(s0[g])) + op_s[c * len(groups) + g] for g in gids]
        for g in gids:
            state_s[g] = (s0[g] + sx[g] + m2_s[c * len(groups) + g]) * gam[:, groups[g]]
        o_s[rows, :] = jnp.concatenate(o, axis=1)
        fill()

    for gi in range(n_chunks // group):
        state_free_part(gi)
    for c in range(n_chunks):
        state_part(c)
    flush()

    o = o_s[...]
    inv_n = 1.0 / n
    mean = seg_sum(o) * inv_n
    d = o - mean
    var = seg_sum(d * d) * inv_n
    on = d * lax.rsqrt(var + A_GN_EPS) * lng_ref[...] + lnb_ref[...]
    y_ref[...] = ((on + bonus_s[...]) * gate_s[...]).astype(y_ref.dtype)


def _rwkv_params(p):
    depth = p["a_w0"].shape[0]
    row = lambda a: a.reshape(depth, 1, -1).astype(F32)
    w = A_WIDTH
    lora = jnp.zeros((depth, A_LORA_IN, 2 * w), F32)
    lora = lora.at[:, :A_DECAY_LORA, :w].set(p["a_w2"]).at[:, A_DECAY_LORA:, w:].set(p["a_a2"]).astype(BF16)
    hid = jnp.arange(A_SUMLANES) // A_HEAD_DIM
    ones = (hid[:, None] == hid[None, :]).astype(BF16)
    ti = jnp.arange(A_CHUNK)
    tri = (ti[:, None] >= ti[None, :]).astype(BF16)
    layered = [row(p["a_shift"]), row(p["a_w0"]), row(p["a_a0"]), lora, p["a_g2"].astype(BF16),
               row(p["a_kk"]), row(p["a_ka"]), row(p["a_rk"]), row(p["a_lnx_g"]), row(p["a_lnx_b"])]
    return layered, [ones, tri]


def _proj_rwkv(x2d, w_in_bf, params, rope_tabs, bsz, seq, l, tm):
    layered, shared = params
    cos2, sin2 = rope_tabs
    k = x2d.shape[1]
    w = A_WIDTH
    per_seq = seq // tm
    n_chunks = tm // A_CHUNK
    n_groups = w // A_GLANES
    row_tile = lambda n: pl.BlockSpec((tm, n), lambda b, i: (b * per_seq + i, 0))
    consts = [w_in_bf] + layered
    stage = pltpu.VMEM((tm, w), F32)
    per_sq = pltpu.VMEM((n_chunks * n_groups, A_GLANES, A_GLANES), F32)
    per_row = pltpu.VMEM((n_chunks * n_groups, A_CHUNK, A_GLANES), F32)
    return pl.pallas_call(
        _proj_rwkv_kernel,
        grid=(bsz, per_seq),
        in_specs=[row_tile(k)] + [_layer_spec(a, l) for a in consts] + [_full(a.shape) for a in shared]
                 + [pl.BlockSpec((tm, B_QK_DIM), lambda b, i: (i, 0))] * 2,
        out_specs=[row_tile(REST_COLS), row_tile(w)],
        out_shape=[jax.ShapeDtypeStruct((bsz * seq, REST_COLS), BF16),
                   jax.ShapeDtypeStruct((bsz * seq, w), BF16)],
        scratch_shapes=[pltpu.VMEM((tm, k), BF16),
                        pltpu.VMEM((8, A_PROJ), F32),
                        pltpu.VMEM((n_groups, A_HEAD_DIM, A_GLANES), F32),
                        stage, stage, stage, stage, stage, stage, stage, stage, stage,
                        pltpu.VMEM((max(8, n_chunks), w), F32),
                        per_sq, per_row, per_row, per_row],
        compiler_params=_cparams("parallel", "arbitrary"),
        name="proj_rwkv7",
    )(x2d, *consts, *shared, cos2, sin2)


def _ret_kernel(z_ref, dmask_ref, qd_ref, kd_ref, cd_ref, y_ref, state_s, upd_s, st_s):
    tc = z_ref.shape[0]
    n_chunks = tc // B_CHUNK
    k_off, v_off = B_QK_WIDTH, 2 * B_QK_WIDTH

    @pl.when(pl.program_id(1) == 0)
    def _():
        state_s[...] = jnp.zeros_like(state_s)

    items = [(c, h) for c in range(n_chunks) for h in range(B_HEADS)]
    rows = lambda c: slice(c * B_CHUNK, (c + 1) * B_CHUNK)
    qcol = lambda h: slice(h * B_QK_DIM, (h + 1) * B_QK_DIM)
    kcol = lambda h: slice(k_off + h * B_QK_DIM, k_off + (h + 1) * B_QK_DIM)
    vcol = lambda h: slice(v_off + h * B_V_DIM, v_off + (h + 1) * B_V_DIM)

    for c, h in items:
        upd_s[c * B_HEADS + h] = _dot_tn(z_ref[rows(c), kcol(h)] * kd_ref[h], z_ref[rows(c), vcol(h)])
    for h in range(B_HEADS):
        st = state_s[h]
        for c in range(n_chunks):
            st_s[c * B_HEADS + h] = st
            st = cd_ref[h] * st + upd_s[c * B_HEADS + h]
        state_s[h] = st

    for c, h in items:
        qc = z_ref[rows(c), qcol(h)]
        scores = _dot_nt(qc, z_ref[rows(c), kcol(h)]) * dmask_ref[h]
        o = _dot(scores, z_ref[rows(c), vcol(h)]) + _dot(qc * qd_ref[h], st_s[c * B_HEADS + h])
        mu = jnp.mean(o, axis=-1, keepdims=True)
        d = o - mu
        var = jnp.mean(d * d, axis=-1, keepdims=True)
        y_ref[rows(c), h * B_V_DIM:(h + 1) * B_V_DIM] = (d * lax.rsqrt(var + B_GN_EPS)).astype(y_ref.dtype)


def _retention_tables(seq):
    f32 = F32
    pos = jnp.arange(seq, dtype=f32)
    half = B_QK_DIM // 2
    inv_freq = B_ROPE_BASE ** (-jnp.arange(half, dtype=f32) / half)
    ang = pos[:, None] * inv_freq[None, :]
    cos, sin = jnp.cos(ang), jnp.sin(ang)
    cos2 = jnp.concatenate([cos, cos], axis=1)
    sin2 = jnp.concatenate([-sin, sin], axis=1)
    log_gamma = jnp.log(1.0 - 2.0 ** (-5.0 - jnp.arange(B_HEADS, dtype=f32)))
    idx = jnp.arange(B_CHUNK, dtype=f32)
    rel = idx[:, None] - idx[None, :]
    dmask = jnp.where(rel >= 0, jnp.exp(log_gamma[:, None, None] * jnp.maximum(rel, 0.0)), 0.0)
    qd = jnp.broadcast_to(jnp.exp(log_gamma[:, None] * (idx + 1.0))[:, :, None],
                          (B_HEADS, B_CHUNK, B_QK_DIM))
    kd = jnp.broadcast_to(jnp.exp(log_gamma[:, None] * (B_CHUNK - 1.0 - idx))[:, :, None],
                          (B_HEADS, B_CHUNK, B_QK_DIM))
    cd = jnp.broadcast_to(jnp.exp(log_gamma * B_CHUNK)[:, None, None], (B_HEADS, B_QK_DIM, B_V_DIM))
    return (cos2, sin2), [dmask, qd.astype(BF16), kd.astype(BF16), cd]


def _retention(zmix, bsz, seq, tabs, tc):
    n_items = (tc // B_CHUNK) * B_HEADS
    per_item = pltpu.VMEM((n_items, B_QK_DIM, B_V_DIM), F32)
    return pl.pallas_call(
        _ret_kernel,
        grid=(bsz, seq // tc),
        in_specs=[pl.BlockSpec((None, tc, QKV_COLS), lambda b, i: (b, i, 0))] + [_full(a.shape) for a in tabs],
        out_specs=pl.BlockSpec((None, tc, B_V_WIDTH), lambda b, i: (b, i, 0)),
        out_shape=jax.ShapeDtypeStruct((bsz, seq, B_V_WIDTH), BF16),
        scratch_shapes=[pltpu.VMEM((B_HEADS, B_QK_DIM, B_V_DIM), F32), per_item, per_item],
        compiler_params=_cparams("parallel", "arbitrary"),
        name="retention",
    )(zmix, *tabs)


def _s5_kernel(u_ref, bbr_ref, bbi_ref, ar_ref, ai_ref, cr_ref, ci_ref, d_ref, wg_ref, bg_ref,
               y_ref, xr_s, xi_s, u_s, sr_s, si_s):
    bsz, tt, _ = u_ref.shape
    tp = tt // C_PARTS
    rp = tp * bsz

    @pl.when(pl.program_id(0) == 0)
    def _():
        sr_s[...] = jnp.zeros_like(sr_s)
        si_s[...] = jnp.zeros_like(si_s)

    blk = [slice(m * C_SLANES, (m + 1) * C_SLANES) for m in range(C_BLOCKS)]


    def drive(p):
        rows = slice(p * rp, (p + 1) * rp)
        u = pltpu.einshape("btc->tbc", u_ref[:, p * tp:(p + 1) * tp, :].astype(F32)).reshape(rp, C_WIDTH)
        u_s[rows, :] = u
        ub = u.astype(BF16)
        for m in range(C_BLOCKS):
            um = ub[:, m * C_ULANES:(m + 1) * C_ULANES]
            xr_s[rows, blk[m]] = jnp.dot(um, bbr_ref[m], preferred_element_type=F32)
            xi_s[rows, blk[m]] = jnp.dot(um, bbi_ref[m], preferred_element_type=F32)

    def scan(p):
        for m in range(C_BLOCKS):
            cols = blk[m]
            ar = jnp.broadcast_to(ar_ref[:, cols], (bsz, C_SLANES))
            ai = jnp.broadcast_to(ai_ref[:, cols], (bsz, C_SLANES))
            xr, xi = sr_s[:, cols], si_s[:, cols]
            for t in range(p * tp, (p + 1) * tp):
                rows = slice(t * bsz, (t + 1) * bsz)
                xr, xi = (ar * xr - ai * xi + xr_s[rows, cols], ar * xi + ai * xr + xi_s[rows, cols])
                xr_s[rows, cols] = xr
                xi_s[rows, cols] = xi
            sr_s[:, cols] = xr
            si_s[:, cols] = xi

    def readout(p):
        rows = slice(p * rp, (p + 1) * rp)
        parts = [_dot(xr_s[rows, blk[m]], cr_ref[m]) - _dot(xi_s[rows, blk[m]], ci_ref[m])
                 for m in range(C_BLOCKS)]
        y = jnp.concatenate(parts, axis=1) + d_ref[...] * u_s[rows, :]
        y = jax.nn.gelu(y)
        y = y * _sigmoid(_dot(y, wg_ref[...]) + bg_ref[...])
        y_ref[:, p * tp:(p + 1) * tp, :] = pltpu.einshape(
            "tbc->btc", y.reshape(tp, bsz, C_WIDTH)).astype(y_ref.dtype)

    drive(0)
    for p in range(C_PARTS):
        if p + 1 < C_PARTS:
            drive(p + 1)
        scan(p)
        readout(p)


def _s5_params(p):
    f32 = F32
    depth = p["c_log_dt"].shape[0]
    dt = jnp.exp(p["c_log_dt"].astype(f32))[..., None]
    lr, li = p["c_lam_re"].astype(f32), p["c_lam_im"].astype(f32)
    mag = jnp.exp(lr * dt)
    ab_re, ab_im = mag * jnp.cos(li * dt), mag * jnp.sin(li * dt)
    den = lr * lr + li * li
    f_re = ((ab_re - 1.0) * lr + ab_im * li) / den
    f_im = (ab_im * lr - (ab_re - 1.0) * li) / den
    bre, bim = p["c_b_re"].astype(f32), p["c_b_im"].astype(f32)
    bb_re = f_re[..., None] * bre - f_im[..., None] * bim
    bb_im = f_re[..., None] * bim + f_im[..., None] * bre
    gpb = C_GROUPS // C_BLOCKS
    eye = jnp.eye(gpb, dtype=f32)

    def in_blocks(bb):
        bb = bb.reshape(depth, C_BLOCKS, gpb, C_STATE, C_GROUP)
        return jnp.einsum("lmgpc,gh->lmgchp", bb, eye).reshape(depth, C_BLOCKS, C_ULANES, C_SLANES).astype(BF16)

    def out_blocks(cc):
        cc = cc.astype(f32).reshape(depth, C_BLOCKS, gpb, C_GROUP, C_STATE)
        return jnp.einsum("lmgcp,gh->lmgphc", cc, eye).reshape(depth, C_BLOCKS, C_SLANES, C_ULANES).astype(BF16)

    return [in_blocks(bb_re), in_blocks(bb_im), ab_re.reshape(depth, 1, C_LANES), ab_im.reshape(depth, 1, C_LANES),
            out_blocks(p["c_c_re"]), out_blocks(p["c_c_im"]), p["c_d"].reshape(depth, 1, C_WIDTH).astype(f32),
            p["c_w_glu"].astype(BF16), p["c_b_glu"].reshape(depth, 1, C_WIDTH).astype(f32)]


def _s5(zmix, bsz, seq, consts, l, tt):
    rows = tt * bsz
    return pl.pallas_call(
        _s5_kernel,
        grid=(seq // tt,),
        in_specs=[pl.BlockSpec((bsz, tt, C_WIDTH), lambda i: (0, i, U_BLK))] + [_layer_spec(a, l) for a in consts],
        out_specs=pl.BlockSpec((bsz, tt, C_WIDTH), lambda i: (0, i, 0)),
        out_shape=jax.ShapeDtypeStruct((bsz, seq, C_WIDTH), BF16),
        scratch_shapes=[pltpu.VMEM((rows, C_LANES), F32), pltpu.VMEM((rows, C_LANES), F32),
                        pltpu.VMEM((rows, C_WIDTH), F32),
                        pltpu.VMEM((bsz, C_LANES), F32), pltpu.VMEM((bsz, C_LANES), F32)],
        compiler_params=_cparams("arbitrary"),
        name="s5",
    )(zmix, *consts)


def _merge_kernel(alpha, x_ref, ya_ref, yb_ref, gb_ref, yc_ref, win_ref, bgate_ref, wb_ref,
                  wout_ref, g_ref, b_ref, o_ref):
    tm, d = x_ref.shape
    b_lo, c_lo = A_WIDTH, A_WIDTH + B_V_WIDTH
    for rows in _row_parts(tm):
        x = x_ref[rows, :]
        gates = _sigmoid(_dot(x, win_ref[:, MIX_COLS:]) + bgate_ref[...])
        yb = yb_ref[rows, :].astype(F32) * gb_ref[rows, :].astype(F32)
        merged = (gates[:, :d] * _dot(ya_ref[rows, :], wb_ref[:b_lo, :])
                  + gates[:, d:2 * d] * _dot(yb, wb_ref[b_lo:c_lo, :])
                  + gates[:, 2 * d:] * _dot(yc_ref[rows, :], wb_ref[c_lo:, :]))
        o_ref[rows, :] = _layer_norm(alpha * x + _dot(merged, wout_ref[...]), g_ref[...], b_ref[...])


def _merge(x2d, ya, yb, zmix2d, yc, consts, l, alpha, tm):
    m, d = x2d.shape
    tile = lambda n: pl.BlockSpec((tm, n), lambda i: (i, 0))
    gate_tile = pl.BlockSpec((tm, B_V_WIDTH), lambda i: (i, G_BLK))
    return pl.pallas_call(
        functools.partial(_merge_kernel, alpha),
        grid=(m // tm,),
        in_specs=[tile(d), tile(A_WIDTH), tile(B_V_WIDTH), gate_tile, tile(C_WIDTH)]
                 + [_layer_spec(a, l) for a in consts],
        out_specs=tile(d),
        out_shape=jax.ShapeDtypeStruct((m, d), F32),
        compiler_params=_cparams("parallel"),
        name="merge",
    )(x2d, ya, yb, zmix2d, yc, *consts)


def _ret_merge_kernel(alpha, x_ref, ya_ref, z_ref, gb_ref, yc_ref, dmask_ref, qd_ref, kd_ref, cd_ref,
                      win_ref, bgate_ref, wb_ref, wout_ref, g_ref, b_ref, o_ref, state_s, upd_s, st_s, yb_s):
    _ret_kernel(z_ref, dmask_ref, qd_ref, kd_ref, cd_ref, yb_s, state_s, upd_s, st_s)
    _merge_kernel(alpha, x_ref, ya_ref, yb_s, gb_ref, yc_ref, win_ref, bgate_ref, wb_ref, wout_ref, g_ref, b_ref,
                  o_ref)


def _ret_merge(x2d, ya, zmix2d, yc, tabs, consts, bsz, seq, l, alpha, tm):
    d = x2d.shape[1]
    per_seq = seq // tm
    row_tile = lambda n, j=0: pl.BlockSpec((tm, n), lambda b, i: (b * per_seq + i, j))
    n_items = (tm // B_CHUNK) * B_HEADS
    per_item = pltpu.VMEM((n_items, B_QK_DIM, B_V_DIM), F32)
    return pl.pallas_call(
        functools.partial(_ret_merge_kernel, alpha),
        grid=(bsz, per_seq),
        in_specs=[row_tile(d), row_tile(A_WIDTH), row_tile(QKV_COLS), row_tile(B_V_WIDTH, G_BLK), row_tile(C_WIDTH)]
                 + [_full(a.shape) for a in tabs] + [_layer_spec(a, l) for a in consts],
        out_specs=row_tile(d),
        out_shape=jax.ShapeDtypeStruct((bsz * seq, d), F32),
        scratch_shapes=[pltpu.VMEM((B_HEADS, B_QK_DIM, B_V_DIM), F32), per_item, per_item,
                        pltpu.VMEM((tm, B_V_WIDTH), BF16)],
        compiler_params=_cparams("parallel", "arbitrary"),
        name="ret_merge",
    )(x2d, ya, zmix2d, zmix2d, yc, *tabs, *consts)


def _ffn_kernel(alpha, tf, x_ref, w1_ref, w2_ref, g_ref, b_ref, o_ref):
    x = x_ref[...]
    xb = x.astype(BF16)
    acc = None
    for j in range(w1_ref.shape[1] // tf):
        h = jnp.maximum(jnp.dot(xb, w1_ref[:, j * tf:(j + 1) * tf], preferred_element_type=F32), 0.0)
        part = jnp.dot((h * h).astype(BF16), w2_ref[j * tf:(j + 1) * tf, :], preferred_element_type=F32)
        acc = part if acc is None else acc + part
    o_ref[...] = _layer_norm(alpha * x + acc, g_ref[...], b_ref[...])


def _ffn(x2d, consts, l, alpha, tm, tf):
    m, d = x2d.shape
    return pl.pallas_call(
        functools.partial(_ffn_kernel, alpha, tf),
        grid=(m // tm,),
        in_specs=[pl.BlockSpec((tm, d), lambda i: (i, 0))] + [_layer_spec(a, l) for a in consts],
        out_specs=pl.BlockSpec((tm, d), lambda i: (i, 0)),
        out_shape=jax.ShapeDtypeStruct((m, d), F32),
        compiler_params=_cparams("parallel"),
        name="ffn",
    )(x2d, *consts)


def _tile(n, want):
    t = min(n, want)
    assert n % t == 0, (n, want)
    return t


def kernel(x, w_in, b_gate, a_shift, a_w0, a_w2, a_a0, a_a2, a_g2, a_kk, a_ka, a_rk, a_lnx_g, a_lnx_b, c_lam_re, c_lam_im, c_log_dt, c_b_re, c_b_im, c_c_re, c_c_im, c_d, c_w_glu, c_b_glu, w_branch, w_out, ln1_g, ln1_b, w_ff1, w_ff2, ln2_g, ln2_b):
    bsz, seq, d = x.shape
    depth = w_in.shape[0]
    alpha = (2.0 * depth) ** 0.25
    tokens = bsz * seq
    rows = lambda a: a.reshape(depth, 1, -1)
    rwkv_params = _rwkv_params(dict(a_shift=a_shift, a_w0=a_w0, a_w2=a_w2, a_a0=a_a0, a_a2=a_a2, a_g2=a_g2,
                                    a_kk=a_kk, a_ka=a_ka, a_rk=a_rk, a_lnx_g=a_lnx_g, a_lnx_b=a_lnx_b))
    s5_params = _s5_params(dict(c_lam_re=c_lam_re, c_lam_im=c_lam_im, c_log_dt=c_log_dt, c_b_re=c_b_re,
                                c_b_im=c_b_im, c_c_re=c_c_re, c_c_im=c_c_im, c_d=c_d, c_w_glu=c_w_glu,
                                c_b_glu=c_b_glu))
    rope_tabs, ret_tables = _retention_tables(seq)
    w_in_bf = w_in.astype(BF16)
    merge_consts = [w_in_bf, rows(b_gate), w_branch.astype(BF16), w_out.astype(BF16),
                    rows(ln1_g), rows(ln1_b)]
    ffn_consts = [w_ff1.astype(BF16), w_ff2.astype(BF16), rows(ln2_g), rows(ln2_b)]
    xt = x.reshape(tokens, d)
    for l in range(depth):
        zmix, ya = _proj_rwkv(xt, w_in_bf, rwkv_params, rope_tabs, bsz, seq, l, _tile(seq, TM_PROJ_RWKV))
        zmix = zmix.reshape(bsz, seq, REST_COLS)
        yc = _s5(zmix, bsz, seq, s5_params, l, _tile(seq, TT_S5))
        x1 = _ret_merge(xt, ya.reshape(tokens, -1), zmix.reshape(tokens, REST_COLS), yc.reshape(tokens, -1),
                        ret_tables, merge_consts, bsz, seq, l, alpha, _tile(seq, TM_MERGE))
        xt = _ffn(x1, ffn_consts, l, alpha, _tile(tokens, TM_FFN), TF_FFN)
    return xt.reshape(bsz, seq, d)
```

```python
import functools
import math

import jax
import jax.numpy as jnp
from jax import lax
from jax.experimental import pallas as pl
from jax.experimental.pallas import tpu as pltpu

F32 = jnp.float32
BF16 = jnp.bfloat16

A_HEADS = 8
A_HEAD_DIM = 64
A_WIDTH = A_HEADS * A_HEAD_DIM
A_DECAY_LORA = 64
A_ICLR_LORA = 64
A_GATE_LORA = 128
A_PROJ = 3 * A_WIDTH + A_DECAY_LORA + A_ICLR_LORA + A_GATE_LORA
A_GN_EPS = 64e-5
A_CHUNK = 64
A_GROUP = 8
A_GLANES = 128
A_SUMLANES = 256
A_LORA_IN = A_DECAY_LORA + A_ICLR_LORA
PROJ_CHUNK = 512
PROJ_EVERY = 4

B_HEADS = 4
B_QK_DIM = 128
B_V_DIM = 256
B_QK_WIDTH = B_HEADS * B_QK_DIM
B_V_WIDTH = B_HEADS * B_V_DIM
B_CHUNK = 256
B_ROPE_BASE = 10000.0
B_GN_EPS = 1e-5

C_WIDTH = 512
C_GROUP = 16
C_GROUPS = C_WIDTH // C_GROUP
C_STATE = 64
C_LANES = C_GROUPS * C_STATE
C_BLOCKS = 4
C_ULANES = C_WIDTH // C_BLOCKS
C_SLANES = C_LANES // C_BLOCKS
C_PARTS = 4

LN_EPS = 1e-5

RET_COLS = 2 * B_QK_WIDTH + 2 * B_V_WIDTH
REST_COLS = RET_COLS + C_WIDTH
MIX_COLS = A_PROJ + REST_COLS
QKV_COLS = 2 * B_QK_WIDTH + B_V_WIDTH
U_BLK = RET_COLS // C_WIDTH
G_BLK = QKV_COLS // B_V_WIDTH
assert RET_COLS % C_WIDTH == 0 and QKV_COLS % B_V_WIDTH == 0

V7X_VMEM_LIMIT_BYTES = 56 * 1024 * 1024

TM_PROJ_RWKV = 512
TC_RETENTION = 1024
TT_S5 = 64
TM_MERGE = 512
TM_FFN = 1024
TF_FFN = 1024


def _cparams(*sem):
    return pltpu.CompilerParams(dimension_semantics=sem, vmem_limit_bytes=V7X_VMEM_LIMIT_BYTES)


def _full(shape):
    n = len(shape)
    return pl.BlockSpec(shape, lambda *_: (0,) * n)


def _layer_spec(a, l):
    n = a.ndim - 1
    return pl.BlockSpec((None,) + a.shape[1:], lambda *_: (l,) + (0,) * n, pipeline_mode=pl.Buffered(1))


def _dot(a, b):
    return jnp.dot(a.astype(BF16), b.astype(BF16), preferred_element_type=F32)


def _dot_nt(a, b):
    return lax.dot_general(a.astype(BF16), b.astype(BF16), (((1,), (1,)), ((), ())),
                           preferred_element_type=F32)


def _dot_tn(a, b):
    return lax.dot_general(a.astype(BF16), b.astype(BF16), (((0,), (0,)), ((), ())),
                           preferred_element_type=F32)


def _split2(x):
    hi = x.astype(BF16)
    lo = (x - hi.astype(F32)).astype(BF16)
    return hi, lo


def _sigmoid(x):
    return 1.0 / (1.0 + jnp.exp(-x))


def _row_parts(n, parts=2):
    step = n // parts
    return [slice(i * step, (i + 1) * step) for i in range(parts)]


def _layer_norm(y, g, b):
    mu = jnp.mean(y, axis=-1, keepdims=True)
    d = y - mu
    var = jnp.mean(d * d, axis=-1, keepdims=True)
    return d * lax.rsqrt(var + LN_EPS) * g + b


def _chunk_rows(c):
    return pl.ds(c * A_CHUNK, A_CHUNK)


def _proj_rwkv_kernel(x_ref, win_ref, mu_ref, w0_ref, a0_ref, lora_ref, g2_ref, kkp_ref, kap_ref,
                      rkp_ref, lng_ref, lnb_ref, ones_ref, tri_ref, cos_ref, sin_ref,
                      rest_ref, y_ref,
                      xb_s, carry_s, state_s, r_s, k_s, v_s, kn_s, b0_s, lw_s, gate_s, bonus_s, o_s,
                      gam_s, x_s, m2_s, q_s, op_s):
    tm = x_ref.shape[0]
    n_chunks = tm // A_CHUNK
    w = A_WIDTH

    @pl.when(pl.program_id(1) == 0)
    def _():
        carry_s[...] = jnp.zeros_like(carry_s)
        state_s[...] = jnp.zeros_like(state_s)

    xb_s[...] = x_ref[...].astype(BF16)
    z = jnp.dot(xb_s[...], win_ref[:, :A_PROJ], preferred_element_type=F32)

    cos2 = cos_ref[...]
    sin2 = sin_ref[...]

    def rope(t):
        return t * cos2 + pltpu.roll(t, B_QK_DIM // 2, 1) * sin2

    k_off, v_off, g_off = B_QK_WIDTH, 2 * B_QK_WIDTH, 2 * B_QK_WIDTH + B_V_WIDTH

    def piece(lo):
        hi = lo + PROJ_CHUNK
        val = jnp.dot(xb_s[...], win_ref[:, A_PROJ + lo:A_PROJ + hi], preferred_element_type=F32)
        if lo < v_off:
            scale = 1.0 if lo < k_off else B_QK_DIM ** -0.5
            for j in range(PROJ_CHUNK // B_QK_DIM):
                hs = slice(j * B_QK_DIM, (j + 1) * B_QK_DIM)
                rest_ref[:, lo + j * B_QK_DIM:lo + (j + 1) * B_QK_DIM] = (
                    rope(val[:, hs]) * scale).astype(rest_ref.dtype)
        elif g_off <= lo < RET_COLS:
            rest_ref[:, lo:hi] = (val * _sigmoid(val)).astype(rest_ref.dtype)
        else:
            rest_ref[:, lo:hi] = val.astype(rest_ref.dtype)

    pending = list(range(0, REST_COLS, PROJ_CHUNK))
    calls = [0]

    def fill(n=1):
        for _ in range(n):
            if pending and calls[0] % PROJ_EVERY == 0:
                piece(pending.pop(0))
            calls[0] += 1

    def flush():
        while pending:
            piece(pending.pop(0))

    ones = ones_ref[...]
    gw = A_GLANES
    groups = [slice(g * gw, (g + 1) * gw) for g in range(w // gw)]
    slabs = [slice(g * A_SUMLANES, (g + 1) * A_SUMLANES) for g in range(w // A_SUMLANES)]

    def seg_sum(t):
        tb = t.astype(BF16)
        return jnp.concatenate([jnp.dot(tb[:, s], ones, preferred_element_type=F32) for s in slabs], axis=1)

    rolled = pltpu.roll(z, 1, 0)
    rowid = lax.broadcasted_iota(jnp.int32, z.shape, 0)
    prev = jnp.where(rowid == 0, jnp.broadcast_to(carry_s[0:1, :], z.shape), rolled)
    zs = z + mu_ref[...] * (prev - z)
    carry_s[0:1, :] = z[tm - 1:tm, :]
    fill()

    lz = zs[:, 3 * w:3 * w + A_LORA_IN]
    lane = lax.broadcasted_iota(jnp.int32, lz.shape, 1)
    lin = jnp.where(lane < A_DECAY_LORA, jnp.tanh(lz), lz)
    wa = _dot(lin, lora_ref[...])
    lw_s[...] = (-math.exp(-0.5)) * _sigmoid(w0_ref[...] + wa[:, :w])
    ia = _sigmoid(a0_ref[...] + wa[:, w:])
    gate_s[...] = _dot(_sigmoid(zs[:, 3 * w + A_LORA_IN:3 * w + A_LORA_IN + A_GATE_LORA]), g2_ref[...])

    r = zs[:, :w]
    k = zs[:, w:2 * w]
    v = zs[:, 2 * w:3 * w]
    kk = k * kkp_ref[...]
    kn = kk * lax.rsqrt(jnp.maximum(seg_sum(kk * kk), 1e-24))
    kmod = k * (1.0 + (ia - 1.0) * kap_ref[...])
    r_s[...] = r
    k_s[...] = kmod
    v_s[...] = v
    kn_s[...] = kn
    b0_s[...] = kn * ia
    bonus_s[...] = seg_sum(r * kmod * rkp_ref[...]) * v
    fill()

    tri = tri_ref[...]
    hpg = gw // A_HEAD_DIM
    rid = lax.broadcasted_iota(jnp.int32, (A_CHUNK, gw), 0)
    cid = lax.broadcasted_iota(jnp.int32, (A_CHUNK, gw), 1) % A_HEAD_DIM
    strict = rid > cid
    incl = rid >= cid
    eye = (rid == cid).astype(F32)
    brow = lax.broadcasted_iota(jnp.int32, (gw, gw), 0) // A_HEAD_DIM
    bcol = lax.broadcasted_iota(jnp.int32, (gw, gw), 1) // A_HEAD_DIM
    same_head = brow == bcol
    same_head_bf = same_head.astype(BF16)
    n = A_HEAD_DIM

    def bd(x):
        xb = x.astype(BF16)
        return jnp.concatenate([xb] * hpg, axis=0) * same_head_bf


    group = min(A_GROUP, n_chunks)
    assert n_chunks % group == 0

    def state_free_part(gi):
        items = []
        for j in range(group):
            c = gi * group + j
            rows = _chunk_rows(c)
            lw = lw_s[rows, :]
            hi, lo = _split2(lw)
            cum = (jnp.dot(tri, hi, preferred_element_type=F32)
                   + jnp.dot(tri, lo, preferred_element_type=F32))
            e_in = jnp.exp(cum)
            e_ex = jnp.exp(cum - lw)
            e_ng = jnp.exp(-cum)
            rt = r_s[rows, :] * e_in
            at = -kn_s[rows, :] * e_ex
            bt = b0_s[rows, :] * e_ng
            kt = k_s[rows, :] * e_ng
            vv = v_s[rows, :]
            gam_s[pl.ds(c, 1), :] = e_in[A_CHUNK - 1:A_CHUNK, :]
            for g, s in enumerate(groups):
                items.append((c * len(groups) + g, at[:, s], rt[:, s], bt[:, s], kt[:, s], vv[:, s]))
            if j % 2 == 1:
                fill()
        ids = range(len(items))
        idx = [it[0] for it in items]
        a_ = [it[1] for it in items]
        r_ = [it[2] for it in items]
        b_ = [it[3] for it in items]
        k_ = [it[4] for it in items]
        v_ = [it[5] for it in items]
        ar = [jnp.concatenate([a_[i], r_[i]], axis=0) for i in ids]
        gb = [_dot_nt(ar[i], bd(b_[i])) for i in ids]
        fill()
        gk = [_dot_nt(ar[i], bd(k_[i])) for i in ids]
        fill()
        l_ab = [jnp.where(strict, gb[i][:A_CHUNK], 0.0) for i in ids]
        a_qb = [jnp.where(incl, gb[i][A_CHUNK:], 0.0) for i in ids]
        akq = [jnp.concatenate([jnp.where(strict, gk[i][:A_CHUNK], 0.0),
                                jnp.where(incl, gk[i][A_CHUNK:], 0.0)], axis=0) for i in ids]
        akqv = [_dot(akq[i], bd(v_[i])) for i in ids]
        fill()
        akv = [akqv[i][:A_CHUNK] for i in ids]
        ov = [akqv[i][A_CHUNK:] for i in ids]
        tinv = [eye + l_ab[i] for i in ids]
        p = [_dot(l_ab[i], bd(l_ab[i])) for i in ids]
        fill()
        for _ in range(4):
            tp = [_dot(jnp.concatenate([tinv[i], p[i]], axis=0), bd(p[i])) for i in ids]
            fill()
            tinv = [tinv[i] + tp[i][:A_CHUNK] for i in ids]
            p = [tp[i][A_CHUNK:] for i in ids]
        tinv = [tinv[i] + _dot(tinv[i], bd(p[i])) for i in ids]
        fill()
        wu = [_dot(tinv[i], jnp.concatenate([bd(a_[i]), bd(akv[i])], axis=1)) for i in ids]
        fill()
        wm = [wu[i][:, :gw] for i in ids]
        uv = [wu[i][:, gw:] for i in ids]
        for i in ids:
            x_s[idx[i]] = jnp.where(same_head, _dot_tn(wm[i], b_[i]), 0.0)
        fill()
        for i in ids:
            m2 = jnp.where(same_head, _dot_tn(jnp.concatenate([uv[i], v_[i]], axis=0),
                                              jnp.concatenate([b_[i], k_[i]], axis=0)), 0.0)
            m2_s[idx[i]] = sum(m2[h * n:(h + 1) * n] for h in range(1, hpg)) + m2[:n]
        fill()
        qo = [_dot(a_qb[i], jnp.concatenate([bd(wm[i]), bd(uv[i])], axis=1)) for i in ids]
        fill()
        for i in ids:
            q_s[idx[i]] = r_[i] + qo[i][:, :gw]
        for i in ids:
            op_s[idx[i]] = qo[i][:, gw:] + ov[i]

    def state_part(c):
        rows = _chunk_rows(c)
        gam = gam_s[pl.ds(c, 1), :]
        gids = range(len(groups))
        s0 = [state_s[g] for g in gids]
        sx = [_dot(s0[g], x_s[c * len(groups) + g]) for g in gids]
        o = [_dot_nt(q_s[c * len(groups) + g], bd(s0[g])) + op_s[c * len(groups) + g] for g in gids]
        for g in gids:
            state_s[g] = (s0[g] + sx[g] + m2_s[c * len(groups) + g]) * gam[:, groups[g]]
        o_s[rows, :] = jnp.concatenate(o, axis=1)
        fill()

    for gi in range(n_chunks // group):
        state_free_part(gi)
    for c in range(n_chunks):
        state_part(c)
    flush()

    o = o_s[...]
    inv_n = 1.0 / n
    mean = seg_sum(o) * inv_n
    d = o - mean
    var = seg_sum(d * d) * inv_n
    on = d * lax.rsqrt(var + A_GN_EPS) * lng_ref[...] + lnb_ref[...]
    y_ref[...] = ((on + bonus_s[...]) * gate_s[...]).astype(y_ref.dtype)


def _rwkv_params(p):
    depth = p["a_w0"].shape[0]
    row = lambda a: a.reshape(depth, 1, -1).astype(F32)
    w = A_WIDTH
    lora = jnp.zeros((depth, A_LORA_IN, 2 * w), F32)
    lora = lora.at[:, :A_DECAY_LORA, :w].set(p["a_w2"]).at[:, A_DECAY_LORA:, w:].set(p["a_a2"]).astype(BF16)
    hid = jnp.arange(A_SUMLANES) // A_HEAD_DIM
    ones = (hid[:, None] == hid[None, :]).astype(BF16)
    ti = jnp.arange(A_CHUNK)
    tri = (ti[:, None] >= ti[None, :]).astype(BF16)
    layered = [row(p["a_shift"]), row(p["a_w0"]), row(p["a_a0"]), lora, p["a_g2"].astype(BF16),
               row(p["a_kk"]), row(p["a_ka"]), row(p["a_rk"]), row(p["a_lnx_g"]), row(p["a_lnx_b"])]
    return layered, [ones, tri]


def _proj_rwkv(x2d, w_in_bf, params, rope_tabs, bsz, seq, l, tm):
    layered, shared = params
    cos2, sin2 = rope_tabs
    k = x2d.shape[1]
    w = A_WIDTH
    per_seq = seq // tm
    n_chunks = tm // A_CHUNK
    n_groups = w // A_GLANES
    row_tile = lambda n: pl.BlockSpec((tm, n), lambda b, i: (b * per_seq + i, 0))
    consts = [w_in_bf] + layered
    stage = pltpu.VMEM((tm, w), F32)
    per_sq = pltpu.VMEM((n_chunks * n_groups, A_GLANES, A_GLANES), F32)
    per_row = pltpu.VMEM((n_chunks * n_groups, A_CHUNK, A_GLANES), F32)
    return pl.pallas_call(
        _proj_rwkv_kernel,
        grid=(bsz, per_seq),
        in_specs=[row_tile(k)] + [_layer_spec(a, l) for a in consts] + [_full(a.shape) for a in shared]
                 + [pl.BlockSpec((tm, B_QK_DIM), lambda b, i: (i, 0))] * 2,
        out_specs=[row_tile(REST_COLS), row_tile(w)],
        out_shape=[jax.ShapeDtypeStruct((bsz * seq, REST_COLS), BF16),
                   jax.ShapeDtypeStruct((bsz * seq, w), BF16)],
        scratch_shapes=[pltpu.VMEM((tm, k), BF16),
                        pltpu.VMEM((8, A_PROJ), F32),
                        pltpu.VMEM((n_groups, A_HEAD_DIM, A_GLANES), F32),
                        stage, stage, stage, stage, stage, stage, stage, stage, stage,
                        pltpu.VMEM((max(8, n_chunks), w), F32),
                        per_sq, per_row, per_row, per_row],
        compiler_params=_cparams("parallel", "arbitrary"),
        name="proj_rwkv7",
    )(x2d, *consts, *shared, cos2, sin2)


def _ret_kernel(z_ref, dmask_ref, qd_ref, kd_ref, cd_ref, y_ref, state_s, upd_s, st_s):
    tc = z_ref.shape[0]
    n_chunks = tc // B_CHUNK
    k_off, v_off = B_QK_WIDTH, 2 * B_QK_WIDTH

    @pl.when(pl.program_id(1) == 0)
    def _():
        state_s[...] = jnp.zeros_like(state_s)

    items = [(c, h) for c in range(n_chunks) for h in range(B_HEADS)]
    rows = lambda c: slice(c * B_CHUNK, (c + 1) * B_CHUNK)
    qcol = lambda h: slice(h * B_QK_DIM, (h + 1) * B_QK_DIM)
    kcol = lambda h: slice(k_off + h * B_QK_DIM, k_off + (h + 1) * B_QK_DIM)
    vcol = lambda h: slice(v_off + h * B_V_DIM, v_off + (h + 1) * B_V_DIM)

    for c, h in items:
        upd_s[c * B_HEADS + h] = _dot_tn(z_ref[rows(c), kcol(h)] * kd_ref[h], z_ref[rows(c), vcol(h)])
    for h in range(B_HEADS):
        st = state_s[h]
        for c in range(n_chunks):
            st_s[c * B_HEADS + h] = st
            st = cd_ref[h] * st + upd_s[c * B_HEADS + h]
        state_s[h] = st

    for c, h in items:
        qc = z_ref[rows(c), qcol(h)]
        scores = _dot_nt(qc, z_ref[rows(c), kcol(h)]) * dmask_ref[h]
        o = _dot(scores, z_ref[rows(c), vcol(h)]) + _dot(qc * qd_ref[h], st_s[c * B_HEADS + h])
        mu = jnp.mean(o, axis=-1, keepdims=True)
        d = o - mu
        var = jnp.mean(d * d, axis=-1, keepdims=True)
        y_ref[rows(c), h * B_V_DIM:(h + 1) * B_V_DIM] = (d * lax.rsqrt(var + B_GN_EPS)).astype(y_ref.dtype)


def _retention_tables(seq):
    f32 = F32
    pos = jnp.arange(seq, dtype=f32)
    half = B_QK_DIM // 2
    inv_freq = B_ROPE_BASE ** (-jnp.arange(half, dtype=f32) / half)
    ang = pos[:, None] * inv_freq[None, :]
    cos, sin = jnp.cos(ang), jnp.sin(ang)
    cos2 = jnp.concatenate([cos, cos], axis=1)
    sin2 = jnp.concatenate([-sin, sin], axis=1)
    log_gamma = jnp.log(1.0 - 2.0 ** (-5.0 - jnp.arange(B_HEADS, dtype=f32)))
    idx = jnp.arange(B_CHUNK, dtype=f32)
    rel = idx[:, None] - idx[None, :]
    dmask = jnp.where(rel >= 0, jnp.exp(log_gamma[:, None, None] * jnp.maximum(rel, 0.0)), 0.0)
    qd = jnp.broadcast_to(jnp.exp(log_gamma[:, None] * (idx + 1.0))[:, :, None],
                          (B_HEADS, B_CHUNK, B_QK_DIM))
    kd = jnp.broadcast_to(jnp.exp(log_gamma[:, None] * (B_CHUNK - 1.0 - idx))[:, :, None],
                          (B_HEADS, B_CHUNK, B_QK_DIM))
    cd = jnp.broadcast_to(jnp.exp(log_gamma * B_CHUNK)[:, None, None], (B_HEADS, B_QK_DIM, B_V_DIM))
    return (cos2, sin2), [dmask, qd.astype(BF16), kd.astype(BF16), cd]


def _retention(zmix, bsz, seq, tabs, tc):
    n_items = (tc // B_CHUNK) * B_HEADS
    per_item = pltpu.VMEM((n_items, B_QK_DIM, B_V_DIM), F32)
    return pl.pallas_call(
        _ret_kernel,
        grid=(bsz, seq // tc),
        in_specs=[pl.BlockSpec((None, tc, QKV_COLS), lambda b, i: (b, i, 0))] + [_full(a.shape) for a in tabs],
        out_specs=pl.BlockSpec((None, tc, B_V_WIDTH), lambda b, i: (b, i, 0)),
        out_shape=jax.ShapeDtypeStruct((bsz, seq, B_V_WIDTH), BF16),
        scratch_shapes=[pltpu.VMEM((B_HEADS, B_QK_DIM, B_V_DIM), F32), per_item, per_item],
        compiler_params=_cparams("parallel", "arbitrary"),
        name="retention",
    )(zmix, *tabs)


def _s5_kernel(u_ref, bbr_ref, bbi_ref, ar_ref, ai_ref, cr_ref, ci_ref, d_ref, wg_ref, bg_ref,
               y_ref, xr_s, xi_s, u_s, sr_s, si_s):
    bsz, tt, _ = u_ref.shape
    tp = tt // C_PARTS
    rp = tp * bsz

    @pl.when(pl.program_id(0) == 0)
    def _():
        sr_s[...] = jnp.zeros_like(sr_s)
        si_s[...] = jnp.zeros_like(si_s)

    blk = [slice(m * C_SLANES, (m + 1) * C_SLANES) for m in range(C_BLOCKS)]


    def drive(p):
        rows = slice(p * rp, (p + 1) * rp)
        u = pltpu.einshape("btc->tbc", u_ref[:, p * tp:(p + 1) * tp, :].astype(F32)).reshape(rp, C_WIDTH)
        u_s[rows, :] = u
        ub = u.astype(BF16)
        for m in range(C_BLOCKS):
            um = ub[:, m * C_ULANES:(m + 1) * C_ULANES]
            xr_s[rows, blk[m]] = jnp.dot(um, bbr_ref[m], preferred_element_type=F32)
            xi_s[rows, blk[m]] = jnp.dot(um, bbi_ref[m], preferred_element_type=F32)

    def scan(p):
        for m in range(C_BLOCKS):
            cols = blk[m]
            ar = jnp.broadcast_to(ar_ref[:, cols], (bsz, C_SLANES))
            ai = jnp.broadcast_to(ai_ref[:, cols], (bsz, C_SLANES))
            xr, xi = sr_s[:, cols], si_s[:, cols]
            for t in range(p * tp, (p + 1) * tp):
                rows = slice(t * bsz, (t + 1) * bsz)
                xr, xi = (ar * xr - ai * xi + xr_s[rows, cols], ar * xi + ai * xr + xi_s[rows, cols])
                xr_s[rows, cols] = xr
                xi_s[rows, cols] = xi
            sr_s[:, cols] = xr
            si_s[:, cols] = xi

    def readout(p):
        rows = slice(p * rp, (p + 1) * rp)
        parts = [_dot(xr_s[rows, blk[m]], cr_ref[m]) - _dot(xi_s[rows, blk[m]], ci_ref[m])
                 for m in range(C_BLOCKS)]
        y = jnp.concatenate(parts, axis=1) + d_ref[...] * u_s[rows, :]
        y = jax.nn.gelu(y)
        y = y * _sigmoid(_dot(y, wg_ref[...]) + bg_ref[...])
        y_ref[:, p * tp:(p + 1) * tp, :] = pltpu.einshape(
            "tbc->btc", y.reshape(tp, bsz, C_WIDTH)).astype(y_ref.dtype)

    drive(0)
    for p in range(C_PARTS):
        if p + 1 < C_PARTS:
            drive(p + 1)
        scan(p)
        readout(p)


def _s5_params(p):
    f32 = F32
    depth = p["c_log_dt"].shape[0]
    dt = jnp.exp(p["c_log_dt"].astype(f32))[..., None]
    lr, li = p["c_lam_re"].astype(f32), p["c_lam_im"].astype(f32)
    mag = jnp.exp(lr * dt)
    ab_re, ab_im = mag * jnp.cos(li * dt), mag * jnp.sin(li * dt)
    den = lr * lr + li * li
    f_re = ((ab_re - 1.0) * lr + ab_im * li) / den
    f_im = (ab_im * lr - (ab_re - 1.0) * li) / den
    bre, bim = p["c_b_re"].astype(f32), p["c_b_im"].astype(f32)
    bb_re = f_re[..., None] * bre - f_im[..., None] * bim
    bb_im = f_re[..., None] * bim + f_im[..., None] * bre
    gpb = C_GROUPS // C_BLOCKS
    eye = jnp.eye(gpb, dtype=f32)

    def in_blocks(bb):
        bb = bb.reshape(depth, C_BLOCKS, gpb, C_STATE, C_GROUP)
        return jnp.einsum("lmgpc,gh->lmgchp", bb, eye).reshape(depth, C_BLOCKS, C_ULANES, C_SLANES).astype(BF16)

    def out_blocks(cc):
        cc = cc.astype(f32).reshape(depth, C_BLOCKS, gpb, C_GROUP, C_STATE)
        return jnp.einsum("lmgcp,gh->lmgphc", cc, eye).reshape(depth, C_BLOCKS, C_SLANES, C_ULANES).astype(BF16)

    return [in_blocks(bb_re), in_blocks(bb_im), ab_re.reshape(depth, 1, C_LANES), ab_im.reshape(depth, 1, C_LANES),
            out_blocks(p["c_c_re"]), out_blocks(p["c_c_im"]), p["c_d"].reshape(depth, 1, C_WIDTH).astype(f32),
            p["c_w_glu"].astype(BF16), p["c_b_glu"].reshape(depth, 1, C_WIDTH).astype(f32)]


def _s5(zmix, bsz, seq, consts, l, tt):
    rows = tt * bsz
    return pl.pallas_call(
        _s5_kernel,
        grid=(seq // tt,),
        in_specs=[pl.BlockSpec((bsz, tt, C_WIDTH), lambda i: (0, i, U_BLK))] + [_layer_spec(a, l) for a in consts],
        out_specs=pl.BlockSpec((bsz, tt, C_WIDTH), lambda i: (0, i, 0)),
        out_shape=jax.ShapeDtypeStruct((bsz, seq, C_WIDTH), BF16),
        scratch_shapes=[pltpu.VMEM((rows, C_LANES), F32), pltpu.VMEM((rows, C_LANES), F32),
                        pltpu.VMEM((rows, C_WIDTH), F32),
                        pltpu.VMEM((bsz, C_LANES), F32), pltpu.VMEM((bsz, C_LANES), F32)],
        compiler_params=_cparams("arbitrary"),
        name="s5",
    )(zmix, *consts)


def _merge_kernel(alpha, x_ref, ya_ref, yb_ref, gb_ref, yc_ref, win_ref, bgate_ref, wb_ref,
                  wout_ref, g_ref, b_ref, o_ref):
    tm, d = x_ref.shape
    b_lo, c_lo = A_WIDTH, A_WIDTH + B_V_WIDTH
    for rows in _row_parts(tm):
        x = x_ref[rows, :]
        gates = _sigmoid(_dot(x, win_ref[:, MIX_COLS:]) + bgate_ref[...])
        yb = yb_ref[rows, :].astype(F32) * gb_ref[rows, :].astype(F32)
        merged = (gates[:, :d] * _dot(ya_ref[rows, :], wb_ref[:b_lo, :])
                  + gates[:, d:2 * d] * _dot(yb, wb_ref[b_lo:c_lo, :])
                  + gates[:, 2 * d:] * _dot(yc_ref[rows, :], wb_ref[c_lo:, :]))
        o_ref[rows, :] = _layer_norm(alpha * x + _dot(merged, wout_ref[...]), g_ref[...], b_ref[...])


def _merge(x2d, ya, yb, zmix2d, yc, consts, l, alpha, tm):
    m, d = x2d.shape
    tile = lambda n: pl.BlockSpec((tm, n), lambda i: (i, 0))
    gate_tile = pl.BlockSpec((tm, B_V_WIDTH), lambda i: (i, G_BLK))
    return pl.pallas_call(
        functools.partial(_merge_kernel, alpha),
        grid=(m // tm,),
        in_specs=[tile(d), tile(A_WIDTH), tile(B_V_WIDTH), gate_tile, tile(C_WIDTH)]
                 + [_layer_spec(a, l) for a in consts],
        out_specs=tile(d),
        out_shape=jax.ShapeDtypeStruct((m, d), F32),
        compiler_params=_cparams("parallel"),
        name="merge",
    )(x2d, ya, yb, zmix2d, yc, *consts)


def _ffn_kernel(alpha, tf, x_ref, w1_ref, w2_ref, g_ref, b_ref, o_ref):
    x = x_ref[...]
    xb = x.astype(BF16)
    acc = None
    for j in range(w1_ref.shape[1] // tf):
        h = jnp.maximum(jnp.dot(xb, w1_ref[:, j * tf:(j + 1) * tf], preferred_element_type=F32), 0.0)
        part = jnp.dot((h * h).astype(BF16), w2_ref[j * tf:(j + 1) * tf, :], preferred_element_type=F32)
        acc = part if acc is None else acc + part
    o_ref[...] = _layer_norm(alpha * x + acc, g_ref[...], b_ref[...])


def _ffn(x2d, consts, l, alpha, tm, tf):
    m, d = x2d.shape
    return pl.pallas_call(
        functools.partial(_ffn_kernel, alpha, tf),
        grid=(m // tm,),
        in_specs=[pl.BlockSpec((tm, d), lambda i: (i, 0))] + [_layer_spec(a, l) for a in consts],
        out_specs=pl.BlockSpec((tm, d), lambda i: (i, 0)),
        out_shape=jax.ShapeDtypeStruct((m, d), F32),
        compiler_params=_cparams("parallel"),
        name="ffn",
    )(x2d, *consts)


def _tile(n, want):
    t = min(n, want)
    assert n % t == 0, (n, want)
    return t


def kernel(x, w_in, b_gate, a_shift, a_w0, a_w2, a_a0, a_a2, a_g2, a_kk, a_ka, a_rk, a_lnx_g, a_lnx_b, c_lam_re, c_lam_im, c_log_dt, c_b_re, c_b_im, c_c_re, c_c_im, c_d, c_w_glu, c_b_glu, w_branch, w_out, ln1_g, ln1_b, w_ff1, w_ff2, ln2_g, ln2_b):
    bsz, seq, d = x.shape
    depth = w_in.shape[0]
    alpha = (2.0 * depth) ** 0.25
    tokens = bsz * seq
    rows = lambda a: a.reshape(depth, 1, -1)
    rwkv_params = _rwkv_params(dict(a_shift=a_shift, a_w0=a_w0, a_w2=a_w2, a_a0=a_a0, a_a2=a_a2, a_g2=a_g2,
                                    a_kk=a_kk, a_ka=a_ka, a_rk=a_rk, a_lnx_g=a_lnx_g, a_lnx_b=a_lnx_b))
    s5_params = _s5_params(dict(c_lam_re=c_lam_re, c_lam_im=c_lam_im, c_log_dt=c_log_dt, c_b_re=c_b_re,
                                c_b_im=c_b_im, c_c_re=c_c_re, c_c_im=c_c_im, c_d=c_d, c_w_glu=c_w_glu,
                                c_b_glu=c_b_glu))
    rope_tabs, ret_tables = _retention_tables(seq)
    w_in_bf = w_in.astype(BF16)
    merge_consts = [w_in_bf, rows(b_gate), w_branch.astype(BF16), w_out.astype(BF16),
                    rows(ln1_g), rows(ln1_b)]
    ffn_consts = [w_ff1.astype(BF16), w_ff2.astype(BF16), rows(ln2_g), rows(ln2_b)]
    xt = x.reshape(tokens, d)
    for l in range(depth):
        zmix, ya = _proj_rwkv(xt, w_in_bf, rwkv_params, rope_tabs, bsz, seq, l, _tile(seq, TM_PROJ_RWKV))
        zmix = zmix.reshape(bsz, seq, REST_COLS)
        yb = _retention(zmix, bsz, seq, ret_tables, _tile(seq, TC_RETENTION))
        yc = _s5(zmix, bsz, seq, s5_params, l, _tile(seq, TT_S5))
        x1 = _merge(xt, ya.reshape(tokens, -1), yb.reshape(tokens, -1), zmix.reshape(tokens, REST_COLS),
                    yc.reshape(tokens, -1), merge_consts, l, alpha, _tile(tokens, TM_MERGE))
        xt = _ffn(x1, ffn_consts, l, alpha, _tile(tokens, TM_FFN), TF_FFN)
    return xt.reshape(bsz, seq, d)
```
